```python
import jax
import jax.numpy as jnp
from jax import lax
import numpy as np

D_MODEL = 2048
BATCH = 2
SEQ = 4096
DEPTH = 1
DEC_BATCH = 128
DEC_SEQ = 1
PAST_LEN = 2048
PAGE_SIZE = 128

N_HEADS_A = 8
N_KV = 2
GROUP = N_HEADS_A // N_KV
HEAD_DIM = 128
ROT_DIM = HEAD_DIM // 4
ROPE_THETA = 500000.0
CMP_LEN = 32
CMP_STRIDE = 16
SEL_LEN = 64
SEL_TOP = 16
FORCE_BONUS = 100.0
WINDOW = 512
Q_BLOCK = 128
N_HEADS_B = 4
DQK_B = 128
DV_B = 256
MLSTM_CHUNK = 64
GATE_CAP = 15.0
N_GROUPS = 4
EXPERTS_PER_GROUP = 8
N_EXPERTS = N_GROUPS * EXPERTS_PER_GROUP
TOP_K = 2
D_EXPERT = 1024
MOE_BLOCK = 64

EPS = 1e-6
NEG = -1e30

A_Q = N_HEADS_A * HEAD_DIM
A_KV = N_KV * HEAD_DIM
B_QK = N_HEADS_B * DQK_B
B_V = N_HEADS_B * DV_B
IN_SPLITS = (A_Q, A_KV, A_KV, A_KV, A_KV, A_KV, A_KV, 3 * N_HEADS_A,
             B_QK, B_QK, B_V, N_HEADS_B, N_HEADS_B, B_V, D_MODEL, D_MODEL)
IN_COLS = sum(IN_SPLITS)
F_GATE_OFFSET = sum(IN_SPLITS[:12])

kernel_name = 'nsa_mlstm_hier_moe_step'


def rms_norm(x, g):
    xf = x.astype(jnp.float32)
    y = xf * lax.rsqrt(jnp.mean(xf * xf, axis=-1, keepdims=True) + EPS)
    return (y * g.astype(jnp.float32)).astype(x.dtype)


def partial_rope(x, pos):
    half = ROT_DIM // 2
    inv = ROPE_THETA ** (-jnp.arange(half, dtype=jnp.float32) / half)
    ang = pos.astype(jnp.float32)[:, None] * inv[None, :]
    ang = ang.reshape((1, pos.shape[0]) + (1,) * (x.ndim - 3) + (half,))
    cos, sin = jnp.cos(ang), jnp.sin(ang)
    xf = x.astype(jnp.float32)
    x1, x2 = xf[..., :half], xf[..., half:ROT_DIM]
    out = jnp.concatenate([x1 * cos - x2 * sin, x2 * cos + x1 * sin, xf[..., ROT_DIM:]], axis=-1)
    return out.astype(x.dtype)


def masked_softmax(s, mask):
    s = jnp.where(mask, s, NEG)
    p = jnp.where(mask, jnp.exp(s - jnp.max(s, axis=-1, keepdims=True)), 0.0)
    return p / jnp.maximum(jnp.sum(p, axis=-1, keepdims=True), 1e-30)


def compress_blocks(x, pos_emb, w1, w2):
    B, T = x.shape[:2]
    n_cmp = (T - CMP_LEN) // CMP_STRIDE + 1
    idx = jnp.arange(n_cmp)[:, None] * CMP_STRIDE + jnp.arange(CMP_LEN)[None, :]
    blk = x[:, idx] + pos_emb[None, None, :, None, :].astype(x.dtype)
    blk = jnp.moveaxis(blk, 3, 2).reshape(B, n_cmp, N_KV, CMP_LEN * HEAD_DIM)
    return jax.nn.silu(blk @ w1) @ w2


def nsa_block(q_raw, q_rot, q_pos, kc, vc, c_end, cover, ks_b, vs_b):
    B, Q = q_raw.shape[:2]
    scale = HEAD_DIM ** -0.5
    s = jnp.einsum('bqhgd,bchd->bqhgc', q_raw, kc).astype(jnp.float32) * scale
    p = masked_softmax(s, (c_end[None, :] <= q_pos[:, None])[None, :, None, None, :])
    o_cmp = jnp.einsum('bqhgc,bchd->bqhgd', p.astype(vc.dtype), vc)
    imp = jnp.einsum('bqhgc,jc->bqhj', p, cover)
    n_sel = cover.shape[0]
    blk = jnp.arange(n_sel, dtype=jnp.int32)[None, :]
    cur = (q_pos // SEL_LEN)[:, None]
    forced = ((blk == 0) | (blk == cur) | (blk == cur - 1)).astype(jnp.float32)
    valid = blk * SEL_LEN <= q_pos[:, None]
    score = jnp.where(valid[None, :, None, :], imp + FORCE_BONUS * forced[None, :, None, :], -1.0)
    _, idx = lax.top_k(score, min(SEL_TOP, n_sel))
    n_top = idx.shape[-1]
    bi = jnp.arange(B)[:, None, None, None]
    hi = jnp.arange(N_KV)[None, None, :, None]
    kg = ks_b[bi, hi, idx].reshape(B, Q, N_KV, n_top * SEL_LEN, HEAD_DIM)
    vg = vs_b[bi, hi, idx].reshape(B, Q, N_KV, n_top * SEL_LEN, HEAD_DIM)
    kpos = (idx[..., None] * SEL_LEN + jnp.arange(SEL_LEN)).reshape(B, Q, N_KV, 1, n_top * SEL_LEN)
    s2 = jnp.einsum('bqhgd,bqhkd->bqhgk', q_rot, kg).astype(jnp.float32) * scale
    p2 = masked_softmax(s2, kpos <= q_pos[None, :, None, None, None])
    o_sel = jnp.einsum('bqhgk,bqhkd->bqhgd', p2.astype(vg.dtype), vg)
    return o_cmp, o_sel


def nsa_cmp_sel(q_raw, q_rot, q_pos, k_cmp, v_cmp, k_sel, v_sel, cmp_pos, cmp_w1, cmp_w2):
    B, Tk = k_cmp.shape[:2]
    kc = compress_blocks(k_cmp, cmp_pos[0], cmp_w1[0], cmp_w2[0])
    vc = compress_blocks(v_cmp, cmp_pos[1], cmp_w1[1], cmp_w2[1])
    c_start = jnp.arange(kc.shape[1], dtype=jnp.int32) * CMP_STRIDE
    c_end = c_start + CMP_LEN - 1
    n_sel = -(-Tk // SEL_LEN)
    s_start = jnp.arange(n_sel, dtype=jnp.int32)[:, None] * SEL_LEN
    cover = ((c_start[None, :] < s_start + SEL_LEN) & (c_end[None, :] >= s_start)).astype(jnp.float32)
    pad = n_sel * SEL_LEN - Tk

    def to_blocks(t):
        t = jnp.pad(t, ((0, 0), (0, pad), (0, 0), (0, 0)))
        return t.reshape(B, n_sel, SEL_LEN, N_KV, HEAD_DIM).transpose(0, 3, 1, 2, 4)

    ks_b, vs_b = to_blocks(k_sel), to_blocks(v_sel)
    Tq = q_raw.shape[1]
    qb = min(Q_BLOCK, Tq)
    nqb = -(-Tq // qb)
    qpad = nqb * qb - Tq

    def to_qblocks(t):
        t = jnp.pad(t, ((0, 0), (0, qpad)) + ((0, 0),) * (t.ndim - 2))
        return jnp.swapaxes(t.reshape((B, nqb, qb) + t.shape[2:]), 0, 1)

    pos_b = jnp.pad(q_pos, (0, qpad), mode='edge').reshape(nqb, qb)

    def run(args):
        qr_, qo_, qp_ = args
        return nsa_block(qr_, qo_, qp_, kc, vc, c_end, cover, ks_b, vs_b)

    o_cmp, o_sel = lax.map(run, (to_qblocks(q_raw), to_qblocks(q_rot), pos_b))

    def from_qblocks(t):
        return jnp.swapaxes(t, 0, 1).reshape((B, nqb * qb) + t.shape[3:])[:, :Tq]

    return from_qblocks(o_cmp), from_qblocks(o_sel)


def window_attend_banded(q, k, v):
    B, T = q.shape[:2]
    nqb = T // Q_BLOCK
    span = Q_BLOCK + WINDOW
    padw = ((0, 0), (WINDOW, 0), (0, 0), (0, 0))
    idx = jnp.arange(nqb)[:, None] * Q_BLOCK + jnp.arange(span)[None, :]
    kb = jnp.pad(k, padw)[:, idx]
    vb = jnp.pad(v, padw)[:, idx]
    qb = q.reshape(B, nqb, Q_BLOCK, N_KV, GROUP, HEAD_DIM)
    s = jnp.einsum('bnqhgd,bnkhd->bnqhgk', qb, kb).astype(jnp.float32) * HEAD_DIM ** -0.5
    tpos = jnp.arange(T).reshape(nqb, Q_BLOCK)[:, :, None]
    kpos = (idx - WINDOW)[:, None, :]
    mask = (kpos >= 0) & (kpos <= tpos) & (kpos >= tpos - WINDOW)
    p = masked_softmax(s, mask[None, :, :, None, None, :])
    o = jnp.einsum('bnqhgk,bnkhd->bnqhgd', p.astype(vb.dtype), vb)
    return o.reshape(B, T, N_KV, GROUP, HEAD_DIM)


def window_attend_direct(q, q_pos, k, v, k_pos):
    s = jnp.einsum('bqhgd,bkhd->bqhgk', q, k).astype(jnp.float32) * HEAD_DIM ** -0.5
    mask = (k_pos[None, :] <= q_pos[:, None]) & (k_pos[None, :] >= q_pos[:, None] - WINDOW)
    p = masked_softmax(s, mask[None, :, None, None, :])
    return jnp.einsum('bqhgk,bkhd->bqhgd', p.astype(v.dtype), v)


def combine_nsa(g_pre, o_cmp, o_sel, o_win):
    B, T = g_pre.shape[:2]
    g = jax.nn.sigmoid(g_pre.astype(jnp.float32)).reshape(B, T, N_KV, GROUP, 3)
    o = (g[..., 0:1] * o_cmp.astype(jnp.float32) + g[..., 1:2] * o_sel.astype(jnp.float32)
         + g[..., 2:3] * o_win.astype(jnp.float32))
    return o.reshape(B, T, A_Q).astype(g_pre.dtype)


def mlstm_chunkwise(q, k, v, log_i, log_f, c0, n0, m0):
    B, T, H, Dk = q.shape
    L = min(MLSTM_CHUNK, T)
    nc = -(-T // L)
    pad = nc * L - T
    if pad:
        padt = lambda a, val: jnp.pad(a, [(0, 0), (0, pad)] + [(0, 0)] * (a.ndim - 2), constant_values=val)
        q, k, v = padt(q, 0.0), padt(k, 0.0), padt(v, 0.0)
        log_i, log_f = padt(log_i, NEG), padt(log_f, 0.0)
    chunk = lambda a: jnp.moveaxis(a.reshape((B, nc, L) + a.shape[2:]), 3, 1)
    qc, kc, vc, ic, fc = chunk(q), chunk(k), chunk(v), chunk(log_i), chunk(log_f)
    bcum = jnp.cumsum(fc, axis=-1)
    btot = bcum[..., -1]
    causal = jnp.tril(jnp.ones((L, L), dtype=bool))
    dlog = jnp.where(causal, bcum[..., :, None] - bcum[..., None, :] + ic[..., None, :], NEG)
    wlog = btot[..., None] - bcum + ic

    def step(carry, xs):
        c, n, m = carry
        k_, v_, w_, bt = xs
        m_new = jnp.maximum(bt + m, jnp.max(w_, axis=-1))
        decay = jnp.exp(bt + m - m_new)
        w = jnp.exp(w_ - m_new[..., None])
        c_new = decay[..., None, None] * c + jnp.einsum('bhl,bhlk,bhlv->bhkv', w, k_, v_)
        n_new = decay[..., None] * n + jnp.einsum('bhl,bhlk->bhk', w, k_)
        return (c_new, n_new, m_new), (c, n, m)

    xs = (jnp.moveaxis(kc, 2, 0), jnp.moveaxis(vc, 2, 0), jnp.moveaxis(wlog, 2, 0), jnp.moveaxis(btot, 2, 0))
    (c_fin, n_fin, m_fin), (cs, ns, ms) = lax.scan(step, (c0, n0, m0), xs)
    cs, ns, ms = jnp.moveaxis(cs, 0, 2), jnp.moveaxis(ns, 0, 2), jnp.moveaxis(ms, 0, 2)
    inter = bcum + ms[..., None]
    m_t = jnp.maximum(inter, jnp.max(dlog, axis=-1))
    w_intra = jnp.exp(dlog - m_t[..., None]) * jnp.einsum('bhcld,bhcsd->bhcls', qc, kc)
    w_inter = jnp.exp(inter - m_t)
    num = (w_inter[..., None] * jnp.einsum('bhcld,bhcdv->bhclv', qc, cs)
           + jnp.einsum('bhcls,bhcsv->bhclv', w_intra, vc))
    den = w_inter * jnp.einsum('bhcld,bhcd->bhcl', qc, ns) + jnp.sum(w_intra, axis=-1)
    h = num / jnp.maximum(jnp.abs(den), jnp.exp(-m_t))[..., None]
    h = jnp.moveaxis(h, 1, 3).reshape(B, nc * L, H, v.shape[-1])[:, :T]
    return h, c_fin, n_fin, m_fin


def mlstm_branch(q, k, v, i_pre, f_pre, o_pre, head_gain, c0, n0, m0):
    B, T = q.shape[:2]
    f32 = jnp.float32
    log_i = GATE_CAP * jnp.tanh(i_pre.astype(f32) / GATE_CAP)
    log_f = jax.nn.log_sigmoid(GATE_CAP * jnp.tanh(f_pre.astype(f32) / GATE_CAP))
    h, c, n, m = mlstm_chunkwise(q.astype(f32), k.astype(f32) * DQK_B ** -0.5, v.astype(f32), log_i, log_f,
                                 c0.astype(f32), n0.astype(f32), m0.astype(f32))
    h = h * lax.rsqrt(jnp.mean(h * h, axis=-1, keepdims=True) + EPS) * head_gain.astype(f32).reshape(N_HEADS_B, DV_B)
    out = h.reshape(B, T, B_V) * jax.nn.sigmoid(o_pre.astype(f32))
    return out.astype(q.dtype), c, n, m


def project_inputs(x, pos, norm_g, w_in, b_in):
    B, T, _ = x.shape
    z = rms_norm(x, norm_g) @ w_in + b_in
    cuts = np.cumsum(IN_SPLITS)[:-1].tolist()
    (q_a, k_c, v_c, k_s, v_s, k_w, v_w, g_nsa, q_b, k_b, v_b, i_b, f_b, o_b, g_ma, g_mb) = jnp.split(z, cuts, axis=-1)
    kv = lambda t: t.reshape(B, T, N_KV, HEAD_DIM)
    q_a = q_a.reshape(B, T, N_KV, GROUP, HEAD_DIM)
    return (q_a, partial_rope(q_a, pos), kv(k_c), kv(v_c), partial_rope(kv(k_s), pos), kv(v_s),
            partial_rope(kv(k_w), pos), kv(v_w), g_nsa,
            q_b.reshape(B, T, N_HEADS_B, DQK_B), k_b.reshape(B, T, N_HEADS_B, DQK_B),
            v_b.reshape(B, T, N_HEADS_B, DV_B), i_b, f_b, o_b, g_ma, g_mb)


def routed_experts(x2d, expert, gates, w_gate, w_up, w_down):
    N, D = x2d.shape
    A = N * TOP_K
    flat_e = expert.reshape(A)
    order = jnp.argsort(flat_e)
    e_sorted = flat_e[order]
    tok_sorted = order // TOP_K
    counts = jnp.bincount(flat_e, length=N_EXPERTS)
    padded = (counts + MOE_BLOCK - 1) // MOE_BLOCK * MOE_BLOCK
    pad_end = jnp.cumsum(padded)
    dest = (pad_end - padded)[e_sorted] + jnp.arange(A) - (jnp.cumsum(counts) - counts)[e_sorted]
    n_blocks = (A + N_EXPERTS * (MOE_BLOCK - 1) + MOE_BLOCK - 1) // MOE_BLOCK
    rows = jnp.zeros((n_blocks * MOE_BLOCK, D), x2d.dtype).at[dest].set(x2d[tok_sorted])
    blk_expert = jnp.minimum(jnp.searchsorted(pad_end, jnp.arange(n_blocks) * MOE_BLOCK, side='right'), N_EXPERTS - 1)

    def expert_block(args):
        xb, e = args
        return (jax.nn.silu(xb @ w_gate[e]) * (xb @ w_up[e])) @ w_down[e]

    out = lax.map(expert_block, (rows.reshape(n_blocks, MOE_BLOCK, D), blk_expert)).reshape(-1, D)
    contrib = out[dest].astype(jnp.float32) * gates.reshape(A)[order][:, None]
    return jax.ops.segment_sum(contrib, tok_sorted, num_segments=N).astype(x2d.dtype)


def moe_ffn(x2d, w_rg, b_rg, w_re, b_re, w_gate, w_up, w_down):
    N = x2d.shape[0]
    rows = jnp.arange(N)
    grp_logits = (x2d @ w_rg).astype(jnp.float32) + b_rg.astype(jnp.float32)
    grp_prob = jax.nn.softmax(grp_logits, axis=-1)
    grp = jnp.argmax(grp_logits, axis=-1)
    exp_logits = ((x2d @ w_re).astype(jnp.float32) + b_re.astype(jnp.float32)).reshape(N, N_GROUPS, EXPERTS_PER_GROUP)
    top_val, top_idx = lax.top_k(exp_logits[rows, grp], TOP_K)
    gates = jax.nn.softmax(top_val, axis=-1) * grp_prob[rows, grp][:, None]
    expert = grp[:, None] * EXPERTS_PER_GROUP + top_idx
    return routed_experts(x2d, expert, gates, w_gate, w_up, w_down)


def merge_and_ffn(x, o_a, h_b, g_ma, g_mb, w_up_a, w_up_b, w_out, norm_ffn, w_rg, b_rg, w_re, b_re,
                  w_gate, w_up, w_down):
    B, T, D = x.shape
    merged = jax.nn.sigmoid(g_ma) * (o_a @ w_up_a) + jax.nn.sigmoid(g_mb) * (h_b @ w_up_b)
    x = x + merged @ w_out
    ffn = moe_ffn(rms_norm(x, norm_ffn).reshape(B * T, D), w_rg, b_rg, w_re, b_re, w_gate, w_up, w_down)
    return x + ffn.reshape(B, T, D)


def setup_inputs(seed: int = 0) -> dict:
    key = jax.random.key(seed)
    kit = iter(jax.random.split(key, 40))
    f32 = jnp.float32

    def nrm(shape, scale):
        return scale * jax.random.normal(next(kit), shape, f32)

    n_pages = PAST_LEN // PAGE_SIZE
    n_phys = (DEC_BATCH * n_pages * 5 + 3) // 4
    win_buf = min(WINDOW, PAST_LEN)
    paged = (DEPTH, n_phys, PAGE_SIZE, N_KV, HEAD_DIM)
    winb = (DEPTH, DEC_BATCH, win_buf, N_KV, HEAD_DIM)
    page_table = jax.random.permutation(next(kit), n_phys)[:DEC_BATCH * n_pages].reshape(DEC_BATCH, n_pages).astype(jnp.int32)
    b_in = nrm((DEPTH, IN_COLS), 0.02).at[:, F_GATE_OFFSET:F_GATE_OFFSET + N_HEADS_B].add(jnp.linspace(3.0, 6.0, N_HEADS_B))
    return {
        'x_prompt': nrm((BATCH, SEQ, D_MODEL), 1.0),
        'x_sample': nrm((DEC_BATCH, DEC_SEQ, D_MODEL), 1.0),
        'cache_cmp_k': nrm(paged, 1.0),
        'cache_cmp_v': nrm(paged, 1.0),
        'cache_sel_k': nrm(paged, 1.0),
        'cache_sel_v': nrm(paged, 1.0),
        'cache_win_k': nrm(winb, 1.0),
        'cache_win_v': nrm(winb, 1.0),
        'state_mlstm_c': nrm((DEPTH, DEC_BATCH, N_HEADS_B, DQK_B, DV_B), DQK_B ** -0.5),
        'state_mlstm_n': nrm((DEPTH, DEC_BATCH, N_HEADS_B, DQK_B), 1.0),
        'state_mlstm_m': nrm((DEPTH, DEC_BATCH, N_HEADS_B), 0.5),
        'page_table': page_table,
        'norm_mix': 1.0 + nrm((DEPTH, D_MODEL), 0.02),
        'w_in': nrm((DEPTH, D_MODEL, IN_COLS), D_MODEL ** -0.5),
        'b_in': b_in,
        'cmp_pos': nrm((DEPTH, 2, CMP_LEN, HEAD_DIM), 0.1),
        'cmp_w1': nrm((DEPTH, 2, CMP_LEN * HEAD_DIM, HEAD_DIM), (CMP_LEN * HEAD_DIM) ** -0.5),
        'cmp_w2': nrm((DEPTH, 2, HEAD_DIM, HEAD_DIM), HEAD_DIM ** -0.5),
        'mlstm_norm': 1.0 + nrm((DEPTH, B_V), 0.02),
        'w_up_a': nrm((DEPTH, A_Q, D_MODEL), A_Q ** -0.5),
        'w_up_b': nrm((DEPTH, B_V, D_MODEL), B_V ** -0.5),
        'w_out': nrm((DEPTH, D_MODEL, D_MODEL), D_MODEL ** -0.5),
        'norm_ffn': 1.0 + nrm((DEPTH, D_MODEL), 0.02),
        'w_router_grp': nrm((DEPTH, D_MODEL, N_GROUPS), D_MODEL ** -0.5),
        'b_router_grp': nrm((DEPTH, N_GROUPS), 0.01),
        'w_router_exp': nrm((DEPTH, D_MODEL, N_EXPERTS), D_MODEL ** -0.5),
        'b_router_exp': nrm((DEPTH, N_EXPERTS), 0.01),
        'w_gate': nrm((DEPTH, N_EXPERTS, D_MODEL, D_EXPERT), D_MODEL ** -0.5),
        'w_up': nrm((DEPTH, N_EXPERTS, D_MODEL, D_EXPERT), D_MODEL ** -0.5),
        'w_down': nrm((DEPTH, N_EXPERTS, D_EXPERT, D_MODEL), D_EXPERT ** -0.5),
        'norm_final': 1.0 + nrm((D_MODEL,), 0.02),
    }


def reference(x_prompt, x_sample, cache_cmp_k, cache_cmp_v, cache_sel_k, cache_sel_v, cache_win_k, cache_win_v,
              state_mlstm_c, state_mlstm_n, state_mlstm_m, page_table, norm_mix, w_in, b_in, cmp_pos, cmp_w1, cmp_w2,
              mlstm_norm, w_up_a, w_up_b, w_out, norm_ffn, w_router_grp, b_router_grp, w_router_exp, b_router_exp,
              w_gate, w_up, w_down, norm_final):
    f32 = jnp.float32
    Bp, Tp = x_prompt.shape[:2]
    Bs, Ts = x_sample.shape[:2]
    past_len = page_table.shape[1] * PAGE_SIZE
    pos_p = jnp.arange(Tp, dtype=jnp.int32)
    pos_s = past_len + jnp.arange(Ts, dtype=jnp.int32)
    wb = cache_win_k.shape[2]
    win_pos = past_len - wb + jnp.arange(wb + Ts, dtype=jnp.int32)

    def gather_pages(pool):
        return pool[page_table].reshape(Bs, past_len, N_KV, HEAD_DIM)

    xp, xs = x_prompt, x_sample
    prompt_new, sample_new = [], []
    for l in range(DEPTH):
        cmp_w = (cmp_pos[l], cmp_w1[l], cmp_w2[l])
        ffn_w = (w_up_a[l], w_up_b[l], w_out[l], norm_ffn[l], w_router_grp[l], b_router_grp[l],
                 w_router_exp[l], b_router_exp[l], w_gate[l], w_up[l], w_down[l])

        (qa, qr, kc, vc, ks, vs, kw, vw, gn, qb, kb, vb, ib, fb, ob, gma, gmb) = project_inputs(
            xp, pos_p, norm_mix[l], w_in[l], b_in[l])
        o_cmp, o_sel = nsa_cmp_sel(qa, qr, pos_p, kc, vc, ks, vs, *cmp_w)
        o_win = window_attend_banded(qr, kw, vw)
        o_a = combine_nsa(gn, o_cmp, o_sel, o_win)
        h_b, c_p, n_p, m_p = mlstm_branch(qb, kb, vb, ib, fb, ob, mlstm_norm[l],
                                          jnp.zeros((Bp, N_HEADS_B, DQK_B, DV_B), f32),
                                          jnp.zeros((Bp, N_HEADS_B, DQK_B), f32),
                                          jnp.zeros((Bp, N_HEADS_B), f32))
        xp = merge_and_ffn(xp, o_a, h_b, gma, gmb, *ffn_w)
        wl = min(WINDOW, Tp)
        prompt_new.append((kc, vc, ks, vs, kw[:, -wl:], vw[:, -wl:],
                           c_p.astype(xp.dtype), n_p.astype(xp.dtype), m_p.astype(xp.dtype)))

        (qa, qr, kc, vc, ks, vs, kw, vw, gn, qb, kb, vb, ib, fb, ob, gma, gmb) = project_inputs(
            xs, pos_s, norm_mix[l], w_in[l], b_in[l])
        kc_all = jnp.concatenate([gather_pages(cache_cmp_k[l]), kc], axis=1)
        vc_all = jnp.concatenate([gather_pages(cache_cmp_v[l]), vc], axis=1)
        ks_all = jnp.concatenate([gather_pages(cache_sel_k[l]), ks], axis=1)
        vs_all = jnp.concatenate([gather_pages(cache_sel_v[l]), vs], axis=1)
        o_cmp, o_sel = nsa_cmp_sel(qa, qr, pos_s, kc_all, vc_all, ks_all, vs_all, *cmp_w)
        wk = jnp.concatenate([cache_win_k[l], kw], axis=1)
        wv = jnp.concatenate([cache_win_v[l], vw], axis=1)
        o_win = window_attend_direct(qr, pos_s, wk, wv, win_pos)
        o_a = combine_nsa(gn, o_cmp, o_sel, o_win)
        h_b, c_s, n_s, m_s = mlstm_branch(qb, kb, vb, ib, fb, ob, mlstm_norm[l],
                                          state_mlstm_c[l], state_mlstm_n[l], state_mlstm_m[l])
        xs = merge_and_ffn(xs, o_a, h_b, gma, gmb, *ffn_w)
        wl = min(WINDOW, past_len + Ts)
        sd = state_mlstm_c.dtype
        sample_new.append((kc, vc, ks, vs, wk[:, -wl:], wv[:, -wl:], c_s.astype(sd), n_s.astype(sd), m_s.astype(sd)))

    y_prompt = rms_norm(xp, norm_final)
    y_sample = rms_norm(xs, norm_final)
    (p_cmp_k, p_cmp_v, p_sel_k, p_sel_v, p_win_k, p_win_v, p_c, p_n, p_m) = [jnp.stack(a) for a in zip(*prompt_new)]
    (s_cmp_k, s_cmp_v, s_sel_k, s_sel_v, s_win_k, s_win_v, s_c, s_n, s_m) = [jnp.stack(a) for a in zip(*sample_new)]
    return (y_prompt, y_sample, p_cmp_k, p_cmp_v, p_sel_k, p_sel_v, p_win_k, p_win_v, p_c, p_n, p_m,
            s_cmp_k, s_cmp_v, s_sel_k, s_sel_v, s_win_k, s_win_v, s_c, s_n, s_m)
```

```python
import functools

import numpy as np
import jax
import jax.numpy as jnp
from jax import lax
from jax.experimental import pallas as pl
from jax.experimental.pallas import tpu as pltpu

D_MODEL = 2048
PAGE_SIZE = 128
N_HEADS_A = 8
N_KV = 2
GROUP = N_HEADS_A // N_KV
HEAD_DIM = 128
ROT_DIM = HEAD_DIM // 4
ROPE_THETA = 500000.0
CMP_LEN = 32
CMP_STRIDE = 16
SEL_LEN = 64
SEL_TOP = 16
FORCE_BONUS = 100.0
WINDOW = 512
Q_BLOCK = 128
N_HEADS_B = 4
DQK_B = 128
DV_B = 256
MLSTM_CHUNK = 64
GATE_CAP = 15.0
N_GROUPS = 4
EXPERTS_PER_GROUP = 8
N_EXPERTS = N_GROUPS * EXPERTS_PER_GROUP
TOP_K = 2
D_EXPERT = 1024
MOE_BLOCK = 64
EPS = 1e-6
NEG = -1e30

A_Q = N_HEADS_A * HEAD_DIM
A_KV = N_KV * HEAD_DIM
B_QK = N_HEADS_B * DQK_B
B_V = N_HEADS_B * DV_B
IN_SPLITS = (A_Q, A_KV, A_KV, A_KV, A_KV, A_KV, A_KV, 3 * N_HEADS_A,
             B_QK, B_QK, B_V, N_HEADS_B, N_HEADS_B, B_V, D_MODEL, D_MODEL)
IN_COLS = sum(IN_SPLITS)

F32 = jnp.float32
BF16 = jnp.bfloat16
VMEM_LIMIT = 48 * 1024 * 1024

_CUTS = np.concatenate([[0], np.cumsum(IN_SPLITS)]).tolist()
_SRC = dict(zip(('q_a', 'k_c', 'v_c', 'k_s', 'v_s', 'k_w', 'v_w', 'g_nsa', 'q_b', 'k_b', 'v_b', 'i_b', 'f_b',
                 'o_b', 'g_ma', 'g_mb'), zip(_CUTS[:-1], _CUTS[1:])))
_Z_ORDER = ('q_a', 'q_a', 'k_c', 'v_c', 'k_s', 'v_s', 'k_w', 'v_w', 'q_b', 'k_b', 'v_b', 'o_b', 'g_ma', 'g_mb',
            'g_nsa', 'i_b', 'f_b')
PROJ_TN = 512
Z_QRAW, Z_QROT, Z_KC, Z_VC, Z_KS, Z_VS, Z_KW, Z_VW = 0, 1024, 2048, 2304, 2560, 2816, 3072, 3328
Z_QB, Z_KB, Z_VB, Z_OB, Z_GMA, Z_GMB, Z_SMALL = 3584, 4096, 4608, 5632, 6656, 8704, 10752
Z_GNSA, Z_IB, Z_FB = Z_SMALL, Z_SMALL + 24, Z_SMALL + 28
NZ = 11264
_ROPE_ALL_TILES = (Z_QROT // PROJ_TN, Z_QROT // PROJ_TN + 1)
_ROPE_HALF_TILES = (Z_KS // PROJ_TN, Z_KW // PROJ_TN)


def _proj_kernel(x_ref, g_ref, w_ref, b_ref, cos_ref, sa_ref, sb_ref, z_ref, xn_ref):
    j = pl.program_id(1)

    @pl.when(j == 0)
    def _():
        xf = x_ref[...]
        ms = jnp.mean(xf * xf, axis=-1, keepdims=True)
        xn_ref[...] = (xf * lax.rsqrt(ms + EPS) * g_ref[...]).astype(BF16)

    acc = jnp.dot(xn_ref[...], w_ref[...], preferred_element_type=F32) + b_ref[...]

    def rope(blk):
        return (blk * cos_ref[...] + pltpu.roll(blk, HEAD_DIM - ROT_DIM // 2, 1) * sa_ref[...]
                + pltpu.roll(blk, ROT_DIM // 2, 1) * sb_ref[...])

    is_all = (j == _ROPE_ALL_TILES[0]) | (j == _ROPE_ALL_TILES[1])
    is_half = (j == _ROPE_HALF_TILES[0]) | (j == _ROPE_HALF_TILES[1])
    n_blk = PROJ_TN // HEAD_DIM

    @pl.when(is_all)
    def _():
        for c in range(n_blk):
            z_ref[:, c * HEAD_DIM:(c + 1) * HEAD_DIM] = rope(acc[:, c * HEAD_DIM:(c + 1) * HEAD_DIM])

    @pl.when(is_half)
    def _():
        for c in range(n_blk // 2):
            z_ref[:, c * HEAD_DIM:(c + 1) * HEAD_DIM] = rope(acc[:, c * HEAD_DIM:(c + 1) * HEAD_DIM])
        z_ref[:, PROJ_TN // 2:] = acc[:, PROJ_TN // 2:]

    @pl.when(jnp.logical_not(is_all | is_half))
    def _():
        z_ref[...] = acc


def _rope_tables(pos):
    half = ROT_DIM // 2
    inv = ROPE_THETA ** (-jnp.arange(half, dtype=F32) / half)
    ang = pos.astype(F32)[:, None] * inv[None, :]
    cos, sin = jnp.cos(ang), jnp.sin(ang)
    n = pos.shape[0]
    ones = jnp.ones((n, HEAD_DIM - ROT_DIM), F32)
    zeros = jnp.zeros((n, HEAD_DIM - half), F32)
    cos_t = jnp.concatenate([cos, cos, ones], axis=1)
    sa_t = jnp.concatenate([-sin, zeros], axis=1)
    sb_t = jnp.concatenate([jnp.zeros((n, half), F32), sin, jnp.zeros((n, HEAD_DIM - ROT_DIM), F32)], axis=1)
    return cos_t, sa_t, sb_t


def _permute_in_proj(w_in, b_in):
    pad = NZ - sum(_SRC[k][1] - _SRC[k][0] for k in _Z_ORDER)
    w = jnp.concatenate([w_in[:, _SRC[k][0]:_SRC[k][1]] for k in _Z_ORDER]
                        + [jnp.zeros((D_MODEL, pad), w_in.dtype)], axis=1).astype(BF16)
    b = jnp.concatenate([b_in[_SRC[k][0]:_SRC[k][1]] for k in _Z_ORDER] + [jnp.zeros((pad,), b_in.dtype)])
    return w, b.reshape(1, NZ).astype(F32)


def _project(x2d, pos_rows, norm_g, w_perm, b_perm, tm):
    m = x2d.shape[0]
    cos_t, sa_t, sb_t = _rope_tables(pos_rows)
    row = lambda i, j: (i, 0)
    return pl.pallas_call(
        _proj_kernel,
        out_shape=jax.ShapeDtypeStruct((m, NZ), F32),
        grid=(m // tm, NZ // PROJ_TN),
        in_specs=[pl.BlockSpec((tm, D_MODEL), row),
                  pl.BlockSpec((1, D_MODEL), lambda i, j: (0, 0)),
                  pl.BlockSpec((D_MODEL, PROJ_TN), lambda i, j: (0, j)),
                  pl.BlockSpec((1, PROJ_TN), lambda i, j: (0, j)),
                  pl.BlockSpec((tm, HEAD_DIM), row),
                  pl.BlockSpec((tm, HEAD_DIM), row),
                  pl.BlockSpec((tm, HEAD_DIM), row)],
        out_specs=pl.BlockSpec((tm, PROJ_TN), lambda i, j: (i, j)),
        scratch_shapes=[pltpu.VMEM((tm, D_MODEL), BF16)],
        compiler_params=pltpu.CompilerParams(dimension_semantics=("parallel", "arbitrary"),
                                             vmem_limit_bytes=VMEM_LIMIT),
        name="proj",
    )(x2d, norm_g.reshape(1, D_MODEL), w_perm, b_perm, cos_t, sa_t, sb_t)


def rms_norm(x, g):
    xf = x.astype(jnp.float32)
    y = xf * lax.rsqrt(jnp.mean(xf * xf, axis=-1, keepdims=True) + EPS)
    return (y * g.astype(jnp.float32)).astype(x.dtype)


def masked_softmax(s, mask):
    s = jnp.where(mask, s, NEG)
    p = jnp.where(mask, jnp.exp(s - jnp.max(s, axis=-1, keepdims=True)), 0.0)
    return p / jnp.maximum(jnp.sum(p, axis=-1, keepdims=True), 1e-30)


def compress_blocks(x, pos_emb, w1, w2):
    B, T = x.shape[:2]
    n_cmp = (T - CMP_LEN) // CMP_STRIDE + 1
    idx = jnp.arange(n_cmp)[:, None] * CMP_STRIDE + jnp.arange(CMP_LEN)[None, :]
    blk = x[:, idx] + pos_emb[None, None, :, None, :].astype(x.dtype)
    blk = jnp.moveaxis(blk, 3, 2).reshape(B, n_cmp, N_KV, CMP_LEN * HEAD_DIM)
    return jax.nn.silu(blk @ w1) @ w2


def nsa_block(q_raw, q_rot, q_pos, kc, vc, c_end, cover, ks_b, vs_b):
    B, Q = q_raw.shape[:2]
    scale = HEAD_DIM ** -0.5
    s = jnp.einsum('bqhgd,bchd->bqhgc', q_raw, kc).astype(jnp.float32) * scale
    p = masked_softmax(s, (c_end[None, :] <= q_pos[:, None])[None, :, None, None, :])
    o_cmp = jnp.einsum('bqhgc,bchd->bqhgd', p.astype(vc.dtype), vc)
    imp = jnp.einsum('bqhgc,jc->bqhj', p, cover)
    n_sel = cover.shape[0]
    blk = jnp.arange(n_sel, dtype=jnp.int32)[None, :]
    cur = (q_pos // SEL_LEN)[:, None]
    forced = ((blk == 0) | (blk == cur) | (blk == cur - 1)).astype(jnp.float32)
    valid = blk * SEL_LEN <= q_pos[:, None]
    score = jnp.where(valid[None, :, None, :], imp + FORCE_BONUS * forced[None, :, None, :], -1.0)
    _, idx = lax.top_k(score, min(SEL_TOP, n_sel))
    n_top = idx.shape[-1]
    bi = jnp.arange(B)[:, None, None, None]
    hi = jnp.arange(N_KV)[None, None, :, None]
    kg = ks_b[bi, hi, idx].reshape(B, Q, N_KV, n_top * SEL_LEN, HEAD_DIM)
    vg = vs_b[bi, hi, idx].reshape(B, Q, N_KV, n_top * SEL_LEN, HEAD_DIM)
    kpos = (idx[..., None] * SEL_LEN + jnp.arange(SEL_LEN)).reshape(B, Q, N_KV, 1, n_top * SEL_LEN)
    s2 = jnp.einsum('bqhgd,bqhkd->bqhgk', q_rot, kg).astype(jnp.float32) * scale
    p2 = masked_softmax(s2, kpos <= q_pos[None, :, None, None, None])
    o_sel = jnp.einsum('bqhgk,bqhkd->bqhgd', p2.astype(vg.dtype), vg)
    return o_cmp, o_sel


def nsa_cmp_sel(q_raw, q_rot, q_pos, k_cmp, v_cmp, k_sel, v_sel, cmp_pos, cmp_w1, cmp_w2):
    B, Tk = k_cmp.shape[:2]
    kc = compress_blocks(k_cmp, cmp_pos[0], cmp_w1[0], cmp_w2[0])
    vc = compress_blocks(v_cmp, cmp_pos[1], cmp_w1[1], cmp_w2[1])
    c_start = jnp.arange(kc.shape[1], dtype=jnp.int32) * CMP_STRIDE
    c_end = c_start + CMP_LEN - 1
    n_sel = -(-Tk // SEL_LEN)
    s_start = jnp.arange(n_sel, dtype=jnp.int32)[:, None] * SEL_LEN
    cover = ((c_start[None, :] < s_start + SEL_LEN) & (c_end[None, :] >= s_start)).astype(jnp.float32)
    pad = n_sel * SEL_LEN - Tk

    def to_blocks(t):
        t = jnp.pad(t, ((0, 0), (0, pad), (0, 0), (0, 0)))
        return t.reshape(B, n_sel, SEL_LEN, N_KV, HEAD_DIM).transpose(0, 3, 1, 2, 4)

    ks_b, vs_b = to_blocks(k_sel), to_blocks(v_sel)
    Tq = q_raw.shape[1]
    qb = min(Q_BLOCK, Tq)
    nqb = -(-Tq // qb)
    qpad = nqb * qb - Tq

    def to_qblocks(t):
        t = jnp.pad(t, ((0, 0), (0, qpad)) + ((0, 0),) * (t.ndim - 2))
        return jnp.swapaxes(t.reshape((B, nqb, qb) + t.shape[2:]), 0, 1)

    pos_b = jnp.pad(q_pos, (0, qpad), mode='edge').reshape(nqb, qb)

    def run(args):
        qr_, qo_, qp_ = args
        return nsa_block(qr_, qo_, qp_, kc, vc, c_end, cover, ks_b, vs_b)

    o_cmp, o_sel = lax.map(run, (to_qblocks(q_raw), to_qblocks(q_rot), pos_b))

    def from_qblocks(t):
        return jnp.swapaxes(t, 0, 1).reshape((B, nqb * qb) + t.shape[3:])[:, :Tq]

    return from_qblocks(o_cmp), from_qblocks(o_sel)


def window_attend_banded(q, k, v):
    B, T = q.shape[:2]
    nqb = T // Q_BLOCK
    span = Q_BLOCK + WINDOW
    padw = ((0, 0), (WINDOW, 0), (0, 0), (0, 0))
    idx = jnp.arange(nqb)[:, None] * Q_BLOCK + jnp.arange(span)[None, :]
    kb = jnp.pad(k, padw)[:, idx]
    vb = jnp.pad(v, padw)[:, idx]
    qb = q.reshape(B, nqb, Q_BLOCK, N_KV, GROUP, HEAD_DIM)
    s = jnp.einsum('bnqhgd,bnkhd->bnqhgk', qb, kb).astype(jnp.float32) * HEAD_DIM ** -0.5
    tpos = jnp.arange(T).reshape(nqb, Q_BLOCK)[:, :, None]
    kpos = (idx - WINDOW)[:, None, :]
    mask = (kpos >= 0) & (kpos <= tpos) & (kpos >= tpos - WINDOW)
    p = masked_softmax(s, mask[None, :, :, None, None, :])
    o = jnp.einsum('bnqhgk,bnkhd->bnqhgd', p.astype(vb.dtype), vb)
    return o.reshape(B, T, N_KV, GROUP, HEAD_DIM)


def window_attend_direct(q, q_pos, k, v, k_pos):
    s = jnp.einsum('bqhgd,bkhd->bqhgk', q, k).astype(jnp.float32) * HEAD_DIM ** -0.5
    mask = (k_pos[None, :] <= q_pos[:, None]) & (k_pos[None, :] >= q_pos[:, None] - WINDOW)
    p = masked_softmax(s, mask[None, :, None, None, :])
    return jnp.einsum('bqhgk,bkhd->bqhgd', p.astype(v.dtype), v)


def combine_nsa(g_pre, o_cmp, o_sel, o_win):
    B, T = g_pre.shape[:2]
    g = jax.nn.sigmoid(g_pre.astype(jnp.float32)).reshape(B, T, N_KV, GROUP, 3)
    o = (g[..., 0:1] * o_cmp.astype(jnp.float32) + g[..., 1:2] * o_sel.astype(jnp.float32)
         + g[..., 2:3] * o_win.astype(jnp.float32))
    return o.reshape(B, T, A_Q).astype(g_pre.dtype)


def mlstm_chunkwise(q, k, v, log_i, log_f, c0, n0, m0):
    B, T, H, Dk = q.shape
    L = min(MLSTM_CHUNK, T)
    nc = -(-T // L)
    chunk = lambda a: jnp.moveaxis(a.reshape((B, nc, L) + a.shape[2:]), 3, 1)
    qc, kc, vc, ic, fc = chunk(q), chunk(k), chunk(v), chunk(log_i), chunk(log_f)
    bcum = jnp.cumsum(fc, axis=-1)
    btot = bcum[..., -1]
    causal = jnp.tril(jnp.ones((L, L), dtype=bool))
    dlog = jnp.where(causal, bcum[..., :, None] - bcum[..., None, :] + ic[..., None, :], NEG)
    wlog = btot[..., None] - bcum + ic

    def step(carry, xs):
        c, n, m = carry
        k_, v_, w_, bt = xs
        m_new = jnp.maximum(bt + m, jnp.max(w_, axis=-1))
        decay = jnp.exp(bt + m - m_new)
        w = jnp.exp(w_ - m_new[..., None])
        c_new = decay[..., None, None] * c + jnp.einsum('bhl,bhlk,bhlv->bhkv', w, k_, v_)
        n_new = decay[..., None] * n + jnp.einsum('bhl,bhlk->bhk', w, k_)
        return (c_new, n_new, m_new), (c, n, m)

    xs = (jnp.moveaxis(kc, 2, 0), jnp.moveaxis(vc, 2, 0), jnp.moveaxis(wlog, 2, 0), jnp.moveaxis(btot, 2, 0))
    (c_fin, n_fin, m_fin), (cs, ns, ms) = lax.scan(step, (c0, n0, m0), xs)
    cs, ns, ms = jnp.moveaxis(cs, 0, 2), jnp.moveaxis(ns, 0, 2), jnp.moveaxis(ms, 0, 2)
    inter = bcum + ms[..., None]
    m_t = jnp.maximum(inter, jnp.max(dlog, axis=-1))
    w_intra = jnp.exp(dlog - m_t[..., None]) * jnp.einsum('bhcld,bhcsd->bhcls', qc, kc)
    w_inter = jnp.exp(inter - m_t)
    num = (w_inter[..., None] * jnp.einsum('bhcld,bhcdv->bhclv', qc, cs)
           + jnp.einsum('bhcls,bhcsv->bhclv', w_intra, vc))
    den = w_inter * jnp.einsum('bhcld,bhcd->bhcl', qc, ns) + jnp.sum(w_intra, axis=-1)
    h = num / jnp.maximum(jnp.abs(den), jnp.exp(-m_t))[..., None]
    h = jnp.moveaxis(h, 1, 3).reshape(B, nc * L, H, v.shape[-1])[:, :T]
    return h, c_fin, n_fin, m_fin


def mlstm_branch(q, k, v, i_pre, f_pre, o_pre, head_gain, c0, n0, m0):
    B, T = q.shape[:2]
    f32 = jnp.float32
    log_i = GATE_CAP * jnp.tanh(i_pre.astype(f32) / GATE_CAP)
    log_f = jax.nn.log_sigmoid(GATE_CAP * jnp.tanh(f_pre.astype(f32) / GATE_CAP))
    h, c, n, m = mlstm_chunkwise(q.astype(f32), k.astype(f32) * DQK_B ** -0.5, v.astype(f32), log_i, log_f,
                                 c0.astype(f32), n0.astype(f32), m0.astype(f32))
    h = h * lax.rsqrt(jnp.mean(h * h, axis=-1, keepdims=True) + EPS) * head_gain.astype(f32).reshape(N_HEADS_B, DV_B)
    out = h.reshape(B, T, B_V) * jax.nn.sigmoid(o_pre.astype(f32))
    return out.astype(q.dtype), c, n, m


def routed_experts(x2d, expert, gates, w_gate, w_up, w_down):
    N, D = x2d.shape
    A = N * TOP_K
    flat_e = expert.reshape(A)
    order = jnp.argsort(flat_e)
    e_sorted = flat_e[order]
    tok_sorted = order // TOP_K
    counts = jnp.bincount(flat_e, length=N_EXPERTS)
    padded = (counts + MOE_BLOCK - 1) // MOE_BLOCK * MOE_BLOCK
    pad_end = jnp.cumsum(padded)
    dest = (pad_end - padded)[e_sorted] + jnp.arange(A) - (jnp.cumsum(counts) - counts)[e_sorted]
    n_blocks = (A + N_EXPERTS * (MOE_BLOCK - 1) + MOE_BLOCK - 1) // MOE_BLOCK
    rows = jnp.zeros((n_blocks * MOE_BLOCK, D), x2d.dtype).at[dest].set(x2d[tok_sorted])
    blk_expert = jnp.minimum(jnp.searchsorted(pad_end, jnp.arange(n_blocks) * MOE_BLOCK, side='right'), N_EXPERTS - 1)

    def expert_block(args):
        xb, e = args
        return (jax.nn.silu(xb @ w_gate[e]) * (xb @ w_up[e])) @ w_down[e]

    out = lax.map(expert_block, (rows.reshape(n_blocks, MOE_BLOCK, D), blk_expert)).reshape(-1, D)
    contrib = out[dest].astype(jnp.float32) * gates.reshape(A)[order][:, None]
    return jax.ops.segment_sum(contrib, tok_sorted, num_segments=N).astype(x2d.dtype)


def moe_ffn(x2d, w_rg, b_rg, w_re, b_re, w_gate, w_up, w_down):
    N = x2d.shape[0]
    rows = jnp.arange(N)
    grp_logits = (x2d @ w_rg).astype(jnp.float32) + b_rg.astype(jnp.float32)
    grp_prob = jax.nn.softmax(grp_logits, axis=-1)
    grp = jnp.argmax(grp_logits, axis=-1)
    exp_logits = ((x2d @ w_re).astype(jnp.float32) + b_re.astype(jnp.float32)).reshape(N, N_GROUPS, EXPERTS_PER_GROUP)
    top_val, top_idx = lax.top_k(exp_logits[rows, grp], TOP_K)
    gates = jax.nn.softmax(top_val, axis=-1) * grp_prob[rows, grp][:, None]
    expert = grp[:, None] * EXPERTS_PER_GROUP + top_idx
    return routed_experts(x2d, expert, gates, w_gate, w_up, w_down)


def merge_and_ffn(x, o_a, h_b, g_ma, g_mb, w_up_a, w_up_b, w_out, norm_ffn, w_rg, b_rg, w_re, b_re,
                  w_gate, w_up, w_down):
    B, T, D = x.shape
    merged = jax.nn.sigmoid(g_ma) * (o_a @ w_up_a) + jax.nn.sigmoid(g_mb) * (h_b @ w_up_b)
    x = x + merged @ w_out
    ffn = moe_ffn(rms_norm(x, norm_ffn).reshape(B * T, D), w_rg, b_rg, w_re, b_re, w_gate, w_up, w_down)
    return x + ffn.reshape(B, T, D)


def _split_z(z, B, T):
    z = z.reshape(B, T, NZ)
    kv = lambda o: z[..., o:o + A_KV].reshape(B, T, N_KV, HEAD_DIM)
    qa = lambda o: z[..., o:o + A_Q].reshape(B, T, N_KV, GROUP, HEAD_DIM)
    return (qa(Z_QRAW), qa(Z_QROT), kv(Z_KC), kv(Z_VC), kv(Z_KS), kv(Z_VS), kv(Z_KW), kv(Z_VW),
            z[..., Z_GNSA:Z_GNSA + 3 * N_HEADS_A],
            z[..., Z_QB:Z_QB + B_QK].reshape(B, T, N_HEADS_B, DQK_B),
            z[..., Z_KB:Z_KB + B_QK].reshape(B, T, N_HEADS_B, DQK_B),
            z[..., Z_VB:Z_VB + B_V].reshape(B, T, N_HEADS_B, DV_B),
            z[..., Z_IB:Z_IB + N_HEADS_B], z[..., Z_FB:Z_FB + N_HEADS_B], z[..., Z_OB:Z_OB + B_V],
            z[..., Z_GMA:Z_GMA + D_MODEL], z[..., Z_GMB:Z_GMB + D_MODEL])


def kernel(x_prompt, x_sample, cache_cmp_k, cache_cmp_v, cache_sel_k, cache_sel_v, cache_win_k, cache_win_v,
           state_mlstm_c, state_mlstm_n, state_mlstm_m, page_table, norm_mix, w_in, b_in, cmp_pos, cmp_w1, cmp_w2,
           mlstm_norm, w_up_a, w_up_b, w_out, norm_ffn, w_router_grp, b_router_grp, w_router_exp, b_router_exp,
           w_gate, w_up, w_down, norm_final):
    f32 = jnp.float32
    Bp, Tp = x_prompt.shape[:2]
    Bs, Ts = x_sample.shape[:2]
    past_len = page_table.shape[1] * PAGE_SIZE
    pos_p = jnp.arange(Tp, dtype=jnp.int32)
    pos_s = past_len + jnp.arange(Ts, dtype=jnp.int32)
    wb = cache_win_k.shape[2]
    win_pos = past_len - wb + jnp.arange(wb + Ts, dtype=jnp.int32)

    def gather_pages(pool):
        return pool[page_table].reshape(Bs, past_len, N_KV, HEAD_DIM)

    l = 0
    cmp_w = (cmp_pos[l], cmp_w1[l], cmp_w2[l])
    ffn_w = (w_up_a[l], w_up_b[l], w_out[l], norm_ffn[l], w_router_grp[l], b_router_grp[l],
             w_router_exp[l], b_router_exp[l], w_gate[l], w_up[l], w_down[l])
    w_perm, b_perm = _permute_in_proj(w_in[l], b_in[l])

    zp = _project(x_prompt.reshape(Bp * Tp, D_MODEL), jnp.tile(pos_p, Bp), norm_mix[l], w_perm, b_perm, 512)
    (qa, qr, kc, vc, ks, vs, kw, vw, gn, qb, kb, vb, ib, fb, ob, gma, gmb) = _split_z(zp, Bp, Tp)
    o_cmp, o_sel = nsa_cmp_sel(qa, qr, pos_p, kc, vc, ks, vs, *cmp_w)
    o_win = window_attend_banded(qr, kw, vw)
    o_a = combine_nsa(gn, o_cmp, o_sel, o_win)
    h_b, c_p, n_p, m_p = mlstm_branch(qb, kb, vb, ib, fb, ob, mlstm_norm[l],
                                      jnp.zeros((Bp, N_HEADS_B, DQK_B, DV_B), f32),
                                      jnp.zeros((Bp, N_HEADS_B, DQK_B), f32),
                                      jnp.zeros((Bp, N_HEADS_B), f32))
    xp = merge_and_ffn(x_prompt, o_a, h_b, gma, gmb, *ffn_w)
    wl = min(WINDOW, Tp)
    prompt_new = (kc, vc, ks, vs, kw[:, -wl:], vw[:, -wl:], c_p, n_p, m_p)

    zs = _project(x_sample.reshape(Bs * Ts, D_MODEL), jnp.repeat(pos_s, Bs), norm_mix[l], w_perm, b_perm, 128)
    (qa, qr, kc, vc, ks, vs, kw, vw, gn, qb, kb, vb, ib, fb, ob, gma, gmb) = _split_z(zs, Bs, Ts)
    kc_all = jnp.concatenate([gather_pages(cache_cmp_k[l]), kc], axis=1)
    vc_all = jnp.concatenate([gather_pages(cache_cmp_v[l]), vc], axis=1)
    ks_all = jnp.concatenate([gather_pages(cache_sel_k[l]), ks], axis=1)
    vs_all = jnp.concatenate([gather_pages(cache_sel_v[l]), vs], axis=1)
    o_cmp, o_sel = nsa_cmp_sel(qa, qr, pos_s, kc_all, vc_all, ks_all, vs_all, *cmp_w)
    wk = jnp.concatenate([cache_win_k[l], kw], axis=1)
    wv = jnp.concatenate([cache_win_v[l], vw], axis=1)
    o_win = window_attend_direct(qr, pos_s, wk, wv, win_pos)
    o_a = combine_nsa(gn, o_cmp, o_sel, o_win)
    h_b, c_s, n_s, m_s = mlstm_branch(qb, kb, vb, ib, fb, ob, mlstm_norm[l],
                                      state_mlstm_c[l], state_mlstm_n[l], state_mlstm_m[l])
    xs = merge_and_ffn(x_sample, o_a, h_b, gma, gmb, *ffn_w)
    wl = min(WINDOW, past_len + Ts)
    sample_new = (kc, vc, ks, vs, wk[:, -wl:], wv[:, -wl:], c_s, n_s, m_s)

    y_prompt = rms_norm(xp, norm_final)
    y_sample = rms_norm(xs, norm_final)
    return (y_prompt, y_sample) + tuple(a[None] for a in prompt_new) + tuple(a[None] for a in sample_new)
```

```python
import functools

import numpy as np
import jax
import jax.numpy as jnp
from jax import lax
from jax.experimental import pallas as pl
from jax.experimental.pallas import tpu as pltpu

D_MODEL = 2048
PAGE_SIZE = 128
N_HEADS_A = 8
N_KV = 2
GROUP = N_HEADS_A // N_KV
HEAD_DIM = 128
ROT_DIM = HEAD_DIM // 4
ROPE_THETA = 500000.0
CMP_LEN = 32
CMP_STRIDE = 16
SEL_LEN = 64
SEL_TOP = 16
FORCE_BONUS = 100.0
WINDOW = 512
Q_BLOCK = 128
N_HEADS_B = 4
DQK_B = 128
DV_B = 256
MLSTM_CHUNK = 64
GATE_CAP = 15.0
N_GROUPS = 4
EXPERTS_PER_GROUP = 8
N_EXPERTS = N_GROUPS * EXPERTS_PER_GROUP
TOP_K = 2
D_EXPERT = 1024
MOE_BLOCK = 64
EPS = 1e-6
NEG = -1e30

A_Q = N_HEADS_A * HEAD_DIM
A_KV = N_KV * HEAD_DIM
B_QK = N_HEADS_B * DQK_B
B_V = N_HEADS_B * DV_B
IN_SPLITS = (A_Q, A_KV, A_KV, A_KV, A_KV, A_KV, A_KV, 3 * N_HEADS_A,
             B_QK, B_QK, B_V, N_HEADS_B, N_HEADS_B, B_V, D_MODEL, D_MODEL)
IN_COLS = sum(IN_SPLITS)

F32 = jnp.float32
BF16 = jnp.bfloat16
VMEM_LIMIT = 48 * 1024 * 1024

_CUTS = np.concatenate([[0], np.cumsum(IN_SPLITS)]).tolist()
_SRC = dict(zip(('q_a', 'k_c', 'v_c', 'k_s', 'v_s', 'k_w', 'v_w', 'g_nsa', 'q_b', 'k_b', 'v_b', 'i_b', 'f_b',
                 'o_b', 'g_ma', 'g_mb'), zip(_CUTS[:-1], _CUTS[1:])))
_Z_ORDER = ('q_a', 'q_a', 'k_c', 'v_c', 'k_s', 'v_s', 'k_w', 'v_w', 'q_b', 'k_b', 'v_b', 'o_b', 'g_ma', 'g_mb')
PROJ_TN = 512
Z_QRAW, Z_QROT, Z_KC, Z_VC, Z_KS, Z_VS, Z_KW, Z_VW = 0, 1024, 2048, 2304, 2560, 2816, 3072, 3328
Z_QB, Z_KB, Z_VB, Z_OB, Z_GMA, Z_GMB, Z_SMALL = 3584, 4096, 4608, 5632, 6656, 8704, 10752
N_GATE = 3 * GROUP
Z_GNSA, Z_IB, Z_FB = Z_SMALL, Z_SMALL + 2 * HEAD_DIM, Z_SMALL + 2 * HEAD_DIM + N_HEADS_B
NZ = 11264
_ROPE_ALL_TILES = (Z_QROT // PROJ_TN, Z_QROT // PROJ_TN + 1)
_ROPE_HALF_TILES = (Z_KS // PROJ_TN, Z_KW // PROJ_TN)


def _proj_kernel(x_ref, g_ref, w_ref, b_ref, cos_ref, sa_ref, sb_ref, z_ref, xn_ref):
    j = pl.program_id(1)

    @pl.when(j == 0)
    def _():
        xf = x_ref[...]
        ms = jnp.mean(xf * xf, axis=-1, keepdims=True)
        xn_ref[...] = (xf * lax.rsqrt(ms + EPS) * g_ref[...]).astype(BF16)

    acc = jnp.dot(xn_ref[...], w_ref[...], preferred_element_type=F32) + b_ref[...]

    def rope(blk):
        return (blk * cos_ref[...] + pltpu.roll(blk, HEAD_DIM - ROT_DIM // 2, 1) * sa_ref[...]
                + pltpu.roll(blk, ROT_DIM // 2, 1) * sb_ref[...])

    is_all = (j == _ROPE_ALL_TILES[0]) | (j == _ROPE_ALL_TILES[1])
    is_half = (j == _ROPE_HALF_TILES[0]) | (j == _ROPE_HALF_TILES[1])
    n_blk = PROJ_TN // HEAD_DIM

    @pl.when(is_all)
    def _():
        for c in range(n_blk):
            z_ref[:, c * HEAD_DIM:(c + 1) * HEAD_DIM] = rope(acc[:, c * HEAD_DIM:(c + 1) * HEAD_DIM])

    @pl.when(is_half)
    def _():
        for c in range(n_blk // 2):
            z_ref[:, c * HEAD_DIM:(c + 1) * HEAD_DIM] = rope(acc[:, c * HEAD_DIM:(c + 1) * HEAD_DIM])
        z_ref[:, PROJ_TN // 2:] = acc[:, PROJ_TN // 2:]

    @pl.when(jnp.logical_not(is_all | is_half))
    def _():
        z_ref[...] = acc


def _rope_tables(pos):
    half = ROT_DIM // 2
    inv = ROPE_THETA ** (-jnp.arange(half, dtype=F32) / half)
    ang = pos.astype(F32)[:, None] * inv[None, :]
    cos, sin = jnp.cos(ang), jnp.sin(ang)
    n = pos.shape[0]
    ones = jnp.ones((n, HEAD_DIM - ROT_DIM), F32)
    zeros = jnp.zeros((n, HEAD_DIM - half), F32)
    cos_t = jnp.concatenate([cos, cos, ones], axis=1)
    sa_t = jnp.concatenate([-sin, zeros], axis=1)
    sb_t = jnp.concatenate([jnp.zeros((n, half), F32), sin, jnp.zeros((n, HEAD_DIM - ROT_DIM), F32)], axis=1)
    return cos_t, sa_t, sb_t


def _permute_in_proj(w_in, b_in):
    g0 = _SRC['g_nsa'][0]
    spans = [_SRC[k] for k in _Z_ORDER]
    spans += [(g0, g0 + N_GATE), HEAD_DIM - N_GATE, (g0 + N_GATE, g0 + 2 * N_GATE), HEAD_DIM - N_GATE,
              _SRC['i_b'], _SRC['f_b'], NZ - Z_FB - N_HEADS_B]
    wb = jnp.concatenate([w_in, b_in[None, :]], axis=0)
    parts = [jnp.zeros((D_MODEL + 1, s), wb.dtype) if isinstance(s, int) else wb[:, s[0]:s[1]] for s in spans]
    wb = jnp.concatenate(parts, axis=1)
    return wb[:D_MODEL].astype(BF16), wb[D_MODEL:].astype(F32)


def _project(x2d, pos_rows, norm_g, w_perm, b_perm, tm):
    m = x2d.shape[0]
    cos_t, sa_t, sb_t = _rope_tables(pos_rows)
    row = lambda i, j: (i, 0)
    return pl.pallas_call(
        _proj_kernel,
        out_shape=jax.ShapeDtypeStruct((m, NZ), F32),
        grid=(m // tm, NZ // PROJ_TN),
        in_specs=[pl.BlockSpec((tm, D_MODEL), row),
                  pl.BlockSpec((1, D_MODEL), lambda i, j: (0, 0)),
                  pl.BlockSpec((D_MODEL, PROJ_TN), lambda i, j: (0, j)),
                  pl.BlockSpec((1, PROJ_TN), lambda i, j: (0, j)),
                  pl.BlockSpec((tm, HEAD_DIM), row),
                  pl.BlockSpec((tm, HEAD_DIM), row),
                  pl.BlockSpec((tm, HEAD_DIM), row)],
        out_specs=pl.BlockSpec((tm, PROJ_TN), lambda i, j: (i, j)),
        scratch_shapes=[pltpu.VMEM((tm, D_MODEL), BF16)],
        compiler_params=pltpu.CompilerParams(dimension_semantics=("parallel", "arbitrary"),
                                             vmem_limit_bytes=VMEM_LIMIT),
        name="proj",
    )(x2d, norm_g.reshape(1, D_MODEL), w_perm, b_perm, cos_t, sa_t, sb_t)


def _compress_body(x_ref, w1, pb, w2, r):
    bias = pb[0:1, :HEAD_DIM] + pb[1:2, HEAD_DIM:]
    outs = []
    for h in range(N_KV):
        xh = jnp.concatenate([x_ref[:, (N_KV * s + h) * HEAD_DIM:(N_KV * s + h + 1) * HEAD_DIM]
                              for s in range(CMP_STRIDE)], axis=1).astype(BF16)
        zz = jnp.dot(xh, w1, preferred_element_type=F32)
        pre = zz[:, :HEAD_DIM] + pltpu.roll(zz[:, HEAD_DIM:], r - 1, 0) + bias
        mid = pre * jax.nn.sigmoid(pre)
        outs.append(jnp.dot(mid.astype(BF16), w2, preferred_element_type=F32))
    return outs


def _compress_kernel(xk_ref, xv_ref, w1_ref, p_ref, w2_ref, o_ref):
    r = xk_ref.shape[0]
    for which, x_ref in enumerate((xk_ref, xv_ref)):
        w1 = w1_ref[which]
        pb = jnp.dot(p_ref[which], w1, preferred_element_type=F32)
        for h, o in enumerate(_compress_body(x_ref, w1, pb, w2_ref[which], r)):
            o_ref[which, h] = o


def _compress_weights(cmp_pos, cmp_w1, cmp_w2):
    assert CMP_LEN == 2 * CMP_STRIDE
    half = CMP_STRIDE * HEAD_DIM
    w1 = jnp.concatenate([cmp_w1[:, :half], cmp_w1[:, half:]], axis=2).astype(BF16)
    p = cmp_pos.reshape(2, 2, half)
    p = jnp.concatenate([p, jnp.zeros((2, 6, half), p.dtype)], axis=1).astype(BF16)
    return w1, p, cmp_w2.astype(BF16)


def _compress_prompt(kc, vc, cw):
    B, T = kc.shape[:2]
    r = T // CMP_STRIDE
    width = CMP_STRIDE * N_KV * HEAD_DIM
    w1, p, w2 = cw
    full = lambda a: pl.BlockSpec(a.shape, lambda b: (0,) * a.ndim)
    xspec = pl.BlockSpec((None, r, width), lambda b: (b, 0, 0))
    return pl.pallas_call(
        _compress_kernel,
        out_shape=jax.ShapeDtypeStruct((B, 2, N_KV, r, HEAD_DIM), F32),
        grid=(B,),
        in_specs=[xspec, xspec, full(w1), full(p), full(w2)],
        out_specs=pl.BlockSpec((None, 2, N_KV, r, HEAD_DIM), lambda b: (b, 0, 0, 0, 0)),
        compiler_params=pltpu.CompilerParams(dimension_semantics=("parallel",), vmem_limit_bytes=VMEM_LIMIT),
        name="compress_prompt",
    )(kc.reshape(B, r, width), vc.reshape(B, r, width), w1, p, w2)


def _group_scores(q, k, tq):
    s = lax.dot_general(q, k, (((1,), (1,)), ((), ())), preferred_element_type=F32) * HEAD_DIM ** -0.5
    return s.reshape(GROUP, tq, k.shape[0])


def _stack_heads(ref):
    return jnp.concatenate([ref[:, g * HEAD_DIM:(g + 1) * HEAD_DIM] for g in range(GROUP)], axis=0).astype(BF16)


def _select_blocks(psum, qpos, n_sel):
    tq, r = psum.shape
    ci = lax.broadcasted_iota(jnp.int32, (r, n_sel), 0) * CMP_STRIDE
    cj = lax.broadcasted_iota(jnp.int32, (r, n_sel), 1) * SEL_LEN
    cover_t = ((ci < cj + SEL_LEN) & (ci + (CMP_LEN - 1) >= cj)).astype(BF16)
    p_hi = psum.astype(BF16)
    p_lo = (psum - p_hi.astype(F32)).astype(BF16)
    imp = (jnp.dot(p_hi, cover_t, preferred_element_type=F32) + jnp.dot(p_lo, cover_t, preferred_element_type=F32))
    blk = lax.broadcasted_iota(jnp.int32, (tq, n_sel), 1)
    cur = jnp.right_shift(qpos, SEL_LEN.bit_length() - 1)
    forced = (blk == 0) | (blk == cur) | (blk == cur - 1)
    valid = blk * SEL_LEN <= qpos
    score = jnp.where(valid, imp + jnp.where(forced, FORCE_BONUS, 0.0), -1.0)
    blkf = blk.astype(F32)
    sel = jnp.zeros((tq, n_sel), F32)
    for _ in range(min(SEL_TOP, n_sel)):
        m = jnp.max(score, axis=-1, keepdims=True)
        first = jnp.min(jnp.where(score == m, blkf, float(n_sel)), axis=-1, keepdims=True)
        pick = blkf == first
        sel = jnp.where(pick, 1.0, sel)
        score = jnp.where(pick, -3.0, score)
    return sel


def _nsa_prompt_kernel(qraw_ref, qrot_ref, ks_ref, vs_ref, kw_ref, vw_ref, kc_ref, vc_ref, gate_ref, o_ref,
                       m_ref, l_ref, acc_ref, *, tq, tk, seq_len):
    q0 = pl.program_id(2) * tq
    r = kc_ref.shape[0]
    n_sel = seq_len // SEL_LEN
    sel_shift = SEL_LEN.bit_length() - 1
    qpos = q0 + lax.broadcasted_iota(jnp.int32, (tq, 1), 0)

    s = _group_scores(_stack_heads(qraw_ref), kc_ref[...].astype(BF16), tq)
    c_end = lax.broadcasted_iota(jnp.int32, (tq, r), 1) * CMP_STRIDE + (CMP_LEN - 1)
    cmask = c_end <= qpos
    s = s + jnp.where(cmask, 0.0, NEG)[None]
    p = jnp.exp(s - jnp.max(s, axis=-1, keepdims=True)) * jnp.where(cmask, 1.0, 0.0)[None]
    p = p / jnp.maximum(jnp.sum(p, axis=-1, keepdims=True), 1e-30)
    o_cmp = jnp.dot(p.reshape(GROUP * tq, r).astype(BF16), vc_ref[...].astype(BF16), preferred_element_type=F32)
    psum = p[0]
    for g in range(1, GROUP):
        psum = psum + p[g]
    sel_b = _select_blocks(psum, qpos, n_sel).astype(BF16)

    q_rot = _stack_heads(qrot_ref)
    m_ref[...] = jnp.full(m_ref.shape, NEG, F32)
    l_ref[...] = jnp.zeros(l_ref.shape, F32)
    acc_ref[...] = jnp.zeros(acc_ref.shape, F32)

    def sel_tile(kt, carry):
        k0 = pl.multiple_of(kt * tk, tk)
        k = ks_ref[pl.ds(k0, tk), :].astype(BF16)
        v = vs_ref[pl.ds(k0, tk), :].astype(BF16)
        s2 = _group_scores(q_rot, k, tq)
        ej = lax.broadcasted_iota(jnp.int32, (n_sel, tk), 0)
        et = k0 + lax.broadcasted_iota(jnp.int32, (n_sel, tk), 1)
        expand = jnp.where(jnp.right_shift(et, sel_shift) == ej, 1.0, 0.0).astype(BF16)
        selm = jnp.dot(sel_b, expand, preferred_element_type=F32)
        kpos = k0 + lax.broadcasted_iota(jnp.int32, (tq, tk), 1)
        ok = (selm > 0.5) & (kpos <= qpos)
        s2 = s2 + jnp.where(ok, 0.0, NEG)[None]
        m_prev = m_ref[...]
        m_new = jnp.maximum(m_prev, jnp.max(s2, axis=-1, keepdims=True))
        alpha = jnp.exp(m_prev - m_new)
        p2 = jnp.exp(s2 - m_new)
        l_ref[...] = alpha * l_ref[...] + jnp.sum(p2, axis=-1, keepdims=True)
        pv = jnp.dot(p2.reshape(GROUP * tq, tk).astype(BF16), v, preferred_element_type=F32)
        acc_ref[...] = alpha.reshape(GROUP * tq, 1) * acc_ref[...] + pv
        m_ref[...] = m_new
        return carry

    lax.fori_loop(0, lax.div(q0 + tq + tk - 1, tk), sel_tile, 0)
    o_sel = acc_ref[...] / jnp.maximum(l_ref[...].reshape(GROUP * tq, 1), 1e-30)

    span = WINDOW + tq
    w0 = pl.multiple_of(jnp.maximum(q0 - WINDOW, 0), tq)
    s3 = _group_scores(q_rot, kw_ref[pl.ds(w0, span), :].astype(BF16), tq)
    kp = w0 + lax.broadcasted_iota(jnp.int32, (tq, span), 1)
    okw = (kp <= qpos) & (kp >= qpos - WINDOW)
    s3 = s3 + jnp.where(okw, 0.0, NEG)[None]
    p3 = jnp.exp(s3 - jnp.max(s3, axis=-1, keepdims=True))
    p3 = p3 / jnp.sum(p3, axis=-1, keepdims=True)
    o_win = jnp.dot(p3.reshape(GROUP * tq, span).astype(BF16), vw_ref[pl.ds(w0, span), :].astype(BF16),
                    preferred_element_type=F32)

    gts = jax.nn.sigmoid(gate_ref[...])
    for g in range(GROUP):
        rows = slice(g * tq, (g + 1) * tq)
        o_ref[:, g * HEAD_DIM:(g + 1) * HEAD_DIM] = (gts[:, 3 * g:3 * g + 1] * o_cmp[rows]
                                                     + gts[:, 3 * g + 1:3 * g + 2] * o_sel[rows]
                                                     + gts[:, 3 * g + 2:3 * g + 3] * o_win[rows])


def _nsa_prompt(z, cmp, B, T, tq=128, tk=512):
    assert T % tk == 0 and tk % tq == 0 and tq % HEAD_DIM == 0 and T >= WINDOW + tq
    assert SEL_LEN & (SEL_LEN - 1) == 0 and tk % SEL_LEN == 0
    nq = T // tq
    r = cmp.shape[3]
    gw = GROUP * HEAD_DIM
    qspec = lambda off: pl.BlockSpec((tq, gw), lambda b, h, q: (b * nq + q, off // gw + h))
    kvspec = lambda off: pl.BlockSpec((T, HEAD_DIM), lambda b, h, q: (b, off // HEAD_DIM + h))
    cspec = lambda which: pl.BlockSpec((None, None, None, r, HEAD_DIM), lambda b, h, q: (b, which, h, 0, 0))
    return pl.pallas_call(
        functools.partial(_nsa_prompt_kernel, tq=tq, tk=tk, seq_len=T),
        out_shape=jax.ShapeDtypeStruct((B * T, A_Q), F32),
        grid=(B, N_KV, nq),
        in_specs=[qspec(Z_QRAW), qspec(Z_QROT), kvspec(Z_KS), kvspec(Z_VS), kvspec(Z_KW), kvspec(Z_VW),
                  cspec(0), cspec(1),
                  pl.BlockSpec((tq, HEAD_DIM), lambda b, h, q: (b * nq + q, Z_GNSA // HEAD_DIM + h))],
        out_specs=pl.BlockSpec((tq, gw), lambda b, h, q: (b * nq + q, h)),
        scratch_shapes=[pltpu.VMEM((GROUP, tq, 1), F32), pltpu.VMEM((GROUP, tq, 1), F32),
                        pltpu.VMEM((GROUP * tq, HEAD_DIM), F32)],
        compiler_params=pltpu.CompilerParams(dimension_semantics=("parallel", "parallel", "arbitrary"),
                                             vmem_limit_bytes=VMEM_LIMIT),
        name="nsa_prompt",
    )(z, z, z, z, z, z, cmp, cmp, z)


def rms_norm(x, g):
    xf = x.astype(jnp.float32)
    y = xf * lax.rsqrt(jnp.mean(xf * xf, axis=-1, keepdims=True) + EPS)
    return (y * g.astype(jnp.float32)).astype(x.dtype)


def masked_softmax(s, mask):
    s = jnp.where(mask, s, NEG)
    p = jnp.where(mask, jnp.exp(s - jnp.max(s, axis=-1, keepdims=True)), 0.0)
    return p / jnp.maximum(jnp.sum(p, axis=-1, keepdims=True), 1e-30)


def compress_blocks(x, pos_emb, w1, w2):
    B, T = x.shape[:2]
    n_cmp = (T - CMP_LEN) // CMP_STRIDE + 1
    idx = jnp.arange(n_cmp)[:, None] * CMP_STRIDE + jnp.arange(CMP_LEN)[None, :]
    blk = x[:, idx] + pos_emb[None, None, :, None, :].astype(x.dtype)
    blk = jnp.moveaxis(blk, 3, 2).reshape(B, n_cmp, N_KV, CMP_LEN * HEAD_DIM)
    return jax.nn.silu(blk @ w1) @ w2


def nsa_block(q_raw, q_rot, q_pos, kc, vc, c_end, cover, ks_b, vs_b):
    B, Q = q_raw.shape[:2]
    scale = HEAD_DIM ** -0.5
    s = jnp.einsum('bqhgd,bchd->bqhgc', q_raw, kc).astype(jnp.float32) * scale
    p = masked_softmax(s, (c_end[None, :] <= q_pos[:, None])[None, :, None, None, :])
    o_cmp = jnp.einsum('bqhgc,bchd->bqhgd', p.astype(vc.dtype), vc)
    imp = jnp.einsum('bqhgc,jc->bqhj', p, cover)
    n_sel = cover.shape[0]
    blk = jnp.arange(n_sel, dtype=jnp.int32)[None, :]
    cur = (q_pos // SEL_LEN)[:, None]
    forced = ((blk == 0) | (blk == cur) | (blk == cur - 1)).astype(jnp.float32)
    valid = blk * SEL_LEN <= q_pos[:, None]
    score = jnp.where(valid[None, :, None, :], imp + FORCE_BONUS * forced[None, :, None, :], -1.0)
    _, idx = lax.top_k(score, min(SEL_TOP, n_sel))
    n_top = idx.shape[-1]
    bi = jnp.arange(B)[:, None, None, None]
    hi = jnp.arange(N_KV)[None, None, :, None]
    kg = ks_b[bi, hi, idx].reshape(B, Q, N_KV, n_top * SEL_LEN, HEAD_DIM)
    vg = vs_b[bi, hi, idx].reshape(B, Q, N_KV, n_top * SEL_LEN, HEAD_DIM)
    kpos = (idx[..., None] * SEL_LEN + jnp.arange(SEL_LEN)).reshape(B, Q, N_KV, 1, n_top * SEL_LEN)
    s2 = jnp.einsum('bqhgd,bqhkd->bqhgk', q_rot, kg).astype(jnp.float32) * scale
    p2 = masked_softmax(s2, kpos <= q_pos[None, :, None, None, None])
    o_sel = jnp.einsum('bqhgk,bqhkd->bqhgd', p2.astype(vg.dtype), vg)
    return o_cmp, o_sel


def nsa_cmp_sel(q_raw, q_rot, q_pos, k_cmp, v_cmp, k_sel, v_sel, cmp_pos, cmp_w1, cmp_w2):
    B, Tk = k_cmp.shape[:2]
    kc = compress_blocks(k_cmp, cmp_pos[0], cmp_w1[0], cmp_w2[0])
    vc = compress_blocks(v_cmp, cmp_pos[1], cmp_w1[1], cmp_w2[1])
    c_start = jnp.arange(kc.shape[1], dtype=jnp.int32) * CMP_STRIDE
    c_end = c_start + CMP_LEN - 1
    n_sel = -(-Tk // SEL_LEN)
    s_start = jnp.arange(n_sel, dtype=jnp.int32)[:, None] * SEL_LEN
    cover = ((c_start[None, :] < s_start + SEL_LEN) & (c_end[None, :] >= s_start)).astype(jnp.float32)
    pad = n_sel * SEL_LEN - Tk

    def to_blocks(t):
        t = jnp.pad(t, ((0, 0), (0, pad), (0, 0), (0, 0)))
        return t.reshape(B, n_sel, SEL_LEN, N_KV, HEAD_DIM).transpose(0, 3, 1, 2, 4)

    ks_b, vs_b = to_blocks(k_sel), to_blocks(v_sel)
    Tq = q_raw.shape[1]
    qb = min(Q_BLOCK, Tq)
    nqb = -(-Tq // qb)
    qpad = nqb * qb - Tq

    def to_qblocks(t):
        t = jnp.pad(t, ((0, 0), (0, qpad)) + ((0, 0),) * (t.ndim - 2))
        return jnp.swapaxes(t.reshape((B, nqb, qb) + t.shape[2:]), 0, 1)

    pos_b = jnp.pad(q_pos, (0, qpad), mode='edge').reshape(nqb, qb)

    def run(args):
        qr_, qo_, qp_ = args
        return nsa_block(qr_, qo_, qp_, kc, vc, c_end, cover, ks_b, vs_b)

    o_cmp, o_sel = lax.map(run, (to_qblocks(q_raw), to_qblocks(q_rot), pos_b))

    def from_qblocks(t):
        return jnp.swapaxes(t, 0, 1).reshape((B, nqb * qb) + t.shape[3:])[:, :Tq]

    return from_qblocks(o_cmp), from_qblocks(o_sel)


def window_attend_banded(q, k, v):
    B, T = q.shape[:2]
    nqb = T // Q_BLOCK
    span = Q_BLOCK + WINDOW
    padw = ((0, 0), (WINDOW, 0), (0, 0), (0, 0))
    idx = jnp.arange(nqb)[:, None] * Q_BLOCK + jnp.arange(span)[None, :]
    kb = jnp.pad(k, padw)[:, idx]
    vb = jnp.pad(v, padw)[:, idx]
    qb = q.reshape(B, nqb, Q_BLOCK, N_KV, GROUP, HEAD_DIM)
    s = jnp.einsum('bnqhgd,bnkhd->bnqhgk', qb, kb).astype(jnp.float32) * HEAD_DIM ** -0.5
    tpos = jnp.arange(T).reshape(nqb, Q_BLOCK)[:, :, None]
    kpos = (idx - WINDOW)[:, None, :]
    mask = (kpos >= 0) & (kpos <= tpos) & (kpos >= tpos - WINDOW)
    p = masked_softmax(s, mask[None, :, :, None, None, :])
    o = jnp.einsum('bnqhgk,bnkhd->bnqhgd', p.astype(vb.dtype), vb)
    return o.reshape(B, T, N_KV, GROUP, HEAD_DIM)


def window_attend_direct(q, q_pos, k, v, k_pos):
    s = jnp.einsum('bqhgd,bkhd->bqhgk', q, k).astype(jnp.float32) * HEAD_DIM ** -0.5
    mask = (k_pos[None, :] <= q_pos[:, None]) & (k_pos[None, :] >= q_pos[:, None] - WINDOW)
    p = masked_softmax(s, mask[None, :, None, None, :])
    return jnp.einsum('bqhgk,bkhd->bqhgd', p.astype(v.dtype), v)


def combine_nsa(g_pre, o_cmp, o_sel, o_win):
    B, T = g_pre.shape[:2]
    g = jax.nn.sigmoid(g_pre.astype(jnp.float32)).reshape(B, T, N_KV, GROUP, 3)
    o = (g[..., 0:1] * o_cmp.astype(jnp.float32) + g[..., 1:2] * o_sel.astype(jnp.float32)
         + g[..., 2:3] * o_win.astype(jnp.float32))
    return o.reshape(B, T, A_Q).astype(g_pre.dtype)


def mlstm_chunkwise(q, k, v, log_i, log_f, c0, n0, m0):
    B, T, H, Dk = q.shape
    L = min(MLSTM_CHUNK, T)
    nc = -(-T // L)
    chunk = lambda a: jnp.moveaxis(a.reshape((B, nc, L) + a.shape[2:]), 3, 1)
    qc, kc, vc, ic, fc = chunk(q), chunk(k), chunk(v), chunk(log_i), chunk(log_f)
    bcum = jnp.cumsum(fc, axis=-1)
    btot = bcum[..., -1]
    causal = jnp.tril(jnp.ones((L, L), dtype=bool))
    dlog = jnp.where(causal, bcum[..., :, None] - bcum[..., None, :] + ic[..., None, :], NEG)
    wlog = btot[..., None] - bcum + ic

    def step(carry, xs):
        c, n, m = carry
        k_, v_, w_, bt = xs
        m_new = jnp.maximum(bt + m, jnp.max(w_, axis=-1))
        decay = jnp.exp(bt + m - m_new)
        w = jnp.exp(w_ - m_new[..., None])
        c_new = decay[..., None, None] * c + jnp.einsum('bhl,bhlk,bhlv->bhkv', w, k_, v_)
        n_new = decay[..., None] * n + jnp.einsum('bhl,bhlk->bhk', w, k_)
        return (c_new, n_new, m_new), (c, n, m)

    xs = (jnp.moveaxis(kc, 2, 0), jnp.moveaxis(vc, 2, 0), jnp.moveaxis(wlog, 2, 0), jnp.moveaxis(btot, 2, 0))
    (c_fin, n_fin, m_fin), (cs, ns, ms) = lax.scan(step, (c0, n0, m0), xs)
    cs, ns, ms = jnp.moveaxis(cs, 0, 2), jnp.moveaxis(ns, 0, 2), jnp.moveaxis(ms, 0, 2)
    inter = bcum + ms[..., None]
    m_t = jnp.maximum(inter, jnp.max(dlog, axis=-1))
    w_intra = jnp.exp(dlog - m_t[..., None]) * jnp.einsum('bhcld,bhcsd->bhcls', qc, kc)
    w_inter = jnp.exp(inter - m_t)
    num = (w_inter[..., None] * jnp.einsum('bhcld,bhcdv->bhclv', qc, cs)
           + jnp.einsum('bhcls,bhcsv->bhclv', w_intra, vc))
    den = w_inter * jnp.einsum('bhcld,bhcd->bhcl', qc, ns) + jnp.sum(w_intra, axis=-1)
    h = num / jnp.maximum(jnp.abs(den), jnp.exp(-m_t))[..., None]
    h = jnp.moveaxis(h, 1, 3).reshape(B, nc * L, H, v.shape[-1])[:, :T]
    return h, c_fin, n_fin, m_fin


def mlstm_branch(q, k, v, i_pre, f_pre, o_pre, head_gain, c0, n0, m0):
    B, T = q.shape[:2]
    f32 = jnp.float32
    log_i = GATE_CAP * jnp.tanh(i_pre.astype(f32) / GATE_CAP)
    log_f = jax.nn.log_sigmoid(GATE_CAP * jnp.tanh(f_pre.astype(f32) / GATE_CAP))
    h, c, n, m = mlstm_chunkwise(q.astype(f32), k.astype(f32) * DQK_B ** -0.5, v.astype(f32), log_i, log_f,
                                 c0.astype(f32), n0.astype(f32), m0.astype(f32))
    h = h * lax.rsqrt(jnp.mean(h * h, axis=-1, keepdims=True) + EPS) * head_gain.astype(f32).reshape(N_HEADS_B, DV_B)
    out = h.reshape(B, T, B_V) * jax.nn.sigmoid(o_pre.astype(f32))
    return out.astype(q.dtype), c, n, m


def routed_experts(x2d, expert, gates, w_gate, w_up, w_down):
    N, D = x2d.shape
    A = N * TOP_K
    flat_e = expert.reshape(A)
    order = jnp.argsort(flat_e)
    e_sorted = flat_e[order]
    tok_sorted = order // TOP_K
    counts = jnp.bincount(flat_e, length=N_EXPERTS)
    padded = (counts + MOE_BLOCK - 1) // MOE_BLOCK * MOE_BLOCK
    pad_end = jnp.cumsum(padded)
    dest = (pad_end - padded)[e_sorted] + jnp.arange(A) - (jnp.cumsum(counts) - counts)[e_sorted]
    n_blocks = (A + N_EXPERTS * (MOE_BLOCK - 1) + MOE_BLOCK - 1) // MOE_BLOCK
    rows = jnp.zeros((n_blocks * MOE_BLOCK, D), x2d.dtype).at[dest].set(x2d[tok_sorted])
    blk_expert = jnp.minimum(jnp.searchsorted(pad_end, jnp.arange(n_blocks) * MOE_BLOCK, side='right'), N_EXPERTS - 1)

    def expert_block(args):
        xb, e = args
        return (jax.nn.silu(xb @ w_gate[e]) * (xb @ w_up[e])) @ w_down[e]

    out = lax.map(expert_block, (rows.reshape(n_blocks, MOE_BLOCK, D), blk_expert)).reshape(-1, D)
    contrib = out[dest].astype(jnp.float32) * gates.reshape(A)[order][:, None]
    return jax.ops.segment_sum(contrib, tok_sorted, num_segments=N).astype(x2d.dtype)


def moe_ffn(x2d, w_rg, b_rg, w_re, b_re, w_gate, w_up, w_down):
    N = x2d.shape[0]
    rows = jnp.arange(N)
    grp_logits = (x2d @ w_rg).astype(jnp.float32) + b_rg.astype(jnp.float32)
    grp_prob = jax.nn.softmax(grp_logits, axis=-1)
    grp = jnp.argmax(grp_logits, axis=-1)
    exp_logits = ((x2d @ w_re).astype(jnp.float32) + b_re.astype(jnp.float32)).reshape(N, N_GROUPS, EXPERTS_PER_GROUP)
    top_val, top_idx = lax.top_k(exp_logits[rows, grp], TOP_K)
    gates = jax.nn.softmax(top_val, axis=-1) * grp_prob[rows, grp][:, None]
    expert = grp[:, None] * EXPERTS_PER_GROUP + top_idx
    return routed_experts(x2d, expert, gates, w_gate, w_up, w_down)


def merge_and_ffn(x, o_a, h_b, g_ma, g_mb, w_up_a, w_up_b, w_out, norm_ffn, w_rg, b_rg, w_re, b_re,
                  w_gate, w_up, w_down):
    B, T, D = x.shape
    merged = jax.nn.sigmoid(g_ma) * (o_a @ w_up_a) + jax.nn.sigmoid(g_mb) * (h_b @ w_up_b)
    x = x + merged @ w_out
    ffn = moe_ffn(rms_norm(x, norm_ffn).reshape(B * T, D), w_rg, b_rg, w_re, b_re, w_gate, w_up, w_down)
    return x + ffn.reshape(B, T, D)


def _split_z(z, B, T):
    z = z.reshape(B, T, NZ)
    kv = lambda o: z[..., o:o + A_KV].reshape(B, T, N_KV, HEAD_DIM)
    qa = lambda o: z[..., o:o + A_Q].reshape(B, T, N_KV, GROUP, HEAD_DIM)
    return (qa(Z_QRAW), qa(Z_QROT), kv(Z_KC), kv(Z_VC), kv(Z_KS), kv(Z_VS), kv(Z_KW), kv(Z_VW),
            jnp.concatenate([z[..., Z_GNSA:Z_GNSA + N_GATE],
                             z[..., Z_GNSA + HEAD_DIM:Z_GNSA + HEAD_DIM + N_GATE]], axis=-1),
            z[..., Z_QB:Z_QB + B_QK].reshape(B, T, N_HEADS_B, DQK_B),
            z[..., Z_KB:Z_KB + B_QK].reshape(B, T, N_HEADS_B, DQK_B),
            z[..., Z_VB:Z_VB + B_V].reshape(B, T, N_HEADS_B, DV_B),
            z[..., Z_IB:Z_IB + N_HEADS_B], z[..., Z_FB:Z_FB + N_HEADS_B], z[..., Z_OB:Z_OB + B_V],
            z[..., Z_GMA:Z_GMA + D_MODEL], z[..., Z_GMB:Z_GMB + D_MODEL])


def kernel(x_prompt, x_sample, cache_cmp_k, cache_cmp_v, cache_sel_k, cache_sel_v, cache_win_k, cache_win_v,
           state_mlstm_c, state_mlstm_n, state_mlstm_m, page_table, norm_mix, w_in, b_in, cmp_pos, cmp_w1, cmp_w2,
           mlstm_norm, w_up_a, w_up_b, w_out, norm_ffn, w_router_grp, b_router_grp, w_router_exp, b_router_exp,
           w_gate, w_up, w_down, norm_final):
    f32 = jnp.float32
    Bp, Tp = x_prompt.shape[:2]
    Bs, Ts = x_sample.shape[:2]
    past_len = page_table.shape[1] * PAGE_SIZE
    pos_p = jnp.arange(Tp, dtype=jnp.int32)
    pos_s = past_len + jnp.arange(Ts, dtype=jnp.int32)
    wb = cache_win_k.shape[2]
    win_pos = past_len - wb + jnp.arange(wb + Ts, dtype=jnp.int32)

    def gather_pages(pool):
        return pool[page_table].reshape(Bs, past_len, N_KV, HEAD_DIM)

    l = 0
    cmp_w = (cmp_pos[l], cmp_w1[l], cmp_w2[l])
    ffn_w = (w_up_a[l], w_up_b[l], w_out[l], norm_ffn[l], w_router_grp[l], b_router_grp[l],
             w_router_exp[l], b_router_exp[l], w_gate[l], w_up[l], w_down[l])
    w_perm, b_perm = _permute_in_proj(w_in[l], b_in[l])

    zp = _project(x_prompt.reshape(Bp * Tp, D_MODEL), jnp.tile(pos_p, Bp), norm_mix[l], w_perm, b_perm, 512)
    (qa, qr, kc, vc, ks, vs, kw, vw, gn, qb, kb, vb, ib, fb, ob, gma, gmb) = _split_z(zp, Bp, Tp)
    cw = _compress_weights(*cmp_w)
    o_a = _nsa_prompt(zp, _compress_prompt(kc, vc, cw), Bp, Tp).reshape(Bp, Tp, A_Q)
    h_b, c_p, n_p, m_p = mlstm_branch(qb, kb, vb, ib, fb, ob, mlstm_norm[l],
                                      jnp.zeros((Bp, N_HEADS_B, DQK_B, DV_B), f32),
                                      jnp.zeros((Bp, N_HEADS_B, DQK_B), f32),
                                      jnp.zeros((Bp, N_HEADS_B), f32))
    xp = merge_and_ffn(x_prompt, o_a, h_b, gma, gmb, *ffn_w)
    wl = min(WINDOW, Tp)
    prompt_new = (kc, vc, ks, vs, kw[:, -wl:], vw[:, -wl:], c_p, n_p, m_p)

    zs = _project(x_sample.reshape(Bs * Ts, D_MODEL), jnp.repeat(pos_s, Bs), norm_mix[l], w_perm, b_perm, 128)
    (qa, qr, kc, vc, ks, vs, kw, vw, gn, qb, kb, vb, ib, fb, ob, gma, gmb) = _split_z(zs, Bs, Ts)
    kc_all = jnp.concatenate([gather_pages(cache_cmp_k[l]), kc], axis=1)
    vc_all = jnp.concatenate([gather_pages(cache_cmp_v[l]), vc], axis=1)
    ks_all = jnp.concatenate([gather_pages(cache_sel_k[l]), ks], axis=1)
    vs_all = jnp.concatenate([gather_pages(cache_sel_v[l]), vs], axis=1)
    o_cmp, o_sel = nsa_cmp_sel(qa, qr, pos_s, kc_all, vc_all, ks_all, vs_all, *cmp_w)
    wk = jnp.concatenate([cache_win_k[l], kw], axis=1)
    wv = jnp.concatenate([cache_win_v[l], vw], axis=1)
    o_win = window_attend_direct(qr, pos_s, wk, wv, win_pos)
    o_a = combine_nsa(gn, o_cmp, o_sel, o_win)
    h_b, c_s, n_s, m_s = mlstm_branch(qb, kb, vb, ib, fb, ob, mlstm_norm[l],
                                      state_mlstm_c[l], state_mlstm_n[l], state_mlstm_m[l])
    xs = merge_and_ffn(x_sample, o_a, h_b, gma, gmb, *ffn_w)
    wl = min(WINDOW, past_len + Ts)
    sample_new = (kc, vc, ks, vs, wk[:, -wl:], wv[:, -wl:], c_s, n_s, m_s)

    y_prompt = rms_norm(xp, norm_final)
    y_sample = rms_norm(xs, norm_final)
    return (y_prompt, y_sample) + tuple(a[None] for a in prompt_new) + tuple(a[None] for a in sample_new)
```

```python
import functools

import numpy as np
import jax
import jax.numpy as jnp
from jax import lax
from jax.experimental import pallas as pl
from jax.experimental.pallas import tpu as pltpu

D_MODEL = 2048
PAGE_SIZE = 128
N_HEADS_A = 8
N_KV = 2
GROUP = N_HEADS_A // N_KV
HEAD_DIM = 128
ROT_DIM = HEAD_DIM // 4
ROPE_THETA = 500000.0
CMP_LEN = 32
CMP_STRIDE = 16
SEL_LEN = 64
SEL_TOP = 16
FORCE_BONUS = 100.0
WINDOW = 512
Q_BLOCK = 128
N_HEADS_B = 4
DQK_B = 128
DV_B = 256
MLSTM_CHUNK = 64
GATE_CAP = 15.0
N_GROUPS = 4
EXPERTS_PER_GROUP = 8
N_EXPERTS = N_GROUPS * EXPERTS_PER_GROUP
TOP_K = 2
D_EXPERT = 1024
EPS = 1e-6
NEG = -1e30

A_Q = N_HEADS_A * HEAD_DIM
A_KV = N_KV * HEAD_DIM
B_QK = N_HEADS_B * DQK_B
B_V = N_HEADS_B * DV_B
IN_SPLITS = (A_Q, A_KV, A_KV, A_KV, A_KV, A_KV, A_KV, 3 * N_HEADS_A,
             B_QK, B_QK, B_V, N_HEADS_B, N_HEADS_B, B_V, D_MODEL, D_MODEL)

F32 = jnp.float32
BF16 = jnp.bfloat16
VMEM_LIMIT = 48 * 1024 * 1024
VMEM_LIMIT_MERGE = 56 * 1024 * 1024

_CUTS = np.concatenate([[0], np.cumsum(IN_SPLITS)]).tolist()
_SRC = dict(zip(('q_a', 'k_c', 'v_c', 'k_s', 'v_s', 'k_w', 'v_w', 'g_nsa', 'q_b', 'k_b', 'v_b', 'i_b', 'f_b',
                 'o_b', 'g_ma', 'g_mb'), zip(_CUTS[:-1], _CUTS[1:])))
_Z_ORDER = ('g_ma', 'g_mb', 'q_a', 'q_a', 'k_c', 'v_c', 'k_s', 'v_s', 'k_w', 'v_w', 'q_b', 'k_b', 'v_b', 'o_b')
PROJ_TN = 512
Z_GMA, Z_GMB, Z_QRAW, Z_QROT = 0, 2048, 4096, 5120
Z_KC, Z_VC, Z_KS, Z_VS, Z_KW, Z_VW = 6144, 6400, 6656, 6912, 7168, 7424
Z_QB, Z_KB, Z_VB, Z_OB, Z_SMALL = 7680, 8192, 8704, 9728, 10752
N_GATE = 3 * GROUP
Z_GNSA, Z_IB, Z_FB = Z_SMALL, Z_SMALL + 2 * HEAD_DIM, Z_SMALL + 2 * HEAD_DIM + N_HEADS_B
NZ = 11264
_ROPE_ALL_TILES = (Z_QROT // PROJ_TN, Z_QROT // PROJ_TN + 1)
_ROPE_HALF_TILES = (Z_KS // PROJ_TN, Z_KW // PROJ_TN)

ROUTE_W = 128
MOE_TILE = 256
MOE_FC = 256


def _rms(x, g):
    return x * lax.rsqrt(jnp.mean(x * x, axis=-1, keepdims=True) + EPS) * g


def _proj_kernel(x_ref, g_ref, w_ref, b_ref, cos_ref, sa_ref, sb_ref, z_ref, xn_ref):
    j = pl.program_id(1)

    @pl.when(j == 0)
    def _():
        xn_ref[...] = _rms(x_ref[...], g_ref[...]).astype(BF16)

    acc = jnp.dot(xn_ref[...], w_ref[...], preferred_element_type=F32) + b_ref[...]

    def rope(blk):
        return (blk * cos_ref[...] + pltpu.roll(blk, HEAD_DIM - ROT_DIM // 2, 1) * sa_ref[...]
                + pltpu.roll(blk, ROT_DIM // 2, 1) * sb_ref[...])

    is_all = (j == _ROPE_ALL_TILES[0]) | (j == _ROPE_ALL_TILES[1])
    is_half = (j == _ROPE_HALF_TILES[0]) | (j == _ROPE_HALF_TILES[1])
    n_blk = PROJ_TN // HEAD_DIM

    @pl.when(is_all)
    def _():
        for c in range(n_blk):
            z_ref[:, c * HEAD_DIM:(c + 1) * HEAD_DIM] = rope(acc[:, c * HEAD_DIM:(c + 1) * HEAD_DIM])

    @pl.when(is_half)
    def _():
        for c in range(n_blk // 2):
            z_ref[:, c * HEAD_DIM:(c + 1) * HEAD_DIM] = rope(acc[:, c * HEAD_DIM:(c + 1) * HEAD_DIM])
        z_ref[:, PROJ_TN // 2:] = acc[:, PROJ_TN // 2:]

    @pl.when(jnp.logical_not(is_all | is_half))
    def _():
        z_ref[...] = acc


def _rope_tables(pos):
    half = ROT_DIM // 2
    inv = ROPE_THETA ** (-jnp.arange(half, dtype=F32) / half)
    ang = pos.astype(F32)[:, None] * inv[None, :]
    cos, sin = jnp.cos(ang), jnp.sin(ang)
    n = pos.shape[0]
    ones = jnp.ones((n, HEAD_DIM - ROT_DIM), F32)
    zeros = jnp.zeros((n, HEAD_DIM - half), F32)
    cos_t = jnp.concatenate([cos, cos, ones], axis=1)
    sa_t = jnp.concatenate([-sin, zeros], axis=1)
    sb_t = jnp.concatenate([jnp.zeros((n, half), F32), sin, jnp.zeros((n, HEAD_DIM - ROT_DIM), F32)], axis=1)
    return cos_t, sa_t, sb_t


def _permute_in_proj(w_in, b_in):
    g0 = _SRC['g_nsa'][0]
    spans = [_SRC[k] for k in _Z_ORDER]
    spans += [(g0, g0 + N_GATE), HEAD_DIM - N_GATE, (g0 + N_GATE, g0 + 2 * N_GATE), HEAD_DIM - N_GATE,
              _SRC['i_b'], _SRC['f_b'], NZ - Z_FB - N_HEADS_B]
    wb = jnp.concatenate([w_in, b_in[None, :]], axis=0)
    parts = [jnp.zeros((D_MODEL + 1, s), wb.dtype) if isinstance(s, int) else wb[:, s[0]:s[1]] for s in spans]
    wb = jnp.concatenate(parts, axis=1)
    return wb[:D_MODEL].astype(BF16), wb[D_MODEL:].astype(F32)


def _project(x2d, pos_rows, norm_g, w_perm, b_perm, tm):
    m = x2d.shape[0]
    cos_t, sa_t, sb_t = _rope_tables(pos_rows)
    row = lambda i, j: (i, 0)
    return pl.pallas_call(
        _proj_kernel,
        out_shape=jax.ShapeDtypeStruct((m, NZ), F32),
        grid=(m // tm, NZ // PROJ_TN),
        in_specs=[pl.BlockSpec((tm, D_MODEL), row),
                  pl.BlockSpec((1, D_MODEL), lambda i, j: (0, 0)),
                  pl.BlockSpec((D_MODEL, PROJ_TN), lambda i, j: (0, j)),
                  pl.BlockSpec((1, PROJ_TN), lambda i, j: (0, j)),
                  pl.BlockSpec((tm, HEAD_DIM), row),
                  pl.BlockSpec((tm, HEAD_DIM), row),
                  pl.BlockSpec((tm, HEAD_DIM), row)],
        out_specs=pl.BlockSpec((tm, PROJ_TN), lambda i, j: (i, j)),
        scratch_shapes=[pltpu.VMEM((tm, D_MODEL), BF16)],
        compiler_params=pltpu.CompilerParams(dimension_semantics=("parallel", "arbitrary"),
                                             vmem_limit_bytes=VMEM_LIMIT),
        name="proj",
    )(x2d, norm_g.reshape(1, D_MODEL), w_perm, b_perm, cos_t, sa_t, sb_t)


def _compress_body(x_ref, w1, pb, w2, r):
    bias = pb[0:1, :HEAD_DIM] + pb[1:2, HEAD_DIM:]
    outs = []
    for h in range(N_KV):
        xh = jnp.concatenate([x_ref[:, (N_KV * s + h) * HEAD_DIM:(N_KV * s + h + 1) * HEAD_DIM]
                              for s in range(CMP_STRIDE)], axis=1).astype(BF16)
        zz = jnp.dot(xh, w1, preferred_element_type=F32)
        pre = zz[:, :HEAD_DIM] + pltpu.roll(zz[:, HEAD_DIM:], r - 1, 0) + bias
        mid = pre * jax.nn.sigmoid(pre)
        outs.append(jnp.dot(mid.astype(BF16), w2, preferred_element_type=F32))
    return outs


def _compress_kernel(xk_ref, xv_ref, w1_ref, p_ref, w2_ref, o_ref):
    r = xk_ref.shape[0]
    for which, x_ref in enumerate((xk_ref, xv_ref)):
        w1 = w1_ref[which]
        pb = jnp.dot(p_ref[which], w1, preferred_element_type=F32)
        for h, o in enumerate(_compress_body(x_ref, w1, pb, w2_ref[which], r)):
            o_ref[which, h] = o


def _compress_weights(cmp_pos, cmp_w1, cmp_w2):
    assert CMP_LEN == 2 * CMP_STRIDE
    half = CMP_STRIDE * HEAD_DIM
    w1 = jnp.concatenate([cmp_w1[:, :half], cmp_w1[:, half:]], axis=2).astype(BF16)
    p = cmp_pos.reshape(2, 2, half)
    p = jnp.concatenate([p, jnp.zeros((2, 6, half), p.dtype)], axis=1).astype(BF16)
    return w1, p, cmp_w2.astype(BF16)


def _compress_prompt(kc, vc, cw):
    B, T = kc.shape[:2]
    r = T // CMP_STRIDE
    width = CMP_STRIDE * N_KV * HEAD_DIM
    w1, p, w2 = cw
    full = lambda a: pl.BlockSpec(a.shape, lambda b: (0,) * a.ndim)
    xspec = pl.BlockSpec((None, r, width), lambda b: (b, 0, 0))
    return pl.pallas_call(
        _compress_kernel,
        out_shape=jax.ShapeDtypeStruct((B, 2, N_KV, r, HEAD_DIM), F32),
        grid=(B,),
        in_specs=[xspec, xspec, full(w1), full(p), full(w2)],
        out_specs=pl.BlockSpec((None, 2, N_KV, r, HEAD_DIM), lambda b: (b, 0, 0, 0, 0)),
        compiler_params=pltpu.CompilerParams(dimension_semantics=("parallel",), vmem_limit_bytes=VMEM_LIMIT),
        name="compress_prompt",
    )(kc.reshape(B, r, width), vc.reshape(B, r, width), w1, p, w2)


def _group_scores(q, k, tq):
    s = lax.dot_general(q, k, (((1,), (1,)), ((), ())), preferred_element_type=F32) * HEAD_DIM ** -0.5
    return s.reshape(GROUP, tq, k.shape[0])


def _stack_heads(ref):
    return jnp.concatenate([ref[:, g * HEAD_DIM:(g + 1) * HEAD_DIM] for g in range(GROUP)], axis=0).astype(BF16)


def _select_blocks(psum, qpos, n_sel):
    tq, r = psum.shape
    ci = lax.broadcasted_iota(jnp.int32, (r, n_sel), 0) * CMP_STRIDE
    cj = lax.broadcasted_iota(jnp.int32, (r, n_sel), 1) * SEL_LEN
    cover_t = ((ci < cj + SEL_LEN) & (ci + (CMP_LEN - 1) >= cj)).astype(BF16)
    p_hi = psum.astype(BF16)
    p_lo = (psum - p_hi.astype(F32)).astype(BF16)
    imp = (jnp.dot(p_hi, cover_t, preferred_element_type=F32) + jnp.dot(p_lo, cover_t, preferred_element_type=F32))
    blk = lax.broadcasted_iota(jnp.int32, (tq, n_sel), 1)
    cur = jnp.right_shift(qpos, SEL_LEN.bit_length() - 1)
    forced = (blk == 0) | (blk == cur) | (blk == cur - 1)
    valid = blk * SEL_LEN <= qpos
    score = jnp.where(valid, imp + jnp.where(forced, FORCE_BONUS, 0.0), -1.0)
    blkf = blk.astype(F32)
    sel = jnp.zeros((tq, n_sel), F32)
    for _ in range(min(SEL_TOP, n_sel)):
        m = jnp.max(score, axis=-1, keepdims=True)
        first = jnp.min(jnp.where(score == m, blkf, float(n_sel)), axis=-1, keepdims=True)
        pick = blkf == first
        sel = jnp.where(pick, 1.0, sel)
        score = jnp.where(pick, -3.0, score)
    return sel


def _nsa_prompt_kernel(qraw_ref, qrot_ref, ks_ref, vs_ref, kw_ref, vw_ref, kc_ref, vc_ref, gate_ref, o_ref,
                       m_ref, l_ref, acc_ref, *, tq, tk, seq_len):
    q0 = pl.program_id(2) * tq
    r = kc_ref.shape[0]
    n_sel = seq_len // SEL_LEN
    sel_shift = SEL_LEN.bit_length() - 1
    qpos = q0 + lax.broadcasted_iota(jnp.int32, (tq, 1), 0)

    s = _group_scores(_stack_heads(qraw_ref), kc_ref[...].astype(BF16), tq)
    c_end = lax.broadcasted_iota(jnp.int32, (tq, r), 1) * CMP_STRIDE + (CMP_LEN - 1)
    cmask = c_end <= qpos
    s = s + jnp.where(cmask, 0.0, NEG)[None]
    p = jnp.exp(s - jnp.max(s, axis=-1, keepdims=True)) * jnp.where(cmask, 1.0, 0.0)[None]
    p = p / jnp.maximum(jnp.sum(p, axis=-1, keepdims=True), 1e-30)
    o_cmp = jnp.dot(p.reshape(GROUP * tq, r).astype(BF16), vc_ref[...].astype(BF16), preferred_element_type=F32)
    psum = p[0]
    for g in range(1, GROUP):
        psum = psum + p[g]
    sel_b = _select_blocks(psum, qpos, n_sel).astype(BF16)

    q_rot = _stack_heads(qrot_ref)
    m_ref[...] = jnp.full(m_ref.shape, NEG, F32)
    l_ref[...] = jnp.zeros(l_ref.shape, F32)
    acc_ref[...] = jnp.zeros(acc_ref.shape, F32)

    def sel_tile(kt, carry):
        k0 = pl.multiple_of(kt * tk, tk)
        k = ks_ref[pl.ds(k0, tk), :].astype(BF16)
        v = vs_ref[pl.ds(k0, tk), :].astype(BF16)
        s2 = _group_scores(q_rot, k, tq)
        ej = lax.broadcasted_iota(jnp.int32, (n_sel, tk), 0)
        et = k0 + lax.broadcasted_iota(jnp.int32, (n_sel, tk), 1)
        expand = jnp.where(jnp.right_shift(et, sel_shift) == ej, 1.0, 0.0).astype(BF16)
        selm = jnp.dot(sel_b, expand, preferred_element_type=F32)
        kpos = k0 + lax.broadcasted_iota(jnp.int32, (tq, tk), 1)
        ok = (selm > 0.5) & (kpos <= qpos)
        s2 = s2 + jnp.where(ok, 0.0, NEG)[None]
        m_prev = m_ref[...]
        m_new = jnp.maximum(m_prev, jnp.max(s2, axis=-1, keepdims=True))
        alpha = jnp.exp(m_prev - m_new)
        p2 = jnp.exp(s2 - m_new)
        l_ref[...] = alpha * l_ref[...] + jnp.sum(p2, axis=-1, keepdims=True)
        pv = jnp.dot(p2.reshape(GROUP * tq, tk).astype(BF16), v, preferred_element_type=F32)
        acc_ref[...] = alpha.reshape(GROUP * tq, 1) * acc_ref[...] + pv
        m_ref[...] = m_new
        return carry

    lax.fori_loop(0, lax.div(q0 + tq + tk - 1, tk), sel_tile, 0)
    o_sel = acc_ref[...] / jnp.maximum(l_ref[...].reshape(GROUP * tq, 1), 1e-30)

    span = WINDOW + tq
    w0 = pl.multiple_of(jnp.maximum(q0 - WINDOW, 0), tq)
    s3 = _group_scores(q_rot, kw_ref[pl.ds(w0, span), :].astype(BF16), tq)
    kp = w0 + lax.broadcasted_iota(jnp.int32, (tq, span), 1)
    okw = (kp <= qpos) & (kp >= qpos - WINDOW)
    s3 = s3 + jnp.where(okw, 0.0, NEG)[None]
    p3 = jnp.exp(s3 - jnp.max(s3, axis=-1, keepdims=True))
    p3 = p3 / jnp.sum(p3, axis=-1, keepdims=True)
    o_win = jnp.dot(p3.reshape(GROUP * tq, span).astype(BF16), vw_ref[pl.ds(w0, span), :].astype(BF16),
                    preferred_element_type=F32)

    gts = jax.nn.sigmoid(gate_ref[...])
    for g in range(GROUP):
        rows = slice(g * tq, (g + 1) * tq)
        o_ref[:, g * HEAD_DIM:(g + 1) * HEAD_DIM] = (gts[:, 3 * g:3 * g + 1] * o_cmp[rows]
                                                     + gts[:, 3 * g + 1:3 * g + 2] * o_sel[rows]
                                                     + gts[:, 3 * g + 2:3 * g + 3] * o_win[rows]).astype(o_ref.dtype)


def _nsa_prompt(z, cmp, B, T, tq=128, tk=512):
    assert T % tk == 0 and tk % tq == 0 and tq % HEAD_DIM == 0 and T >= WINDOW + tq
    assert SEL_LEN & (SEL_LEN - 1) == 0 and tk % SEL_LEN == 0
    nq = T // tq
    r = cmp.shape[3]
    gw = GROUP * HEAD_DIM
    qspec = lambda off: pl.BlockSpec((tq, gw), lambda b, h, q: (b * nq + q, off // gw + h))
    kvspec = lambda off: pl.BlockSpec((T, HEAD_DIM), lambda b, h, q: (b, off // HEAD_DIM + h))
    cspec = lambda which: pl.BlockSpec((None, None, None, r, HEAD_DIM), lambda b, h, q: (b, which, h, 0, 0))
    return pl.pallas_call(
        functools.partial(_nsa_prompt_kernel, tq=tq, tk=tk, seq_len=T),
        out_shape=jax.ShapeDtypeStruct((B * T, A_Q), BF16),
        grid=(B, N_KV, nq),
        in_specs=[qspec(Z_QRAW), qspec(Z_QROT), kvspec(Z_KS), kvspec(Z_VS), kvspec(Z_KW), kvspec(Z_VW),
                  cspec(0), cspec(1),
                  pl.BlockSpec((tq, HEAD_DIM), lambda b, h, q: (b * nq + q, Z_GNSA // HEAD_DIM + h))],
        out_specs=pl.BlockSpec((tq, gw), lambda b, h, q: (b * nq + q, h)),
        scratch_shapes=[pltpu.VMEM((GROUP, tq, 1), F32), pltpu.VMEM((GROUP, tq, 1), F32),
                        pltpu.VMEM((GROUP * tq, HEAD_DIM), F32)],
        compiler_params=pltpu.CompilerParams(dimension_semantics=("parallel", "parallel", "arbitrary"),
                                             vmem_limit_bytes=VMEM_LIMIT),
        name="nsa_prompt",
    )(z, z, z, z, z, z, cmp, cmp, z)


def _gate_logs(i_pre, f_pre):
    log_i = GATE_CAP * jnp.tanh(i_pre / GATE_CAP)
    fc = GATE_CAP * jnp.tanh(f_pre / GATE_CAP)
    log_f = jnp.minimum(fc, 0.0) - jnp.log1p(jnp.exp(-jnp.abs(fc)))
    return log_i, log_f


def _mlstm_prompt_kernel(q_ref, k_ref, v_ref, op_ref, grow_ref, gcol_ref, gain_ref,
                         h_ref, c_ref, n_ref, m_ref):
    L = q_ref.shape[0]

    @pl.when(pl.program_id(2) == 0)
    def _():
        c_ref[...] = jnp.zeros(c_ref.shape, F32)
        n_ref[...] = jnp.zeros(n_ref.shape, F32)
        m_ref[...] = jnp.zeros(m_ref.shape, F32)

    q = q_ref[...]
    qb = q.astype(BF16)
    k = k_ref[...] * DQK_B ** -0.5
    vb = v_ref[...].astype(BF16)
    i_r, f_r = _gate_logs(grow_ref[0:1, :], grow_ref[1:2, :])
    i_c, f_c = _gate_logs(gcol_ref[:, 0:1], gcol_ref[:, 1:2])
    li = lax.broadcasted_iota(jnp.int32, (L, L), 0)
    si = lax.broadcasted_iota(jnp.int32, (L, L), 1)
    tri = si <= li
    bcum_c = jnp.sum(jnp.where(tri, f_r, 0.0), axis=1, keepdims=True)
    bcum_r = jnp.sum(jnp.where(li <= si, f_c, 0.0), axis=0, keepdims=True)
    btot = jnp.sum(f_r, axis=1, keepdims=True)
    c_old, n_old, m_old = c_ref[...], n_ref[...], m_ref[0:1, 0:1]

    dlog = jnp.where(tri, bcum_c - bcum_r + i_r, NEG)
    inter = bcum_c + m_old
    m_t = jnp.maximum(inter, jnp.max(dlog, axis=1, keepdims=True))
    qk = lax.dot_general(qb, k.astype(BF16), (((1,), (1,)), ((), ())), preferred_element_type=F32)
    w_intra = jnp.exp(dlog - m_t) * qk
    w_inter = jnp.exp(inter - m_t)
    num = (w_inter * jnp.dot(qb, c_old.astype(BF16), preferred_element_type=F32)
           + jnp.dot(w_intra.astype(BF16), vb, preferred_element_type=F32))
    den = w_inter * jnp.sum(q * n_old, axis=1, keepdims=True) + jnp.sum(w_intra, axis=1, keepdims=True)
    h = num / jnp.maximum(jnp.abs(den), jnp.exp(-m_t))
    h_ref[...] = (_rms(h, gain_ref[...]) * jax.nn.sigmoid(op_ref[...])).astype(h_ref.dtype)

    wlog_c = btot - bcum_c + i_c
    m_new = jnp.maximum(btot + m_old, jnp.max(btot - bcum_r + i_r, axis=1, keepdims=True))
    decay = jnp.exp(btot + m_old - m_new)
    kw = k * jnp.exp(wlog_c - m_new)
    c_ref[...] = decay * c_old + lax.dot_general(kw.astype(BF16), vb, (((0,), (0,)), ((), ())),
                                                 preferred_element_type=F32)
    n_ref[...] = decay * n_old + jnp.sum(kw, axis=0, keepdims=True)
    m_ref[...] = jnp.broadcast_to(m_new, m_ref.shape)


def _mlstm_prompt(z, gain, B, T):
    L = MLSTM_CHUNK
    assert T % L == 0
    nc, H = T // L, N_HEADS_B
    g = z[:, Z_IB:Z_IB + 2 * H].reshape(B, nc, L, 2, H)
    grow = g.transpose(0, 4, 1, 3, 2).reshape(B * H * nc, 2, L)
    gcol = g.transpose(0, 4, 1, 2, 3).reshape(B * H * nc, L, 2)
    rows = lambda w, off: pl.BlockSpec((L, w), lambda b, h, c: (b * nc + c, off // w + h))
    gidx = lambda b, h, c: ((b * H + h) * nc + c, 0, 0)
    state = lambda *s: pl.BlockSpec((None, None) + s, lambda b, h, c: (b, h, 0, 0))
    h, c, n, m = pl.pallas_call(
        _mlstm_prompt_kernel,
        out_shape=(jax.ShapeDtypeStruct((B * T, B_V), BF16), jax.ShapeDtypeStruct((B, H, DQK_B, DV_B), F32),
                   jax.ShapeDtypeStruct((B, H, 1, DQK_B), F32), jax.ShapeDtypeStruct((B, H, 1, DQK_B), F32)),
        grid=(B, H, nc),
        in_specs=[rows(DQK_B, Z_QB), rows(DQK_B, Z_KB), rows(DV_B, Z_VB), rows(DV_B, Z_OB),
                  pl.BlockSpec((None, 2, L), gidx), pl.BlockSpec((None, L, 2), gidx),
                  pl.BlockSpec((None, 1, DV_B), lambda b, h, c: (h, 0, 0))],
        out_specs=(rows(DV_B, 0), state(DQK_B, DV_B), state(1, DQK_B), state(1, DQK_B)),
        compiler_params=pltpu.CompilerParams(dimension_semantics=("parallel", "parallel", "arbitrary"),
                                             vmem_limit_bytes=VMEM_LIMIT),
        name="mlstm_prompt",
    )(z, z, z, z, grow, gcol, gain.reshape(H, 1, DV_B))
    return h, c, n[:, :, 0], m[:, :, 0, 0]


def _merge_kernel(x_ref, oa_ref, hb_ref, gma_ref, gmb_ref, wa_ref, wb_ref, wo_ref, nf_ref, wrh_ref, wrl_ref, br_ref,
                  *refs):
    x2_ref, xn_ref, route_ref = refs[-3:]
    ua = jnp.dot(oa_ref[...], wa_ref[...], preferred_element_type=F32)
    ub = jnp.dot(hb_ref[...], wb_ref[...], preferred_element_type=F32)
    merged = jax.nn.sigmoid(gma_ref[...]) * ua + jax.nn.sigmoid(gmb_ref[...]) * ub
    x2 = x_ref[...] + jnp.dot(merged.astype(BF16), wo_ref[...], preferred_element_type=F32)
    x2_ref[...] = x2
    xn = _rms(x2, nf_ref[...])
    xn_ref[...] = xn
    hi = xn.astype(BF16)
    lo = (xn - hi.astype(F32)).astype(BF16)
    lg = (jnp.dot(hi, wrh_ref[...], preferred_element_type=F32) + jnp.dot(lo, wrh_ref[...], preferred_element_type=F32)
          + jnp.dot(hi, wrl_ref[...], preferred_element_type=F32)) + br_ref[...]
    lane = lax.broadcasted_iota(jnp.int32, lg.shape, 1)
    lanef = lane.astype(F32)
    ninf = -jnp.inf
    gmask = (lane >= N_EXPERTS) & (lane < N_EXPERTS + N_GROUPS)
    gl = jnp.where(gmask, lg, ninf)
    gmax = jnp.max(gl, axis=1, keepdims=True)
    gidx = jnp.min(jnp.where(gl == gmax, lanef, 1e9), axis=1, keepdims=True) - N_EXPERTS
    gprob = 1.0 / jnp.sum(jnp.where(gmask, jnp.exp(lg - gmax), 0.0), axis=1, keepdims=True)
    e_lo = gidx * EXPERTS_PER_GROUP
    el = jnp.where((lanef >= e_lo) & (lanef < e_lo + EXPERTS_PER_GROUP), lg, ninf)
    v1 = jnp.max(el, axis=1, keepdims=True)
    i1 = jnp.min(jnp.where(el == v1, lanef, 1e9), axis=1, keepdims=True)
    el2 = jnp.where(lanef == i1, ninf, el)
    v2 = jnp.max(el2, axis=1, keepdims=True)
    i2 = jnp.min(jnp.where(el2 == v2, lanef, 1e9), axis=1, keepdims=True)
    e2 = jnp.exp(v2 - v1)
    g1 = gprob / (1.0 + e2)
    g2 = gprob * e2 / (1.0 + e2)
    route_ref[...] = jnp.where(lane == 0, i1, jnp.where(lane == 1, i2, jnp.where(lane == 2, g1,
                               jnp.where(lane == 3, g2, 0.0))))


def _merge_weights(w_up_a, w_up_b, w_out, norm_ffn, w_rg, b_rg, w_re, b_re):
    assert TOP_K == 2
    pad = ROUTE_W - N_EXPERTS - N_GROUPS
    w = jnp.concatenate([w_re, w_rg, jnp.zeros((D_MODEL, pad), F32)], axis=1)
    b = jnp.concatenate([b_re, b_rg, jnp.zeros((pad,), F32)]).reshape(1, ROUTE_W)
    hi = w.astype(BF16)
    return (w_up_a.astype(BF16), w_up_b.astype(BF16), w_out.astype(BF16), norm_ffn.reshape(1, D_MODEL).astype(F32),
            hi, (w - hi.astype(F32)).astype(BF16), b)


def _merge(x2d, o_a, h_b, z, mw, n_total, row0, tm, bufs=None):
    m = x2d.shape[0]
    i0 = row0 // tm
    row = lambda w, c=0: pl.BlockSpec((tm, w), lambda i: (i, c))
    const = lambda a: pl.BlockSpec(a.shape, lambda i: (0,) * a.ndim, pipeline_mode=pl.Buffered(1))
    outw = (D_MODEL, D_MODEL, ROUTE_W)
    in_specs = [row(D_MODEL), row(A_Q), row(B_V), row(D_MODEL, Z_GMA // D_MODEL), row(D_MODEL, Z_GMB // D_MODEL)]
    in_specs += [const(a) for a in mw]
    args = [x2d, o_a, h_b, z, z] + list(mw)
    aliases = {}
    if bufs is not None:
        in_specs += [pl.BlockSpec(memory_space=pl.ANY)] * 3
        aliases = {len(args) + k: k for k in range(3)}
        args += list(bufs)
    return pl.pallas_call(
        _merge_kernel,
        out_shape=tuple(jax.ShapeDtypeStruct((n_total, w), F32) for w in outw),
        grid=(m // tm,),
        in_specs=in_specs,
        out_specs=tuple(pl.BlockSpec((tm, w), lambda i: (i0 + i, 0)) for w in outw),
        input_output_aliases=aliases,
        compiler_params=pltpu.CompilerParams(dimension_semantics=("parallel",), vmem_limit_bytes=VMEM_LIMIT_MERGE),
        name="merge_route",
    )(*args)


def _moe_plan(route, n_tiles):
    n = route.shape[0]
    a = n * TOP_K
    flat_e = route[:, 0:TOP_K].astype(jnp.int32).reshape(a)
    onehot = (flat_e[:, None] == jnp.arange(N_EXPERTS, dtype=jnp.int32)[None, :]).astype(jnp.int32)
    rank = jnp.take_along_axis(jnp.cumsum(onehot, axis=0) - onehot, flat_e[:, None], axis=1)[:, 0]
    counts = jnp.sum(onehot, axis=0)
    ntile = (counts + MOE_TILE - 1) // MOE_TILE
    tile_end = jnp.cumsum(ntile)
    dest = ((tile_end - ntile)[flat_e] * MOE_TILE + rank).astype(jnp.int32)
    n_used = tile_end[-1].astype(jnp.int32)
    t = jnp.minimum(jnp.arange(n_tiles, dtype=jnp.int32), n_used - 1)
    tile_expert = jnp.minimum(jnp.searchsorted(tile_end, t, side='right'), N_EXPERTS - 1).astype(jnp.int32)
    src = jnp.full((n_tiles * MOE_TILE,), -1, jnp.int32).at[dest].set(jnp.arange(a, dtype=jnp.int32) // TOP_K)
    return dest, src, tile_expert, n_used.reshape(1)


def _row_gather_kernel(src_ref, nu_ref, x_hbm, o_ref, sem):
    t = pl.program_id(0)
    rows = o_ref.shape[0]

    def copy(r):
        s = src_ref[t * rows + r]
        return s, pltpu.make_async_copy(x_hbm.at[pl.ds(jnp.maximum(s, 0), 1)], o_ref.at[pl.ds(r, 1)], sem)

    @pl.when(t < nu_ref[0])
    def _():
        def start(r, c):
            s, cp = copy(r)

            @pl.when(s >= 0)
            def _():
                cp.start()
            return c

        def wait(r, c):
            s, cp = copy(r)

            @pl.when(s >= 0)
            def _():
                cp.wait()
            return c

        lax.fori_loop(0, rows, start, 0)
        lax.fori_loop(0, rows, wait, 0)


def _moe_gather(xn, src, n_used, n_tiles):
    d = xn.shape[1]
    return pl.pallas_call(
        _row_gather_kernel,
        out_shape=jax.ShapeDtypeStruct((n_tiles * MOE_TILE, d), xn.dtype),
        grid_spec=pltpu.PrefetchScalarGridSpec(
            num_scalar_prefetch=2, grid=(n_tiles,),
            in_specs=[pl.BlockSpec(memory_space=pl.ANY)],
            out_specs=pl.BlockSpec((MOE_TILE, d), lambda t, src, nu: (jnp.minimum(t, nu[0] - 1), 0)),
            scratch_shapes=[pltpu.SemaphoreType.DMA(())]),
        compiler_params=pltpu.CompilerParams(dimension_semantics=("arbitrary",), vmem_limit_bytes=VMEM_LIMIT),
        name="moe_gather",
    )(src, n_used, xn)


def _expert_kernel(te_ref, nu_ref, x_ref, wg_ref, wu_ref, wd_ref, o_ref, xb_ref):
    t, j = pl.program_id(0), pl.program_id(1)

    @pl.when(t < nu_ref[0])
    def _():
        @pl.when(j == 0)
        def _():
            xb_ref[...] = x_ref[...].astype(BF16)

        xb = xb_ref[...]
        g = jnp.dot(xb, wg_ref[...].astype(BF16), preferred_element_type=F32)
        u = jnp.dot(xb, wu_ref[...].astype(BF16), preferred_element_type=F32)
        h = (g * jax.nn.sigmoid(g) * u).astype(BF16)
        y = jnp.dot(h, wd_ref[...].astype(BF16), preferred_element_type=F32)

        @pl.when(j == 0)
        def _():
            o_ref[...] = y

        @pl.when(j > 0)
        def _():
            o_ref[...] += y


def _moe_experts(xs, tile_expert, n_used, w_gate, w_up, w_down):
    n_tiles = xs.shape[0] // MOE_TILE
    nj = D_EXPERT // MOE_FC
    tix = lambda t, j, te, nu: (jnp.minimum(t, nu[0] - 1), 0)
    jj = lambda t, j, nu: jnp.where(t < nu[0], j, nj - 1)
    return pl.pallas_call(
        _expert_kernel,
        out_shape=jax.ShapeDtypeStruct(xs.shape, F32),
        grid_spec=pltpu.PrefetchScalarGridSpec(
            num_scalar_prefetch=2, grid=(n_tiles, nj),
            in_specs=[pl.BlockSpec((MOE_TILE, D_MODEL), tix),
                      pl.BlockSpec((None, D_MODEL, MOE_FC), lambda t, j, te, nu: (te[t], 0, jj(t, j, nu))),
                      pl.BlockSpec((None, D_MODEL, MOE_FC), lambda t, j, te, nu: (te[t], 0, jj(t, j, nu))),
                      pl.BlockSpec((None, MOE_FC, D_MODEL), lambda t, j, te, nu: (te[t], jj(t, j, nu), 0))],
            out_specs=pl.BlockSpec((MOE_TILE, D_MODEL), tix),
            scratch_shapes=[pltpu.VMEM((MOE_TILE, D_MODEL), BF16)]),
        compiler_params=pltpu.CompilerParams(dimension_semantics=("arbitrary", "arbitrary"),
                                             vmem_limit_bytes=VMEM_LIMIT),
        name="moe_experts",
    )(tile_expert, n_used, xs, w_gate, w_up, w_down)


def _combine_kernel(dest_ref, x2_ref, route_ref, g_ref, y_hbm, op_ref, os_ref, buf_ref, sem, *, n_prompt_tiles):
    i = pl.program_id(0)
    tm = x2_ref.shape[0]

    def copy(r, k):
        d = dest_ref[(i * tm + r) * TOP_K + k]
        return pltpu.make_async_copy(y_hbm.at[pl.ds(d, 1)], buf_ref.at[k, pl.ds(r, 1)], sem)

    def start(r, c):
        for k in range(TOP_K):
            copy(r, k).start()
        return c

    def wait(r, c):
        for k in range(TOP_K):
            copy(r, k).wait()
        return c

    lax.fori_loop(0, tm, start, 0)
    lax.fori_loop(0, tm, wait, 0)
    ffn = buf_ref[0] * route_ref[:, TOP_K:TOP_K + 1]
    for k in range(1, TOP_K):
        ffn = ffn + buf_ref[k] * route_ref[:, TOP_K + k:TOP_K + k + 1]
    y = _rms(x2_ref[...] + ffn, g_ref[...])

    @pl.when(i < n_prompt_tiles)
    def _():
        op_ref[...] = y

    @pl.when(i >= n_prompt_tiles)
    def _():
        os_ref[...] = y


def _moe_combine(x2, route, dest, ys, norm_final, n_prompt, tm=128):
    n, d = x2.shape
    npt = n_prompt // tm
    nst = (n - n_prompt) // tm
    return pl.pallas_call(
        functools.partial(_combine_kernel, n_prompt_tiles=npt),
        out_shape=(jax.ShapeDtypeStruct((n_prompt, d), F32), jax.ShapeDtypeStruct((n - n_prompt, d), F32)),
        grid_spec=pltpu.PrefetchScalarGridSpec(
            num_scalar_prefetch=1, grid=(n // tm,),
            in_specs=[pl.BlockSpec((tm, d), lambda i, ds: (i, 0)), pl.BlockSpec((tm, ROUTE_W), lambda i, ds: (i, 0)),
                      pl.BlockSpec((1, d), lambda i, ds: (0, 0)), pl.BlockSpec(memory_space=pl.ANY)],
            out_specs=(pl.BlockSpec((tm, d), lambda i, ds: (jnp.minimum(i, npt - 1), 0)),
                       pl.BlockSpec((tm, d), lambda i, ds: (jnp.clip(i - npt, 0, nst - 1), 0))),
            scratch_shapes=[pltpu.VMEM((TOP_K, tm, d), F32), pltpu.SemaphoreType.DMA(())]),
        compiler_params=pltpu.CompilerParams(dimension_semantics=("arbitrary",), vmem_limit_bytes=VMEM_LIMIT),
        name="moe_combine",
    )(dest, x2, route, norm_final.reshape(1, d), ys)


def _moe(x2, xn, route, w_gate, w_up, w_down, norm_final, n_prompt):
    n = x2.shape[0]
    n_tiles = (n * TOP_K + N_EXPERTS * (MOE_TILE - 1) + MOE_TILE - 1) // MOE_TILE
    dest, src, tile_expert, n_used = _moe_plan(route, n_tiles)
    xs = _moe_gather(xn, src, n_used, n_tiles)
    ys = _moe_experts(xs, tile_expert, n_used, w_gate, w_up, w_down)
    return _moe_combine(x2, route, dest, ys, norm_final, n_prompt)


def masked_softmax(s, mask):
    s = jnp.where(mask, s, NEG)
    p = jnp.where(mask, jnp.exp(s - jnp.max(s, axis=-1, keepdims=True)), 0.0)
    return p / jnp.maximum(jnp.sum(p, axis=-1, keepdims=True), 1e-30)


def compress_blocks(x, pos_emb, w1, w2):
    B, T = x.shape[:2]
    n_cmp = (T - CMP_LEN) // CMP_STRIDE + 1
    idx = jnp.arange(n_cmp)[:, None] * CMP_STRIDE + jnp.arange(CMP_LEN)[None, :]
    blk = x[:, idx] + pos_emb[None, None, :, None, :].astype(x.dtype)
    blk = jnp.moveaxis(blk, 3, 2).reshape(B, n_cmp, N_KV, CMP_LEN * HEAD_DIM)
    return jax.nn.silu(blk @ w1) @ w2


def nsa_block(q_raw, q_rot, q_pos, kc, vc, c_end, cover, ks_b, vs_b):
    B, Q = q_raw.shape[:2]
    scale = HEAD_DIM ** -0.5
    s = jnp.einsum('bqhgd,bchd->bqhgc', q_raw, kc).astype(jnp.float32) * scale
    p = masked_softmax(s, (c_end[None, :] <= q_pos[:, None])[None, :, None, None, :])
    o_cmp = jnp.einsum('bqhgc,bchd->bqhgd', p.astype(vc.dtype), vc)
    imp = jnp.einsum('bqhgc,jc->bqhj', p, cover)
    n_sel = cover.shape[0]
    blk = jnp.arange(n_sel, dtype=jnp.int32)[None, :]
    cur = (q_pos // SEL_LEN)[:, None]
    forced = ((blk == 0) | (blk == cur) | (blk == cur - 1)).astype(jnp.float32)
    valid = blk * SEL_LEN <= q_pos[:, None]
    score = jnp.where(valid[None, :, None, :], imp + FORCE_BONUS * forced[None, :, None, :], -1.0)
    _, idx = lax.top_k(score, min(SEL_TOP, n_sel))
    n_top = idx.shape[-1]
    bi = jnp.arange(B)[:, None, None, None]
    hi = jnp.arange(N_KV)[None, None, :, None]
    kg = ks_b[bi, hi, idx].reshape(B, Q, N_KV, n_top * SEL_LEN, HEAD_DIM)
    vg = vs_b[bi, hi, idx].reshape(B, Q, N_KV, n_top * SEL_LEN, HEAD_DIM)
    kpos = (idx[..., None] * SEL_LEN + jnp.arange(SEL_LEN)).reshape(B, Q, N_KV, 1, n_top * SEL_LEN)
    s2 = jnp.einsum('bqhgd,bqhkd->bqhgk', q_rot, kg).astype(jnp.float32) * scale
    p2 = masked_softmax(s2, kpos <= q_pos[None, :, None, None, None])
    o_sel = jnp.einsum('bqhgk,bqhkd->bqhgd', p2.astype(vg.dtype), vg)
    return o_cmp, o_sel


def nsa_cmp_sel(q_raw, q_rot, q_pos, k_cmp, v_cmp, k_sel, v_sel, cmp_pos, cmp_w1, cmp_w2):
    B, Tk = k_cmp.shape[:2]
    kc = compress_blocks(k_cmp, cmp_pos[0], cmp_w1[0], cmp_w2[0])
    vc = compress_blocks(v_cmp, cmp_pos[1], cmp_w1[1], cmp_w2[1])
    c_start = jnp.arange(kc.shape[1], dtype=jnp.int32) * CMP_STRIDE
    c_end = c_start + CMP_LEN - 1
    n_sel = -(-Tk // SEL_LEN)
    s_start = jnp.arange(n_sel, dtype=jnp.int32)[:, None] * SEL_LEN
    cover = ((c_start[None, :] < s_start + SEL_LEN) & (c_end[None, :] >= s_start)).astype(jnp.float32)
    pad = n_sel * SEL_LEN - Tk

    def to_blocks(t):
        t = jnp.pad(t, ((0, 0), (0, pad), (0, 0), (0, 0)))
        return t.reshape(B, n_sel, SEL_LEN, N_KV, HEAD_DIM).transpose(0, 3, 1, 2, 4)

    ks_b, vs_b = to_blocks(k_sel), to_blocks(v_sel)
    return nsa_block(q_raw, q_rot, q_pos, kc, vc, c_end, cover, ks_b, vs_b)


def window_attend_direct(q, q_pos, k, v, k_pos):
    s = jnp.einsum('bqhgd,bkhd->bqhgk', q, k).astype(jnp.float32) * HEAD_DIM ** -0.5
    mask = (k_pos[None, :] <= q_pos[:, None]) & (k_pos[None, :] >= q_pos[:, None] - WINDOW)
    p = masked_softmax(s, mask[None, :, None, None, :])
    return jnp.einsum('bqhgk,bkhd->bqhgd', p.astype(v.dtype), v)


def combine_nsa(g_pre, o_cmp, o_sel, o_win):
    B, T = g_pre.shape[:2]
    g = jax.nn.sigmoid(g_pre.astype(jnp.float32)).reshape(B, T, N_KV, GROUP, 3)
    o = (g[..., 0:1] * o_cmp.astype(jnp.float32) + g[..., 1:2] * o_sel.astype(jnp.float32)
         + g[..., 2:3] * o_win.astype(jnp.float32))
    return o.reshape(B, T, A_Q).astype(g_pre.dtype)


def mlstm_step(q, k, v, i_pre, f_pre, o_pre, head_gain, c, n, m):
    log_i, log_f = _gate_logs(i_pre, f_pre)
    k = k * DQK_B ** -0.5
    m_new = jnp.maximum(log_f + m, log_i)
    decay = jnp.exp(log_f + m - m_new)
    w = jnp.exp(log_i - m_new)
    c_new = decay[..., None, None] * c + jnp.einsum('bh,bhk,bhv->bhkv', w, k, v)
    n_new = decay[..., None] * n + w[..., None] * k
    inter = log_f + m
    m_t = jnp.maximum(inter, log_i)
    w_intra = jnp.exp(log_i - m_t) * jnp.einsum('bhd,bhd->bh', q, k)
    w_inter = jnp.exp(inter - m_t)
    num = w_inter[..., None] * jnp.einsum('bhd,bhdv->bhv', q, c) + w_intra[..., None] * v
    den = w_inter * jnp.einsum('bhd,bhd->bh', q, n) + w_intra
    h = num / jnp.maximum(jnp.abs(den), jnp.exp(-m_t))[..., None]
    h = _rms(h, head_gain.reshape(N_HEADS_B, DV_B))
    out = h.reshape(h.shape[0], B_V) * jax.nn.sigmoid(o_pre)
    return out, c_new, n_new, m_new


def _split_kv(z, B, T):
    z = z.reshape(B, T, NZ)
    kv = lambda o: z[..., o:o + A_KV].reshape(B, T, N_KV, HEAD_DIM)
    return kv(Z_KC), kv(Z_VC), kv(Z_KS), kv(Z_VS), kv(Z_KW), kv(Z_VW)


def kernel(x_prompt, x_sample, cache_cmp_k, cache_cmp_v, cache_sel_k, cache_sel_v, cache_win_k, cache_win_v,
           state_mlstm_c, state_mlstm_n, state_mlstm_m, page_table, norm_mix, w_in, b_in, cmp_pos, cmp_w1, cmp_w2,
           mlstm_norm, w_up_a, w_up_b, w_out, norm_ffn, w_router_grp, b_router_grp, w_router_exp, b_router_exp,
           w_gate, w_up, w_down, norm_final):
    assert w_in.shape[0] == 1, "single layer"
    Bp, Tp = x_prompt.shape[:2]
    Bs, Ts = x_sample.shape[:2]
    assert Ts == 1
    n_p, n_s = Bp * Tp, Bs * Ts
    past_len = page_table.shape[1] * PAGE_SIZE
    pos_p = jnp.arange(Tp, dtype=jnp.int32)
    pos_s = past_len + jnp.arange(Ts, dtype=jnp.int32)
    wb = cache_win_k.shape[2]
    win_pos = past_len - wb + jnp.arange(wb + Ts, dtype=jnp.int32)
    l = 0

    w_perm, b_perm = _permute_in_proj(w_in[l], b_in[l])
    cw = _compress_weights(cmp_pos[l], cmp_w1[l], cmp_w2[l])
    mw = _merge_weights(w_up_a[l], w_up_b[l], w_out[l], norm_ffn[l], w_router_grp[l], b_router_grp[l],
                        w_router_exp[l], b_router_exp[l])

    xp2d = x_prompt.reshape(n_p, D_MODEL)
    zp = _project(xp2d, jnp.tile(pos_p, Bp), norm_mix[l], w_perm, b_perm, 512)
    kc, vc, ks, vs, kw, vw = _split_kv(zp, Bp, Tp)
    o_a = _nsa_prompt(zp, _compress_prompt(kc, vc, cw), Bp, Tp)
    h_b, c_p, n_p_state, m_p = _mlstm_prompt(zp, mlstm_norm[l], Bp, Tp)
    bufs = _merge(xp2d, o_a, h_b, zp, mw, n_p + n_s, 0, 256)
    wl = min(WINDOW, Tp)
    prompt_new = (kc, vc, ks, vs, kw[:, -wl:], vw[:, -wl:], c_p, n_p_state, m_p)

    xs2d = x_sample.reshape(n_s, D_MODEL)
    zs = _project(xs2d, jnp.repeat(pos_s, Bs), norm_mix[l], w_perm, b_perm, 128)
    kc, vc, ks, vs, kw, vw = _split_kv(zs, Bs, Ts)
    qa = zs[:, Z_QRAW:Z_QRAW + A_Q].reshape(Bs, Ts, N_KV, GROUP, HEAD_DIM)
    qr = zs[:, Z_QROT:Z_QROT + A_Q].reshape(Bs, Ts, N_KV, GROUP, HEAD_DIM)
    gn = jnp.concatenate([zs[:, Z_GNSA:Z_GNSA + N_GATE], zs[:, Z_GNSA + HEAD_DIM:Z_GNSA + HEAD_DIM + N_GATE]],
                         axis=-1).reshape(Bs, Ts, 2 * N_GATE)

    def gather_pages(pool):
        return pool[page_table].reshape(Bs, past_len, N_KV, HEAD_DIM)

    kc_all = jnp.concatenate([gather_pages(cache_cmp_k[l]), kc], axis=1)
    vc_all = jnp.concatenate([gather_pages(cache_cmp_v[l]), vc], axis=1)
    ks_all = jnp.concatenate([gather_pages(cache_sel_k[l]), ks], axis=1)
    vs_all = jnp.concatenate([gather_pages(cache_sel_v[l]), vs], axis=1)
    o_cmp, o_sel = nsa_cmp_sel(qa, qr, pos_s, kc_all, vc_all, ks_all, vs_all, cmp_pos[l], cmp_w1[l], cmp_w2[l])
    wk = jnp.concatenate([cache_win_k[l], kw], axis=1)
    wv = jnp.concatenate([cache_win_v[l], vw], axis=1)
    o_win = window_attend_direct(qr, pos_s, wk, wv, win_pos)
    o_a_s = combine_nsa(gn, o_cmp, o_sel, o_win).reshape(n_s, A_Q)
    h_s, c_s, n_s_state, m_s = mlstm_step(
        zs[:, Z_QB:Z_QB + B_QK].reshape(Bs, N_HEADS_B, DQK_B), zs[:, Z_KB:Z_KB + B_QK].reshape(Bs, N_HEADS_B, DQK_B),
        zs[:, Z_VB:Z_VB + B_V].reshape(Bs, N_HEADS_B, DV_B), zs[:, Z_IB:Z_IB + N_HEADS_B],
        zs[:, Z_FB:Z_FB + N_HEADS_B], zs[:, Z_OB:Z_OB + B_V], mlstm_norm[l],
        state_mlstm_c[l], state_mlstm_n[l], state_mlstm_m[l])
    x2, xn, route = _merge(xs2d, o_a_s.astype(BF16), h_s.astype(BF16), zs, mw, n_p + n_s, n_p, 128, bufs=bufs)
    wl = min(WINDOW, past_len + Ts)
    sample_new = (kc, vc, ks, vs, wk[:, -wl:], wv[:, -wl:], c_s, n_s_state, m_s)

    y_p, y_s = _moe(x2, xn, route, w_gate[l], w_up[l], w_down[l], norm_final, n_p)
    return ((y_p.reshape(Bp, Tp, D_MODEL), y_s.reshape(Bs, Ts, D_MODEL))
            + tuple(a[None] for a in prompt_new) + tuple(a[None] for a in sample_new))
```

```python
import functools

import numpy as np
import jax
import jax.numpy as jnp
from jax import lax
from jax.experimental import pallas as pl
from jax.experimental.pallas import tpu as pltpu

D_MODEL = 2048
PAGE_SIZE = 128
N_HEADS_A = 8
N_KV = 2
GROUP = N_HEADS_A // N_KV
HEAD_DIM = 128
ROT_DIM = HEAD_DIM // 4
ROPE_THETA = 500000.0
CMP_LEN = 32
CMP_STRIDE = 16
SEL_LEN = 64
SEL_TOP = 16
FORCE_BONUS = 100.0
WINDOW = 512
Q_BLOCK = 128
N_HEADS_B = 4
DQK_B = 128
DV_B = 256
MLSTM_CHUNK = 64
GATE_CAP = 15.0
N_GROUPS = 4
EXPERTS_PER_GROUP = 8
N_EXPERTS = N_GROUPS * EXPERTS_PER_GROUP
TOP_K = 2
D_EXPERT = 1024
EPS = 1e-6
NEG = -1e30

A_Q = N_HEADS_A * HEAD_DIM
A_KV = N_KV * HEAD_DIM
B_QK = N_HEADS_B * DQK_B
B_V = N_HEADS_B * DV_B
IN_SPLITS = (A_Q, A_KV, A_KV, A_KV, A_KV, A_KV, A_KV, 3 * N_HEADS_A,
             B_QK, B_QK, B_V, N_HEADS_B, N_HEADS_B, B_V, D_MODEL, D_MODEL)

F32 = jnp.float32
BF16 = jnp.bfloat16
VMEM_LIMIT = 48 * 1024 * 1024
VMEM_LIMIT_MERGE = 56 * 1024 * 1024

_CUTS = np.concatenate([[0], np.cumsum(IN_SPLITS)]).tolist()
_SRC = dict(zip(('q_a', 'k_c', 'v_c', 'k_s', 'v_s', 'k_w', 'v_w', 'g_nsa', 'q_b', 'k_b', 'v_b', 'i_b', 'f_b',
                 'o_b', 'g_ma', 'g_mb'), zip(_CUTS[:-1], _CUTS[1:])))
_Z_ORDER = ('g_ma', 'g_mb', 'q_a', 'q_a', 'v_b', 'o_b', 'k_c', 'v_c', 'k_s', 'v_s', 'k_w', 'v_w', 'q_b', 'k_b')
PROJ_TN = 512
Z_GMA, Z_GMB, Z_QRAW, Z_QROT, Z_VB, Z_OB = 0, 2048, 4096, 5120, 6144, 7168
Z_KC, Z_VC, Z_KS, Z_VS, Z_KW, Z_VW = 8192, 8448, 8704, 8960, 9216, 9472
Z_QB, Z_KB, Z_SMALL = 9728, 10240, 10752
N_GATE = 3 * GROUP
Z_GNSA, Z_IB, Z_FB = Z_SMALL, Z_SMALL + 2 * HEAD_DIM, Z_SMALL + 2 * HEAD_DIM + N_HEADS_B
NZ = 11264
_ROPE_ALL_TILES = (Z_QROT // PROJ_TN, Z_QROT // PROJ_TN + 1)
_ROPE_HALF_TILES = (Z_KS // PROJ_TN, Z_KW // PROJ_TN)

ROUTE_W = 128
MOE_TILE = 256
MOE_FC = 256


def _rms(x, g):
    return x * lax.rsqrt(jnp.mean(x * x, axis=-1, keepdims=True) + EPS) * g


def _proj_kernel(x_ref, g_ref, w_ref, b_ref, cos_ref, sa_ref, sb_ref, z_ref, xn_ref):
    j = pl.program_id(1)

    @pl.when(j == 0)
    def _():
        xn_ref[...] = _rms(x_ref[...], g_ref[...]).astype(BF16)

    acc = jnp.dot(xn_ref[...], w_ref[...], preferred_element_type=F32) + b_ref[...]

    def rope(blk):
        return (blk * cos_ref[...] + pltpu.roll(blk, HEAD_DIM - ROT_DIM // 2, 1) * sa_ref[...]
                + pltpu.roll(blk, ROT_DIM // 2, 1) * sb_ref[...])

    is_all = (j == _ROPE_ALL_TILES[0]) | (j == _ROPE_ALL_TILES[1])
    is_half = (j == _ROPE_HALF_TILES[0]) | (j == _ROPE_HALF_TILES[1])
    n_blk = PROJ_TN // HEAD_DIM

    @pl.when(is_all)
    def _():
        for c in range(n_blk):
            z_ref[:, c * HEAD_DIM:(c + 1) * HEAD_DIM] = rope(acc[:, c * HEAD_DIM:(c + 1) * HEAD_DIM])

    @pl.when(is_half)
    def _():
        for c in range(n_blk // 2):
            z_ref[:, c * HEAD_DIM:(c + 1) * HEAD_DIM] = rope(acc[:, c * HEAD_DIM:(c + 1) * HEAD_DIM])
        z_ref[:, PROJ_TN // 2:] = acc[:, PROJ_TN // 2:]

    @pl.when(jnp.logical_not(is_all | is_half))
    def _():
        z_ref[...] = acc


def _rope_tables(pos):
    half = ROT_DIM // 2
    inv = ROPE_THETA ** (-jnp.arange(half, dtype=F32) / half)
    ang = pos.astype(F32)[:, None] * inv[None, :]
    cos, sin = jnp.cos(ang), jnp.sin(ang)
    n = pos.shape[0]
    ones = jnp.ones((n, HEAD_DIM - ROT_DIM), F32)
    zeros = jnp.zeros((n, HEAD_DIM - half), F32)
    cos_t = jnp.concatenate([cos, cos, ones], axis=1)
    sa_t = jnp.concatenate([-sin, zeros], axis=1)
    sb_t = jnp.concatenate([jnp.zeros((n, half), F32), sin, jnp.zeros((n, HEAD_DIM - ROT_DIM), F32)], axis=1)
    return cos_t, sa_t, sb_t


def _permute_in_proj(w_in, b_in):
    g0 = _SRC['g_nsa'][0]
    spans = [_SRC[k] for k in _Z_ORDER]
    spans += [(g0, g0 + N_GATE), HEAD_DIM - N_GATE, (g0 + N_GATE, g0 + 2 * N_GATE), HEAD_DIM - N_GATE,
              _SRC['i_b'], _SRC['f_b'], NZ - Z_FB - N_HEADS_B]
    wb = jnp.concatenate([w_in, b_in[None, :]], axis=0)
    parts = [jnp.zeros((D_MODEL + 1, s), wb.dtype) if isinstance(s, int) else wb[:, s[0]:s[1]] for s in spans]
    wb = jnp.concatenate(parts, axis=1)
    return wb[:D_MODEL].astype(BF16), wb[D_MODEL:].astype(F32)


def _project(x2d, pos_rows, norm_g, w_perm, b_perm, tm):
    m = x2d.shape[0]
    cos_t, sa_t, sb_t = _rope_tables(pos_rows)
    row = lambda i, j: (i, 0)
    return pl.pallas_call(
        _proj_kernel,
        out_shape=jax.ShapeDtypeStruct((m, NZ), F32),
        grid=(m // tm, NZ // PROJ_TN),
        in_specs=[pl.BlockSpec((tm, D_MODEL), row),
                  pl.BlockSpec((1, D_MODEL), lambda i, j: (0, 0)),
                  pl.BlockSpec((D_MODEL, PROJ_TN), lambda i, j: (0, j)),
                  pl.BlockSpec((1, PROJ_TN), lambda i, j: (0, j)),
                  pl.BlockSpec((tm, HEAD_DIM), row),
                  pl.BlockSpec((tm, HEAD_DIM), row),
                  pl.BlockSpec((tm, HEAD_DIM), row)],
        out_specs=pl.BlockSpec((tm, PROJ_TN), lambda i, j: (i, j)),
        scratch_shapes=[pltpu.VMEM((tm, D_MODEL), BF16)],
        compiler_params=pltpu.CompilerParams(dimension_semantics=("parallel", "arbitrary"),
                                             vmem_limit_bytes=VMEM_LIMIT),
        name="proj",
    )(x2d, norm_g.reshape(1, D_MODEL), w_perm, b_perm, cos_t, sa_t, sb_t)


def _compress_body(x_ref, w1, pb, w2, r):
    bias = pb[0:1, :HEAD_DIM] + pb[1:2, HEAD_DIM:]
    outs = []
    for h in range(N_KV):
        xh = jnp.concatenate([x_ref[:, (N_KV * s + h) * HEAD_DIM:(N_KV * s + h + 1) * HEAD_DIM]
                              for s in range(CMP_STRIDE)], axis=1).astype(BF16)
        zz = jnp.dot(xh, w1, preferred_element_type=F32)
        pre = zz[:, :HEAD_DIM] + pltpu.roll(zz[:, HEAD_DIM:], r - 1, 0) + bias
        mid = pre * jax.nn.sigmoid(pre)
        outs.append(jnp.dot(mid.astype(BF16), w2, preferred_element_type=F32))
    return outs


def _compress_kernel(xk_ref, xv_ref, w1_ref, p_ref, w2_ref, o_ref):
    r = xk_ref.shape[0]
    for which, x_ref in enumerate((xk_ref, xv_ref)):
        w1 = w1_ref[which]
        pb = jnp.dot(p_ref[which], w1, preferred_element_type=F32)
        for h, o in enumerate(_compress_body(x_ref, w1, pb, w2_ref[which], r)):
            o_ref[which, h] = o


def _compress_weights(cmp_pos, cmp_w1, cmp_w2):
    assert CMP_LEN == 2 * CMP_STRIDE
    half = CMP_STRIDE * HEAD_DIM
    w1 = jnp.concatenate([cmp_w1[:, :half], cmp_w1[:, half:]], axis=2).astype(BF16)
    p = cmp_pos.reshape(2, 2, half)
    p = jnp.concatenate([p, jnp.zeros((2, 6, half), p.dtype)], axis=1).astype(BF16)
    return w1, p, cmp_w2.astype(BF16)


def _compress_prompt(kc, vc, cw):
    B, T = kc.shape[:2]
    r = T // CMP_STRIDE
    width = CMP_STRIDE * N_KV * HEAD_DIM
    w1, p, w2 = cw
    full = lambda a: pl.BlockSpec(a.shape, lambda b: (0,) * a.ndim)
    xspec = pl.BlockSpec((None, r, width), lambda b: (b, 0, 0))
    return pl.pallas_call(
        _compress_kernel,
        out_shape=jax.ShapeDtypeStruct((B, 2, N_KV, r, HEAD_DIM), F32),
        grid=(B,),
        in_specs=[xspec, xspec, full(w1), full(p), full(w2)],
        out_specs=pl.BlockSpec((None, 2, N_KV, r, HEAD_DIM), lambda b: (b, 0, 0, 0, 0)),
        compiler_params=pltpu.CompilerParams(dimension_semantics=("parallel",), vmem_limit_bytes=VMEM_LIMIT),
        name="compress_prompt",
    )(kc.reshape(B, r, width), vc.reshape(B, r, width), w1, p, w2)


def _group_scores(q, k, tq):
    s = lax.dot_general(q, k, (((1,), (1,)), ((), ())), preferred_element_type=F32) * HEAD_DIM ** -0.5
    return s.reshape(GROUP, tq, k.shape[0])


def _stack_heads(ref):
    return jnp.concatenate([ref[:, g * HEAD_DIM:(g + 1) * HEAD_DIM] for g in range(GROUP)], axis=0).astype(BF16)


def _select_blocks(psum, qpos, n_sel):
    tq, r = psum.shape
    ci = lax.broadcasted_iota(jnp.int32, (r, n_sel), 0) * CMP_STRIDE
    cj = lax.broadcasted_iota(jnp.int32, (r, n_sel), 1) * SEL_LEN
    cover_t = ((ci < cj + SEL_LEN) & (ci + (CMP_LEN - 1) >= cj)).astype(BF16)
    p_hi = psum.astype(BF16)
    p_lo = (psum - p_hi.astype(F32)).astype(BF16)
    imp = (jnp.dot(p_hi, cover_t, preferred_element_type=F32) + jnp.dot(p_lo, cover_t, preferred_element_type=F32))
    blk = lax.broadcasted_iota(jnp.int32, (tq, n_sel), 1)
    cur = jnp.right_shift(qpos, SEL_LEN.bit_length() - 1)
    forced = (blk == 0) | (blk == cur) | (blk == cur - 1)
    valid = blk * SEL_LEN <= qpos
    score = jnp.where(valid, imp + jnp.where(forced, FORCE_BONUS, 0.0), -1.0)
    blkf = blk.astype(F32)
    sel = jnp.zeros((tq, n_sel), F32)
    for _ in range(min(SEL_TOP, n_sel)):
        m = jnp.max(score, axis=-1, keepdims=True)
        first = jnp.min(jnp.where(score == m, blkf, float(n_sel)), axis=-1, keepdims=True)
        pick = blkf == first
        sel = jnp.where(pick, 1.0, sel)
        score = jnp.where(pick, -3.0, score)
    return sel


def _nsa_prompt_kernel(qraw_ref, qrot_ref, ks_ref, vs_ref, kw_ref, vw_ref, kc_ref, vc_ref, gate_ref, o_ref,
                       m_ref, l_ref, acc_ref, *, tq, tk, seq_len):
    q0 = pl.program_id(2) * tq
    r = kc_ref.shape[0]
    n_sel = seq_len // SEL_LEN
    sel_shift = SEL_LEN.bit_length() - 1
    qpos = q0 + lax.broadcasted_iota(jnp.int32, (tq, 1), 0)

    s = _group_scores(_stack_heads(qraw_ref), kc_ref[...].astype(BF16), tq)
    c_end = lax.broadcasted_iota(jnp.int32, (tq, r), 1) * CMP_STRIDE + (CMP_LEN - 1)
    cmask = c_end <= qpos
    s = s + jnp.where(cmask, 0.0, NEG)[None]
    p = jnp.exp(s - jnp.max(s, axis=-1, keepdims=True)) * jnp.where(cmask, 1.0, 0.0)[None]
    p = p / jnp.maximum(jnp.sum(p, axis=-1, keepdims=True), 1e-30)
    o_cmp = jnp.dot(p.reshape(GROUP * tq, r).astype(BF16), vc_ref[...].astype(BF16), preferred_element_type=F32)
    psum = p[0]
    for g in range(1, GROUP):
        psum = psum + p[g]
    sel_b = _select_blocks(psum, qpos, n_sel).astype(BF16)

    q_rot = _stack_heads(qrot_ref)
    m_ref[...] = jnp.full(m_ref.shape, NEG, F32)
    l_ref[...] = jnp.zeros(l_ref.shape, F32)
    acc_ref[...] = jnp.zeros(acc_ref.shape, F32)

    def sel_tile(kt, carry):
        k0 = pl.multiple_of(kt * tk, tk)
        k = ks_ref[pl.ds(k0, tk), :].astype(BF16)
        v = vs_ref[pl.ds(k0, tk), :].astype(BF16)
        s2 = _group_scores(q_rot, k, tq)
        ej = lax.broadcasted_iota(jnp.int32, (n_sel, tk), 0)
        et = k0 + lax.broadcasted_iota(jnp.int32, (n_sel, tk), 1)
        expand = jnp.where(jnp.right_shift(et, sel_shift) == ej, 1.0, 0.0).astype(BF16)
        selm = jnp.dot(sel_b, expand, preferred_element_type=F32)
        kpos = k0 + lax.broadcasted_iota(jnp.int32, (tq, tk), 1)
        ok = (selm > 0.5) & (kpos <= qpos)
        s2 = s2 + jnp.where(ok, 0.0, NEG)[None]
        m_prev = m_ref[...]
        m_new = jnp.maximum(m_prev, jnp.max(s2, axis=-1, keepdims=True))
        alpha = jnp.exp(m_prev - m_new)
        p2 = jnp.exp(s2 - m_new)
        l_ref[...] = alpha * l_ref[...] + jnp.sum(p2, axis=-1, keepdims=True)
        pv = jnp.dot(p2.reshape(GROUP * tq, tk).astype(BF16), v, preferred_element_type=F32)
        acc_ref[...] = alpha.reshape(GROUP * tq, 1) * acc_ref[...] + pv
        m_ref[...] = m_new
        return carry

    lax.fori_loop(0, lax.div(q0 + tq + tk - 1, tk), sel_tile, 0)
    o_sel = acc_ref[...] / jnp.maximum(l_ref[...].reshape(GROUP * tq, 1), 1e-30)

    span = WINDOW + tq
    w0 = pl.multiple_of(jnp.maximum(q0 - WINDOW, 0), tq)
    s3 = _group_scores(q_rot, kw_ref[pl.ds(w0, span), :].astype(BF16), tq)
    kp = w0 + lax.broadcasted_iota(jnp.int32, (tq, span), 1)
    okw = (kp <= qpos) & (kp >= qpos - WINDOW)
    s3 = s3 + jnp.where(okw, 0.0, NEG)[None]
    p3 = jnp.exp(s3 - jnp.max(s3, axis=-1, keepdims=True))
    p3 = p3 / jnp.sum(p3, axis=-1, keepdims=True)
    o_win = jnp.dot(p3.reshape(GROUP * tq, span).astype(BF16), vw_ref[pl.ds(w0, span), :].astype(BF16),
                    preferred_element_type=F32)

    gts = jax.nn.sigmoid(gate_ref[...])
    for g in range(GROUP):
        rows = slice(g * tq, (g + 1) * tq)
        o_ref[:, g * HEAD_DIM:(g + 1) * HEAD_DIM] = (gts[:, 3 * g:3 * g + 1] * o_cmp[rows]
                                                     + gts[:, 3 * g + 1:3 * g + 2] * o_sel[rows]
                                                     + gts[:, 3 * g + 2:3 * g + 3] * o_win[rows]).astype(o_ref.dtype)


def _nsa_prompt(z, cmp, B, T, tq=128, tk=512):
    assert T % tk == 0 and tk % tq == 0 and tq % HEAD_DIM == 0 and T >= WINDOW + tq
    assert SEL_LEN & (SEL_LEN - 1) == 0 and tk % SEL_LEN == 0
    nq = T // tq
    r = cmp.shape[3]
    gw = GROUP * HEAD_DIM
    qspec = lambda off: pl.BlockSpec((tq, gw), lambda b, h, q: (b * nq + q, off // gw + h))
    kvspec = lambda off: pl.BlockSpec((T, HEAD_DIM), lambda b, h, q: (b, off // HEAD_DIM + h))
    cspec = lambda which: pl.BlockSpec((None, None, None, r, HEAD_DIM), lambda b, h, q: (b, which, h, 0, 0))
    return pl.pallas_call(
        functools.partial(_nsa_prompt_kernel, tq=tq, tk=tk, seq_len=T),
        out_shape=jax.ShapeDtypeStruct((B * T, A_Q), BF16),
        grid=(B, N_KV, nq),
        in_specs=[qspec(Z_QRAW), qspec(Z_QROT), kvspec(Z_KS), kvspec(Z_VS), kvspec(Z_KW), kvspec(Z_VW),
                  cspec(0), cspec(1),
                  pl.BlockSpec((tq, HEAD_DIM), lambda b, h, q: (b * nq + q, Z_GNSA // HEAD_DIM + h))],
        out_specs=pl.BlockSpec((tq, gw), lambda b, h, q: (b * nq + q, h)),
        scratch_shapes=[pltpu.VMEM((GROUP, tq, 1), F32), pltpu.VMEM((GROUP, tq, 1), F32),
                        pltpu.VMEM((GROUP * tq, HEAD_DIM), F32)],
        compiler_params=pltpu.CompilerParams(dimension_semantics=("parallel", "parallel", "arbitrary"),
                                             vmem_limit_bytes=VMEM_LIMIT),
        name="nsa_prompt",
    )(z, z, z, z, z, z, cmp, cmp, z)


def _gate_logs(i_pre, f_pre):
    log_i = GATE_CAP * jnp.tanh(i_pre / GATE_CAP)
    fc = GATE_CAP * jnp.tanh(f_pre / GATE_CAP)
    log_f = jnp.minimum(fc, 0.0) - jnp.log1p(jnp.exp(-jnp.abs(fc)))
    return log_i, log_f


def _mlstm_prompt_kernel(q_ref, k_ref, v_ref, op_ref, grow_ref, gcol_ref, gain_ref,
                         h_ref, c_ref, n_ref, m_ref):
    L = q_ref.shape[0]

    @pl.when(pl.program_id(2) == 0)
    def _():
        c_ref[...] = jnp.zeros(c_ref.shape, F32)
        n_ref[...] = jnp.zeros(n_ref.shape, F32)
        m_ref[...] = jnp.zeros(m_ref.shape, F32)

    q = q_ref[...]
    qb = q.astype(BF16)
    k = k_ref[...] * DQK_B ** -0.5
    vb = v_ref[...].astype(BF16)
    i_r, f_r = _gate_logs(grow_ref[0:1, :], grow_ref[1:2, :])
    i_c, f_c = _gate_logs(gcol_ref[:, 0:1], gcol_ref[:, 1:2])
    li = lax.broadcasted_iota(jnp.int32, (L, L), 0)
    si = lax.broadcasted_iota(jnp.int32, (L, L), 1)
    tri = si <= li
    bcum_c = jnp.sum(jnp.where(tri, f_r, 0.0), axis=1, keepdims=True)
    bcum_r = jnp.sum(jnp.where(li <= si, f_c, 0.0), axis=0, keepdims=True)
    btot = jnp.sum(f_r, axis=1, keepdims=True)
    c_old, n_old, m_old = c_ref[...], n_ref[...], m_ref[0:1, 0:1]

    dlog = jnp.where(tri, bcum_c - bcum_r + i_r, NEG)
    inter = bcum_c + m_old
    m_t = jnp.maximum(inter, jnp.max(dlog, axis=1, keepdims=True))
    qk = lax.dot_general(qb, k.astype(BF16), (((1,), (1,)), ((), ())), preferred_element_type=F32)
    w_intra = jnp.exp(dlog - m_t) * qk
    w_inter = jnp.exp(inter - m_t)
    num = (w_inter * jnp.dot(qb, c_old.astype(BF16), preferred_element_type=F32)
           + jnp.dot(w_intra.astype(BF16), vb, preferred_element_type=F32))
    den = w_inter * jnp.sum(q * n_old, axis=1, keepdims=True) + jnp.sum(w_intra, axis=1, keepdims=True)
    h = num / jnp.maximum(jnp.abs(den), jnp.exp(-m_t))
    h_ref[...] = (_rms(h, gain_ref[...]) * jax.nn.sigmoid(op_ref[...])).astype(h_ref.dtype)

    wlog_c = btot - bcum_c + i_c
    m_new = jnp.maximum(btot + m_old, jnp.max(btot - bcum_r + i_r, axis=1, keepdims=True))
    decay = jnp.exp(btot + m_old - m_new)
    kw = k * jnp.exp(wlog_c - m_new)
    c_ref[...] = decay * c_old + lax.dot_general(kw.astype(BF16), vb, (((0,), (0,)), ((), ())),
                                                 preferred_element_type=F32)
    n_ref[...] = decay * n_old + jnp.sum(kw, axis=0, keepdims=True)
    m_ref[...] = jnp.broadcast_to(m_new, m_ref.shape)


def _mlstm_prompt(z, gain, B, T):
    L = MLSTM_CHUNK
    assert T % L == 0
    nc, H = T // L, N_HEADS_B
    g = z[:, Z_IB:Z_IB + 2 * H].reshape(B, nc, L, 2, H)
    grow = g.transpose(0, 4, 1, 3, 2).reshape(B * H * nc, 2, L)
    gcol = g.transpose(0, 4, 1, 2, 3).reshape(B * H * nc, L, 2)
    rows = lambda w, off: pl.BlockSpec((L, w), lambda b, h, c: (b * nc + c, off // w + h))
    gidx = lambda b, h, c: ((b * H + h) * nc + c, 0, 0)
    state = lambda *s: pl.BlockSpec((None, None) + s, lambda b, h, c: (b, h, 0, 0))
    h, c, n, m = pl.pallas_call(
        _mlstm_prompt_kernel,
        out_shape=(jax.ShapeDtypeStruct((B * T, B_V), BF16), jax.ShapeDtypeStruct((B, H, DQK_B, DV_B), F32),
                   jax.ShapeDtypeStruct((B, H, 1, DQK_B), F32), jax.ShapeDtypeStruct((B, H, 1, DQK_B), F32)),
        grid=(B, H, nc),
        in_specs=[rows(DQK_B, Z_QB), rows(DQK_B, Z_KB), rows(DV_B, Z_VB), rows(DV_B, Z_OB),
                  pl.BlockSpec((None, 2, L), gidx), pl.BlockSpec((None, L, 2), gidx),
                  pl.BlockSpec((None, 1, DV_B), lambda b, h, c: (h, 0, 0))],
        out_specs=(rows(DV_B, 0), state(DQK_B, DV_B), state(1, DQK_B), state(1, DQK_B)),
        compiler_params=pltpu.CompilerParams(dimension_semantics=("parallel", "parallel", "arbitrary"),
                                             vmem_limit_bytes=VMEM_LIMIT),
        name="mlstm_prompt",
    )(z, z, z, z, grow, gcol, gain.reshape(H, 1, DV_B))
    return h, c, n[:, :, 0], m[:, :, 0, 0]


def _merge_kernel(x_ref, oa_ref, hb_ref, gma_ref, gmb_ref, wa_ref, wb_ref, wo_ref, nf_ref, wrh_ref, wrl_ref, br_ref,
                  *refs):
    x2_ref, xn_ref, route_ref = refs[-3:]
    ua = jnp.dot(oa_ref[...], wa_ref[...], preferred_element_type=F32)
    ub = jnp.dot(hb_ref[...], wb_ref[...], preferred_element_type=F32)
    merged = jax.nn.sigmoid(gma_ref[...]) * ua + jax.nn.sigmoid(gmb_ref[...]) * ub
    x2 = x_ref[...] + jnp.dot(merged.astype(BF16), wo_ref[...], preferred_element_type=F32)
    x2_ref[...] = x2
    xn = _rms(x2, nf_ref[...])
    xn_ref[...] = xn
    hi = xn.astype(BF16)
    lo = (xn - hi.astype(F32)).astype(BF16)
    lg = (jnp.dot(hi, wrh_ref[...], preferred_element_type=F32) + jnp.dot(lo, wrh_ref[...], preferred_element_type=F32)
          + jnp.dot(hi, wrl_ref[...], preferred_element_type=F32)) + br_ref[...]
    lane = lax.broadcasted_iota(jnp.int32, lg.shape, 1)
    lanef = lane.astype(F32)
    ninf = -jnp.inf
    gmask = (lane >= N_EXPERTS) & (lane < N_EXPERTS + N_GROUPS)
    gl = jnp.where(gmask, lg, ninf)
    gmax = jnp.max(gl, axis=1, keepdims=True)
    gidx = jnp.min(jnp.where(gl == gmax, lanef, 1e9), axis=1, keepdims=True) - N_EXPERTS
    gprob = 1.0 / jnp.sum(jnp.where(gmask, jnp.exp(lg - gmax), 0.0), axis=1, keepdims=True)
    e_lo = gidx * EXPERTS_PER_GROUP
    el = jnp.where((lanef >= e_lo) & (lanef < e_lo + EXPERTS_PER_GROUP), lg, ninf)
    v1 = jnp.max(el, axis=1, keepdims=True)
    i1 = jnp.min(jnp.where(el == v1, lanef, 1e9), axis=1, keepdims=True)
    el2 = jnp.where(lanef == i1, ninf, el)
    v2 = jnp.max(el2, axis=1, keepdims=True)
    i2 = jnp.min(jnp.where(el2 == v2, lanef, 1e9), axis=1, keepdims=True)
    e2 = jnp.exp(v2 - v1)
    g1 = gprob / (1.0 + e2)
    g2 = gprob * e2 / (1.0 + e2)
    route_ref[...] = jnp.where(lane == 0, i1, jnp.where(lane == 1, i2, jnp.where(lane == 2, g1,
                               jnp.where(lane == 3, g2, 0.0))))


def _merge_weights(w_up_a, w_up_b, w_out, norm_ffn, w_rg, b_rg, w_re, b_re):
    assert TOP_K == 2
    pad = ROUTE_W - N_EXPERTS - N_GROUPS
    w = jnp.concatenate([w_re, w_rg, jnp.zeros((D_MODEL, pad), F32)], axis=1)
    b = jnp.concatenate([b_re, b_rg, jnp.zeros((pad,), F32)]).reshape(1, ROUTE_W)
    hi = w.astype(BF16)
    return (w_up_a.astype(BF16), w_up_b.astype(BF16), w_out.astype(BF16), norm_ffn.reshape(1, D_MODEL).astype(F32),
            hi, (w - hi.astype(F32)).astype(BF16), b)


def _merge(x2d, o_a, h_b, z, mw, n_total, row0, tm, bufs=None):
    m = x2d.shape[0]
    i0 = row0 // tm
    row = lambda w, c=0: pl.BlockSpec((tm, w), lambda i: (i, c))
    const = lambda a: pl.BlockSpec(a.shape, lambda i: (0,) * a.ndim, pipeline_mode=pl.Buffered(1))
    outw = (D_MODEL, D_MODEL, ROUTE_W)
    in_specs = [row(D_MODEL), row(A_Q), row(B_V), row(D_MODEL, Z_GMA // D_MODEL), row(D_MODEL, Z_GMB // D_MODEL)]
    in_specs += [const(a) for a in mw]
    args = [x2d, o_a, h_b, z, z] + list(mw)
    aliases = {}
    if bufs is not None:
        in_specs += [pl.BlockSpec(memory_space=pl.ANY)] * 3
        aliases = {len(args) + k: k for k in range(3)}
        args += list(bufs)
    return pl.pallas_call(
        _merge_kernel,
        out_shape=tuple(jax.ShapeDtypeStruct((n_total, w), F32) for w in outw),
        grid=(m // tm,),
        in_specs=in_specs,
        out_specs=tuple(pl.BlockSpec((tm, w), lambda i: (i0 + i, 0)) for w in outw),
        input_output_aliases=aliases,
        compiler_params=pltpu.CompilerParams(dimension_semantics=("parallel",), vmem_limit_bytes=VMEM_LIMIT_MERGE),
        name="merge_route",
    )(*args)


def _moe_plan(route, n_tiles):
    n = route.shape[0]
    a = n * TOP_K
    flat_e = route[:, 0:TOP_K].astype(jnp.int32).reshape(a)
    onehot = (flat_e[:, None] == jnp.arange(N_EXPERTS, dtype=jnp.int32)[None, :]).astype(jnp.int32)
    rank = jnp.take_along_axis(jnp.cumsum(onehot, axis=0) - onehot, flat_e[:, None], axis=1)[:, 0]
    counts = jnp.sum(onehot, axis=0)
    ntile = (counts + MOE_TILE - 1) // MOE_TILE
    tile_end = jnp.cumsum(ntile)
    dest = ((tile_end - ntile)[flat_e] * MOE_TILE + rank).astype(jnp.int32)
    n_used = tile_end[-1].astype(jnp.int32)
    t = jnp.minimum(jnp.arange(n_tiles, dtype=jnp.int32), n_used - 1)
    tile_expert = jnp.minimum(jnp.searchsorted(tile_end, t, side='right'), N_EXPERTS - 1).astype(jnp.int32)
    src = jnp.full((n_tiles * MOE_TILE,), -1, jnp.int32).at[dest].set(jnp.arange(a, dtype=jnp.int32) // TOP_K)
    return dest, src, tile_expert, n_used.reshape(1)


def _row_gather_kernel(src_ref, nu_ref, x_hbm, o_ref, sem):
    t = pl.program_id(0)
    rows = o_ref.shape[0]

    def copy(r):
        s = src_ref[t * rows + r]
        return s, pltpu.make_async_copy(x_hbm.at[pl.ds(jnp.maximum(s, 0), 1)], o_ref.at[pl.ds(r, 1)], sem)

    @pl.when(t < nu_ref[0])
    def _():
        def start(r, c):
            s, cp = copy(r)

            @pl.when(s >= 0)
            def _():
                cp.start()
            return c

        def wait(r, c):
            s, cp = copy(r)

            @pl.when(s >= 0)
            def _():
                cp.wait()
            return c

        lax.fori_loop(0, rows, start, 0)
        lax.fori_loop(0, rows, wait, 0)


def _moe_gather(xn, src, n_used, n_tiles):
    d = xn.shape[1]
    return pl.pallas_call(
        _row_gather_kernel,
        out_shape=jax.ShapeDtypeStruct((n_tiles * MOE_TILE, d), xn.dtype),
        grid_spec=pltpu.PrefetchScalarGridSpec(
            num_scalar_prefetch=2, grid=(n_tiles,),
            in_specs=[pl.BlockSpec(memory_space=pl.ANY)],
            out_specs=pl.BlockSpec((MOE_TILE, d), lambda t, src, nu: (jnp.minimum(t, nu[0] - 1), 0)),
            scratch_shapes=[pltpu.SemaphoreType.DMA(())]),
        compiler_params=pltpu.CompilerParams(dimension_semantics=("arbitrary",), vmem_limit_bytes=VMEM_LIMIT),
        name="moe_gather",
    )(src, n_used, xn)


def _expert_kernel(te_ref, nu_ref, x_ref, wg_ref, wu_ref, wd_ref, o_ref, xb_ref):
    t, j = pl.program_id(0), pl.program_id(1)

    @pl.when(t < nu_ref[0])
    def _():
        @pl.when(j == 0)
        def _():
            xb_ref[...] = x_ref[...].astype(BF16)

        xb = xb_ref[...]
        g = jnp.dot(xb, wg_ref[...].astype(BF16), preferred_element_type=F32)
        u = jnp.dot(xb, wu_ref[...].astype(BF16), preferred_element_type=F32)
        h = (g * jax.nn.sigmoid(g) * u).astype(BF16)
        y = jnp.dot(h, wd_ref[...].astype(BF16), preferred_element_type=F32)

        @pl.when(j == 0)
        def _():
            o_ref[...] = y

        @pl.when(j > 0)
        def _():
            o_ref[...] += y


def _moe_experts(xs, tile_expert, n_used, w_gate, w_up, w_down):
    n_tiles = xs.shape[0] // MOE_TILE
    nj = D_EXPERT // MOE_FC
    tix = lambda t, j, te, nu: (jnp.minimum(t, nu[0] - 1), 0)
    jj = lambda t, j, nu: jnp.where(t < nu[0], j, nj - 1)
    return pl.pallas_call(
        _expert_kernel,
        out_shape=jax.ShapeDtypeStruct(xs.shape, F32),
        grid_spec=pltpu.PrefetchScalarGridSpec(
            num_scalar_prefetch=2, grid=(n_tiles, nj),
            in_specs=[pl.BlockSpec((MOE_TILE, D_MODEL), tix),
                      pl.BlockSpec((None, D_MODEL, MOE_FC), lambda t, j, te, nu: (te[t], 0, jj(t, j, nu))),
                      pl.BlockSpec((None, D_MODEL, MOE_FC), lambda t, j, te, nu: (te[t], 0, jj(t, j, nu))),
                      pl.BlockSpec((None, MOE_FC, D_MODEL), lambda t, j, te, nu: (te[t], jj(t, j, nu), 0))],
            out_specs=pl.BlockSpec((MOE_TILE, D_MODEL), tix),
            scratch_shapes=[pltpu.VMEM((MOE_TILE, D_MODEL), BF16)]),
        compiler_params=pltpu.CompilerParams(dimension_semantics=("arbitrary", "arbitrary"),
                                             vmem_limit_bytes=VMEM_LIMIT),
        name="moe_experts",
    )(tile_expert, n_used, xs, w_gate, w_up, w_down)


def _combine_kernel(dest_ref, x2_ref, route_ref, g_ref, y_hbm, op_ref, os_ref, buf_ref, sem, *, n_prompt_tiles):
    i = pl.program_id(0)
    tm = x2_ref.shape[0]

    def copy(r, k):
        d = dest_ref[(i * tm + r) * TOP_K + k]
        return pltpu.make_async_copy(y_hbm.at[pl.ds(d, 1)], buf_ref.at[k, pl.ds(r, 1)], sem)

    def start(r, c):
        for k in range(TOP_K):
            copy(r, k).start()
        return c

    def wait(r, c):
        for k in range(TOP_K):
            copy(r, k).wait()
        return c

    lax.fori_loop(0, tm, start, 0)
    lax.fori_loop(0, tm, wait, 0)
    ffn = buf_ref[0] * route_ref[:, TOP_K:TOP_K + 1]
    for k in range(1, TOP_K):
        ffn = ffn + buf_ref[k] * route_ref[:, TOP_K + k:TOP_K + k + 1]
    y = _rms(x2_ref[...] + ffn, g_ref[...])

    @pl.when(i < n_prompt_tiles)
    def _():
        op_ref[...] = y

    @pl.when(i >= n_prompt_tiles)
    def _():
        os_ref[...] = y


def _moe_combine(x2, route, dest, ys, norm_final, n_prompt, tm=128):
    n, d = x2.shape
    npt = n_prompt // tm
    nst = (n - n_prompt) // tm
    return pl.pallas_call(
        functools.partial(_combine_kernel, n_prompt_tiles=npt),
        out_shape=(jax.ShapeDtypeStruct((n_prompt, d), F32), jax.ShapeDtypeStruct((n - n_prompt, d), F32)),
        grid_spec=pltpu.PrefetchScalarGridSpec(
            num_scalar_prefetch=1, grid=(n // tm,),
            in_specs=[pl.BlockSpec((tm, d), lambda i, ds: (i, 0)), pl.BlockSpec((tm, ROUTE_W), lambda i, ds: (i, 0)),
                      pl.BlockSpec((1, d), lambda i, ds: (0, 0)), pl.BlockSpec(memory_space=pl.ANY)],
            out_specs=(pl.BlockSpec((tm, d), lambda i, ds: (jnp.minimum(i, npt - 1), 0)),
                       pl.BlockSpec((tm, d), lambda i, ds: (jnp.clip(i - npt, 0, nst - 1), 0))),
            scratch_shapes=[pltpu.VMEM((TOP_K, tm, d), F32), pltpu.SemaphoreType.DMA(())]),
        compiler_params=pltpu.CompilerParams(dimension_semantics=("arbitrary",), vmem_limit_bytes=VMEM_LIMIT),
        name="moe_combine",
    )(dest, x2, route, norm_final.reshape(1, d), ys)


def _moe(x2, xn, route, w_gate, w_up, w_down, norm_final, n_prompt):
    n = x2.shape[0]
    n_tiles = (n * TOP_K + N_EXPERTS * (MOE_TILE - 1) + MOE_TILE - 1) // MOE_TILE
    dest, src, tile_expert, n_used = _moe_plan(route, n_tiles)
    xs = _moe_gather(xn, src, n_used, n_tiles)
    ys = _moe_experts(xs, tile_expert, n_used, w_gate, w_up, w_down)
    return _moe_combine(x2, route, dest, ys, norm_final, n_prompt)


POOL_ROWS = 256
ROWS_PER_PAGE = PAGE_SIZE // CMP_STRIDE
ZW = 2 * N_KV * HEAD_DIM
QROWS = 8


def _compress_pool_kernel(xk_ref, xv_ref, w1_ref, p_ref, zk_ref, zv_ref):
    for which, (x_ref, z_ref) in enumerate(((xk_ref, zk_ref), (xv_ref, zv_ref))):
        w1 = w1_ref[which]
        pb = jnp.dot(p_ref[which], w1, preferred_element_type=F32)
        bias = pb[0:1, :HEAD_DIM] + pb[1:2, HEAD_DIM:]
        for h in range(N_KV):
            xh = jnp.concatenate([x_ref[:, (N_KV * s + h) * HEAD_DIM:(N_KV * s + h + 1) * HEAD_DIM]
                                  for s in range(CMP_STRIDE)], axis=1).astype(BF16)
            zz = jnp.dot(xh, w1, preferred_element_type=F32)
            z_ref[:, 2 * h * HEAD_DIM:(2 * h + 1) * HEAD_DIM] = zz[:, :HEAD_DIM] + bias
            z_ref[:, (2 * h + 1) * HEAD_DIM:(2 * h + 2) * HEAD_DIM] = zz[:, HEAD_DIM:]


def _compress_pool(pool_k, pool_v, cw):
    n_phys = pool_k.shape[0]
    rows = n_phys * ROWS_PER_PAGE
    width = CMP_STRIDE * N_KV * HEAD_DIM
    assert rows % POOL_ROWS == 0
    w1, p, _ = cw
    full = lambda a: pl.BlockSpec(a.shape, lambda i: (0,) * a.ndim)
    zk, zv = pl.pallas_call(
        _compress_pool_kernel,
        out_shape=(jax.ShapeDtypeStruct((rows, ZW), F32),) * 2,
        grid=(rows // POOL_ROWS,),
        in_specs=[pl.BlockSpec((POOL_ROWS, width), lambda i: (i, 0))] * 2 + [full(w1), full(p)],
        out_specs=(pl.BlockSpec((POOL_ROWS, ZW), lambda i: (i, 0)),) * 2,
        compiler_params=pltpu.CompilerParams(dimension_semantics=("parallel",), vmem_limit_bytes=VMEM_LIMIT),
        name="compress_pool",
    )(pool_k.reshape(rows, width), pool_v.reshape(rows, width), w1, p)
    return zk.reshape(n_phys, ROWS_PER_PAGE, ZW), zv.reshape(n_phys, ROWS_PER_PAGE, ZW)


def _nsa_sample_kernel(pt_ref, qraw_ref, qrot_ref, knew_ref, wnew_ref, gate_ref, w2_ref, wk_ref, wv_ref,
                       zk_hbm, zv_hbm, ks_hbm, vs_hbm, o_ref, zbuf, sbuf, sem, *, n_pages, past_len):
    b = pl.program_id(0)
    nb = pl.num_programs(0)
    slot = lax.rem(b, 2)

    def copies(seq, sl):
        out = []
        for pg in range(n_pages):
            page = pt_ref[seq, pg]
            for which, (zsrc, ssrc) in enumerate(((zk_hbm, ks_hbm), (zv_hbm, vs_hbm))):
                out.append(pltpu.make_async_copy(
                    zsrc.at[page], zbuf.at[sl, which, pl.ds(pg * ROWS_PER_PAGE, ROWS_PER_PAGE)], sem.at[sl]))
                out.append(pltpu.make_async_copy(
                    ssrc.at[page], sbuf.at[sl, which, pl.ds(pg * PAGE_SIZE, PAGE_SIZE)], sem.at[sl]))
        return out

    @pl.when(b == 0)
    def _():
        for cp in copies(0, 0):
            cp.start()

    @pl.when(b + 1 < nb)
    def _():
        for cp in copies(b + 1, 1 - slot):
            cp.start()

    for cp in copies(b, slot):
        cp.wait()

    scale = HEAD_DIM ** -0.5
    r = n_pages * ROWS_PER_PAGE
    qpos = jnp.full((QROWS, 1), past_len, jnp.int32)
    n_sel_pad = HEAD_DIM
    sel_shift = SEL_LEN.bit_length() - 1

    def heads8(ref, h):
        rows = [ref[:, (h * GROUP + g) * HEAD_DIM:(h * GROUP + g + 1) * HEAD_DIM] for g in range(GROUP)]
        return jnp.concatenate(rows + [jnp.zeros((QROWS - GROUP, HEAD_DIM), F32)], axis=0).astype(BF16)

    def scores(q, k):
        return lax.dot_general(q, k, (((1,), (1,)), ((), ())), preferred_element_type=F32) * scale

    cmp = []
    for which in range(2):
        per_head = []
        for h in range(N_KV):
            zh = zbuf[slot, which, :, 2 * h * HEAD_DIM:(2 * h + 2) * HEAD_DIM]
            pre = zh[:, :HEAD_DIM] + pltpu.roll(zh[:, HEAD_DIM:], r - 1, 0)
            mid = pre * jax.nn.sigmoid(pre)
            per_head.append(jnp.dot(mid.astype(BF16), w2_ref[which], preferred_element_type=F32).astype(BF16))
        cmp.append(per_head)
    c_end = lax.broadcasted_iota(jnp.int32, (QROWS, r), 1) * CMP_STRIDE + (CMP_LEN - 1)
    cmask = c_end <= qpos
    o_cmp, psums = [], []
    for h in range(N_KV):
        s = scores(heads8(qraw_ref, h), cmp[0][h]) + jnp.where(cmask, 0.0, NEG)
        p = jnp.exp(s - jnp.max(s, axis=-1, keepdims=True)) * jnp.where(cmask, 1.0, 0.0)
        p = p / jnp.maximum(jnp.sum(p, axis=-1, keepdims=True), 1e-30)
        o_cmp.append(jnp.dot(p.astype(BF16), cmp[1][h], preferred_element_type=F32))
        psums.append(jnp.sum(p[0:GROUP], axis=0, keepdims=True))
    psum = jnp.concatenate(psums + [jnp.zeros((QROWS - N_KV, r), F32)], axis=0)
    sel = _select_blocks(psum, qpos, n_sel_pad)
    ej = lax.broadcasted_iota(jnp.int32, (n_sel_pad, past_len), 0)
    et = lax.broadcasted_iota(jnp.int32, (n_sel_pad, past_len), 1)
    expand = jnp.where(jnp.right_shift(et, sel_shift) == ej, 1.0, 0.0).astype(BF16)
    selm = jnp.dot(sel.astype(BF16), expand, preferred_element_type=F32)
    new_blk = past_len // SEL_LEN

    def attend(q, k, v, k_new, v_new, bias, new_bias):
        s = scores(q, k) + bias
        s_new = jnp.sum(q.astype(F32) * k_new.astype(BF16).astype(F32), axis=1, keepdims=True) * scale + new_bias
        m = jnp.maximum(jnp.max(s, axis=-1, keepdims=True), s_new)
        p = jnp.exp(s - m)
        p_new = jnp.exp(s_new - m)
        l = jnp.sum(p, axis=-1, keepdims=True) + p_new
        o = (jnp.dot(p.astype(BF16), v, preferred_element_type=F32)
             + p_new.astype(BF16).astype(F32) * v_new.astype(BF16).astype(F32))
        return o / jnp.maximum(l, 1e-30)

    gts = jax.nn.sigmoid(gate_ref[...])
    for h in range(N_KV):
        hs = slice(h * HEAD_DIM, (h + 1) * HEAD_DIM)
        vs_new = slice(A_KV + h * HEAD_DIM, A_KV + (h + 1) * HEAD_DIM)
        q = heads8(qrot_ref, h)
        sel_bias = jnp.where(selm[h:h + 1, :] > 0.5, 0.0, NEG)
        new_bias = jnp.where(sel[h:h + 1, new_blk:new_blk + 1] > 0.5, 0.0, NEG)
        o_sel = attend(q, sbuf[slot, 0, :, hs].astype(BF16), sbuf[slot, 1, :, hs].astype(BF16),
                       knew_ref[:, hs], knew_ref[:, vs_new], sel_bias, new_bias)
        o_win = attend(q, wk_ref[:, hs].astype(BF16), wv_ref[:, hs].astype(BF16),
                       wnew_ref[:, hs], wnew_ref[:, vs_new], 0.0, 0.0)
        for g in range(GROUP):
            c0 = h * HEAD_DIM + 3 * g
            o = (gts[:, c0:c0 + 1] * o_cmp[h][g:g + 1] + gts[:, c0 + 1:c0 + 2] * o_sel[g:g + 1]
                 + gts[:, c0 + 2:c0 + 3] * o_win[g:g + 1])
            o_ref[:, (h * GROUP + g) * HEAD_DIM:(h * GROUP + g + 1) * HEAD_DIM] = o.astype(o_ref.dtype)


def _nsa_sample(zs, page_table, zk_pool, zv_pool, sel_k, sel_v, win_k, win_v, w2):
    bs, n_pages = page_table.shape
    past_len = n_pages * PAGE_SIZE
    wb = win_k.shape[1]
    assert wb <= WINDOW and past_len % SEL_LEN == 0 and past_len // SEL_LEN < HEAD_DIM
    n_phys = sel_k.shape[0]
    zs3 = zs.reshape(bs, 1, NZ)
    zspec = lambda w, off: pl.BlockSpec((None, 1, w), lambda b, pt: (b, 0, off // w))
    anyspec = pl.BlockSpec(memory_space=pl.ANY)
    wspec = pl.BlockSpec((None, wb, A_KV), lambda b, pt: (b, 0, 0))
    out = pl.pallas_call(
        functools.partial(_nsa_sample_kernel, n_pages=n_pages, past_len=past_len),
        out_shape=jax.ShapeDtypeStruct((bs, 1, A_Q), BF16),
        grid_spec=pltpu.PrefetchScalarGridSpec(
            num_scalar_prefetch=1, grid=(bs,),
            in_specs=[zspec(A_Q, Z_QRAW), zspec(A_Q, Z_QROT), zspec(2 * A_KV, Z_KS), zspec(2 * A_KV, Z_KW),
                      zspec(2 * HEAD_DIM, Z_GNSA), pl.BlockSpec(w2.shape, lambda b, pt: (0, 0, 0)), wspec, wspec,
                      anyspec, anyspec, anyspec, anyspec],
            out_specs=pl.BlockSpec((None, 1, A_Q), lambda b, pt: (b, 0, 0)),
            scratch_shapes=[pltpu.VMEM((2, 2, n_pages * ROWS_PER_PAGE, ZW), F32),
                            pltpu.VMEM((2, 2, past_len, A_KV), F32), pltpu.SemaphoreType.DMA((2,))]),
        compiler_params=pltpu.CompilerParams(dimension_semantics=("arbitrary",), vmem_limit_bytes=VMEM_LIMIT),
        name="nsa_sample",
    )(page_table, zs3, zs3, zs3, zs3, zs3, w2, win_k.reshape(bs, wb, A_KV), win_v.reshape(bs, wb, A_KV),
      zk_pool, zv_pool, sel_k.reshape(n_phys, PAGE_SIZE, A_KV), sel_v.reshape(n_phys, PAGE_SIZE, A_KV))
    return out.reshape(bs, A_Q)


def _mlstm_sample_kernel(q_ref, k_ref, v_ref, op_ref, g_ref, gain_ref, c_ref, n_ref, m_ref,
                         h_ref, co_ref, no_ref, mo_ref):
    eye = (lax.broadcasted_iota(jnp.int32, (DQK_B, DQK_B), 0) == lax.broadcasted_iota(jnp.int32, (DQK_B, DQK_B), 1))
    col = lambda row: jnp.sum(jnp.where(eye, row, 0.0), axis=1, keepdims=True)
    lane = lax.broadcasted_iota(jnp.int32, (1, N_HEADS_B), 1)
    m_out = jnp.zeros((1, N_HEADS_B), F32)
    for h in range(N_HEADS_B):
        q = q_ref[:, h * DQK_B:(h + 1) * DQK_B]
        k = k_ref[:, h * DQK_B:(h + 1) * DQK_B] * DQK_B ** -0.5
        v = v_ref[:, h * DV_B:(h + 1) * DV_B]
        log_i, log_f = _gate_logs(g_ref[:, h:h + 1], g_ref[:, N_HEADS_B + h:N_HEADS_B + h + 1])
        c_old, n_old, m_old = c_ref[h], n_ref[h:h + 1, :], m_ref[:, h:h + 1]
        inter = log_f + m_old
        m_new = jnp.maximum(inter, log_i)
        decay = jnp.exp(inter - m_new)
        w = jnp.exp(log_i - m_new)
        co_ref[h] = decay * c_old + col(w * k) * v
        no_ref[h:h + 1, :] = decay * n_old + w * k
        m_out = jnp.where(lane == h, m_new, m_out)
        w_intra = w * jnp.sum(q * k, axis=1, keepdims=True)
        num = decay * jnp.sum(c_old * col(q), axis=0, keepdims=True) + w_intra * v
        den = decay * jnp.sum(q * n_old, axis=1, keepdims=True) + w_intra
        hh = num / jnp.maximum(jnp.abs(den), jnp.exp(-m_new))
        out = _rms(hh, gain_ref[h]) * jax.nn.sigmoid(op_ref[:, h * DV_B:(h + 1) * DV_B])
        h_ref[:, h * DV_B:(h + 1) * DV_B] = out.astype(h_ref.dtype)
    mo_ref[...] = m_out


def _mlstm_sample(zs, gain, c0, n0, m0):
    bs, H = zs.shape[0], N_HEADS_B
    zs3 = zs.reshape(bs, 1, NZ)
    zspec = lambda w, off: pl.BlockSpec((None, 1, w), lambda b: (b, 0, off // w))
    cspec = pl.BlockSpec((None, H, DQK_B, DV_B), lambda b: (b, 0, 0, 0))
    nspec = pl.BlockSpec((None, H, DQK_B), lambda b: (b, 0, 0))
    mspec = pl.BlockSpec((None, 1, H), lambda b: (b, 0, 0))
    h, c, n, m = pl.pallas_call(
        _mlstm_sample_kernel,
        out_shape=(jax.ShapeDtypeStruct((bs, 1, B_V), BF16), jax.ShapeDtypeStruct(c0.shape, F32),
                   jax.ShapeDtypeStruct(n0.shape, F32), jax.ShapeDtypeStruct((bs, 1, H), F32)),
        grid=(bs,),
        in_specs=[zspec(B_QK, Z_QB), zspec(B_QK, Z_KB), zspec(B_V, Z_VB), zspec(B_V, Z_OB), zspec(HEAD_DIM, Z_IB),
                  pl.BlockSpec((H, 1, DV_B), lambda b: (0, 0, 0)), cspec, nspec, mspec],
        out_specs=(pl.BlockSpec((None, 1, B_V), lambda b: (b, 0, 0)), cspec, nspec, mspec),
        compiler_params=pltpu.CompilerParams(dimension_semantics=("parallel",), vmem_limit_bytes=VMEM_LIMIT),
        name="mlstm_sample",
    )(zs3, zs3, zs3, zs3, zs3, gain.reshape(H, 1, DV_B), c0, n0, m0.reshape(bs, 1, H))
    return h.reshape(bs, B_V), c, n, m.reshape(bs, H)


def _split_kv(z, B, T):
    z = z.reshape(B, T, NZ)
    kv = lambda o: z[..., o:o + A_KV].reshape(B, T, N_KV, HEAD_DIM)
    return kv(Z_KC), kv(Z_VC), kv(Z_KS), kv(Z_VS), kv(Z_KW), kv(Z_VW)


def kernel(x_prompt, x_sample, cache_cmp_k, cache_cmp_v, cache_sel_k, cache_sel_v, cache_win_k, cache_win_v,
           state_mlstm_c, state_mlstm_n, state_mlstm_m, page_table, norm_mix, w_in, b_in, cmp_pos, cmp_w1, cmp_w2,
           mlstm_norm, w_up_a, w_up_b, w_out, norm_ffn, w_router_grp, b_router_grp, w_router_exp, b_router_exp,
           w_gate, w_up, w_down, norm_final):
    assert w_in.shape[0] == 1, "single layer"
    Bp, Tp = x_prompt.shape[:2]
    Bs, Ts = x_sample.shape[:2]
    assert Ts == 1
    n_p, n_s = Bp * Tp, Bs * Ts
    past_len = page_table.shape[1] * PAGE_SIZE
    pos_p = jnp.arange(Tp, dtype=jnp.int32)
    pos_s = past_len + jnp.arange(Ts, dtype=jnp.int32)
    l = 0

    w_perm, b_perm = _permute_in_proj(w_in[l], b_in[l])
    cw = _compress_weights(cmp_pos[l], cmp_w1[l], cmp_w2[l])
    mw = _merge_weights(w_up_a[l], w_up_b[l], w_out[l], norm_ffn[l], w_router_grp[l], b_router_grp[l],
                        w_router_exp[l], b_router_exp[l])

    xp2d = x_prompt.reshape(n_p, D_MODEL)
    zp = _project(xp2d, jnp.tile(pos_p, Bp), norm_mix[l], w_perm, b_perm, 512)
    kc, vc, ks, vs, kw, vw = _split_kv(zp, Bp, Tp)
    o_a = _nsa_prompt(zp, _compress_prompt(kc, vc, cw), Bp, Tp)
    h_b, c_p, n_p_state, m_p = _mlstm_prompt(zp, mlstm_norm[l], Bp, Tp)
    bufs = _merge(xp2d, o_a, h_b, zp, mw, n_p + n_s, 0, 256)
    wl = min(WINDOW, Tp)
    prompt_new = (kc, vc, ks, vs, kw[:, -wl:], vw[:, -wl:], c_p, n_p_state, m_p)

    xs2d = x_sample.reshape(n_s, D_MODEL)
    zs = _project(xs2d, jnp.repeat(pos_s, Bs), norm_mix[l], w_perm, b_perm, 128)
    kc, vc, ks, vs, kw, vw = _split_kv(zs, Bs, Ts)
    assert (past_len + Ts - CMP_LEN) // CMP_STRIDE + 1 == past_len // CMP_STRIDE - 1
    zk_pool, zv_pool = _compress_pool(cache_cmp_k[l], cache_cmp_v[l], cw)
    o_a_s = _nsa_sample(zs, page_table, zk_pool, zv_pool, cache_sel_k[l], cache_sel_v[l],
                        cache_win_k[l], cache_win_v[l], cw[2])
    h_s, c_s, n_s_state, m_s = _mlstm_sample(zs, mlstm_norm[l], state_mlstm_c[l], state_mlstm_n[l],
                                             state_mlstm_m[l])
    x2, xn, route = _merge(xs2d, o_a_s, h_s, zs, mw, n_p + n_s, n_p, 128, bufs=bufs)
    wk = jnp.concatenate([cache_win_k[l], kw], axis=1)
    wv = jnp.concatenate([cache_win_v[l], vw], axis=1)
    wl = min(WINDOW, past_len + Ts)
    sample_new = (kc, vc, ks, vs, wk[:, -wl:], wv[:, -wl:], c_s, n_s_state, m_s)

    y_p, y_s = _moe(x2, xn, route, w_gate[l], w_up[l], w_down[l], norm_final, n_p)
    return ((y_p.reshape(Bp, Tp, D_MODEL), y_s.reshape(Bs, Ts, D_MODEL))
            + tuple(a[None] for a in prompt_new) + tuple(a[None] for a in sample_new))
```

```python
import functools

import numpy as np
import jax
import jax.numpy as jnp
from jax import lax
from jax.experimental import pallas as pl
from jax.experimental.pallas import tpu as pltpu

D_MODEL = 2048
PAGE_SIZE = 128
N_HEADS_A = 8
N_KV = 2
GROUP = N_HEADS_A // N_KV
HEAD_DIM = 128
ROT_DIM = HEAD_DIM // 4
ROPE_THETA = 500000.0
CMP_LEN = 32
CMP_STRIDE = 16
SEL_LEN = 64
SEL_TOP = 16
FORCE_BONUS = 100.0
WINDOW = 512
Q_BLOCK = 128
N_HEADS_B = 4
DQK_B = 128
DV_B = 256
MLSTM_CHUNK = 64
GATE_CAP = 15.0
N_GROUPS = 4
EXPERTS_PER_GROUP = 8
N_EXPERTS = N_GROUPS * EXPERTS_PER_GROUP
TOP_K = 2
D_EXPERT = 1024
EPS = 1e-6
NEG = -1e30

A_Q = N_HEADS_A * HEAD_DIM
A_KV = N_KV * HEAD_DIM
B_QK = N_HEADS_B * DQK_B
B_V = N_HEADS_B * DV_B
IN_SPLITS = (A_Q, A_KV, A_KV, A_KV, A_KV, A_KV, A_KV, 3 * N_HEADS_A,
             B_QK, B_QK, B_V, N_HEADS_B, N_HEADS_B, B_V, D_MODEL, D_MODEL)

F32 = jnp.float32
BF16 = jnp.bfloat16
VMEM_LIMIT = 48 * 1024 * 1024
VMEM_LIMIT_MERGE = 56 * 1024 * 1024

_CUTS = np.concatenate([[0], np.cumsum(IN_SPLITS)]).tolist()
_SRC = dict(zip(('q_a', 'k_c', 'v_c', 'k_s', 'v_s', 'k_w', 'v_w', 'g_nsa', 'q_b', 'k_b', 'v_b', 'i_b', 'f_b',
                 'o_b', 'g_ma', 'g_mb'), zip(_CUTS[:-1], _CUTS[1:])))
_Z_ORDER = ('g_ma', 'g_mb', 'q_a', 'q_a', 'v_b', 'o_b', 'k_c', 'v_c', 'k_s', 'v_s', 'k_w', 'v_w', 'q_b', 'k_b')
PROJ_TN = 512
Z_GMA, Z_GMB, Z_QRAW, Z_QROT, Z_VB, Z_OB = 0, 2048, 4096, 5120, 6144, 7168
Z_KC, Z_VC, Z_KS, Z_VS, Z_KW, Z_VW = 8192, 8448, 8704, 8960, 9216, 9472
Z_QB, Z_KB, Z_SMALL = 9728, 10240, 10752
N_GATE = 3 * GROUP
Z_GNSA, Z_IB, Z_FB = Z_SMALL, Z_SMALL + 2 * HEAD_DIM, Z_SMALL + 2 * HEAD_DIM + N_HEADS_B
NZ = 11264
_ROPE_ALL_TILES = (Z_QROT // PROJ_TN, Z_QROT // PROJ_TN + 1)
_ROPE_HALF_TILES = (Z_KS // PROJ_TN, Z_KW // PROJ_TN)

ROUTE_W = 128
MOE_TILE = 256
MOE_FC = 256


def _rms(x, g):
    return x * lax.rsqrt(jnp.mean(x * x, axis=-1, keepdims=True) + EPS) * g


def _proj_kernel(x_ref, g_ref, w_ref, b_ref, cos_ref, sa_ref, sb_ref, z_ref, xn_ref):
    j = pl.program_id(1)

    @pl.when(j == 0)
    def _():
        xn_ref[...] = _rms(x_ref[...], g_ref[...]).astype(BF16)

    acc = jnp.dot(xn_ref[...], w_ref[...], preferred_element_type=F32) + b_ref[...]

    def rope(blk):
        return (blk * cos_ref[...] + pltpu.roll(blk, HEAD_DIM - ROT_DIM // 2, 1) * sa_ref[...]
                + pltpu.roll(blk, ROT_DIM // 2, 1) * sb_ref[...])

    is_all = (j == _ROPE_ALL_TILES[0]) | (j == _ROPE_ALL_TILES[1])
    is_half = (j == _ROPE_HALF_TILES[0]) | (j == _ROPE_HALF_TILES[1])
    n_blk = PROJ_TN // HEAD_DIM

    @pl.when(is_all)
    def _():
        for c in range(n_blk):
            z_ref[:, c * HEAD_DIM:(c + 1) * HEAD_DIM] = rope(acc[:, c * HEAD_DIM:(c + 1) * HEAD_DIM])

    @pl.when(is_half)
    def _():
        for c in range(n_blk // 2):
            z_ref[:, c * HEAD_DIM:(c + 1) * HEAD_DIM] = rope(acc[:, c * HEAD_DIM:(c + 1) * HEAD_DIM])
        z_ref[:, PROJ_TN // 2:] = acc[:, PROJ_TN // 2:]

    @pl.when(jnp.logical_not(is_all | is_half))
    def _():
        z_ref[...] = acc


def _rope_tables(pos):
    half = ROT_DIM // 2
    inv = ROPE_THETA ** (-jnp.arange(half, dtype=F32) / half)
    ang = pos.astype(F32)[:, None] * inv[None, :]
    cos, sin = jnp.cos(ang), jnp.sin(ang)
    n = pos.shape[0]
    ones = jnp.ones((n, HEAD_DIM - ROT_DIM), F32)
    zeros = jnp.zeros((n, HEAD_DIM - half), F32)
    cos_t = jnp.concatenate([cos, cos, ones], axis=1)
    sa_t = jnp.concatenate([-sin, zeros], axis=1)
    sb_t = jnp.concatenate([jnp.zeros((n, half), F32), sin, jnp.zeros((n, HEAD_DIM - ROT_DIM), F32)], axis=1)
    return cos_t, sa_t, sb_t


def _permute_in_proj(w_in, b_in):
    g0 = _SRC['g_nsa'][0]
    spans = [_SRC[k] for k in _Z_ORDER]
    spans += [(g0, g0 + N_GATE), HEAD_DIM - N_GATE, (g0 + N_GATE, g0 + 2 * N_GATE), HEAD_DIM - N_GATE,
              _SRC['i_b'], _SRC['f_b'], NZ - Z_FB - N_HEADS_B]
    wb = jnp.concatenate([w_in, b_in[None, :]], axis=0)
    parts = [jnp.zeros((D_MODEL + 1, s), wb.dtype) if isinstance(s, int) else wb[:, s[0]:s[1]] for s in spans]
    wb = jnp.concatenate(parts, axis=1)
    return wb[:D_MODEL].astype(BF16), wb[D_MODEL:].astype(F32)


def _project(x2d, pos_rows, norm_g, w_perm, b_perm, tm):
    m = x2d.shape[0]
    cos_t, sa_t, sb_t = _rope_tables(pos_rows)
    row = lambda i, j: (i, 0)
    return pl.pallas_call(
        _proj_kernel,
        out_shape=jax.ShapeDtypeStruct((m, NZ), F32),
        grid=(m // tm, NZ // PROJ_TN),
        in_specs=[pl.BlockSpec((tm, D_MODEL), row),
                  pl.BlockSpec((1, D_MODEL), lambda i, j: (0, 0)),
                  pl.BlockSpec((D_MODEL, PROJ_TN), lambda i, j: (0, j)),
                  pl.BlockSpec((1, PROJ_TN), lambda i, j: (0, j)),
                  pl.BlockSpec((tm, HEAD_DIM), row),
                  pl.BlockSpec((tm, HEAD_DIM), row),
                  pl.BlockSpec((tm, HEAD_DIM), row)],
        out_specs=pl.BlockSpec((tm, PROJ_TN), lambda i, j: (i, j)),
        scratch_shapes=[pltpu.VMEM((tm, D_MODEL), BF16)],
        compiler_params=pltpu.CompilerParams(dimension_semantics=("parallel", "arbitrary"),
                                             vmem_limit_bytes=VMEM_LIMIT),
        name="proj",
    )(x2d, norm_g.reshape(1, D_MODEL), w_perm, b_perm, cos_t, sa_t, sb_t)


def _compress_body(x_ref, w1, pb, w2, r):
    bias = pb[0:1, :HEAD_DIM] + pb[1:2, HEAD_DIM:]
    outs = []
    for h in range(N_KV):
        xh = jnp.concatenate([x_ref[:, (N_KV * s + h) * HEAD_DIM:(N_KV * s + h + 1) * HEAD_DIM]
                              for s in range(CMP_STRIDE)], axis=1).astype(BF16)
        zz = jnp.dot(xh, w1, preferred_element_type=F32)
        pre = zz[:, :HEAD_DIM] + pltpu.roll(zz[:, HEAD_DIM:], r - 1, 0) + bias
        mid = pre * jax.nn.sigmoid(pre)
        outs.append(jnp.dot(mid.astype(BF16), w2, preferred_element_type=F32))
    return outs


def _compress_kernel(xk_ref, xv_ref, w1_ref, p_ref, w2_ref, o_ref):
    r = xk_ref.shape[0]
    for which, x_ref in enumerate((xk_ref, xv_ref)):
        w1 = w1_ref[which]
        pb = jnp.dot(p_ref[which], w1, preferred_element_type=F32)
        for h, o in enumerate(_compress_body(x_ref, w1, pb, w2_ref[which], r)):
            o_ref[which, h] = o


def _compress_weights(cmp_pos, cmp_w1, cmp_w2):
    assert CMP_LEN == 2 * CMP_STRIDE
    half = CMP_STRIDE * HEAD_DIM
    w1 = jnp.concatenate([cmp_w1[:, :half], cmp_w1[:, half:]], axis=2).astype(BF16)
    p = cmp_pos.reshape(2, 2, half)
    p = jnp.concatenate([p, jnp.zeros((2, 6, half), p.dtype)], axis=1).astype(BF16)
    return w1, p, cmp_w2.astype(BF16)


def _compress_prompt(kc, vc, cw):
    B, T = kc.shape[:2]
    r = T // CMP_STRIDE
    width = CMP_STRIDE * N_KV * HEAD_DIM
    w1, p, w2 = cw
    full = lambda a: pl.BlockSpec(a.shape, lambda b: (0,) * a.ndim)
    xspec = pl.BlockSpec((None, r, width), lambda b: (b, 0, 0))
    return pl.pallas_call(
        _compress_kernel,
        out_shape=jax.ShapeDtypeStruct((B, 2, N_KV, r, HEAD_DIM), F32),
        grid=(B,),
        in_specs=[xspec, xspec, full(w1), full(p), full(w2)],
        out_specs=pl.BlockSpec((None, 2, N_KV, r, HEAD_DIM), lambda b: (b, 0, 0, 0, 0)),
        compiler_params=pltpu.CompilerParams(dimension_semantics=("parallel",), vmem_limit_bytes=VMEM_LIMIT),
        name="compress_prompt",
    )(kc.reshape(B, r, width), vc.reshape(B, r, width), w1, p, w2)


def _group_scores(q, k, tq):
    s = lax.dot_general(q, k, (((1,), (1,)), ((), ())), preferred_element_type=F32) * HEAD_DIM ** -0.5
    return s.reshape(GROUP, tq, k.shape[0])


def _stack_heads(ref):
    return jnp.concatenate([ref[:, g * HEAD_DIM:(g + 1) * HEAD_DIM] for g in range(GROUP)], axis=0).astype(BF16)


def _select_blocks(psum, qpos, n_sel):
    tq, r = psum.shape
    ci = lax.broadcasted_iota(jnp.int32, (r, n_sel), 0) * CMP_STRIDE
    cj = lax.broadcasted_iota(jnp.int32, (r, n_sel), 1) * SEL_LEN
    cover_t = ((ci < cj + SEL_LEN) & (ci + (CMP_LEN - 1) >= cj)).astype(BF16)
    p_hi = psum.astype(BF16)
    p_lo = (psum - p_hi.astype(F32)).astype(BF16)
    imp = (jnp.dot(p_hi, cover_t, preferred_element_type=F32) + jnp.dot(p_lo, cover_t, preferred_element_type=F32))
    blk = lax.broadcasted_iota(jnp.int32, (tq, n_sel), 1)
    cur = jnp.right_shift(qpos, SEL_LEN.bit_length() - 1)
    forced = (blk == 0) | (blk == cur) | (blk == cur - 1)
    valid = blk * SEL_LEN <= qpos
    score = jnp.where(valid, imp + jnp.where(forced, FORCE_BONUS, 0.0), -1.0)
    blkf = blk.astype(F32)
    sel = jnp.zeros((tq, n_sel), F32)
    for _ in range(min(SEL_TOP, n_sel)):
        m = jnp.max(score, axis=-1, keepdims=True)
        first = jnp.min(jnp.where(score == m, blkf, float(n_sel)), axis=-1, keepdims=True)
        pick = blkf == first
        sel = jnp.where(pick, 1.0, sel)
        score = jnp.where(pick, -3.0, score)
    return sel


def _nsa_prompt_kernel(qraw_ref, qrot_ref, ks_ref, vs_ref, kw_ref, vw_ref, kc_ref, vc_ref, gate_ref, o_ref,
                       m_ref, l_ref, acc_ref, *, tq, tk, seq_len):
    q0 = pl.program_id(2) * tq
    r = kc_ref.shape[0]
    n_sel = seq_len // SEL_LEN
    sel_shift = SEL_LEN.bit_length() - 1
    qpos = q0 + lax.broadcasted_iota(jnp.int32, (tq, 1), 0)

    s = _group_scores(_stack_heads(qraw_ref), kc_ref[...].astype(BF16), tq)
    c_end = lax.broadcasted_iota(jnp.int32, (tq, r), 1) * CMP_STRIDE + (CMP_LEN - 1)
    cmask = c_end <= qpos
    s = s + jnp.where(cmask, 0.0, NEG)[None]
    p = jnp.exp(s - jnp.max(s, axis=-1, keepdims=True)) * jnp.where(cmask, 1.0, 0.0)[None]
    p = p / jnp.maximum(jnp.sum(p, axis=-1, keepdims=True), 1e-30)
    o_cmp = jnp.dot(p.reshape(GROUP * tq, r).astype(BF16), vc_ref[...].astype(BF16), preferred_element_type=F32)
    psum = p[0]
    for g in range(1, GROUP):
        psum = psum + p[g]
    sel_b = _select_blocks(psum, qpos, n_sel).astype(BF16)

    q_rot = _stack_heads(qrot_ref)
    m_ref[...] = jnp.full(m_ref.shape, NEG, F32)
    l_ref[...] = jnp.zeros(l_ref.shape, F32)
    acc_ref[...] = jnp.zeros(acc_ref.shape, F32)

    def sel_tile(kt, carry):
        k0 = pl.multiple_of(kt * tk, tk)
        k = ks_ref[pl.ds(k0, tk), :].astype(BF16)
        v = vs_ref[pl.ds(k0, tk), :].astype(BF16)
        s2 = _group_scores(q_rot, k, tq)
        ej = lax.broadcasted_iota(jnp.int32, (n_sel, tk), 0)
        et = k0 + lax.broadcasted_iota(jnp.int32, (n_sel, tk), 1)
        expand = jnp.where(jnp.right_shift(et, sel_shift) == ej, 1.0, 0.0).astype(BF16)
        selm = jnp.dot(sel_b, expand, preferred_element_type=F32)
        kpos = k0 + lax.broadcasted_iota(jnp.int32, (tq, tk), 1)
        ok = (selm > 0.5) & (kpos <= qpos)
        s2 = s2 + jnp.where(ok, 0.0, NEG)[None]
        m_prev = m_ref[...]
        m_new = jnp.maximum(m_prev, jnp.max(s2, axis=-1, keepdims=True))
        alpha = jnp.exp(m_prev - m_new)
        p2 = jnp.exp(s2 - m_new)
        l_ref[...] = alpha * l_ref[...] + jnp.sum(p2, axis=-1, keepdims=True)
        pv = jnp.dot(p2.reshape(GROUP * tq, tk).astype(BF16), v, preferred_element_type=F32)
        acc_ref[...] = alpha.reshape(GROUP * tq, 1) * acc_ref[...] + pv
        m_ref[...] = m_new
        return carry

    lax.fori_loop(0, lax.div(q0 + tq + tk - 1, tk), sel_tile, 0)
    o_sel = acc_ref[...] / jnp.maximum(l_ref[...].reshape(GROUP * tq, 1), 1e-30)

    span = WINDOW + tq
    w0 = pl.multiple_of(jnp.maximum(q0 - WINDOW, 0), tq)
    s3 = _group_scores(q_rot, kw_ref[pl.ds(w0, span), :].astype(BF16), tq)
    kp = w0 + lax.broadcasted_iota(jnp.int32, (tq, span), 1)
    okw = (kp <= qpos) & (kp >= qpos - WINDOW)
    s3 = s3 + jnp.where(okw, 0.0, NEG)[None]
    p3 = jnp.exp(s3 - jnp.max(s3, axis=-1, keepdims=True))
    p3 = p3 / jnp.sum(p3, axis=-1, keepdims=True)
    o_win = jnp.dot(p3.reshape(GROUP * tq, span).astype(BF16), vw_ref[pl.ds(w0, span), :].astype(BF16),
                    preferred_element_type=F32)

    gts = jax.nn.sigmoid(gate_ref[...])
    for g in range(GROUP):
        rows = slice(g * tq, (g + 1) * tq)
        o_ref[:, g * HEAD_DIM:(g + 1) * HEAD_DIM] = (gts[:, 3 * g:3 * g + 1] * o_cmp[rows]
                                                     + gts[:, 3 * g + 1:3 * g + 2] * o_sel[rows]
                                                     + gts[:, 3 * g + 2:3 * g + 3] * o_win[rows]).astype(o_ref.dtype)


def _nsa_prompt(z, cmp, B, T, tq=128, tk=512):
    assert T % tk == 0 and tk % tq == 0 and tq % HEAD_DIM == 0 and T >= WINDOW + tq
    assert SEL_LEN & (SEL_LEN - 1) == 0 and tk % SEL_LEN == 0
    nq = T // tq
    r = cmp.shape[3]
    gw = GROUP * HEAD_DIM
    qspec = lambda off: pl.BlockSpec((tq, gw), lambda b, h, q: (b * nq + q, off // gw + h))
    kvspec = lambda off: pl.BlockSpec((T, HEAD_DIM), lambda b, h, q: (b, off // HEAD_DIM + h))
    cspec = lambda which: pl.BlockSpec((None, None, None, r, HEAD_DIM), lambda b, h, q: (b, which, h, 0, 0))
    return pl.pallas_call(
        functools.partial(_nsa_prompt_kernel, tq=tq, tk=tk, seq_len=T),
        out_shape=jax.ShapeDtypeStruct((B * T, A_Q), BF16),
        grid=(B, N_KV, nq),
        in_specs=[qspec(Z_QRAW), qspec(Z_QROT), kvspec(Z_KS), kvspec(Z_VS), kvspec(Z_KW), kvspec(Z_VW),
                  cspec(0), cspec(1),
                  pl.BlockSpec((tq, HEAD_DIM), lambda b, h, q: (b * nq + q, Z_GNSA // HEAD_DIM + h))],
        out_specs=pl.BlockSpec((tq, gw), lambda b, h, q: (b * nq + q, h)),
        scratch_shapes=[pltpu.VMEM((GROUP, tq, 1), F32), pltpu.VMEM((GROUP, tq, 1), F32),
                        pltpu.VMEM((GROUP * tq, HEAD_DIM), F32)],
        compiler_params=pltpu.CompilerParams(dimension_semantics=("parallel", "parallel", "arbitrary"),
                                             vmem_limit_bytes=VMEM_LIMIT),
        name="nsa_prompt",
    )(z, z, z, z, z, z, cmp, cmp, z)


def _gate_logs(i_pre, f_pre):
    log_i = GATE_CAP * jnp.tanh(i_pre / GATE_CAP)
    fc = GATE_CAP * jnp.tanh(f_pre / GATE_CAP)
    log_f = jnp.minimum(fc, 0.0) - jnp.log1p(jnp.exp(-jnp.abs(fc)))
    return log_i, log_f


def _mlstm_prompt_kernel(q_ref, k_ref, v_ref, op_ref, grow_ref, gcol_ref, gain_ref,
                         h_ref, c_ref, n_ref, m_ref):
    L = q_ref.shape[0]

    @pl.when(pl.program_id(2) == 0)
    def _():
        c_ref[...] = jnp.zeros(c_ref.shape, F32)
        n_ref[...] = jnp.zeros(n_ref.shape, F32)
        m_ref[...] = jnp.zeros(m_ref.shape, F32)

    q = q_ref[...]
    qb = q.astype(BF16)
    k = k_ref[...] * DQK_B ** -0.5
    vb = v_ref[...].astype(BF16)
    i_r, f_r = _gate_logs(grow_ref[0:1, :], grow_ref[1:2, :])
    i_c, f_c = _gate_logs(gcol_ref[:, 0:1], gcol_ref[:, 1:2])
    li = lax.broadcasted_iota(jnp.int32, (L, L), 0)
    si = lax.broadcasted_iota(jnp.int32, (L, L), 1)
    tri = si <= li
    bcum_c = jnp.sum(jnp.where(tri, f_r, 0.0), axis=1, keepdims=True)
    bcum_r = jnp.sum(jnp.where(li <= si, f_c, 0.0), axis=0, keepdims=True)
    btot = jnp.sum(f_r, axis=1, keepdims=True)
    c_old, n_old, m_old = c_ref[...], n_ref[...], m_ref[0:1, 0:1]

    dlog = jnp.where(tri, bcum_c - bcum_r + i_r, NEG)
    inter = bcum_c + m_old
    m_t = jnp.maximum(inter, jnp.max(dlog, axis=1, keepdims=True))
    qk = lax.dot_general(qb, k.astype(BF16), (((1,), (1,)), ((), ())), preferred_element_type=F32)
    w_intra = jnp.exp(dlog - m_t) * qk
    w_inter = jnp.exp(inter - m_t)
    num = (w_inter * jnp.dot(qb, c_old.astype(BF16), preferred_element_type=F32)
           + jnp.dot(w_intra.astype(BF16), vb, preferred_element_type=F32))
    den = w_inter * jnp.sum(q * n_old, axis=1, keepdims=True) + jnp.sum(w_intra, axis=1, keepdims=True)
    h = num / jnp.maximum(jnp.abs(den), jnp.exp(-m_t))
    h_ref[...] = (_rms(h, gain_ref[...]) * jax.nn.sigmoid(op_ref[...])).astype(h_ref.dtype)

    wlog_c = btot - bcum_c + i_c
    m_new = jnp.maximum(btot + m_old, jnp.max(btot - bcum_r + i_r, axis=1, keepdims=True))
    decay = jnp.exp(btot + m_old - m_new)
    kw = k * jnp.exp(wlog_c - m_new)
    c_ref[...] = decay * c_old + lax.dot_general(kw.astype(BF16), vb, (((0,), (0,)), ((), ())),
                                                 preferred_element_type=F32)
    n_ref[...] = decay * n_old + jnp.sum(kw, axis=0, keepdims=True)
    m_ref[...] = jnp.broadcast_to(m_new, m_ref.shape)


def _mlstm_prompt(z, gain, B, T):
    L = MLSTM_CHUNK
    assert T % L == 0
    nc, H = T // L, N_HEADS_B
    g = z[:, Z_IB:Z_IB + 2 * H].reshape(B, nc, L, 2, H)
    grow = g.transpose(0, 4, 1, 3, 2).reshape(B * H * nc, 2, L)
    gcol = g.transpose(0, 4, 1, 2, 3).reshape(B * H * nc, L, 2)
    rows = lambda w, off: pl.BlockSpec((L, w), lambda b, h, c: (b * nc + c, off // w + h))
    gidx = lambda b, h, c: ((b * H + h) * nc + c, 0, 0)
    state = lambda *s: pl.BlockSpec((None, None) + s, lambda b, h, c: (b, h, 0, 0))
    h, c, n, m = pl.pallas_call(
        _mlstm_prompt_kernel,
        out_shape=(jax.ShapeDtypeStruct((B * T, B_V), BF16), jax.ShapeDtypeStruct((B, H, DQK_B, DV_B), F32),
                   jax.ShapeDtypeStruct((B, H, 1, DQK_B), F32), jax.ShapeDtypeStruct((B, H, 1, DQK_B), F32)),
        grid=(B, H, nc),
        in_specs=[rows(DQK_B, Z_QB), rows(DQK_B, Z_KB), rows(DV_B, Z_VB), rows(DV_B, Z_OB),
                  pl.BlockSpec((None, 2, L), gidx), pl.BlockSpec((None, L, 2), gidx),
                  pl.BlockSpec((None, 1, DV_B), lambda b, h, c: (h, 0, 0))],
        out_specs=(rows(DV_B, 0), state(DQK_B, DV_B), state(1, DQK_B), state(1, DQK_B)),
        compiler_params=pltpu.CompilerParams(dimension_semantics=("parallel", "parallel", "arbitrary"),
                                             vmem_limit_bytes=VMEM_LIMIT),
        name="mlstm_prompt",
    )(z, z, z, z, grow, gcol, gain.reshape(H, 1, DV_B))
    return h, c, n[:, :, 0], m[:, :, 0, 0]


def _merge_kernel(x_ref, oa_ref, hb_ref, gma_ref, gmb_ref, wa_ref, wb_ref, wo_ref, nf_ref, wrh_ref, wrl_ref, br_ref,
                  *refs):
    x2_ref, xn_ref, route_ref = refs[-3:]
    ua = jnp.dot(oa_ref[...], wa_ref[...], preferred_element_type=F32)
    ub = jnp.dot(hb_ref[...], wb_ref[...], preferred_element_type=F32)
    merged = jax.nn.sigmoid(gma_ref[...]) * ua + jax.nn.sigmoid(gmb_ref[...]) * ub
    x2 = x_ref[...] + jnp.dot(merged.astype(BF16), wo_ref[...], preferred_element_type=F32)
    x2_ref[...] = x2
    xn = _rms(x2, nf_ref[...])
    xn_ref[...] = xn
    hi = xn.astype(BF16)
    lo = (xn - hi.astype(F32)).astype(BF16)
    lg = (jnp.dot(hi, wrh_ref[...], preferred_element_type=F32) + jnp.dot(lo, wrh_ref[...], preferred_element_type=F32)
          + jnp.dot(hi, wrl_ref[...], preferred_element_type=F32)) + br_ref[...]
    lane = lax.broadcasted_iota(jnp.int32, lg.shape, 1)
    lanef = lane.astype(F32)
    ninf = -jnp.inf
    gmask = (lane >= N_EXPERTS) & (lane < N_EXPERTS + N_GROUPS)
    gl = jnp.where(gmask, lg, ninf)
    gmax = jnp.max(gl, axis=1, keepdims=True)
    gidx = jnp.min(jnp.where(gl == gmax, lanef, 1e9), axis=1, keepdims=True) - N_EXPERTS
    gprob = 1.0 / jnp.sum(jnp.where(gmask, jnp.exp(lg - gmax), 0.0), axis=1, keepdims=True)
    e_lo = gidx * EXPERTS_PER_GROUP
    el = jnp.where((lanef >= e_lo) & (lanef < e_lo + EXPERTS_PER_GROUP), lg, ninf)
    v1 = jnp.max(el, axis=1, keepdims=True)
    i1 = jnp.min(jnp.where(el == v1, lanef, 1e9), axis=1, keepdims=True)
    el2 = jnp.where(lanef == i1, ninf, el)
    v2 = jnp.max(el2, axis=1, keepdims=True)
    i2 = jnp.min(jnp.where(el2 == v2, lanef, 1e9), axis=1, keepdims=True)
    e2 = jnp.exp(v2 - v1)
    g1 = gprob / (1.0 + e2)
    g2 = gprob * e2 / (1.0 + e2)
    route_ref[...] = jnp.where(lane == 0, i1, jnp.where(lane == 1, i2, jnp.where(lane == 2, g1,
                               jnp.where(lane == 3, g2, 0.0))))


def _merge_weights(w_up_a, w_up_b, w_out, norm_ffn, w_rg, b_rg, w_re, b_re):
    assert TOP_K == 2
    pad = ROUTE_W - N_EXPERTS - N_GROUPS
    w = jnp.concatenate([w_re, w_rg, jnp.zeros((D_MODEL, pad), F32)], axis=1)
    b = jnp.concatenate([b_re, b_rg, jnp.zeros((pad,), F32)]).reshape(1, ROUTE_W)
    hi = w.astype(BF16)
    return (w_up_a.astype(BF16), w_up_b.astype(BF16), w_out.astype(BF16), norm_ffn.reshape(1, D_MODEL).astype(F32),
            hi, (w - hi.astype(F32)).astype(BF16), b)


def _merge(x2d, o_a, h_b, z, mw, n_total, row0, tm, bufs=None):
    m = x2d.shape[0]
    i0 = row0 // tm
    row = lambda w, c=0: pl.BlockSpec((tm, w), lambda i: (i, c))
    const = lambda a: pl.BlockSpec(a.shape, lambda i: (0,) * a.ndim, pipeline_mode=pl.Buffered(1))
    outw = (D_MODEL, D_MODEL, ROUTE_W)
    in_specs = [row(D_MODEL), row(A_Q), row(B_V), row(D_MODEL, Z_GMA // D_MODEL), row(D_MODEL, Z_GMB // D_MODEL)]
    in_specs += [const(a) for a in mw]
    args = [x2d, o_a, h_b, z, z] + list(mw)
    aliases = {}
    if bufs is not None:
        in_specs += [pl.BlockSpec(memory_space=pl.ANY)] * 3
        aliases = {len(args) + k: k for k in range(3)}
        args += list(bufs)
    return pl.pallas_call(
        _merge_kernel,
        out_shape=tuple(jax.ShapeDtypeStruct((n_total, w), F32) for w in outw),
        grid=(m // tm,),
        in_specs=in_specs,
        out_specs=tuple(pl.BlockSpec((tm, w), lambda i: (i0 + i, 0)) for w in outw),
        input_output_aliases=aliases,
        compiler_params=pltpu.CompilerParams(dimension_semantics=("parallel",), vmem_limit_bytes=VMEM_LIMIT_MERGE),
        name="merge_route",
    )(*args)


def _moe_plan(route, n_tiles):
    n = route.shape[0]
    a = n * TOP_K
    flat_e = route[:, 0:TOP_K].astype(jnp.int32).reshape(a)
    onehot = (flat_e[:, None] == jnp.arange(N_EXPERTS, dtype=jnp.int32)[None, :]).astype(jnp.int32)
    rank = jnp.take_along_axis(jnp.cumsum(onehot, axis=0) - onehot, flat_e[:, None], axis=1)[:, 0]
    counts = jnp.sum(onehot, axis=0)
    ntile = (counts + MOE_TILE - 1) // MOE_TILE
    tile_end = jnp.cumsum(ntile)
    dest = ((tile_end - ntile)[flat_e] * MOE_TILE + rank).astype(jnp.int32)
    n_used = tile_end[-1].astype(jnp.int32)
    t = jnp.minimum(jnp.arange(n_tiles, dtype=jnp.int32), n_used - 1)
    tile_expert = jnp.minimum(jnp.searchsorted(tile_end, t, side='right'), N_EXPERTS - 1).astype(jnp.int32)
    src = jnp.full((n_tiles * MOE_TILE,), -1, jnp.int32).at[dest].set(jnp.arange(a, dtype=jnp.int32) // TOP_K)
    return dest, src, tile_expert, n_used.reshape(1)


def _row_gather_kernel(src_ref, nu_ref, x_hbm, o_ref, sem):
    t = pl.program_id(0)
    rows = o_ref.shape[0]

    def copy(r):
        s = src_ref[t * rows + r]
        return s, pltpu.make_async_copy(x_hbm.at[pl.ds(jnp.maximum(s, 0), 1)], o_ref.at[pl.ds(r, 1)], sem)

    @pl.when(t < nu_ref[0])
    def _():
        def start(r, c):
            s, cp = copy(r)

            @pl.when(s >= 0)
            def _():
                cp.start()
            return c

        def wait(r, c):
            s, cp = copy(r)

            @pl.when(s >= 0)
            def _():
                cp.wait()
            return c

        lax.fori_loop(0, rows, start, 0)
        lax.fori_loop(0, rows, wait, 0)


def _moe_gather(xn, src, n_used, n_tiles):
    d = xn.shape[1]
    return pl.pallas_call(
        _row_gather_kernel,
        out_shape=jax.ShapeDtypeStruct((n_tiles * MOE_TILE, d), xn.dtype),
        grid_spec=pltpu.PrefetchScalarGridSpec(
            num_scalar_prefetch=2, grid=(n_tiles,),
            in_specs=[pl.BlockSpec(memory_space=pl.ANY)],
            out_specs=pl.BlockSpec((MOE_TILE, d), lambda t, src, nu: (jnp.minimum(t, nu[0] - 1), 0)),
            scratch_shapes=[pltpu.SemaphoreType.DMA(())]),
        compiler_params=pltpu.CompilerParams(dimension_semantics=("arbitrary",), vmem_limit_bytes=VMEM_LIMIT),
        name="moe_gather",
    )(src, n_used, xn)


def _expert_kernel(te_ref, nu_ref, x_ref, wg_ref, wu_ref, wd_ref, o_ref, xb_ref):
    t, j = pl.program_id(0), pl.program_id(1)

    @pl.when(t < nu_ref[0])
    def _():
        @pl.when(j == 0)
        def _():
            xb_ref[...] = x_ref[...].astype(BF16)

        xb = xb_ref[...]
        g = jnp.dot(xb, wg_ref[...].astype(BF16), preferred_element_type=F32)
        u = jnp.dot(xb, wu_ref[...].astype(BF16), preferred_element_type=F32)
        h = (g * jax.nn.sigmoid(g) * u).astype(BF16)
        y = jnp.dot(h, wd_ref[...].astype(BF16), preferred_element_type=F32)

        @pl.when(j == 0)
        def _():
            o_ref[...] = y

        @pl.when(j > 0)
        def _():
            o_ref[...] += y


def _moe_experts(xs, tile_expert, n_used, w_gate, w_up, w_down):
    n_tiles = xs.shape[0] // MOE_TILE
    nj = D_EXPERT // MOE_FC
    tix = lambda t, j, te, nu: (jnp.minimum(t, nu[0] - 1), 0)
    jj = lambda t, j, nu: jnp.where(t < nu[0], j, nj - 1)
    return pl.pallas_call(
        _expert_kernel,
        out_shape=jax.ShapeDtypeStruct(xs.shape, F32),
        grid_spec=pltpu.PrefetchScalarGridSpec(
            num_scalar_prefetch=2, grid=(n_tiles, nj),
            in_specs=[pl.BlockSpec((MOE_TILE, D_MODEL), tix),
                      pl.BlockSpec((None, D_MODEL, MOE_FC), lambda t, j, te, nu: (te[t], 0, jj(t, j, nu))),
                      pl.BlockSpec((None, D_MODEL, MOE_FC), lambda t, j, te, nu: (te[t], 0, jj(t, j, nu))),
                      pl.BlockSpec((None, MOE_FC, D_MODEL), lambda t, j, te, nu: (te[t], jj(t, j, nu), 0))],
            out_specs=pl.BlockSpec((MOE_TILE, D_MODEL), tix),
            scratch_shapes=[pltpu.VMEM((MOE_TILE, D_MODEL), BF16)]),
        compiler_params=pltpu.CompilerParams(dimension_semantics=("arbitrary", "arbitrary"),
                                             vmem_limit_bytes=VMEM_LIMIT),
        name="moe_experts",
    )(tile_expert, n_used, xs, w_gate, w_up, w_down)


def _combine_kernel(dest_ref, x2_ref, route_ref, g_ref, y_hbm, op_ref, os_ref, buf_ref, sem, *, n_prompt_tiles):
    i = pl.program_id(0)
    tm = x2_ref.shape[0]

    def copy(r, k):
        d = dest_ref[(i * tm + r) * TOP_K + k]
        return pltpu.make_async_copy(y_hbm.at[pl.ds(d, 1)], buf_ref.at[k, pl.ds(r, 1)], sem)

    def start(r, c):
        for k in range(TOP_K):
            copy(r, k).start()
        return c

    def wait(r, c):
        for k in range(TOP_K):
            copy(r, k).wait()
        return c

    lax.fori_loop(0, tm, start, 0)
    lax.fori_loop(0, tm, wait, 0)
    ffn = buf_ref[0] * route_ref[:, TOP_K:TOP_K + 1]
    for k in range(1, TOP_K):
        ffn = ffn + buf_ref[k] * route_ref[:, TOP_K + k:TOP_K + k + 1]
    y = _rms(x2_ref[...] + ffn, g_ref[...])

    @pl.when(i < n_prompt_tiles)
    def _():
        op_ref[...] = y

    @pl.when(i >= n_prompt_tiles)
    def _():
        os_ref[...] = y


def _moe_combine(x2, route, dest, ys, norm_final, n_prompt, tm=128):
    n, d = x2.shape
    npt = n_prompt // tm
    nst = (n - n_prompt) // tm
    return pl.pallas_call(
        functools.partial(_combine_kernel, n_prompt_tiles=npt),
        out_shape=(jax.ShapeDtypeStruct((n_prompt, d), F32), jax.ShapeDtypeStruct((n - n_prompt, d), F32)),
        grid_spec=pltpu.PrefetchScalarGridSpec(
            num_scalar_prefetch=1, grid=(n // tm,),
            in_specs=[pl.BlockSpec((tm, d), lambda i, ds: (i, 0)), pl.BlockSpec((tm, ROUTE_W), lambda i, ds: (i, 0)),
                      pl.BlockSpec((1, d), lambda i, ds: (0, 0)), pl.BlockSpec(memory_space=pl.ANY)],
            out_specs=(pl.BlockSpec((tm, d), lambda i, ds: (jnp.minimum(i, npt - 1), 0)),
                       pl.BlockSpec((tm, d), lambda i, ds: (jnp.clip(i - npt, 0, nst - 1), 0))),
            scratch_shapes=[pltpu.VMEM((TOP_K, tm, d), F32), pltpu.SemaphoreType.DMA(())]),
        compiler_params=pltpu.CompilerParams(dimension_semantics=("arbitrary",), vmem_limit_bytes=VMEM_LIMIT),
        name="moe_combine",
    )(dest, x2, route, norm_final.reshape(1, d), ys)


def _moe(x2, xn, route, w_gate, w_up, w_down, norm_final, n_prompt):
    n = x2.shape[0]
    n_tiles = (n * TOP_K + N_EXPERTS * (MOE_TILE - 1) + MOE_TILE - 1) // MOE_TILE
    dest, src, tile_expert, n_used = _moe_plan(route, n_tiles)
    xs = _moe_gather(xn, src, n_used, n_tiles)
    ys = _moe_experts(xs, tile_expert, n_used, w_gate, w_up, w_down)
    return _moe_combine(x2, route, dest, ys, norm_final, n_prompt)


POOL_ROWS = 256
ROWS_PER_PAGE = PAGE_SIZE // CMP_STRIDE
ZW = 2 * N_KV * HEAD_DIM
QROWS = 8


def _compress_pool_kernel(xk_ref, xv_ref, w1_ref, p_ref, zk_ref, zv_ref):
    for which, (x_ref, z_ref) in enumerate(((xk_ref, zk_ref), (xv_ref, zv_ref))):
        w1 = w1_ref[which]
        pb = jnp.dot(p_ref[which], w1, preferred_element_type=F32)
        bias = pb[0:1, :HEAD_DIM] + pb[1:2, HEAD_DIM:]
        for h in range(N_KV):
            xh = jnp.concatenate([x_ref[pl.ds(N_KV * s + h, POOL_ROWS, stride=N_KV * CMP_STRIDE), :]
                                  for s in range(CMP_STRIDE)], axis=1).astype(BF16)
            zz = jnp.dot(xh, w1, preferred_element_type=F32)
            z_ref[:, 2 * h * HEAD_DIM:(2 * h + 1) * HEAD_DIM] = zz[:, :HEAD_DIM] + bias
            z_ref[:, (2 * h + 1) * HEAD_DIM:(2 * h + 2) * HEAD_DIM] = zz[:, HEAD_DIM:]


def _compress_pool(pool_k, pool_v, cw):
    n_phys = pool_k.shape[0]
    rows = n_phys * ROWS_PER_PAGE
    assert rows % POOL_ROWS == 0
    in_rows = POOL_ROWS * CMP_STRIDE * N_KV
    w1, p, _ = cw
    full = lambda a: pl.BlockSpec(a.shape, lambda i: (0,) * a.ndim)
    flat = lambda a: a.reshape(n_phys * PAGE_SIZE * N_KV, HEAD_DIM)
    zk, zv = pl.pallas_call(
        _compress_pool_kernel,
        out_shape=(jax.ShapeDtypeStruct((rows, ZW), F32),) * 2,
        grid=(rows // POOL_ROWS,),
        in_specs=[pl.BlockSpec((in_rows, HEAD_DIM), lambda i: (i, 0))] * 2 + [full(w1), full(p)],
        out_specs=(pl.BlockSpec((POOL_ROWS, ZW), lambda i: (i, 0)),) * 2,
        compiler_params=pltpu.CompilerParams(dimension_semantics=("parallel",), vmem_limit_bytes=VMEM_LIMIT),
        name="compress_pool",
    )(flat(pool_k), flat(pool_v), w1, p)
    return zk.reshape(n_phys, ROWS_PER_PAGE, ZW), zv.reshape(n_phys, ROWS_PER_PAGE, ZW)


def _nsa_sample_kernel(pt_ref, qraw_ref, qrot_ref, knew_ref, wnew_ref, gate_ref, w2_ref, wk_ref, wv_ref,
                       zk_hbm, zv_hbm, ks_hbm, vs_hbm, o_ref, zbuf, sbuf, sem, *, n_pages, past_len):
    b = pl.program_id(0)
    nb = pl.num_programs(0)
    slot = lax.rem(b, 2)
    page_rows = N_KV * PAGE_SIZE

    def copies(seq, sl):
        out = []
        for pg in range(n_pages):
            page = pt_ref[seq, pg]
            for which, (zsrc, ssrc) in enumerate(((zk_hbm, ks_hbm), (zv_hbm, vs_hbm))):
                out.append(pltpu.make_async_copy(
                    zsrc.at[page], zbuf.at[sl, which, pl.ds(pg * ROWS_PER_PAGE, ROWS_PER_PAGE)], sem.at[sl]))
                out.append(pltpu.make_async_copy(
                    ssrc.at[page], sbuf.at[sl, which, pl.ds(pg * page_rows, page_rows)], sem.at[sl]))
        return out

    @pl.when(b == 0)
    def _():
        for cp in copies(0, 0):
            cp.start()

    @pl.when(b + 1 < nb)
    def _():
        for cp in copies(b + 1, 1 - slot):
            cp.start()

    for cp in copies(b, slot):
        cp.wait()

    scale = HEAD_DIM ** -0.5
    nq = N_KV * GROUP
    r = n_pages * ROWS_PER_PAGE
    qpos = jnp.full((nq, 1), past_len, jnp.int32)
    n_sel_pad = HEAD_DIM
    sel_shift = SEL_LEN.bit_length() - 1
    row_head = lax.broadcasted_iota(jnp.int32, (nq, 1), 0) // GROUP
    per_head = lambda a0, a1: jnp.where(row_head == 0, a0, a1)

    def all_heads(ref):
        return jnp.concatenate([ref[:, j * HEAD_DIM:(j + 1) * HEAD_DIM] for j in range(nq)], axis=0).astype(BF16)

    def scores(q, k):
        return lax.dot_general(q, k, (((1,), (1,)), ((), ())), preferred_element_type=F32) * scale

    cmp = []
    for which in range(2):
        heads = []
        for h in range(N_KV):
            zh = zbuf[slot, which, :, 2 * h * HEAD_DIM:(2 * h + 2) * HEAD_DIM]
            pre = zh[:, :HEAD_DIM] + pltpu.roll(zh[:, HEAD_DIM:], r - 1, 0)
            mid = pre * jax.nn.sigmoid(pre)
            heads.append(jnp.dot(mid.astype(BF16), w2_ref[which], preferred_element_type=F32).astype(BF16))
        cmp.append(heads)
    q_raw = all_heads(qraw_ref)
    c_end = lax.broadcasted_iota(jnp.int32, (nq, r), 1) * CMP_STRIDE + (CMP_LEN - 1)
    cmask = c_end <= qpos
    s = per_head(scores(q_raw, cmp[0][0]), scores(q_raw, cmp[0][1])) + jnp.where(cmask, 0.0, NEG)
    p = jnp.exp(s - jnp.max(s, axis=-1, keepdims=True)) * jnp.where(cmask, 1.0, 0.0)
    p = p / jnp.maximum(jnp.sum(p, axis=-1, keepdims=True), 1e-30)
    pb = p.astype(BF16)
    o_cmp = per_head(jnp.dot(pb, cmp[1][0], preferred_element_type=F32),
                     jnp.dot(pb, cmp[1][1], preferred_element_type=F32))
    psum = per_head(jnp.sum(jnp.where(row_head == 0, p, 0.0), axis=0, keepdims=True),
                    jnp.sum(jnp.where(row_head == 1, p, 0.0), axis=0, keepdims=True))
    sel = _select_blocks(psum, qpos, n_sel_pad)

    def attend(q, k, v, k_new, v_new, bias, new_bias):
        s = scores(q, k) + bias
        s_new = jnp.sum(q.astype(F32) * k_new.astype(BF16).astype(F32), axis=1, keepdims=True) * scale + new_bias
        m = jnp.maximum(jnp.max(s, axis=-1, keepdims=True), s_new)
        p = jnp.exp(s - m)
        p_new = jnp.exp(s_new - m)
        l = jnp.sum(p, axis=-1, keepdims=True) + p_new
        o = (jnp.dot(p.astype(BF16), v, preferred_element_type=F32)
             + p_new.astype(BF16).astype(F32) * v_new.astype(BF16).astype(F32))
        return o / jnp.maximum(l, 1e-30)

    def own_head(n_rows):
        col = lax.broadcasted_iota(jnp.int32, (nq, n_rows), 1)
        return jnp.bitwise_and(col, N_KV - 1) == row_head

    q_rot = all_heads(qrot_ref)
    n_rows = N_KV * past_len
    ej = lax.broadcasted_iota(jnp.int32, (n_sel_pad, n_rows), 0)
    et = lax.broadcasted_iota(jnp.int32, (n_sel_pad, n_rows), 1)
    expand = jnp.where(jnp.right_shift(et, sel_shift + 1) == ej, 1.0, 0.0).astype(BF16)
    selm = jnp.dot(sel.astype(BF16), expand, preferred_element_type=F32)
    sel_bias = jnp.where((selm > 0.5) & own_head(n_rows), 0.0, NEG)
    new_blk = past_len // SEL_LEN
    new_bias = jnp.where(sel[:, new_blk:new_blk + 1] > 0.5, 0.0, NEG)
    new_kv = lambda ref, c: per_head(ref[:, c * A_KV:c * A_KV + HEAD_DIM], ref[:, c * A_KV + HEAD_DIM:(c + 1) * A_KV])
    o_sel = attend(q_rot, sbuf[slot, 0].astype(BF16), sbuf[slot, 1].astype(BF16),
                   new_kv(knew_ref, 0), new_kv(knew_ref, 1), sel_bias, new_bias)
    win_bias = jnp.where(own_head(wk_ref.shape[0]), 0.0, NEG)
    o_win = attend(q_rot, wk_ref[...].astype(BF16), wv_ref[...].astype(BF16),
                   new_kv(wnew_ref, 0), new_kv(wnew_ref, 1), win_bias, 0.0)

    gts = jax.nn.sigmoid(gate_ref[...])
    for h in range(N_KV):
        for g in range(GROUP):
            j, c0 = h * GROUP + g, h * HEAD_DIM + 3 * g
            o = (gts[:, c0:c0 + 1] * o_cmp[j:j + 1] + gts[:, c0 + 1:c0 + 2] * o_sel[j:j + 1]
                 + gts[:, c0 + 2:c0 + 3] * o_win[j:j + 1])
            o_ref[:, j * HEAD_DIM:(j + 1) * HEAD_DIM] = o.astype(o_ref.dtype)


def _nsa_sample(zs, page_table, zk_pool, zv_pool, sel_k, sel_v, win_k, win_v, w2):
    bs, n_pages = page_table.shape
    past_len = n_pages * PAGE_SIZE
    wb = win_k.shape[1]
    assert wb <= WINDOW and past_len % SEL_LEN == 0 and past_len // SEL_LEN < HEAD_DIM
    assert N_KV == 2
    zs3 = zs.reshape(bs, 1, NZ)
    zspec = lambda w, off: pl.BlockSpec((None, 1, w), lambda b, pt: (b, 0, off // w))
    anyspec = pl.BlockSpec(memory_space=pl.ANY)
    wspec = pl.BlockSpec((None, N_KV * wb, HEAD_DIM), lambda b, pt: (b, 0, 0))
    rows2d = lambda a: a.reshape(a.shape[0], a.shape[1] * N_KV, HEAD_DIM)
    out = pl.pallas_call(
        functools.partial(_nsa_sample_kernel, n_pages=n_pages, past_len=past_len),
        out_shape=jax.ShapeDtypeStruct((bs, 1, A_Q), BF16),
        grid_spec=pltpu.PrefetchScalarGridSpec(
            num_scalar_prefetch=1, grid=(bs,),
            in_specs=[zspec(A_Q, Z_QRAW), zspec(A_Q, Z_QROT), zspec(2 * A_KV, Z_KS), zspec(2 * A_KV, Z_KW),
                      zspec(2 * HEAD_DIM, Z_GNSA), pl.BlockSpec(w2.shape, lambda b, pt: (0, 0, 0)), wspec, wspec,
                      anyspec, anyspec, anyspec, anyspec],
            out_specs=pl.BlockSpec((None, 1, A_Q), lambda b, pt: (b, 0, 0)),
            scratch_shapes=[pltpu.VMEM((2, 2, n_pages * ROWS_PER_PAGE, ZW), F32),
                            pltpu.VMEM((2, 2, N_KV * past_len, HEAD_DIM), F32), pltpu.SemaphoreType.DMA((2,))]),
        compiler_params=pltpu.CompilerParams(dimension_semantics=("arbitrary",), vmem_limit_bytes=VMEM_LIMIT),
        name="nsa_sample",
    )(page_table, zs3, zs3, zs3, zs3, zs3, w2, rows2d(win_k), rows2d(win_v), zk_pool, zv_pool,
      rows2d(sel_k), rows2d(sel_v))
    return out.reshape(bs, A_Q)


def _mlstm_sample_kernel(q_ref, k_ref, v_ref, op_ref, g_ref, gain_ref, c_ref, n_ref, m_ref,
                         h_ref, co_ref, no_ref, mo_ref):
    eye = (lax.broadcasted_iota(jnp.int32, (DQK_B, DQK_B), 0) == lax.broadcasted_iota(jnp.int32, (DQK_B, DQK_B), 1))
    col = lambda row: jnp.sum(jnp.where(eye, row, 0.0), axis=1, keepdims=True)
    lane = lax.broadcasted_iota(jnp.int32, (1, N_HEADS_B), 1)
    m_out = jnp.zeros((1, N_HEADS_B), F32)
    for h in range(N_HEADS_B):
        q = q_ref[:, h * DQK_B:(h + 1) * DQK_B]
        k = k_ref[:, h * DQK_B:(h + 1) * DQK_B] * DQK_B ** -0.5
        v = v_ref[:, h * DV_B:(h + 1) * DV_B]
        log_i, log_f = _gate_logs(g_ref[:, h:h + 1], g_ref[:, N_HEADS_B + h:N_HEADS_B + h + 1])
        c_old, n_old, m_old = c_ref[h], n_ref[h:h + 1, :], m_ref[:, h:h + 1]
        inter = log_f + m_old
        m_new = jnp.maximum(inter, log_i)
        decay = jnp.exp(inter - m_new)
        w = jnp.exp(log_i - m_new)
        co_ref[h] = decay * c_old + col(w * k) * v
        no_ref[h:h + 1, :] = decay * n_old + w * k
        m_out = jnp.where(lane == h, m_new, m_out)
        w_intra = w * jnp.sum(q * k, axis=1, keepdims=True)
        num = decay * jnp.sum(c_old * col(q), axis=0, keepdims=True) + w_intra * v
        den = decay * jnp.sum(q * n_old, axis=1, keepdims=True) + w_intra
        hh = num / jnp.maximum(jnp.abs(den), jnp.exp(-m_new))
        out = _rms(hh, gain_ref[h]) * jax.nn.sigmoid(op_ref[:, h * DV_B:(h + 1) * DV_B])
        h_ref[:, h * DV_B:(h + 1) * DV_B] = out.astype(h_ref.dtype)
    mo_ref[...] = m_out


def _mlstm_sample(zs, gain, c0, n0, m0):
    bs, H = zs.shape[0], N_HEADS_B
    zs3 = zs.reshape(bs, 1, NZ)
    zspec = lambda w, off: pl.BlockSpec((None, 1, w), lambda b: (b, 0, off // w))
    cspec = pl.BlockSpec((None, H, DQK_B, DV_B), lambda b: (b, 0, 0, 0))
    nspec = pl.BlockSpec((None, H, DQK_B), lambda b: (b, 0, 0))
    mspec = pl.BlockSpec((None, 1, H), lambda b: (b, 0, 0))
    h, c, n, m = pl.pallas_call(
        _mlstm_sample_kernel,
        out_shape=(jax.ShapeDtypeStruct((bs, 1, B_V), BF16), jax.ShapeDtypeStruct(c0.shape, F32),
                   jax.ShapeDtypeStruct(n0.shape, F32), jax.ShapeDtypeStruct((bs, 1, H), F32)),
        grid=(bs,),
        in_specs=[zspec(B_QK, Z_QB), zspec(B_QK, Z_KB), zspec(B_V, Z_VB), zspec(B_V, Z_OB), zspec(HEAD_DIM, Z_IB),
                  pl.BlockSpec((H, 1, DV_B), lambda b: (0, 0, 0)), cspec, nspec, mspec],
        out_specs=(pl.BlockSpec((None, 1, B_V), lambda b: (b, 0, 0)), cspec, nspec, mspec),
        compiler_params=pltpu.CompilerParams(dimension_semantics=("parallel",), vmem_limit_bytes=VMEM_LIMIT),
        name="mlstm_sample",
    )(zs3, zs3, zs3, zs3, zs3, gain.reshape(H, 1, DV_B), c0, n0, m0.reshape(bs, 1, H))
    return h.reshape(bs, B_V), c, n, m.reshape(bs, H)


def _split_kv(z, B, T):
    z = z.reshape(B, T, NZ)
    kv = lambda o: z[..., o:o + A_KV].reshape(B, T, N_KV, HEAD_DIM)
    return kv(Z_KC), kv(Z_VC), kv(Z_KS), kv(Z_VS), kv(Z_KW), kv(Z_VW)


def kernel(x_prompt, x_sample, cache_cmp_k, cache_cmp_v, cache_sel_k, cache_sel_v, cache_win_k, cache_win_v,
           state_mlstm_c, state_mlstm_n, state_mlstm_m, page_table, norm_mix, w_in, b_in, cmp_pos, cmp_w1, cmp_w2,
           mlstm_norm, w_up_a, w_up_b, w_out, norm_ffn, w_router_grp, b_router_grp, w_router_exp, b_router_exp,
           w_gate, w_up, w_down, norm_final):
    assert w_in.shape[0] == 1, "single layer"
    Bp, Tp = x_prompt.shape[:2]
    Bs, Ts = x_sample.shape[:2]
    assert Ts == 1
    n_p, n_s = Bp * Tp, Bs * Ts
    past_len = page_table.shape[1] * PAGE_SIZE
    pos_p = jnp.arange(Tp, dtype=jnp.int32)
    pos_s = past_len + jnp.arange(Ts, dtype=jnp.int32)
    l = 0

    w_perm, b_perm = _permute_in_proj(w_in[l], b_in[l])
    cw = _compress_weights(cmp_pos[l], cmp_w1[l], cmp_w2[l])
    mw = _merge_weights(w_up_a[l], w_up_b[l], w_out[l], norm_ffn[l], w_router_grp[l], b_router_grp[l],
                        w_router_exp[l], b_router_exp[l])

    xp2d = x_prompt.reshape(n_p, D_MODEL)
    zp = _project(xp2d, jnp.tile(pos_p, Bp), norm_mix[l], w_perm, b_perm, 512)
    kc, vc, ks, vs, kw, vw = _split_kv(zp, Bp, Tp)
    o_a = _nsa_prompt(zp, _compress_prompt(kc, vc, cw), Bp, Tp)
    h_b, c_p, n_p_state, m_p = _mlstm_prompt(zp, mlstm_norm[l], Bp, Tp)
    bufs = _merge(xp2d, o_a, h_b, zp, mw, n_p + n_s, 0, 256)
    wl = min(WINDOW, Tp)
    prompt_new = (kc, vc, ks, vs, kw[:, -wl:], vw[:, -wl:], c_p, n_p_state, m_p)

    xs2d = x_sample.reshape(n_s, D_MODEL)
    zs = _project(xs2d, jnp.repeat(pos_s, Bs), norm_mix[l], w_perm, b_perm, 128)
    kc, vc, ks, vs, kw, vw = _split_kv(zs, Bs, Ts)
    assert (past_len + Ts - CMP_LEN) // CMP_STRIDE + 1 == past_len // CMP_STRIDE - 1
    zk_pool, zv_pool = _compress_pool(cache_cmp_k[l], cache_cmp_v[l], cw)
    o_a_s = _nsa_sample(zs, page_table, zk_pool, zv_pool, cache_sel_k[l], cache_sel_v[l],
                        cache_win_k[l], cache_win_v[l], cw[2])
    h_s, c_s, n_s_state, m_s = _mlstm_sample(zs, mlstm_norm[l], state_mlstm_c[l], state_mlstm_n[l],
                                             state_mlstm_m[l])
    x2, xn, route = _merge(xs2d, o_a_s, h_s, zs, mw, n_p + n_s, n_p, 128, bufs=bufs)
    wk = jnp.concatenate([cache_win_k[l], kw], axis=1)
    wv = jnp.concatenate([cache_win_v[l], vw], axis=1)
    wl = min(WINDOW, past_len + Ts)
    sample_new = (kc, vc, ks, vs, wk[:, -wl:], wv[:, -wl:], c_s, n_s_state, m_s)

    y_p, y_s = _moe(x2, xn, route, w_gate[l], w_up[l], w_down[l], norm_final, n_p)
    return ((y_p.reshape(Bp, Tp, D_MODEL), y_s.reshape(Bs, Ts, D_MODEL))
            + tuple(a[None] for a in prompt_new) + tuple(a[None] for a in sample_new))
```

```python
import functools

import numpy as np
import jax
import jax.numpy as jnp
from jax import lax
from jax.experimental import pallas as pl
from jax.experimental.pallas import tpu as pltpu

D_MODEL = 2048
PAGE_SIZE = 128
N_HEADS_A = 8
N_KV = 2
GROUP = N_HEADS_A // N_KV
HEAD_DIM = 128
ROT_DIM = HEAD_DIM // 4
ROPE_THETA = 500000.0
CMP_LEN = 32
CMP_STRIDE = 16
SEL_LEN = 64
SEL_TOP = 16
FORCE_BONUS = 100.0
WINDOW = 512
Q_BLOCK = 128
N_HEADS_B = 4
DQK_B = 128
DV_B = 256
MLSTM_CHUNK = 64
GATE_CAP = 15.0
N_GROUPS = 4
EXPERTS_PER_GROUP = 8
N_EXPERTS = N_GROUPS * EXPERTS_PER_GROUP
TOP_K = 2
D_EXPERT = 1024
EPS = 1e-6
NEG = -1e30

A_Q = N_HEADS_A * HEAD_DIM
A_KV = N_KV * HEAD_DIM
B_QK = N_HEADS_B * DQK_B
B_V = N_HEADS_B * DV_B
IN_SPLITS = (A_Q, A_KV, A_KV, A_KV, A_KV, A_KV, A_KV, 3 * N_HEADS_A,
             B_QK, B_QK, B_V, N_HEADS_B, N_HEADS_B, B_V, D_MODEL, D_MODEL)

F32 = jnp.float32
BF16 = jnp.bfloat16
VMEM_LIMIT = 48 * 1024 * 1024
VMEM_LIMIT_MERGE = 56 * 1024 * 1024

_CUTS = np.concatenate([[0], np.cumsum(IN_SPLITS)]).tolist()
_SRC = dict(zip(('q_a', 'k_c', 'v_c', 'k_s', 'v_s', 'k_w', 'v_w', 'g_nsa', 'q_b', 'k_b', 'v_b', 'i_b', 'f_b',
                 'o_b', 'g_ma', 'g_mb'), zip(_CUTS[:-1], _CUTS[1:])))
_Z_ORDER = ('g_ma', 'g_mb', 'q_a', 'q_a', 'v_b', 'o_b', 'k_c', 'v_c', 'k_s', 'v_s', 'k_w', 'v_w', 'q_b', 'k_b')
PROJ_TN = 512
Z_GMA, Z_GMB, Z_QRAW, Z_QROT, Z_VB, Z_OB = 0, 2048, 4096, 5120, 6144, 7168
Z_KC, Z_VC, Z_KS, Z_VS, Z_KW, Z_VW = 8192, 8448, 8704, 8960, 9216, 9472
Z_QB, Z_KB, Z_SMALL = 9728, 10240, 10752
N_GATE = 3 * GROUP
Z_GNSA, Z_IB, Z_FB = Z_SMALL, Z_SMALL + 2 * HEAD_DIM, Z_SMALL + 2 * HEAD_DIM + N_HEADS_B
NZ = 11264
_ROPE_ALL_TILES = (Z_QROT // PROJ_TN, Z_QROT // PROJ_TN + 1)
_ROPE_HALF_TILES = (Z_KS // PROJ_TN, Z_KW // PROJ_TN)

ROUTE_W = 128
MOE_TILE = 256
MOE_FC = 256


ROW_SUB = D_MODEL // HEAD_DIM


def _rms(x, g):
    return x * lax.rsqrt(jnp.mean(x * x, axis=-1, keepdims=True) + EPS) * g


def _to_slab(ref, val):
    for c in range(ROW_SUB):
        ref[:, c, :] = val[:, c * HEAD_DIM:(c + 1) * HEAD_DIM]


def _from_slab(ref):
    return jnp.concatenate([ref[:, c, :] for c in range(ROW_SUB)], axis=1)


def _proj_kernel(x_ref, g_ref, w_ref, b_ref, cos_ref, sa_ref, sb_ref, z_ref, xn_ref):
    j = pl.program_id(1)

    @pl.when(j == 0)
    def _():
        xn_ref[...] = _rms(x_ref[...], g_ref[...]).astype(BF16)

    acc = jnp.dot(xn_ref[...], w_ref[...], preferred_element_type=F32) + b_ref[...]

    def rope(blk):
        return (blk * cos_ref[...] + pltpu.roll(blk, HEAD_DIM - ROT_DIM // 2, 1) * sa_ref[...]
                + pltpu.roll(blk, ROT_DIM // 2, 1) * sb_ref[...])

    is_all = (j == _ROPE_ALL_TILES[0]) | (j == _ROPE_ALL_TILES[1])
    is_half = (j == _ROPE_HALF_TILES[0]) | (j == _ROPE_HALF_TILES[1])
    n_blk = PROJ_TN // HEAD_DIM

    @pl.when(is_all)
    def _():
        for c in range(n_blk):
            z_ref[:, c * HEAD_DIM:(c + 1) * HEAD_DIM] = rope(acc[:, c * HEAD_DIM:(c + 1) * HEAD_DIM])

    @pl.when(is_half)
    def _():
        for c in range(n_blk // 2):
            z_ref[:, c * HEAD_DIM:(c + 1) * HEAD_DIM] = rope(acc[:, c * HEAD_DIM:(c + 1) * HEAD_DIM])
        z_ref[:, PROJ_TN // 2:] = acc[:, PROJ_TN // 2:]

    @pl.when(jnp.logical_not(is_all | is_half))
    def _():
        z_ref[...] = acc


def _rope_tables(pos):
    half = ROT_DIM // 2
    inv = ROPE_THETA ** (-jnp.arange(half, dtype=F32) / half)
    ang = pos.astype(F32)[:, None] * inv[None, :]
    cos, sin = jnp.cos(ang), jnp.sin(ang)
    n = pos.shape[0]
    ones = jnp.ones((n, HEAD_DIM - ROT_DIM), F32)
    zeros = jnp.zeros((n, HEAD_DIM - half), F32)
    cos_t = jnp.concatenate([cos, cos, ones], axis=1)
    sa_t = jnp.concatenate([-sin, zeros], axis=1)
    sb_t = jnp.concatenate([jnp.zeros((n, half), F32), sin, jnp.zeros((n, HEAD_DIM - ROT_DIM), F32)], axis=1)
    return cos_t, sa_t, sb_t


def _permute_in_proj(w_in, b_in):
    g0 = _SRC['g_nsa'][0]
    spans = [_SRC[k] for k in _Z_ORDER]
    spans += [(g0, g0 + N_GATE), HEAD_DIM - N_GATE, (g0 + N_GATE, g0 + 2 * N_GATE), HEAD_DIM - N_GATE,
              _SRC['i_b'], _SRC['f_b'], NZ - Z_FB - N_HEADS_B]
    wb = jnp.concatenate([w_in, b_in[None, :]], axis=0)
    parts = [jnp.zeros((D_MODEL + 1, s), wb.dtype) if isinstance(s, int) else wb[:, s[0]:s[1]] for s in spans]
    wb = jnp.concatenate(parts, axis=1)
    return wb[:D_MODEL].astype(BF16), wb[D_MODEL:].astype(F32)


def _project(x2d, pos_rows, norm_g, w_perm, b_perm, tm):
    m = x2d.shape[0]
    cos_t, sa_t, sb_t = _rope_tables(pos_rows)
    row = lambda i, j: (i, 0)
    return pl.pallas_call(
        _proj_kernel,
        out_shape=jax.ShapeDtypeStruct((m, NZ), F32),
        grid=(m // tm, NZ // PROJ_TN),
        in_specs=[pl.BlockSpec((tm, D_MODEL), row),
                  pl.BlockSpec((1, D_MODEL), lambda i, j: (0, 0)),
                  pl.BlockSpec((D_MODEL, PROJ_TN), lambda i, j: (0, j)),
                  pl.BlockSpec((1, PROJ_TN), lambda i, j: (0, j)),
                  pl.BlockSpec((tm, HEAD_DIM), row),
                  pl.BlockSpec((tm, HEAD_DIM), row),
                  pl.BlockSpec((tm, HEAD_DIM), row)],
        out_specs=pl.BlockSpec((tm, PROJ_TN), lambda i, j: (i, j)),
        scratch_shapes=[pltpu.VMEM((tm, D_MODEL), BF16)],
        compiler_params=pltpu.CompilerParams(dimension_semantics=("parallel", "arbitrary"),
                                             vmem_limit_bytes=VMEM_LIMIT),
        name="proj",
    )(x2d, norm_g.reshape(1, D_MODEL), w_perm, b_perm, cos_t, sa_t, sb_t)


def _compress_body(x_ref, w1, pb, w2, r):
    bias = pb[0:1, :HEAD_DIM] + pb[1:2, HEAD_DIM:]
    outs = []
    for h in range(N_KV):
        xh = jnp.concatenate([x_ref[:, (N_KV * s + h) * HEAD_DIM:(N_KV * s + h + 1) * HEAD_DIM]
                              for s in range(CMP_STRIDE)], axis=1).astype(BF16)
        zz = jnp.dot(xh, w1, preferred_element_type=F32)
        pre = zz[:, :HEAD_DIM] + pltpu.roll(zz[:, HEAD_DIM:], r - 1, 0) + bias
        mid = pre * jax.nn.sigmoid(pre)
        outs.append(jnp.dot(mid.astype(BF16), w2, preferred_element_type=F32))
    return outs


def _compress_kernel(xk_ref, xv_ref, w1_ref, p_ref, w2_ref, o_ref):
    r = xk_ref.shape[0]
    for which, x_ref in enumerate((xk_ref, xv_ref)):
        w1 = w1_ref[which]
        pb = jnp.dot(p_ref[which], w1, preferred_element_type=F32)
        for h, o in enumerate(_compress_body(x_ref, w1, pb, w2_ref[which], r)):
            o_ref[which, h] = o


def _compress_weights(cmp_pos, cmp_w1, cmp_w2):
    assert CMP_LEN == 2 * CMP_STRIDE
    half = CMP_STRIDE * HEAD_DIM
    w1 = jnp.concatenate([cmp_w1[:, :half], cmp_w1[:, half:]], axis=2).astype(BF16)
    p = cmp_pos.reshape(2, 2, half)
    p = jnp.concatenate([p, jnp.zeros((2, 6, half), p.dtype)], axis=1).astype(BF16)
    return w1, p, cmp_w2.astype(BF16)


def _compress_prompt(kc, vc, cw):
    B, T = kc.shape[:2]
    r = T // CMP_STRIDE
    width = CMP_STRIDE * N_KV * HEAD_DIM
    w1, p, w2 = cw
    full = lambda a: pl.BlockSpec(a.shape, lambda b: (0,) * a.ndim)
    xspec = pl.BlockSpec((None, r, width), lambda b: (b, 0, 0))
    return pl.pallas_call(
        _compress_kernel,
        out_shape=jax.ShapeDtypeStruct((B, 2, N_KV, r, HEAD_DIM), F32),
        grid=(B,),
        in_specs=[xspec, xspec, full(w1), full(p), full(w2)],
        out_specs=pl.BlockSpec((None, 2, N_KV, r, HEAD_DIM), lambda b: (b, 0, 0, 0, 0)),
        compiler_params=pltpu.CompilerParams(dimension_semantics=("parallel",), vmem_limit_bytes=VMEM_LIMIT),
        name="compress_prompt",
    )(kc.reshape(B, r, width), vc.reshape(B, r, width), w1, p, w2)


def _group_scores(q, k, tq):
    s = lax.dot_general(q, k, (((1,), (1,)), ((), ())), preferred_element_type=F32) * HEAD_DIM ** -0.5
    return s.reshape(GROUP, tq, k.shape[0])


def _stack_heads(ref):
    return jnp.concatenate([ref[:, g * HEAD_DIM:(g + 1) * HEAD_DIM] for g in range(GROUP)], axis=0).astype(BF16)


def _importance(psum, n_sel):
    r = psum.shape[1]
    ci = lax.broadcasted_iota(jnp.int32, (r, n_sel), 0) * CMP_STRIDE
    cj = lax.broadcasted_iota(jnp.int32, (r, n_sel), 1) * SEL_LEN
    cover_t = ((ci < cj + SEL_LEN) & (ci + (CMP_LEN - 1) >= cj)).astype(BF16)
    p_hi = psum.astype(BF16)
    p_lo = (psum - p_hi.astype(F32)).astype(BF16)
    return jnp.dot(p_hi, cover_t, preferred_element_type=F32) + jnp.dot(p_lo, cover_t, preferred_element_type=F32)


def _block_scores(imp, blk, qpos):
    cur = jnp.right_shift(qpos, SEL_LEN.bit_length() - 1)
    forced = (blk == 0) | (blk == cur) | (blk == cur - 1)
    return jnp.where(blk * SEL_LEN <= qpos, imp + jnp.where(forced, FORCE_BONUS, 0.0), -1.0)


def _top_rank_mask(score, axis, n_cand):
    idx = lax.broadcasted_iota(jnp.int32, score.shape, axis)
    rank = jnp.zeros(score.shape, F32)
    for j in range(n_cand):
        cand = score[j:j + 1, :] if axis == 0 else score[:, j:j + 1]
        rank = rank + jnp.where((cand > score) | ((cand == score) & (idx > j)), 1.0, 0.0)
    return jnp.where(rank < float(SEL_TOP), 1.0, 0.0)


def _nsa_prompt_kernel(qraw_ref, qrot_ref, ks_ref, vs_ref, kw_ref, vw_ref, kc_ref, vc_ref, gate_ref, o_ref,
                       m_ref, l_ref, acc_ref, *, tq, tk, seq_len):
    q0 = pl.program_id(2) * tq
    r = kc_ref.shape[0]
    n_sel = seq_len // SEL_LEN
    sel_shift = SEL_LEN.bit_length() - 1
    qpos = q0 + lax.broadcasted_iota(jnp.int32, (tq, 1), 0)

    s = _group_scores(_stack_heads(qraw_ref), kc_ref[...].astype(BF16), tq)
    c_end = lax.broadcasted_iota(jnp.int32, (tq, r), 1) * CMP_STRIDE + (CMP_LEN - 1)
    cmask = c_end <= qpos
    s = s + jnp.where(cmask, 0.0, NEG)[None]
    p = jnp.exp(s - jnp.max(s, axis=-1, keepdims=True)) * jnp.where(cmask, 1.0, 0.0)[None]
    p = p / jnp.maximum(jnp.sum(p, axis=-1, keepdims=True), 1e-30)
    o_cmp = jnp.dot(p.reshape(GROUP * tq, r).astype(BF16), vc_ref[...].astype(BF16), preferred_element_type=F32)
    psum = p[0]
    for g in range(1, GROUP):
        psum = psum + p[g]
    imp_t = _importance(psum, HEAD_DIM).T[:n_sel]
    blk_t = lax.broadcasted_iota(jnp.int32, (n_sel, tq), 0)
    qpos_t = q0 + lax.broadcasted_iota(jnp.int32, (n_sel, tq), 1)
    sel_b = _top_rank_mask(_block_scores(imp_t, blk_t, qpos_t), 0, n_sel).astype(BF16)

    q_rot = _stack_heads(qrot_ref)
    m_ref[...] = jnp.full(m_ref.shape, NEG, F32)
    l_ref[...] = jnp.zeros(l_ref.shape, F32)
    acc_ref[...] = jnp.zeros(acc_ref.shape, F32)

    def sel_tile(kt, carry):
        k0 = pl.multiple_of(kt * tk, tk)
        k = ks_ref[pl.ds(k0, tk), :].astype(BF16)
        v = vs_ref[pl.ds(k0, tk), :].astype(BF16)
        s2 = _group_scores(q_rot, k, tq)
        ej = lax.broadcasted_iota(jnp.int32, (n_sel, tk), 0)
        et = k0 + lax.broadcasted_iota(jnp.int32, (n_sel, tk), 1)
        expand = jnp.where(jnp.right_shift(et, sel_shift) == ej, 1.0, 0.0).astype(BF16)
        selm = lax.dot_general(sel_b, expand, (((0,), (0,)), ((), ())), preferred_element_type=F32)
        kpos = k0 + lax.broadcasted_iota(jnp.int32, (tq, tk), 1)
        ok = (selm > 0.5) & (kpos <= qpos)
        s2 = s2 + jnp.where(ok, 0.0, NEG)[None]
        m_prev = m_ref[...]
        m_new = jnp.maximum(m_prev, jnp.max(s2, axis=-1, keepdims=True))
        alpha = jnp.exp(m_prev - m_new)
        p2 = jnp.exp(s2 - m_new)
        l_ref[...] = alpha * l_ref[...] + jnp.sum(p2, axis=-1, keepdims=True)
        pv = jnp.dot(p2.reshape(GROUP * tq, tk).astype(BF16), v, preferred_element_type=F32)
        acc_ref[...] = alpha.reshape(GROUP * tq, 1) * acc_ref[...] + pv
        m_ref[...] = m_new
        return carry

    lax.fori_loop(0, lax.div(q0 + tq + tk - 1, tk), sel_tile, 0)
    o_sel = acc_ref[...] / jnp.maximum(l_ref[...].reshape(GROUP * tq, 1), 1e-30)

    span = WINDOW + tq
    w0 = pl.multiple_of(jnp.maximum(q0 - WINDOW, 0), tq)
    s3 = _group_scores(q_rot, kw_ref[pl.ds(w0, span), :].astype(BF16), tq)
    kp = w0 + lax.broadcasted_iota(jnp.int32, (tq, span), 1)
    okw = (kp <= qpos) & (kp >= qpos - WINDOW)
    s3 = s3 + jnp.where(okw, 0.0, NEG)[None]
    p3 = jnp.exp(s3 - jnp.max(s3, axis=-1, keepdims=True))
    p3 = p3 / jnp.sum(p3, axis=-1, keepdims=True)
    o_win = jnp.dot(p3.reshape(GROUP * tq, span).astype(BF16), vw_ref[pl.ds(w0, span), :].astype(BF16),
                    preferred_element_type=F32)

    gts = jax.nn.sigmoid(gate_ref[...])
    for g in range(GROUP):
        rows = slice(g * tq, (g + 1) * tq)
        o_ref[:, g * HEAD_DIM:(g + 1) * HEAD_DIM] = (gts[:, 3 * g:3 * g + 1] * o_cmp[rows]
                                                     + gts[:, 3 * g + 1:3 * g + 2] * o_sel[rows]
                                                     + gts[:, 3 * g + 2:3 * g + 3] * o_win[rows]).astype(o_ref.dtype)


def _nsa_prompt(z, cmp, B, T, tq=128, tk=512):
    assert T % tk == 0 and tk % tq == 0 and tq % HEAD_DIM == 0 and T >= WINDOW + tq
    assert SEL_LEN & (SEL_LEN - 1) == 0 and tk % SEL_LEN == 0
    assert (T // SEL_LEN) % 8 == 0 and T // SEL_LEN <= HEAD_DIM and tq == HEAD_DIM
    nq = T // tq
    r = cmp.shape[3]
    gw = GROUP * HEAD_DIM
    qspec = lambda off: pl.BlockSpec((tq, gw), lambda b, h, q: (b * nq + q, off // gw + h))
    kvspec = lambda off: pl.BlockSpec((T, HEAD_DIM), lambda b, h, q: (b, off // HEAD_DIM + h))
    cspec = lambda which: pl.BlockSpec((None, None, None, r, HEAD_DIM), lambda b, h, q: (b, which, h, 0, 0))
    return pl.pallas_call(
        functools.partial(_nsa_prompt_kernel, tq=tq, tk=tk, seq_len=T),
        out_shape=jax.ShapeDtypeStruct((B * T, A_Q), BF16),
        grid=(B, N_KV, nq),
        in_specs=[qspec(Z_QRAW), qspec(Z_QROT), kvspec(Z_KS), kvspec(Z_VS), kvspec(Z_KW), kvspec(Z_VW),
                  cspec(0), cspec(1),
                  pl.BlockSpec((tq, HEAD_DIM), lambda b, h, q: (b * nq + q, Z_GNSA // HEAD_DIM + h))],
        out_specs=pl.BlockSpec((tq, gw), lambda b, h, q: (b * nq + q, h)),
        scratch_shapes=[pltpu.VMEM((GROUP, tq, 1), F32), pltpu.VMEM((GROUP, tq, 1), F32),
                        pltpu.VMEM((GROUP * tq, HEAD_DIM), F32)],
        compiler_params=pltpu.CompilerParams(dimension_semantics=("parallel", "parallel", "arbitrary"),
                                             vmem_limit_bytes=VMEM_LIMIT),
        name="nsa_prompt",
    )(z, z, z, z, z, z, cmp, cmp, z)


def _gate_logs(i_pre, f_pre):
    log_i = GATE_CAP * jnp.tanh(i_pre / GATE_CAP)
    fc = GATE_CAP * jnp.tanh(f_pre / GATE_CAP)
    log_f = jnp.minimum(fc, 0.0) - jnp.log1p(jnp.exp(-jnp.abs(fc)))
    return log_i, log_f


def _mlstm_prompt_kernel(q_ref, k_ref, v_ref, op_ref, grow_ref, gcol_ref, gain_ref,
                         h_ref, c_ref, n_ref, m_ref):
    @pl.when(pl.program_id(1) == 0)
    def _():
        c_ref[...] = jnp.zeros(c_ref.shape, F32)
        n_ref[...] = jnp.zeros(n_ref.shape, F32)
        m_ref[...] = jnp.zeros(m_ref.shape, F32)

    for h in range(N_HEADS_B):
        qk_cols = slice(h * DQK_B, (h + 1) * DQK_B)
        v_cols = slice(h * DV_B, (h + 1) * DV_B)
        _mlstm_chunk(q_ref[:, qk_cols], k_ref[:, qk_cols], v_ref[:, v_cols], op_ref[:, v_cols],
                     grow_ref[h:h + 1, :], grow_ref[N_HEADS_B + h:N_HEADS_B + h + 1, :],
                     gcol_ref[:, h:h + 1], gcol_ref[:, N_HEADS_B + h:N_HEADS_B + h + 1], gain_ref[h],
                     h_ref.at[:, v_cols], c_ref.at[h], n_ref.at[h], m_ref.at[h])


def _mlstm_chunk(q, k, v, o_pre, i_row, f_row, i_col, f_col, gain, h_ref, c_ref, n_ref, m_ref):
    L = q.shape[0]
    qb = q.astype(BF16)
    k = k * DQK_B ** -0.5
    vb = v.astype(BF16)
    i_r, f_r = _gate_logs(i_row, f_row)
    i_c, f_c = _gate_logs(i_col, f_col)
    li = lax.broadcasted_iota(jnp.int32, (L, L), 0)
    si = lax.broadcasted_iota(jnp.int32, (L, L), 1)
    tri = si <= li
    bcum_c = jnp.sum(jnp.where(tri, f_r, 0.0), axis=1, keepdims=True)
    bcum_r = jnp.sum(jnp.where(li <= si, f_c, 0.0), axis=0, keepdims=True)
    btot = jnp.sum(f_r, axis=1, keepdims=True)
    c_old, n_old, m_old = c_ref[...], n_ref[...], m_ref[0:1, 0:1]

    dlog = jnp.where(tri, bcum_c - bcum_r + i_r, NEG)
    inter = bcum_c + m_old
    m_t = jnp.maximum(inter, jnp.max(dlog, axis=1, keepdims=True))
    qk = lax.dot_general(qb, k.astype(BF16), (((1,), (1,)), ((), ())), preferred_element_type=F32)
    w_intra = jnp.exp(dlog - m_t) * qk
    w_inter = jnp.exp(inter - m_t)
    num = (w_inter * jnp.dot(qb, c_old.astype(BF16), preferred_element_type=F32)
           + jnp.dot(w_intra.astype(BF16), vb, preferred_element_type=F32))
    den = w_inter * jnp.sum(q * n_old, axis=1, keepdims=True) + jnp.sum(w_intra, axis=1, keepdims=True)
    h = num / jnp.maximum(jnp.abs(den), jnp.exp(-m_t))
    h_ref[...] = (_rms(h, gain) * jax.nn.sigmoid(o_pre)).astype(h_ref.dtype)

    wlog_c = btot - bcum_c + i_c
    m_new = jnp.maximum(btot + m_old, jnp.max(btot - bcum_r + i_r, axis=1, keepdims=True))
    decay = jnp.exp(btot + m_old - m_new)
    kw = k * jnp.exp(wlog_c - m_new)
    c_ref[...] = decay * c_old + lax.dot_general(kw.astype(BF16), vb, (((0,), (0,)), ((), ())),
                                                 preferred_element_type=F32)
    n_ref[...] = decay * n_old + jnp.sum(kw, axis=0, keepdims=True)
    m_ref[...] = jnp.broadcast_to(m_new, m_ref.shape)


def _mlstm_prompt(z, gain, B, T):
    L = MLSTM_CHUNK
    assert T % L == 0
    nc, H = T // L, N_HEADS_B
    gcol = z[:, Z_IB:Z_IB + 2 * H].reshape(B * nc, L, 2 * H)
    grow = gcol.transpose(0, 2, 1)
    rows = lambda w, off: pl.BlockSpec((L, w), lambda b, c: (b * nc + c, off // w))
    gidx = lambda b, c: (b * nc + c, 0, 0)
    state = lambda *s: pl.BlockSpec((None, H) + s, lambda b, c: (b, 0, 0, 0))
    h, c, n, m = pl.pallas_call(
        _mlstm_prompt_kernel,
        out_shape=(jax.ShapeDtypeStruct((B * T, B_V), BF16), jax.ShapeDtypeStruct((B, H, DQK_B, DV_B), F32),
                   jax.ShapeDtypeStruct((B, H, 1, DQK_B), F32), jax.ShapeDtypeStruct((B, H, 1, DQK_B), F32)),
        grid=(B, nc),
        in_specs=[rows(B_QK, Z_QB), rows(B_QK, Z_KB), rows(B_V, Z_VB), rows(B_V, Z_OB),
                  pl.BlockSpec((None, 2 * H, L), gidx), pl.BlockSpec((None, L, 2 * H), gidx),
                  pl.BlockSpec((H, 1, DV_B), lambda b, c: (0, 0, 0))],
        out_specs=(rows(B_V, 0), state(DQK_B, DV_B), state(1, DQK_B), state(1, DQK_B)),
        compiler_params=pltpu.CompilerParams(dimension_semantics=("parallel", "arbitrary"),
                                             vmem_limit_bytes=VMEM_LIMIT),
        name="mlstm_prompt",
    )(z, z, z, z, grow, gcol, gain.reshape(H, 1, DV_B))
    return h, c, n[:, :, 0], m[:, :, 0, 0]


def _merge_kernel(x_ref, oa_ref, hb_ref, gma_ref, gmb_ref, wa_ref, wb_ref, wo_ref, nf_ref, wrh_ref, wrl_ref, br_ref,
                  *refs):
    x2_ref, xn_ref, route_ref = refs[-3:]
    ua = jnp.dot(oa_ref[...], wa_ref[...], preferred_element_type=F32)
    ub = jnp.dot(hb_ref[...], wb_ref[...], preferred_element_type=F32)
    merged = jax.nn.sigmoid(gma_ref[...]) * ua + jax.nn.sigmoid(gmb_ref[...]) * ub
    x2 = x_ref[...] + jnp.dot(merged.astype(BF16), wo_ref[...], preferred_element_type=F32)
    x2_ref[...] = x2
    xn = _rms(x2, nf_ref[...])
    _to_slab(xn_ref, xn)
    hi = xn.astype(BF16)
    lo = (xn - hi.astype(F32)).astype(BF16)
    lg = (jnp.dot(hi, wrh_ref[...], preferred_element_type=F32) + jnp.dot(lo, wrh_ref[...], preferred_element_type=F32)
          + jnp.dot(hi, wrl_ref[...], preferred_element_type=F32)) + br_ref[...]
    lane = lax.broadcasted_iota(jnp.int32, lg.shape, 1)
    lanef = lane.astype(F32)
    ninf = -jnp.inf
    gmask = (lane >= N_EXPERTS) & (lane < N_EXPERTS + N_GROUPS)
    gl = jnp.where(gmask, lg, ninf)
    gmax = jnp.max(gl, axis=1, keepdims=True)
    gidx = jnp.min(jnp.where(gl == gmax, lanef, 1e9), axis=1, keepdims=True) - N_EXPERTS
    gprob = 1.0 / jnp.sum(jnp.where(gmask, jnp.exp(lg - gmax), 0.0), axis=1, keepdims=True)
    e_lo = gidx * EXPERTS_PER_GROUP
    el = jnp.where((lanef >= e_lo) & (lanef < e_lo + EXPERTS_PER_GROUP), lg, ninf)
    v1 = jnp.max(el, axis=1, keepdims=True)
    i1 = jnp.min(jnp.where(el == v1, lanef, 1e9), axis=1, keepdims=True)
    el2 = jnp.where(lanef == i1, ninf, el)
    v2 = jnp.max(el2, axis=1, keepdims=True)
    i2 = jnp.min(jnp.where(el2 == v2, lanef, 1e9), axis=1, keepdims=True)
    e2 = jnp.exp(v2 - v1)
    g1 = gprob / (1.0 + e2)
    g2 = gprob * e2 / (1.0 + e2)
    route_ref[...] = jnp.where(lane == 0, i1, jnp.where(lane == 1, i2, jnp.where(lane == 2, g1,
                               jnp.where(lane == 3, g2, 0.0))))


def _merge_weights(w_up_a, w_up_b, w_out, norm_ffn, w_rg, b_rg, w_re, b_re):
    assert TOP_K == 2
    pad = ROUTE_W - N_EXPERTS - N_GROUPS
    w = jnp.concatenate([w_re, w_rg, jnp.zeros((D_MODEL, pad), F32)], axis=1)
    b = jnp.concatenate([b_re, b_rg, jnp.zeros((pad,), F32)]).reshape(1, ROUTE_W)
    hi = w.astype(BF16)
    return (w_up_a.astype(BF16), w_up_b.astype(BF16), w_out.astype(BF16), norm_ffn.reshape(1, D_MODEL).astype(F32),
            hi, (w - hi.astype(F32)).astype(BF16), b)


def _merge(x2d, o_a, h_b, z, mw, n_total, row0, tm, bufs=None):
    m = x2d.shape[0]
    i0 = row0 // tm
    row = lambda w, c=0: pl.BlockSpec((tm, w), lambda i: (i, c))
    const = lambda a: pl.BlockSpec(a.shape, lambda i: (0,) * a.ndim, pipeline_mode=pl.Buffered(1))
    out_tail = ((D_MODEL,), (ROW_SUB, HEAD_DIM), (ROUTE_W,))
    in_specs = [row(D_MODEL), row(A_Q), row(B_V), row(D_MODEL, Z_GMA // D_MODEL), row(D_MODEL, Z_GMB // D_MODEL)]
    in_specs += [const(a) for a in mw]
    args = [x2d, o_a, h_b, z, z] + list(mw)
    aliases = {}
    if bufs is not None:
        in_specs += [pl.BlockSpec(memory_space=pl.ANY)] * 3
        aliases = {len(args) + k: k for k in range(3)}
        args += list(bufs)
    return pl.pallas_call(
        _merge_kernel,
        out_shape=tuple(jax.ShapeDtypeStruct((n_total,) + t, F32) for t in out_tail),
        grid=(m // tm,),
        in_specs=in_specs,
        out_specs=tuple(pl.BlockSpec((tm,) + t, lambda i, nd=len(t): (i0 + i,) + (0,) * nd) for t in out_tail),
        input_output_aliases=aliases,
        compiler_params=pltpu.CompilerParams(dimension_semantics=("parallel",), vmem_limit_bytes=VMEM_LIMIT_MERGE),
        name="merge_route",
    )(*args)


def _moe_plan(route, n_tiles):
    n = route.shape[0]
    a = n * TOP_K
    flat_e = route[:, 0:TOP_K].astype(jnp.int32).reshape(a)
    onehot = (flat_e[:, None] == jnp.arange(N_EXPERTS, dtype=jnp.int32)[None, :]).astype(jnp.int32)
    rank = jnp.take_along_axis(jnp.cumsum(onehot, axis=0) - onehot, flat_e[:, None], axis=1)[:, 0]
    counts = jnp.sum(onehot, axis=0)
    ntile = (counts + MOE_TILE - 1) // MOE_TILE
    tile_end = jnp.cumsum(ntile)
    dest = ((tile_end - ntile)[flat_e] * MOE_TILE + rank).astype(jnp.int32)
    n_used = tile_end[-1].astype(jnp.int32)
    t = jnp.minimum(jnp.arange(n_tiles, dtype=jnp.int32), n_used - 1)
    tile_expert = jnp.minimum(jnp.searchsorted(tile_end, t, side='right'), N_EXPERTS - 1).astype(jnp.int32)
    src = jnp.full((n_tiles * MOE_TILE,), -1, jnp.int32).at[dest].set(jnp.arange(a, dtype=jnp.int32) // TOP_K)
    return dest, src, tile_expert, n_used.reshape(1)


def _row_gather_kernel(src_ref, nu_ref, x_hbm, o_ref, sem):
    t = pl.program_id(0)
    rows = o_ref.shape[0]

    def copy(r):
        s = src_ref[t * rows + r]
        return s, pltpu.make_async_copy(x_hbm.at[pl.ds(jnp.maximum(s, 0), 1)], o_ref.at[pl.ds(r, 1)], sem)

    @pl.when(t < nu_ref[0])
    def _():
        def start(r, c):
            s, cp = copy(r)

            @pl.when(s >= 0)
            def _():
                cp.start()
            return c

        def wait(r, c):
            s, cp = copy(r)

            @pl.when(s >= 0)
            def _():
                cp.wait()
            return c

        lax.fori_loop(0, rows, start, 0)
        lax.fori_loop(0, rows, wait, 0)


def _moe_gather(xn, src, n_used, n_tiles):
    return pl.pallas_call(
        _row_gather_kernel,
        out_shape=jax.ShapeDtypeStruct((n_tiles * MOE_TILE,) + xn.shape[1:], xn.dtype),
        grid_spec=pltpu.PrefetchScalarGridSpec(
            num_scalar_prefetch=2, grid=(n_tiles,),
            in_specs=[pl.BlockSpec(memory_space=pl.ANY)],
            out_specs=pl.BlockSpec((MOE_TILE,) + xn.shape[1:], lambda t, src, nu: (jnp.minimum(t, nu[0] - 1), 0, 0)),
            scratch_shapes=[pltpu.SemaphoreType.DMA(())]),
        compiler_params=pltpu.CompilerParams(dimension_semantics=("arbitrary",), vmem_limit_bytes=VMEM_LIMIT),
        name="moe_gather",
    )(src, n_used, xn)


def _expert_kernel(te_ref, nu_ref, x_ref, wg_ref, wu_ref, wd_ref, o_ref, xb_ref, acc_ref):
    t, j = pl.program_id(0), pl.program_id(1)
    nj = pl.num_programs(1)

    @pl.when(t < nu_ref[0])
    def _():
        @pl.when(j == 0)
        def _():
            for c in range(ROW_SUB):
                xb_ref[:, c * HEAD_DIM:(c + 1) * HEAD_DIM] = x_ref[:, c, :].astype(BF16)

        xb = xb_ref[...]
        g = jnp.dot(xb, wg_ref[...].astype(BF16), preferred_element_type=F32)
        u = jnp.dot(xb, wu_ref[...].astype(BF16), preferred_element_type=F32)
        h = (g * jax.nn.sigmoid(g) * u).astype(BF16)
        y = jnp.dot(h, wd_ref[...].astype(BF16), preferred_element_type=F32)

        @pl.when(j == 0)
        def _():
            acc_ref[...] = y

        @pl.when((j > 0) & (j < nj - 1))
        def _():
            acc_ref[...] += y

        @pl.when(j == nj - 1)
        def _():
            _to_slab(o_ref, acc_ref[...] + y)


def _moe_experts(xs, tile_expert, n_used, w_gate, w_up, w_down):
    n_tiles = xs.shape[0] // MOE_TILE
    nj = D_EXPERT // MOE_FC
    assert nj >= 2
    tix = lambda t, j, te, nu: (jnp.minimum(t, nu[0] - 1), 0, 0)
    jj = lambda t, j, nu: jnp.where(t < nu[0], j, nj - 1)
    return pl.pallas_call(
        _expert_kernel,
        out_shape=jax.ShapeDtypeStruct(xs.shape, F32),
        grid_spec=pltpu.PrefetchScalarGridSpec(
            num_scalar_prefetch=2, grid=(n_tiles, nj),
            in_specs=[pl.BlockSpec((MOE_TILE, ROW_SUB, HEAD_DIM), tix),
                      pl.BlockSpec((None, D_MODEL, MOE_FC), lambda t, j, te, nu: (te[t], 0, jj(t, j, nu))),
                      pl.BlockSpec((None, D_MODEL, MOE_FC), lambda t, j, te, nu: (te[t], 0, jj(t, j, nu))),
                      pl.BlockSpec((None, MOE_FC, D_MODEL), lambda t, j, te, nu: (te[t], jj(t, j, nu), 0))],
            out_specs=pl.BlockSpec((MOE_TILE, ROW_SUB, HEAD_DIM), tix),
            scratch_shapes=[pltpu.VMEM((MOE_TILE, D_MODEL), BF16), pltpu.VMEM((MOE_TILE, D_MODEL), F32)]),
        compiler_params=pltpu.CompilerParams(dimension_semantics=("arbitrary", "arbitrary"),
                                             vmem_limit_bytes=VMEM_LIMIT),
        name="moe_experts",
    )(tile_expert, n_used, xs, w_gate, w_up, w_down)


def _combine_kernel(dest_ref, x2_ref, route_ref, g_ref, y_hbm, op_ref, os_ref, buf_ref, sem, *, n_prompt_tiles):
    i = pl.program_id(0)
    tm = x2_ref.shape[0]

    def copy(r, k):
        d = dest_ref[(i * tm + r) * TOP_K + k]
        return pltpu.make_async_copy(y_hbm.at[pl.ds(d, 1)], buf_ref.at[k, pl.ds(r, 1)], sem)

    def start(r, c):
        for k in range(TOP_K):
            copy(r, k).start()
        return c

    def wait(r, c):
        for k in range(TOP_K):
            copy(r, k).wait()
        return c

    lax.fori_loop(0, tm, start, 0)
    lax.fori_loop(0, tm, wait, 0)
    ffn = _from_slab(buf_ref.at[0]) * route_ref[:, TOP_K:TOP_K + 1]
    for k in range(1, TOP_K):
        ffn = ffn + _from_slab(buf_ref.at[k]) * route_ref[:, TOP_K + k:TOP_K + k + 1]
    y = _rms(x2_ref[...] + ffn, g_ref[...])

    @pl.when(i < n_prompt_tiles)
    def _():
        op_ref[...] = y

    @pl.when(i >= n_prompt_tiles)
    def _():
        os_ref[...] = y


def _moe_combine(x2, route, dest, ys, norm_final, n_prompt, tm=128):
    n, d = x2.shape
    npt = n_prompt // tm
    nst = (n - n_prompt) // tm
    return pl.pallas_call(
        functools.partial(_combine_kernel, n_prompt_tiles=npt),
        out_shape=(jax.ShapeDtypeStruct((n_prompt, d), F32), jax.ShapeDtypeStruct((n - n_prompt, d), F32)),
        grid_spec=pltpu.PrefetchScalarGridSpec(
            num_scalar_prefetch=1, grid=(n // tm,),
            in_specs=[pl.BlockSpec((tm, d), lambda i, ds: (i, 0)), pl.BlockSpec((tm, ROUTE_W), lambda i, ds: (i, 0)),
                      pl.BlockSpec((1, d), lambda i, ds: (0, 0)), pl.BlockSpec(memory_space=pl.ANY)],
            out_specs=(pl.BlockSpec((tm, d), lambda i, ds: (jnp.minimum(i, npt - 1), 0)),
                       pl.BlockSpec((tm, d), lambda i, ds: (jnp.clip(i - npt, 0, nst - 1), 0))),
            scratch_shapes=[pltpu.VMEM((TOP_K, tm, ROW_SUB, HEAD_DIM), F32), pltpu.SemaphoreType.DMA(())]),
        compiler_params=pltpu.CompilerParams(dimension_semantics=("arbitrary",), vmem_limit_bytes=VMEM_LIMIT),
        name="moe_combine",
    )(dest, x2, route, norm_final.reshape(1, d), ys)


def _moe(x2, xn, route, w_gate, w_up, w_down, norm_final, n_prompt):
    n = x2.shape[0]
    n_tiles = (n * TOP_K + N_EXPERTS * (MOE_TILE - 1) + MOE_TILE - 1) // MOE_TILE
    dest, src, tile_expert, n_used = _moe_plan(route, n_tiles)
    xs = _moe_gather(xn, src, n_used, n_tiles)
    ys = _moe_experts(xs, tile_expert, n_used, w_gate, w_up, w_down)
    return _moe_combine(x2, route, dest, ys, norm_final, n_prompt)


POOL_ROWS = 256
ROWS_PER_PAGE = PAGE_SIZE // CMP_STRIDE
ZW = 2 * N_KV * HEAD_DIM


def _compress_pool_kernel(xk_ref, xv_ref, w1_ref, p_ref, zk_ref, zv_ref):
    for which, (x_ref, z_ref) in enumerate(((xk_ref, zk_ref), (xv_ref, zv_ref))):
        w1 = w1_ref[which]
        pb = jnp.dot(p_ref[which], w1, preferred_element_type=F32)
        bias = pb[0:1, :HEAD_DIM] + pb[1:2, HEAD_DIM:]
        for h in range(N_KV):
            xh = jnp.concatenate([x_ref[pl.ds(N_KV * s + h, POOL_ROWS, stride=N_KV * CMP_STRIDE), :]
                                  for s in range(CMP_STRIDE)], axis=1).astype(BF16)
            zz = jnp.dot(xh, w1, preferred_element_type=F32)
            z_ref[:, 2 * h * HEAD_DIM:(2 * h + 1) * HEAD_DIM] = zz[:, :HEAD_DIM] + bias
            z_ref[:, (2 * h + 1) * HEAD_DIM:(2 * h + 2) * HEAD_DIM] = zz[:, HEAD_DIM:]


def _compress_pool(pool_k, pool_v, cw):
    n_phys = pool_k.shape[0]
    rows = n_phys * ROWS_PER_PAGE
    assert rows % POOL_ROWS == 0
    in_rows = POOL_ROWS * CMP_STRIDE * N_KV
    w1, p, _ = cw
    full = lambda a: pl.BlockSpec(a.shape, lambda i: (0,) * a.ndim)
    flat = lambda a: a.reshape(n_phys * PAGE_SIZE * N_KV, HEAD_DIM)
    zk, zv = pl.pallas_call(
        _compress_pool_kernel,
        out_shape=(jax.ShapeDtypeStruct((rows, ZW), F32),) * 2,
        grid=(rows // POOL_ROWS,),
        in_specs=[pl.BlockSpec((in_rows, HEAD_DIM), lambda i: (i, 0))] * 2 + [full(w1), full(p)],
        out_specs=(pl.BlockSpec((POOL_ROWS, ZW), lambda i: (i, 0)),) * 2,
        compiler_params=pltpu.CompilerParams(dimension_semantics=("parallel",), vmem_limit_bytes=VMEM_LIMIT),
        name="compress_pool",
    )(flat(pool_k), flat(pool_v), w1, p)
    return zk.reshape(n_phys, ROWS_PER_PAGE, ZW), zv.reshape(n_phys, ROWS_PER_PAGE, ZW)


def _nsa_sample_kernel(pt_ref, qraw_ref, qrot_ref, knew_ref, wnew_ref, gate_ref, w2_ref, wk_ref, wv_ref,
                       zk_hbm, zv_hbm, ks_hbm, vs_hbm, o_ref, zbuf, sbuf, sem, *, n_pages, past_len):
    b = pl.program_id(0)
    nb = pl.num_programs(0)
    slot = lax.rem(b, 2)
    page_rows = N_KV * PAGE_SIZE

    def copies(seq, sl):
        out = []
        for pg in range(n_pages):
            page = pt_ref[seq, pg]
            for which, (zsrc, ssrc) in enumerate(((zk_hbm, ks_hbm), (zv_hbm, vs_hbm))):
                out.append(pltpu.make_async_copy(
                    zsrc.at[page], zbuf.at[sl, which, pl.ds(pg * ROWS_PER_PAGE, ROWS_PER_PAGE)], sem.at[sl]))
                out.append(pltpu.make_async_copy(
                    ssrc.at[page], sbuf.at[sl, which, pl.ds(pg * page_rows, page_rows)], sem.at[sl]))
        return out

    @pl.when(b == 0)
    def _():
        for cp in copies(0, 0):
            cp.start()

    @pl.when(b + 1 < nb)
    def _():
        for cp in copies(b + 1, 1 - slot):
            cp.start()

    for cp in copies(b, slot):
        cp.wait()

    scale = HEAD_DIM ** -0.5
    nq = N_KV * GROUP
    r = n_pages * ROWS_PER_PAGE
    qpos = jnp.full((nq, 1), past_len, jnp.int32)
    n_sel_pad = HEAD_DIM
    sel_shift = SEL_LEN.bit_length() - 1
    row_head = lax.broadcasted_iota(jnp.int32, (nq, 1), 0) // GROUP
    per_head = lambda a0, a1: jnp.where(row_head == 0, a0, a1)

    def all_heads(ref):
        return jnp.concatenate([ref[:, j * HEAD_DIM:(j + 1) * HEAD_DIM] for j in range(nq)], axis=0).astype(BF16)

    def scores(q, k):
        return lax.dot_general(q, k, (((1,), (1,)), ((), ())), preferred_element_type=F32) * scale

    cmp = []
    for which in range(2):
        heads = []
        for h in range(N_KV):
            zh = zbuf[slot, which, :, 2 * h * HEAD_DIM:(2 * h + 2) * HEAD_DIM]
            pre = zh[:, :HEAD_DIM] + pltpu.roll(zh[:, HEAD_DIM:], r - 1, 0)
            mid = pre * jax.nn.sigmoid(pre)
            heads.append(jnp.dot(mid.astype(BF16), w2_ref[which], preferred_element_type=F32).astype(BF16))
        cmp.append(heads)
    q_raw = all_heads(qraw_ref)
    c_end = lax.broadcasted_iota(jnp.int32, (nq, r), 1) * CMP_STRIDE + (CMP_LEN - 1)
    cmask = c_end <= qpos
    s = per_head(scores(q_raw, cmp[0][0]), scores(q_raw, cmp[0][1])) + jnp.where(cmask, 0.0, NEG)
    p = jnp.exp(s - jnp.max(s, axis=-1, keepdims=True)) * jnp.where(cmask, 1.0, 0.0)
    p = p / jnp.maximum(jnp.sum(p, axis=-1, keepdims=True), 1e-30)
    pb = p.astype(BF16)
    o_cmp = per_head(jnp.dot(pb, cmp[1][0], preferred_element_type=F32),
                     jnp.dot(pb, cmp[1][1], preferred_element_type=F32))
    psum = per_head(jnp.sum(jnp.where(row_head == 0, p, 0.0), axis=0, keepdims=True),
                    jnp.sum(jnp.where(row_head == 1, p, 0.0), axis=0, keepdims=True))
    blk = lax.broadcasted_iota(jnp.int32, (nq, n_sel_pad), 1)
    sel = _top_rank_mask(_block_scores(_importance(psum, n_sel_pad), blk, qpos), 1,
                         past_len // SEL_LEN + 1)

    def attend(q, k, v, k_new, v_new, bias, new_bias):
        s = scores(q, k) + bias
        s_new = jnp.sum(q.astype(F32) * k_new.astype(BF16).astype(F32), axis=1, keepdims=True) * scale + new_bias
        m = jnp.maximum(jnp.max(s, axis=-1, keepdims=True), s_new)
        p = jnp.exp(s - m)
        p_new = jnp.exp(s_new - m)
        l = jnp.sum(p, axis=-1, keepdims=True) + p_new
        o = (jnp.dot(p.astype(BF16), v, preferred_element_type=F32)
             + p_new.astype(BF16).astype(F32) * v_new.astype(BF16).astype(F32))
        return o / jnp.maximum(l, 1e-30)

    def own_head(n_rows):
        col = lax.broadcasted_iota(jnp.int32, (nq, n_rows), 1)
        return jnp.bitwise_and(col, N_KV - 1) == row_head

    q_rot = all_heads(qrot_ref)
    n_rows = N_KV * past_len
    ej = lax.broadcasted_iota(jnp.int32, (n_sel_pad, n_rows), 0)
    et = lax.broadcasted_iota(jnp.int32, (n_sel_pad, n_rows), 1)
    expand = jnp.where(jnp.right_shift(et, sel_shift + 1) == ej, 1.0, 0.0).astype(BF16)
    selm = jnp.dot(sel.astype(BF16), expand, preferred_element_type=F32)
    sel_bias = jnp.where((selm > 0.5) & own_head(n_rows), 0.0, NEG)
    new_blk = past_len // SEL_LEN
    new_bias = jnp.where(sel[:, new_blk:new_blk + 1] > 0.5, 0.0, NEG)
    new_kv = lambda ref, c: per_head(ref[:, c * A_KV:c * A_KV + HEAD_DIM], ref[:, c * A_KV + HEAD_DIM:(c + 1) * A_KV])
    o_sel = attend(q_rot, sbuf[slot, 0].astype(BF16), sbuf[slot, 1].astype(BF16),
                   new_kv(knew_ref, 0), new_kv(knew_ref, 1), sel_bias, new_bias)
    win_bias = jnp.where(own_head(wk_ref.shape[0]), 0.0, NEG)
    o_win = attend(q_rot, wk_ref[...].astype(BF16), wv_ref[...].astype(BF16),
                   new_kv(wnew_ref, 0), new_kv(wnew_ref, 1), win_bias, 0.0)

    gts = jax.nn.sigmoid(gate_ref[...])
    for h in range(N_KV):
        for g in range(GROUP):
            j, c0 = h * GROUP + g, h * HEAD_DIM + 3 * g
            o = (gts[:, c0:c0 + 1] * o_cmp[j:j + 1] + gts[:, c0 + 1:c0 + 2] * o_sel[j:j + 1]
                 + gts[:, c0 + 2:c0 + 3] * o_win[j:j + 1])
            o_ref[:, j * HEAD_DIM:(j + 1) * HEAD_DIM] = o.astype(o_ref.dtype)


def _nsa_sample(zs, page_table, zk_pool, zv_pool, sel_k, sel_v, win_k, win_v, w2):
    bs, n_pages = page_table.shape
    past_len = n_pages * PAGE_SIZE
    wb = win_k.shape[1]
    assert wb <= WINDOW and past_len % SEL_LEN == 0 and past_len // SEL_LEN < HEAD_DIM
    assert N_KV == 2
    zs3 = zs.reshape(bs, 1, NZ)
    zspec = lambda w, off: pl.BlockSpec((None, 1, w), lambda b, pt: (b, 0, off // w))
    anyspec = pl.BlockSpec(memory_space=pl.ANY)
    wspec = pl.BlockSpec((None, N_KV * wb, HEAD_DIM), lambda b, pt: (b, 0, 0))
    rows2d = lambda a: a.reshape(a.shape[0], a.shape[1] * N_KV, HEAD_DIM)
    out = pl.pallas_call(
        functools.partial(_nsa_sample_kernel, n_pages=n_pages, past_len=past_len),
        out_shape=jax.ShapeDtypeStruct((bs, 1, A_Q), BF16),
        grid_spec=pltpu.PrefetchScalarGridSpec(
            num_scalar_prefetch=1, grid=(bs,),
            in_specs=[zspec(A_Q, Z_QRAW), zspec(A_Q, Z_QROT), zspec(2 * A_KV, Z_KS), zspec(2 * A_KV, Z_KW),
                      zspec(2 * HEAD_DIM, Z_GNSA), pl.BlockSpec(w2.shape, lambda b, pt: (0, 0, 0)), wspec, wspec,
                      anyspec, anyspec, anyspec, anyspec],
            out_specs=pl.BlockSpec((None, 1, A_Q), lambda b, pt: (b, 0, 0)),
            scratch_shapes=[pltpu.VMEM((2, 2, n_pages * ROWS_PER_PAGE, ZW), F32),
                            pltpu.VMEM((2, 2, N_KV * past_len, HEAD_DIM), F32), pltpu.SemaphoreType.DMA((2,))]),
        compiler_params=pltpu.CompilerParams(dimension_semantics=("arbitrary",), vmem_limit_bytes=VMEM_LIMIT),
        name="nsa_sample",
    )(page_table, zs3, zs3, zs3, zs3, zs3, w2, rows2d(win_k), rows2d(win_v), zk_pool, zv_pool,
      rows2d(sel_k), rows2d(sel_v))
    return out.reshape(bs, A_Q)


def _mlstm_sample_kernel(q_ref, k_ref, v_ref, op_ref, g_ref, gain_ref, c_ref, n_ref, m_ref,
                         h_ref, co_ref, no_ref, mo_ref):
    eye = (lax.broadcasted_iota(jnp.int32, (DQK_B, DQK_B), 0) == lax.broadcasted_iota(jnp.int32, (DQK_B, DQK_B), 1))
    col = lambda row: jnp.sum(jnp.where(eye, row, 0.0), axis=1, keepdims=True)
    lane = lax.broadcasted_iota(jnp.int32, (1, N_HEADS_B), 1)
    m_out = jnp.zeros((1, N_HEADS_B), F32)
    for h in range(N_HEADS_B):
        q = q_ref[:, h * DQK_B:(h + 1) * DQK_B]
        k = k_ref[:, h * DQK_B:(h + 1) * DQK_B] * DQK_B ** -0.5
        v = v_ref[:, h * DV_B:(h + 1) * DV_B]
        log_i, log_f = _gate_logs(g_ref[:, h:h + 1], g_ref[:, N_HEADS_B + h:N_HEADS_B + h + 1])
        c_old, n_old, m_old = c_ref[h], n_ref[h:h + 1, :], m_ref[:, h:h + 1]
        inter = log_f + m_old
        m_new = jnp.maximum(inter, log_i)
        decay = jnp.exp(inter - m_new)
        w = jnp.exp(log_i - m_new)
        co_ref[h] = decay * c_old + col(w * k) * v
        no_ref[h:h + 1, :] = decay * n_old + w * k
        m_out = jnp.where(lane == h, m_new, m_out)
        w_intra = w * jnp.sum(q * k, axis=1, keepdims=True)
        num = decay * jnp.sum(c_old * col(q), axis=0, keepdims=True) + w_intra * v
        den = decay * jnp.sum(q * n_old, axis=1, keepdims=True) + w_intra
        hh = num / jnp.maximum(jnp.abs(den), jnp.exp(-m_new))
        out = _rms(hh, gain_ref[h]) * jax.nn.sigmoid(op_ref[:, h * DV_B:(h + 1) * DV_B])
        h_ref[:, h * DV_B:(h + 1) * DV_B] = out.astype(h_ref.dtype)
    mo_ref[...] = m_out


def _mlstm_sample(zs, gain, c0, n0, m0):
    bs, H = zs.shape[0], N_HEADS_B
    zs3 = zs.reshape(bs, 1, NZ)
    zspec = lambda w, off: pl.BlockSpec((None, 1, w), lambda b: (b, 0, off // w))
    cspec = pl.BlockSpec((None, H, DQK_B, DV_B), lambda b: (b, 0, 0, 0))
    nspec = pl.BlockSpec((None, H, DQK_B), lambda b: (b, 0, 0))
    mspec = pl.BlockSpec((None, 1, H), lambda b: (b, 0, 0))
    h, c, n, m = pl.pallas_call(
        _mlstm_sample_kernel,
        out_shape=(jax.ShapeDtypeStruct((bs, 1, B_V), BF16), jax.ShapeDtypeStruct(c0.shape, F32),
                   jax.ShapeDtypeStruct(n0.shape, F32), jax.ShapeDtypeStruct((bs, 1, H), F32)),
        grid=(bs,),
        in_specs=[zspec(B_QK, Z_QB), zspec(B_QK, Z_KB), zspec(B_V, Z_VB), zspec(B_V, Z_OB), zspec(HEAD_DIM, Z_IB),
                  pl.BlockSpec((H, 1, DV_B), lambda b: (0, 0, 0)), cspec, nspec, mspec],
        out_specs=(pl.BlockSpec((None, 1, B_V), lambda b: (b, 0, 0)), cspec, nspec, mspec),
        compiler_params=pltpu.CompilerParams(dimension_semantics=("parallel",), vmem_limit_bytes=VMEM_LIMIT),
        name="mlstm_sample",
    )(zs3, zs3, zs3, zs3, zs3, gain.reshape(H, 1, DV_B), c0, n0, m0.reshape(bs, 1, H))
    return h.reshape(bs, B_V), c, n, m.reshape(bs, H)


def _split_kv(z, B, T):
    z = z.reshape(B, T, NZ)
    kv = lambda o: z[..., o:o + A_KV].reshape(B, T, N_KV, HEAD_DIM)
    return kv(Z_KC), kv(Z_VC), kv(Z_KS), kv(Z_VS), kv(Z_KW), kv(Z_VW)


def kernel(x_prompt, x_sample, cache_cmp_k, cache_cmp_v, cache_sel_k, cache_sel_v, cache_win_k, cache_win_v,
           state_mlstm_c, state_mlstm_n, state_mlstm_m, page_table, norm_mix, w_in, b_in, cmp_pos, cmp_w1, cmp_w2,
           mlstm_norm, w_up_a, w_up_b, w_out, norm_ffn, w_router_grp, b_router_grp, w_router_exp, b_router_exp,
           w_gate, w_up, w_down, norm_final):
    assert w_in.shape[0] == 1, "single layer"
    Bp, Tp = x_prompt.shape[:2]
    Bs, Ts = x_sample.shape[:2]
    assert Ts == 1
    n_p, n_s = Bp * Tp, Bs * Ts
    past_len = page_table.shape[1] * PAGE_SIZE
    pos_p = jnp.arange(Tp, dtype=jnp.int32)
    pos_s = past_len + jnp.arange(Ts, dtype=jnp.int32)
    l = 0

    w_perm, b_perm = _permute_in_proj(w_in[l], b_in[l])
    cw = _compress_weights(cmp_pos[l], cmp_w1[l], cmp_w2[l])
    mw = _merge_weights(w_up_a[l], w_up_b[l], w_out[l], norm_ffn[l], w_router_grp[l], b_router_grp[l],
                        w_router_exp[l], b_router_exp[l])

    xp2d = x_prompt.reshape(n_p, D_MODEL)
    zp = _project(xp2d, jnp.tile(pos_p, Bp), norm_mix[l], w_perm, b_perm, 512)
    kc, vc, ks, vs, kw, vw = _split_kv(zp, Bp, Tp)
    o_a = _nsa_prompt(zp, _compress_prompt(kc, vc, cw), Bp, Tp)
    h_b, c_p, n_p_state, m_p = _mlstm_prompt(zp, mlstm_norm[l], Bp, Tp)
    bufs = _merge(xp2d, o_a, h_b, zp, mw, n_p + n_s, 0, 256)
    wl = min(WINDOW, Tp)
    prompt_new = (kc, vc, ks, vs, kw[:, -wl:], vw[:, -wl:], c_p, n_p_state, m_p)

    xs2d = x_sample.reshape(n_s, D_MODEL)
    zs = _project(xs2d, jnp.repeat(pos_s, Bs), norm_mix[l], w_perm, b_perm, 128)
    kc, vc, ks, vs, kw, vw = _split_kv(zs, Bs, Ts)
    assert (past_len + Ts - CMP_LEN) // CMP_STRIDE + 1 == past_len // CMP_STRIDE - 1
    zk_pool, zv_pool = _compress_pool(cache_cmp_k[l], cache_cmp_v[l], cw)
    o_a_s = _nsa_sample(zs, page_table, zk_pool, zv_pool, cache_sel_k[l], cache_sel_v[l],
                        cache_win_k[l], cache_win_v[l], cw[2])
    h_s, c_s, n_s_state, m_s = _mlstm_sample(zs, mlstm_norm[l], state_mlstm_c[l], state_mlstm_n[l],
                                             state_mlstm_m[l])
    x2, xn, route = _merge(xs2d, o_a_s, h_s, zs, mw, n_p + n_s, n_p, 128, bufs=bufs)
    wk = jnp.concatenate([cache_win_k[l], kw], axis=1)
    wv = jnp.concatenate([cache_win_v[l], vw], axis=1)
    wl = min(WINDOW, past_len + Ts)
    sample_new = (kc, vc, ks, vs, wk[:, -wl:], wv[:, -wl:], c_s, n_s_state, m_s)

    y_p, y_s = _moe(x2, xn, route, w_gate[l], w_up[l], w_down[l], norm_final, n_p)
    return ((y_p.reshape(Bp, Tp, D_MODEL), y_s.reshape(Bs, Ts, D_MODEL))
            + tuple(a[None] for a in prompt_new) + tuple(a[None] for a in sample_new))
```

```python
import functools

import numpy as np
import jax
import jax.numpy as jnp
from jax import lax
from jax.experimental import pallas as pl
from jax.experimental.pallas import tpu as pltpu

D_MODEL = 2048
PAGE_SIZE = 128
N_HEADS_A = 8
N_KV = 2
GROUP = N_HEADS_A // N_KV
HEAD_DIM = 128
ROT_DIM = HEAD_DIM // 4
ROPE_THETA = 500000.0
CMP_LEN = 32
CMP_STRIDE = 16
SEL_LEN = 64
SEL_TOP = 16
FORCE_BONUS = 100.0
WINDOW = 512
Q_BLOCK = 128
N_HEADS_B = 4
DQK_B = 128
DV_B = 256
MLSTM_CHUNK = 64
GATE_CAP = 15.0
N_GROUPS = 4
EXPERTS_PER_GROUP = 8
N_EXPERTS = N_GROUPS * EXPERTS_PER_GROUP
TOP_K = 2
D_EXPERT = 1024
EPS = 1e-6
NEG = -1e30

A_Q = N_HEADS_A * HEAD_DIM
A_KV = N_KV * HEAD_DIM
B_QK = N_HEADS_B * DQK_B
B_V = N_HEADS_B * DV_B
IN_SPLITS = (A_Q, A_KV, A_KV, A_KV, A_KV, A_KV, A_KV, 3 * N_HEADS_A,
             B_QK, B_QK, B_V, N_HEADS_B, N_HEADS_B, B_V, D_MODEL, D_MODEL)

F32 = jnp.float32
BF16 = jnp.bfloat16
VMEM_LIMIT = 48 * 1024 * 1024
VMEM_LIMIT_MERGE = 56 * 1024 * 1024

_CUTS = np.concatenate([[0], np.cumsum(IN_SPLITS)]).tolist()
_SRC = dict(zip(('q_a', 'k_c', 'v_c', 'k_s', 'v_s', 'k_w', 'v_w', 'g_nsa', 'q_b', 'k_b', 'v_b', 'i_b', 'f_b',
                 'o_b', 'g_ma', 'g_mb'), zip(_CUTS[:-1], _CUTS[1:])))
_Z_ORDER = ('g_ma', 'g_mb', 'q_a', 'q_a', 'v_b', 'o_b', 'k_c', 'v_c', 'k_s', 'v_s', 'k_w', 'v_w', 'q_b', 'k_b')
PROJ_TN = 512
Z_GMA, Z_GMB, Z_QRAW, Z_QROT, Z_VB, Z_OB = 0, 2048, 4096, 5120, 6144, 7168
Z_KC, Z_VC, Z_KS, Z_VS, Z_KW, Z_VW = 8192, 8448, 8704, 8960, 9216, 9472
Z_QB, Z_KB, Z_SMALL = 9728, 10240, 10752
N_GATE = 3 * GROUP
Z_GNSA, Z_IB, Z_FB = Z_SMALL, Z_SMALL + 2 * HEAD_DIM, Z_SMALL + 2 * HEAD_DIM + N_HEADS_B
NZ = 11264
_ROPE_ALL_TILES = (Z_QROT // PROJ_TN, Z_QROT // PROJ_TN + 1)
_ROPE_HALF_TILES = (Z_KS // PROJ_TN, Z_KW // PROJ_TN)

ROUTE_W = 128
MOE_TILE = 512
MOE_SUB = 256
MOE_FC = 256


def _rms(x, g):
    return x * lax.rsqrt(jnp.mean(x * x, axis=-1, keepdims=True) + EPS) * g


def _proj_kernel(x_ref, g_ref, w_ref, b_ref, cos_ref, sa_ref, sb_ref, z_ref, xn_ref):
    j = pl.program_id(1)

    @pl.when(j == 0)
    def _():
        xn_ref[...] = _rms(x_ref[...], g_ref[...]).astype(BF16)

    acc = jnp.dot(xn_ref[...], w_ref[...], preferred_element_type=F32) + b_ref[...]

    def rope(blk):
        return (blk * cos_ref[...] + pltpu.roll(blk, HEAD_DIM - ROT_DIM // 2, 1) * sa_ref[...]
                + pltpu.roll(blk, ROT_DIM // 2, 1) * sb_ref[...])

    is_all = (j == _ROPE_ALL_TILES[0]) | (j == _ROPE_ALL_TILES[1])
    is_half = (j == _ROPE_HALF_TILES[0]) | (j == _ROPE_HALF_TILES[1])
    n_blk = PROJ_TN // HEAD_DIM

    @pl.when(is_all)
    def _():
        for c in range(n_blk):
            z_ref[:, c * HEAD_DIM:(c + 1) * HEAD_DIM] = rope(acc[:, c * HEAD_DIM:(c + 1) * HEAD_DIM])

    @pl.when(is_half)
    def _():
        for c in range(n_blk // 2):
            z_ref[:, c * HEAD_DIM:(c + 1) * HEAD_DIM] = rope(acc[:, c * HEAD_DIM:(c + 1) * HEAD_DIM])
        z_ref[:, PROJ_TN // 2:] = acc[:, PROJ_TN // 2:]

    @pl.when(jnp.logical_not(is_all | is_half))
    def _():
        z_ref[...] = acc


def _rope_tables(pos):
    half = ROT_DIM // 2
    inv = ROPE_THETA ** (-jnp.arange(half, dtype=F32) / half)
    ang = pos.astype(F32)[:, None] * inv[None, :]
    cos, sin = jnp.cos(ang), jnp.sin(ang)
    n = pos.shape[0]
    ones = jnp.ones((n, HEAD_DIM - ROT_DIM), F32)
    zeros = jnp.zeros((n, HEAD_DIM - half), F32)
    cos_t = jnp.concatenate([cos, cos, ones], axis=1)
    sa_t = jnp.concatenate([-sin, zeros], axis=1)
    sb_t = jnp.concatenate([jnp.zeros((n, half), F32), sin, jnp.zeros((n, HEAD_DIM - ROT_DIM), F32)], axis=1)
    return cos_t, sa_t, sb_t


def _permute_in_proj(w_in, b_in):
    g0 = _SRC['g_nsa'][0]
    spans = [_SRC[k] for k in _Z_ORDER]
    spans += [(g0, g0 + N_GATE), HEAD_DIM - N_GATE, (g0 + N_GATE, g0 + 2 * N_GATE), HEAD_DIM - N_GATE,
              _SRC['i_b'], _SRC['f_b'], NZ - Z_FB - N_HEADS_B]
    wb = jnp.concatenate([w_in, b_in[None, :]], axis=0)
    parts = [jnp.zeros((D_MODEL + 1, s), wb.dtype) if isinstance(s, int) else wb[:, s[0]:s[1]] for s in spans]
    wb = jnp.concatenate(parts, axis=1)
    return wb[:D_MODEL].astype(BF16), wb[D_MODEL:].astype(F32)


def _project(x2d, pos_rows, norm_g, w_perm, b_perm, tm):
    m = x2d.shape[0]
    cos_t, sa_t, sb_t = _rope_tables(pos_rows)
    row = lambda i, j: (i, 0)
    return pl.pallas_call(
        _proj_kernel,
        out_shape=jax.ShapeDtypeStruct((m, NZ), F32),
        grid=(m // tm, NZ // PROJ_TN),
        in_specs=[pl.BlockSpec((tm, D_MODEL), row),
                  pl.BlockSpec((1, D_MODEL), lambda i, j: (0, 0)),
                  pl.BlockSpec((D_MODEL, PROJ_TN), lambda i, j: (0, j)),
                  pl.BlockSpec((1, PROJ_TN), lambda i, j: (0, j)),
                  pl.BlockSpec((tm, HEAD_DIM), row),
                  pl.BlockSpec((tm, HEAD_DIM), row),
                  pl.BlockSpec((tm, HEAD_DIM), row)],
        out_specs=pl.BlockSpec((tm, PROJ_TN), lambda i, j: (i, j)),
        scratch_shapes=[pltpu.VMEM((tm, D_MODEL), BF16)],
        compiler_params=pltpu.CompilerParams(dimension_semantics=("parallel", "arbitrary"),
                                             vmem_limit_bytes=VMEM_LIMIT),
        name="proj",
    )(x2d, norm_g.reshape(1, D_MODEL), w_perm, b_perm, cos_t, sa_t, sb_t)


def _compress_body(x_ref, w1, pb, w2, r):
    bias = pb[0:1, :HEAD_DIM] + pb[1:2, HEAD_DIM:]
    outs = []
    for h in range(N_KV):
        xh = jnp.concatenate([x_ref[:, (N_KV * s + h) * HEAD_DIM:(N_KV * s + h + 1) * HEAD_DIM]
                              for s in range(CMP_STRIDE)], axis=1).astype(BF16)
        zz = jnp.dot(xh, w1, preferred_element_type=F32)
        pre = zz[:, :HEAD_DIM] + pltpu.roll(zz[:, HEAD_DIM:], r - 1, 0) + bias
        mid = pre * jax.nn.sigmoid(pre)
        outs.append(jnp.dot(mid.astype(BF16), w2, preferred_element_type=F32))
    return outs


def _compress_kernel(xk_ref, xv_ref, w1_ref, p_ref, w2_ref, o_ref):
    r = xk_ref.shape[0]
    for which, x_ref in enumerate((xk_ref, xv_ref)):
        w1 = w1_ref[which]
        pb = jnp.dot(p_ref[which], w1, preferred_element_type=F32)
        for h, o in enumerate(_compress_body(x_ref, w1, pb, w2_ref[which], r)):
            o_ref[which, h] = o


def _compress_weights(cmp_pos, cmp_w1, cmp_w2):
    assert CMP_LEN == 2 * CMP_STRIDE
    half = CMP_STRIDE * HEAD_DIM
    w1 = jnp.concatenate([cmp_w1[:, :half], cmp_w1[:, half:]], axis=2).astype(BF16)
    p = cmp_pos.reshape(2, 2, half)
    p = jnp.concatenate([p, jnp.zeros((2, 6, half), p.dtype)], axis=1).astype(BF16)
    return w1, p, cmp_w2.astype(BF16)


def _compress_prompt(kc, vc, cw):
    B, T = kc.shape[:2]
    r = T // CMP_STRIDE
    width = CMP_STRIDE * N_KV * HEAD_DIM
    w1, p, w2 = cw
    full = lambda a: pl.BlockSpec(a.shape, lambda b: (0,) * a.ndim)
    xspec = pl.BlockSpec((None, r, width), lambda b: (b, 0, 0))
    return pl.pallas_call(
        _compress_kernel,
        out_shape=jax.ShapeDtypeStruct((B, 2, N_KV, r, HEAD_DIM), F32),
        grid=(B,),
        in_specs=[xspec, xspec, full(w1), full(p), full(w2)],
        out_specs=pl.BlockSpec((None, 2, N_KV, r, HEAD_DIM), lambda b: (b, 0, 0, 0, 0)),
        compiler_params=pltpu.CompilerParams(dimension_semantics=("parallel",), vmem_limit_bytes=VMEM_LIMIT),
        name="compress_prompt",
    )(kc.reshape(B, r, width), vc.reshape(B, r, width), w1, p, w2)


def _group_scores(q, k, tq):
    s = lax.dot_general(q, k, (((1,), (1,)), ((), ())), preferred_element_type=F32) * HEAD_DIM ** -0.5
    return s.reshape(GROUP, tq, k.shape[0])


def _stack_heads(ref):
    return jnp.concatenate([ref[:, g * HEAD_DIM:(g + 1) * HEAD_DIM] for g in range(GROUP)], axis=0).astype(BF16)


def _importance(psum, n_sel):
    r = psum.shape[1]
    ci = lax.broadcasted_iota(jnp.int32, (r, n_sel), 0) * CMP_STRIDE
    cj = lax.broadcasted_iota(jnp.int32, (r, n_sel), 1) * SEL_LEN
    cover_t = ((ci < cj + SEL_LEN) & (ci + (CMP_LEN - 1) >= cj)).astype(BF16)
    p_hi = psum.astype(BF16)
    p_lo = (psum - p_hi.astype(F32)).astype(BF16)
    return jnp.dot(p_hi, cover_t, preferred_element_type=F32) + jnp.dot(p_lo, cover_t, preferred_element_type=F32)


def _block_scores(imp, blk, qpos):
    cur = jnp.right_shift(qpos, SEL_LEN.bit_length() - 1)
    forced = (blk == 0) | (blk == cur) | (blk == cur - 1)
    return jnp.where(blk * SEL_LEN <= qpos, imp + jnp.where(forced, FORCE_BONUS, 0.0), -1.0)


def _top_rank_mask(score, axis, n_cand):
    idx = lax.broadcasted_iota(jnp.int32, score.shape, axis)
    rank = jnp.zeros(score.shape, F32)
    for j in range(n_cand):
        cand = score[j:j + 1, :] if axis == 0 else score[:, j:j + 1]
        rank = rank + jnp.where((cand > score) | ((cand == score) & (idx > j)), 1.0, 0.0)
    return jnp.where(rank < float(SEL_TOP), 1.0, 0.0)


def _nsa_prompt_kernel(qraw_ref, qrot_ref, ks_ref, vs_ref, kw_ref, vw_ref, kc_ref, vc_ref, gate_ref, o_ref,
                       m_ref, l_ref, acc_ref, *, tq, tk, seq_len):
    q0 = pl.program_id(2) * tq
    r = kc_ref.shape[0]
    n_sel = seq_len // SEL_LEN
    sel_shift = SEL_LEN.bit_length() - 1
    qpos = q0 + lax.broadcasted_iota(jnp.int32, (tq, 1), 0)

    s = _group_scores(_stack_heads(qraw_ref), kc_ref[...].astype(BF16), tq)
    c_end = lax.broadcasted_iota(jnp.int32, (tq, r), 1) * CMP_STRIDE + (CMP_LEN - 1)
    cmask = c_end <= qpos
    s = s + jnp.where(cmask, 0.0, NEG)[None]
    p = jnp.exp(s - jnp.max(s, axis=-1, keepdims=True)) * jnp.where(cmask, 1.0, 0.0)[None]
    p = p / jnp.maximum(jnp.sum(p, axis=-1, keepdims=True), 1e-30)
    o_cmp = jnp.dot(p.reshape(GROUP * tq, r).astype(BF16), vc_ref[...].astype(BF16), preferred_element_type=F32)
    psum = p[0]
    for g in range(1, GROUP):
        psum = psum + p[g]
    imp_t = _importance(psum, HEAD_DIM).T[:n_sel]
    blk_t = lax.broadcasted_iota(jnp.int32, (n_sel, tq), 0)
    qpos_t = q0 + lax.broadcasted_iota(jnp.int32, (n_sel, tq), 1)
    sel_b = _top_rank_mask(_block_scores(imp_t, blk_t, qpos_t), 0, n_sel).astype(BF16)

    q_rot = _stack_heads(qrot_ref)
    m_ref[...] = jnp.full(m_ref.shape, NEG, F32)
    l_ref[...] = jnp.zeros(l_ref.shape, F32)
    acc_ref[...] = jnp.zeros(acc_ref.shape, F32)

    def sel_tile(kt, carry):
        k0 = pl.multiple_of(kt * tk, tk)
        k = ks_ref[pl.ds(k0, tk), :].astype(BF16)
        v = vs_ref[pl.ds(k0, tk), :].astype(BF16)
        s2 = _group_scores(q_rot, k, tq)
        ej = lax.broadcasted_iota(jnp.int32, (n_sel, tk), 0)
        et = k0 + lax.broadcasted_iota(jnp.int32, (n_sel, tk), 1)
        expand = jnp.where(jnp.right_shift(et, sel_shift) == ej, 1.0, 0.0).astype(BF16)
        selm = lax.dot_general(sel_b, expand, (((0,), (0,)), ((), ())), preferred_element_type=F32)
        kpos = k0 + lax.broadcasted_iota(jnp.int32, (tq, tk), 1)
        ok = (selm > 0.5) & (kpos <= qpos)
        s2 = s2 + jnp.where(ok, 0.0, NEG)[None]
        m_prev = m_ref[...]
        m_new = jnp.maximum(m_prev, jnp.max(s2, axis=-1, keepdims=True))
        alpha = jnp.exp(m_prev - m_new)
        p2 = jnp.exp(s2 - m_new)
        l_ref[...] = alpha * l_ref[...] + jnp.sum(p2, axis=-1, keepdims=True)
        pv = jnp.dot(p2.reshape(GROUP * tq, tk).astype(BF16), v, preferred_element_type=F32)
        acc_ref[...] = alpha.reshape(GROUP * tq, 1) * acc_ref[...] + pv
        m_ref[...] = m_new
        return carry

    lax.fori_loop(0, lax.div(q0 + tq + tk - 1, tk), sel_tile, 0)
    o_sel = acc_ref[...] / jnp.maximum(l_ref[...].reshape(GROUP * tq, 1), 1e-30)

    span = WINDOW + tq
    w0 = pl.multiple_of(jnp.maximum(q0 - WINDOW, 0), tq)
    s3 = _group_scores(q_rot, kw_ref[pl.ds(w0, span), :].astype(BF16), tq)
    kp = w0 + lax.broadcasted_iota(jnp.int32, (tq, span), 1)
    okw = (kp <= qpos) & (kp >= qpos - WINDOW)
    s3 = s3 + jnp.where(okw, 0.0, NEG)[None]
    p3 = jnp.exp(s3 - jnp.max(s3, axis=-1, keepdims=True))
    p3 = p3 / jnp.sum(p3, axis=-1, keepdims=True)
    o_win = jnp.dot(p3.reshape(GROUP * tq, span).astype(BF16), vw_ref[pl.ds(w0, span), :].astype(BF16),
                    preferred_element_type=F32)

    gts = jax.nn.sigmoid(gate_ref[...])
    for g in range(GROUP):
        rows = slice(g * tq, (g + 1) * tq)
        o_ref[:, g * HEAD_DIM:(g + 1) * HEAD_DIM] = (gts[:, 3 * g:3 * g + 1] * o_cmp[rows]
                                                     + gts[:, 3 * g + 1:3 * g + 2] * o_sel[rows]
                                                     + gts[:, 3 * g + 2:3 * g + 3] * o_win[rows]).astype(o_ref.dtype)


def _nsa_prompt(z, cmp, B, T, tq=128, tk=512):
    assert T % tk == 0 and tk % tq == 0 and tq % HEAD_DIM == 0 and T >= WINDOW + tq
    assert SEL_LEN & (SEL_LEN - 1) == 0 and tk % SEL_LEN == 0
    assert (T // SEL_LEN) % 8 == 0 and T // SEL_LEN <= HEAD_DIM and tq == HEAD_DIM
    nq = T // tq
    r = cmp.shape[3]
    gw = GROUP * HEAD_DIM
    qspec = lambda off: pl.BlockSpec((tq, gw), lambda b, h, q: (b * nq + q, off // gw + h))
    kvspec = lambda off: pl.BlockSpec((T, HEAD_DIM), lambda b, h, q: (b, off // HEAD_DIM + h))
    cspec = lambda which: pl.BlockSpec((None, None, None, r, HEAD_DIM), lambda b, h, q: (b, which, h, 0, 0))
    return pl.pallas_call(
        functools.partial(_nsa_prompt_kernel, tq=tq, tk=tk, seq_len=T),
        out_shape=jax.ShapeDtypeStruct((B * T, A_Q), BF16),
        grid=(B, N_KV, nq),
        in_specs=[qspec(Z_QRAW), qspec(Z_QROT), kvspec(Z_KS), kvspec(Z_VS), kvspec(Z_KW), kvspec(Z_VW),
                  cspec(0), cspec(1),
                  pl.BlockSpec((tq, HEAD_DIM), lambda b, h, q: (b * nq + q, Z_GNSA // HEAD_DIM + h))],
        out_specs=pl.BlockSpec((tq, gw), lambda b, h, q: (b * nq + q, h)),
        scratch_shapes=[pltpu.VMEM((GROUP, tq, 1), F32), pltpu.VMEM((GROUP, tq, 1), F32),
                        pltpu.VMEM((GROUP * tq, HEAD_DIM), F32)],
        compiler_params=pltpu.CompilerParams(dimension_semantics=("parallel", "parallel", "arbitrary"),
                                             vmem_limit_bytes=VMEM_LIMIT),
        name="nsa_prompt",
    )(z, z, z, z, z, z, cmp, cmp, z)


def _gate_logs(i_pre, f_pre):
    log_i = GATE_CAP * jnp.tanh(i_pre / GATE_CAP)
    fc = GATE_CAP * jnp.tanh(f_pre / GATE_CAP)
    log_f = jnp.minimum(fc, 0.0) - jnp.log1p(jnp.exp(-jnp.abs(fc)))
    return log_i, log_f


def _mlstm_prompt_kernel(q_ref, k_ref, v_ref, op_ref, grow_ref, gcol_ref, gain_ref,
                         h_ref, c_ref, n_ref, m_ref):
    @pl.when(pl.program_id(1) == 0)
    def _():
        c_ref[...] = jnp.zeros(c_ref.shape, F32)
        n_ref[...] = jnp.zeros(n_ref.shape, F32)
        m_ref[...] = jnp.zeros(m_ref.shape, F32)

    for h in range(N_HEADS_B):
        qk_cols = slice(h * DQK_B, (h + 1) * DQK_B)
        v_cols = slice(h * DV_B, (h + 1) * DV_B)
        _mlstm_chunk(q_ref[:, qk_cols], k_ref[:, qk_cols], v_ref[:, v_cols], op_ref[:, v_cols],
                     grow_ref[h:h + 1, :], grow_ref[N_HEADS_B + h:N_HEADS_B + h + 1, :],
                     gcol_ref[:, h:h + 1], gcol_ref[:, N_HEADS_B + h:N_HEADS_B + h + 1], gain_ref[h],
                     h_ref.at[:, v_cols], c_ref.at[h], n_ref.at[h], m_ref.at[h])


def _mlstm_chunk(q, k, v, o_pre, i_row, f_row, i_col, f_col, gain, h_ref, c_ref, n_ref, m_ref):
    L = q.shape[0]
    qb = q.astype(BF16)
    k = k * DQK_B ** -0.5
    vb = v.astype(BF16)
    i_r, f_r = _gate_logs(i_row, f_row)
    i_c, f_c = _gate_logs(i_col, f_col)
    li = lax.broadcasted_iota(jnp.int32, (L, L), 0)
    si = lax.broadcasted_iota(jnp.int32, (L, L), 1)
    tri = si <= li
    bcum_c = jnp.sum(jnp.where(tri, f_r, 0.0), axis=1, keepdims=True)
    bcum_r = jnp.sum(jnp.where(li <= si, f_c, 0.0), axis=0, keepdims=True)
    btot = jnp.sum(f_r, axis=1, keepdims=True)
    c_old, n_old, m_old = c_ref[...], n_ref[...], m_ref[0:1, 0:1]

    dlog = jnp.where(tri, bcum_c - bcum_r + i_r, NEG)
    inter = bcum_c + m_old
    m_t = jnp.maximum(inter, jnp.max(dlog, axis=1, keepdims=True))
    qk = lax.dot_general(qb, k.astype(BF16), (((1,), (1,)), ((), ())), preferred_element_type=F32)
    w_intra = jnp.exp(dlog - m_t) * qk
    w_inter = jnp.exp(inter - m_t)
    num = (w_inter * jnp.dot(qb, c_old.astype(BF16), preferred_element_type=F32)
           + jnp.dot(w_intra.astype(BF16), vb, preferred_element_type=F32))
    den = w_inter * jnp.sum(q * n_old, axis=1, keepdims=True) + jnp.sum(w_intra, axis=1, keepdims=True)
    h = num / jnp.maximum(jnp.abs(den), jnp.exp(-m_t))
    h_ref[...] = (_rms(h, gain) * jax.nn.sigmoid(o_pre)).astype(h_ref.dtype)

    wlog_c = btot - bcum_c + i_c
    m_new = jnp.maximum(btot + m_old, jnp.max(btot - bcum_r + i_r, axis=1, keepdims=True))
    decay = jnp.exp(btot + m_old - m_new)
    kw = k * jnp.exp(wlog_c - m_new)
    c_ref[...] = decay * c_old + lax.dot_general(kw.astype(BF16), vb, (((0,), (0,)), ((), ())),
                                                 preferred_element_type=F32)
    n_ref[...] = decay * n_old + jnp.sum(kw, axis=0, keepdims=True)
    m_ref[...] = jnp.broadcast_to(m_new, m_ref.shape)


def _mlstm_prompt(z, gain, B, T):
    L = MLSTM_CHUNK
    assert T % L == 0
    nc, H = T // L, N_HEADS_B
    gcol = z[:, Z_IB:Z_IB + 2 * H].reshape(B * nc, L, 2 * H)
    grow = gcol.transpose(0, 2, 1)
    rows = lambda w, off: pl.BlockSpec((L, w), lambda b, c: (b * nc + c, off // w))
    gidx = lambda b, c: (b * nc + c, 0, 0)
    state = lambda *s: pl.BlockSpec((None, H) + s, lambda b, c: (b, 0, 0, 0))
    h, c, n, m = pl.pallas_call(
        _mlstm_prompt_kernel,
        out_shape=(jax.ShapeDtypeStruct((B * T, B_V), BF16), jax.ShapeDtypeStruct((B, H, DQK_B, DV_B), F32),
                   jax.ShapeDtypeStruct((B, H, 1, DQK_B), F32), jax.ShapeDtypeStruct((B, H, 1, DQK_B), F32)),
        grid=(B, nc),
        in_specs=[rows(B_QK, Z_QB), rows(B_QK, Z_KB), rows(B_V, Z_VB), rows(B_V, Z_OB),
                  pl.BlockSpec((None, 2 * H, L), gidx), pl.BlockSpec((None, L, 2 * H), gidx),
                  pl.BlockSpec((H, 1, DV_B), lambda b, c: (0, 0, 0))],
        out_specs=(rows(B_V, 0), state(DQK_B, DV_B), state(1, DQK_B), state(1, DQK_B)),
        compiler_params=pltpu.CompilerParams(dimension_semantics=("parallel", "arbitrary"),
                                             vmem_limit_bytes=VMEM_LIMIT),
        name="mlstm_prompt",
    )(z, z, z, z, grow, gcol, gain.reshape(H, 1, DV_B))
    return h, c, n[:, :, 0], m[:, :, 0, 0]


def _merge_kernel(x_ref, oa_ref, hb_ref, gma_ref, gmb_ref, wa_ref, wb_ref, wo_ref, nf_ref, wrh_ref, wrl_ref, br_ref,
                  *refs):
    x2_ref, xn_ref, route_ref = refs[-3:]
    ua = jnp.dot(oa_ref[...], wa_ref[...], preferred_element_type=F32)
    ub = jnp.dot(hb_ref[...], wb_ref[...], preferred_element_type=F32)
    merged = jax.nn.sigmoid(gma_ref[...]) * ua + jax.nn.sigmoid(gmb_ref[...]) * ub
    x2 = x_ref[...] + jnp.dot(merged.astype(BF16), wo_ref[...], preferred_element_type=F32)
    x2_ref[...] = x2
    xn = _rms(x2, nf_ref[...])
    xn_ref[...] = xn
    hi = xn.astype(BF16)
    lo = (xn - hi.astype(F32)).astype(BF16)
    lg = (jnp.dot(hi, wrh_ref[...], preferred_element_type=F32) + jnp.dot(lo, wrh_ref[...], preferred_element_type=F32)
          + jnp.dot(hi, wrl_ref[...], preferred_element_type=F32)) + br_ref[...]
    lane = lax.broadcasted_iota(jnp.int32, lg.shape, 1)
    lanef = lane.astype(F32)
    ninf = -jnp.inf
    gmask = (lane >= N_EXPERTS) & (lane < N_EXPERTS + N_GROUPS)
    gl = jnp.where(gmask, lg, ninf)
    gmax = jnp.max(gl, axis=1, keepdims=True)
    gidx = jnp.min(jnp.where(gl == gmax, lanef, 1e9), axis=1, keepdims=True) - N_EXPERTS
    gprob = 1.0 / jnp.sum(jnp.where(gmask, jnp.exp(lg - gmax), 0.0), axis=1, keepdims=True)
    e_lo = gidx * EXPERTS_PER_GROUP
    el = jnp.where((lanef >= e_lo) & (lanef < e_lo + EXPERTS_PER_GROUP), lg, ninf)
    v1 = jnp.max(el, axis=1, keepdims=True)
    i1 = jnp.min(jnp.where(el == v1, lanef, 1e9), axis=1, keepdims=True)
    el2 = jnp.where(lanef == i1, ninf, el)
    v2 = jnp.max(el2, axis=1, keepdims=True)
    i2 = jnp.min(jnp.where(el2 == v2, lanef, 1e9), axis=1, keepdims=True)
    e2 = jnp.exp(v2 - v1)
    g1 = gprob / (1.0 + e2)
    g2 = gprob * e2 / (1.0 + e2)
    route_ref[...] = jnp.where(lane == 0, i1, jnp.where(lane == 1, i2, jnp.where(lane == 2, g1,
                               jnp.where(lane == 3, g2, 0.0))))


def _merge_weights(w_up_a, w_up_b, w_out, norm_ffn, w_rg, b_rg, w_re, b_re):
    assert TOP_K == 2
    pad = ROUTE_W - N_EXPERTS - N_GROUPS
    w = jnp.concatenate([w_re, w_rg, jnp.zeros((D_MODEL, pad), F32)], axis=1)
    b = jnp.concatenate([b_re, b_rg, jnp.zeros((pad,), F32)]).reshape(1, ROUTE_W)
    hi = w.astype(BF16)
    return (w_up_a.astype(BF16), w_up_b.astype(BF16), w_out.astype(BF16), norm_ffn.reshape(1, D_MODEL).astype(F32),
            hi, (w - hi.astype(F32)).astype(BF16), b)


def _merge(x2d, o_a, h_b, z, mw, n_total, row0, tm, bufs=None):
    m = x2d.shape[0]
    i0 = row0 // tm
    row = lambda w, c=0: pl.BlockSpec((tm, w), lambda i: (i, c))
    const = lambda a: pl.BlockSpec(a.shape, lambda i: (0,) * a.ndim, pipeline_mode=pl.Buffered(1))
    outw = (D_MODEL, D_MODEL, ROUTE_W)
    in_specs = [row(D_MODEL), row(A_Q), row(B_V), row(D_MODEL, Z_GMA // D_MODEL), row(D_MODEL, Z_GMB // D_MODEL)]
    in_specs += [const(a) for a in mw]
    args = [x2d, o_a, h_b, z, z] + list(mw)
    aliases = {}
    if bufs is not None:
        in_specs += [pl.BlockSpec(memory_space=pl.ANY)] * 3
        aliases = {len(args) + k: k for k in range(3)}
        args += list(bufs)
    return pl.pallas_call(
        _merge_kernel,
        out_shape=tuple(jax.ShapeDtypeStruct((n_total, w), F32) for w in outw),
        grid=(m // tm,),
        in_specs=in_specs,
        out_specs=tuple(pl.BlockSpec((tm, w), lambda i: (i0 + i, 0)) for w in outw),
        input_output_aliases=aliases,
        compiler_params=pltpu.CompilerParams(dimension_semantics=("parallel",), vmem_limit_bytes=VMEM_LIMIT_MERGE),
        name="merge_route",
    )(*args)


def _moe_plan(route, n_tiles):
    n = route.shape[0]
    a = n * TOP_K
    flat_e = route[:, 0:TOP_K].astype(jnp.int32).reshape(a)
    onehot = (flat_e[:, None] == jnp.arange(N_EXPERTS, dtype=jnp.int32)[None, :]).astype(jnp.int32)
    rank = jnp.take_along_axis(jnp.cumsum(onehot, axis=0) - onehot, flat_e[:, None], axis=1)[:, 0]
    counts = jnp.sum(onehot, axis=0)
    ntile = (counts + MOE_TILE - 1) // MOE_TILE
    tile_end = jnp.cumsum(ntile)
    dest = ((tile_end - ntile)[flat_e] * MOE_TILE + rank).astype(jnp.int32)
    n_used = tile_end[-1].astype(jnp.int32)
    t = jnp.minimum(jnp.arange(n_tiles, dtype=jnp.int32), n_used - 1)
    tile_expert = jnp.minimum(jnp.searchsorted(tile_end, t, side='right'), N_EXPERTS - 1).astype(jnp.int32)
    src = jnp.zeros((n_tiles * MOE_TILE,), jnp.int32).at[dest].set(jnp.arange(a, dtype=jnp.int32) // TOP_K)
    first = (tile_end - ntile)[tile_expert]
    tile_rows = jnp.clip(counts[tile_expert] - (t - first) * MOE_TILE, 0, MOE_TILE)
    tile_rows = jnp.where(jnp.arange(n_tiles) < n_used, tile_rows, 0).astype(jnp.int32)
    return dest, src, tile_expert, tile_rows, n_used.reshape(1)


DMA_UNROLL = 8


def _for_rows(n, fn):
    groups = lax.div(n, DMA_UNROLL)

    def group(i, c):
        for u in range(DMA_UNROLL):
            fn(i * DMA_UNROLL + u)
        return c

    def single(r, c):
        fn(r)
        return c

    lax.fori_loop(0, groups, group, 0)
    lax.fori_loop(groups * DMA_UNROLL, n, single, 0)


def _row_gather_kernel(src_ref, cnt_ref, nu_ref, x_hbm, o_ref, sem):
    t = pl.program_id(0)
    rows = o_ref.shape[0]

    def copy(r):
        return pltpu.make_async_copy(x_hbm.at[pl.ds(src_ref[t * rows + r], 1)], o_ref.at[pl.ds(r, 1)], sem)

    _for_rows(cnt_ref[t], lambda r: copy(r).start())
    _for_rows(cnt_ref[t], lambda r: copy(r).wait())


def _moe_gather(xn, src, tile_rows, n_used, n_tiles):
    d = xn.shape[1]
    return pl.pallas_call(
        _row_gather_kernel,
        out_shape=jax.ShapeDtypeStruct((n_tiles * MOE_TILE, d), xn.dtype),
        grid_spec=pltpu.PrefetchScalarGridSpec(
            num_scalar_prefetch=3, grid=(n_tiles,),
            in_specs=[pl.BlockSpec(memory_space=pl.ANY)],
            out_specs=pl.BlockSpec((MOE_TILE, d), lambda t, src, cnt, nu: (jnp.minimum(t, nu[0] - 1), 0)),
            scratch_shapes=[pltpu.SemaphoreType.DMA(())]),
        compiler_params=pltpu.CompilerParams(dimension_semantics=("arbitrary",), vmem_limit_bytes=VMEM_LIMIT),
        name="moe_gather",
    )(src, tile_rows, n_used, xn)


def _expert_kernel(te_ref, cnt_ref, nu_ref, x_ref, wg_ref, wu_ref, wd_ref, o_ref, xb_ref):
    t, j = pl.program_id(0), pl.program_id(1)
    cnt = cnt_ref[t]

    @pl.when(cnt > 0)
    def _():
        wg, wu, wd = wg_ref[...].astype(BF16), wu_ref[...].astype(BF16), wd_ref[...].astype(BF16)
        for s in range(MOE_TILE // MOE_SUB):
            rows = slice(s * MOE_SUB, (s + 1) * MOE_SUB)

            @pl.when(s * MOE_SUB < cnt)
            def _():
                @pl.when(j == 0)
                def _():
                    xb_ref[rows] = x_ref[rows].astype(BF16)

                xb = xb_ref[rows]
                g = jnp.dot(xb, wg, preferred_element_type=F32)
                u = jnp.dot(xb, wu, preferred_element_type=F32)
                h = (g * jax.nn.sigmoid(g) * u).astype(BF16)
                y = jnp.dot(h, wd, preferred_element_type=F32)

                @pl.when(j == 0)
                def _():
                    o_ref[rows] = y

                @pl.when(j > 0)
                def _():
                    o_ref[rows] += y


def _moe_experts(xs, tile_expert, tile_rows, n_used, w_gate, w_up, w_down):
    n_tiles = xs.shape[0] // MOE_TILE
    nj = D_EXPERT // MOE_FC
    tix = lambda t, j, te, cnt, nu: (jnp.minimum(t, nu[0] - 1), 0)
    jj = lambda t, j, nu: jnp.where(t < nu[0], j, nj - 1)
    return pl.pallas_call(
        _expert_kernel,
        out_shape=jax.ShapeDtypeStruct(xs.shape, F32),
        grid_spec=pltpu.PrefetchScalarGridSpec(
            num_scalar_prefetch=3, grid=(n_tiles, nj),
            in_specs=[pl.BlockSpec((MOE_TILE, D_MODEL), tix),
                      pl.BlockSpec((None, D_MODEL, MOE_FC), lambda t, j, te, cnt, nu: (te[t], 0, jj(t, j, nu))),
                      pl.BlockSpec((None, D_MODEL, MOE_FC), lambda t, j, te, cnt, nu: (te[t], 0, jj(t, j, nu))),
                      pl.BlockSpec((None, MOE_FC, D_MODEL), lambda t, j, te, cnt, nu: (te[t], jj(t, j, nu), 0))],
            out_specs=pl.BlockSpec((MOE_TILE, D_MODEL), tix),
            scratch_shapes=[pltpu.VMEM((MOE_TILE, D_MODEL), BF16)]),
        compiler_params=pltpu.CompilerParams(dimension_semantics=("arbitrary", "arbitrary"),
                                             vmem_limit_bytes=VMEM_LIMIT),
        name="moe_experts",
    )(tile_expert, tile_rows, n_used, xs, w_gate, w_up, w_down)


def _combine_kernel(dest_ref, x2_ref, route_ref, g_ref, y_hbm, op_ref, os_ref, buf_ref, sem, *, n_prompt_tiles):
    i = pl.program_id(0)
    tm = x2_ref.shape[0]

    def copy(r, k):
        d = dest_ref[(i * tm + r) * TOP_K + k]
        return pltpu.make_async_copy(y_hbm.at[pl.ds(d, 1)], buf_ref.at[k, pl.ds(r, 1)], sem)

    def start(r, c):
        for k in range(TOP_K):
            copy(r, k).start()
        return c

    def wait(r, c):
        for k in range(TOP_K):
            copy(r, k).wait()
        return c

    lax.fori_loop(0, tm, start, 0, unroll=DMA_UNROLL)
    lax.fori_loop(0, tm, wait, 0, unroll=DMA_UNROLL)
    ffn = buf_ref[0] * route_ref[:, TOP_K:TOP_K + 1]
    for k in range(1, TOP_K):
        ffn = ffn + buf_ref[k] * route_ref[:, TOP_K + k:TOP_K + k + 1]
    y = _rms(x2_ref[...] + ffn, g_ref[...])

    @pl.when(i < n_prompt_tiles)
    def _():
        op_ref[...] = y

    @pl.when(i >= n_prompt_tiles)
    def _():
        os_ref[...] = y


def _moe_combine(x2, route, dest, ys, norm_final, n_prompt, tm=128):
    n, d = x2.shape
    npt = n_prompt // tm
    nst = (n - n_prompt) // tm
    return pl.pallas_call(
        functools.partial(_combine_kernel, n_prompt_tiles=npt),
        out_shape=(jax.ShapeDtypeStruct((n_prompt, d), F32), jax.ShapeDtypeStruct((n - n_prompt, d), F32)),
        grid_spec=pltpu.PrefetchScalarGridSpec(
            num_scalar_prefetch=1, grid=(n // tm,),
            in_specs=[pl.BlockSpec((tm, d), lambda i, ds: (i, 0)), pl.BlockSpec((tm, ROUTE_W), lambda i, ds: (i, 0)),
                      pl.BlockSpec((1, d), lambda i, ds: (0, 0)), pl.BlockSpec(memory_space=pl.ANY)],
            out_specs=(pl.BlockSpec((tm, d), lambda i, ds: (jnp.minimum(i, npt - 1), 0)),
                       pl.BlockSpec((tm, d), lambda i, ds: (jnp.clip(i - npt, 0, nst - 1), 0))),
            scratch_shapes=[pltpu.VMEM((TOP_K, tm, d), F32), pltpu.SemaphoreType.DMA(())]),
        compiler_params=pltpu.CompilerParams(dimension_semantics=("arbitrary",), vmem_limit_bytes=VMEM_LIMIT),
        name="moe_combine",
    )(dest, x2, route, norm_final.reshape(1, d), ys)


def _moe(x2, xn, route, w_gate, w_up, w_down, norm_final, n_prompt):
    n = x2.shape[0]
    n_tiles = (n * TOP_K + N_EXPERTS * (MOE_TILE - 1) + MOE_TILE - 1) // MOE_TILE
    dest, src, tile_expert, tile_rows, n_used = _moe_plan(route, n_tiles)
    xs = _moe_gather(xn, src, tile_rows, n_used, n_tiles)
    ys = _moe_experts(xs, tile_expert, tile_rows, n_used, w_gate, w_up, w_down)
    return _moe_combine(x2, route, dest, ys, norm_final, n_prompt)


POOL_ROWS = 256
ROWS_PER_PAGE = PAGE_SIZE // CMP_STRIDE
ZW = 2 * N_KV * HEAD_DIM


def _compress_pool_kernel(xk_ref, xv_ref, w1_ref, p_ref, zk_ref, zv_ref):
    for which, (x_ref, z_ref) in enumerate(((xk_ref, zk_ref), (xv_ref, zv_ref))):
        w1 = w1_ref[which]
        pb = jnp.dot(p_ref[which], w1, preferred_element_type=F32)
        bias = pb[0:1, :HEAD_DIM] + pb[1:2, HEAD_DIM:]
        for h in range(N_KV):
            xh = jnp.concatenate([x_ref[pl.ds(N_KV * s + h, POOL_ROWS, stride=N_KV * CMP_STRIDE), :]
                                  for s in range(CMP_STRIDE)], axis=1).astype(BF16)
            zz = jnp.dot(xh, w1, preferred_element_type=F32)
            z_ref[:, 2 * h * HEAD_DIM:(2 * h + 1) * HEAD_DIM] = zz[:, :HEAD_DIM] + bias
            z_ref[:, (2 * h + 1) * HEAD_DIM:(2 * h + 2) * HEAD_DIM] = zz[:, HEAD_DIM:]


def _compress_pool(pool_k, pool_v, cw):
    n_phys = pool_k.shape[0]
    rows = n_phys * ROWS_PER_PAGE
    assert rows % POOL_ROWS == 0
    in_rows = POOL_ROWS * CMP_STRIDE * N_KV
    w1, p, _ = cw
    full = lambda a: pl.BlockSpec(a.shape, lambda i: (0,) * a.ndim)
    flat = lambda a: a.reshape(n_phys * PAGE_SIZE * N_KV, HEAD_DIM)
    zk, zv = pl.pallas_call(
        _compress_pool_kernel,
        out_shape=(jax.ShapeDtypeStruct((rows, ZW), F32),) * 2,
        grid=(rows // POOL_ROWS,),
        in_specs=[pl.BlockSpec((in_rows, HEAD_DIM), lambda i: (i, 0))] * 2 + [full(w1), full(p)],
        out_specs=(pl.BlockSpec((POOL_ROWS, ZW), lambda i: (i, 0)),) * 2,
        compiler_params=pltpu.CompilerParams(dimension_semantics=("parallel",), vmem_limit_bytes=VMEM_LIMIT),
        name="compress_pool",
    )(flat(pool_k), flat(pool_v), w1, p)
    return zk.reshape(n_phys, ROWS_PER_PAGE, ZW), zv.reshape(n_phys, ROWS_PER_PAGE, ZW)


def _nsa_sample_kernel(pt_ref, qraw_ref, qrot_ref, knew_ref, wnew_ref, gate_ref, w2_ref, wk_ref, wv_ref,
                       zk_hbm, zv_hbm, ks_hbm, vs_hbm, o_ref, zbuf, sbuf, sem, *, n_pages, past_len):
    b = pl.program_id(0)
    nb = pl.num_programs(0)
    slot = lax.rem(b, 2)
    page_rows = N_KV * PAGE_SIZE

    def copies(seq, sl):
        out = []
        for pg in range(n_pages):
            page = pt_ref[seq, pg]
            for which, (zsrc, ssrc) in enumerate(((zk_hbm, ks_hbm), (zv_hbm, vs_hbm))):
                out.append(pltpu.make_async_copy(
                    zsrc.at[page], zbuf.at[sl, which, pl.ds(pg * ROWS_PER_PAGE, ROWS_PER_PAGE)], sem.at[sl]))
                out.append(pltpu.make_async_copy(
                    ssrc.at[page], sbuf.at[sl, which, pl.ds(pg * page_rows, page_rows)], sem.at[sl]))
        return out

    @pl.when(b == 0)
    def _():
        for cp in copies(0, 0):
            cp.start()

    @pl.when(b + 1 < nb)
    def _():
        for cp in copies(b + 1, 1 - slot):
            cp.start()

    for cp in copies(b, slot):
        cp.wait()

    scale = HEAD_DIM ** -0.5
    nq = N_KV * GROUP
    r = n_pages * ROWS_PER_PAGE
    qpos = jnp.full((nq, 1), past_len, jnp.int32)
    n_sel_pad = HEAD_DIM
    sel_shift = SEL_LEN.bit_length() - 1
    row_head = lax.broadcasted_iota(jnp.int32, (nq, 1), 0) // GROUP
    per_head = lambda a0, a1: jnp.where(row_head == 0, a0, a1)

    def all_heads(ref):
        return jnp.concatenate([ref[:, j * HEAD_DIM:(j + 1) * HEAD_DIM] for j in range(nq)], axis=0).astype(BF16)

    def scores(q, k):
        return lax.dot_general(q, k, (((1,), (1,)), ((), ())), preferred_element_type=F32) * scale

    cmp = []
    for which in range(2):
        heads = []
        for h in range(N_KV):
            zh = zbuf[slot, which, :, 2 * h * HEAD_DIM:(2 * h + 2) * HEAD_DIM]
            pre = zh[:, :HEAD_DIM] + pltpu.roll(zh[:, HEAD_DIM:], r - 1, 0)
            mid = pre * jax.nn.sigmoid(pre)
            heads.append(jnp.dot(mid.astype(BF16), w2_ref[which], preferred_element_type=F32).astype(BF16))
        cmp.append(heads)
    q_raw = all_heads(qraw_ref)
    c_end = lax.broadcasted_iota(jnp.int32, (nq, r), 1) * CMP_STRIDE + (CMP_LEN - 1)
    cmask = c_end <= qpos
    s = per_head(scores(q_raw, cmp[0][0]), scores(q_raw, cmp[0][1])) + jnp.where(cmask, 0.0, NEG)
    p = jnp.exp(s - jnp.max(s, axis=-1, keepdims=True)) * jnp.where(cmask, 1.0, 0.0)
    p = p / jnp.maximum(jnp.sum(p, axis=-1, keepdims=True), 1e-30)
    pb = p.astype(BF16)
    o_cmp = per_head(jnp.dot(pb, cmp[1][0], preferred_element_type=F32),
                     jnp.dot(pb, cmp[1][1], preferred_element_type=F32))
    psum = per_head(jnp.sum(jnp.where(row_head == 0, p, 0.0), axis=0, keepdims=True),
                    jnp.sum(jnp.where(row_head == 1, p, 0.0), axis=0, keepdims=True))
    blk = lax.broadcasted_iota(jnp.int32, (nq, n_sel_pad), 1)
    sel = _top_rank_mask(_block_scores(_importance(psum, n_sel_pad), blk, qpos), 1,
                         past_len // SEL_LEN + 1)

    def attend(q, k, v, k_new, v_new, bias, new_bias):
        s = scores(q, k) + bias
        s_new = jnp.sum(q.astype(F32) * k_new.astype(BF16).astype(F32), axis=1, keepdims=True) * scale + new_bias
        m = jnp.maximum(jnp.max(s, axis=-1, keepdims=True), s_new)
        p = jnp.exp(s - m)
        p_new = jnp.exp(s_new - m)
        l = jnp.sum(p, axis=-1, keepdims=True) + p_new
        o = (jnp.dot(p.astype(BF16), v, preferred_element_type=F32)
             + p_new.astype(BF16).astype(F32) * v_new.astype(BF16).astype(F32))
        return o / jnp.maximum(l, 1e-30)

    def own_head(n_rows):
        col = lax.broadcasted_iota(jnp.int32, (nq, n_rows), 1)
        return jnp.bitwise_and(col, N_KV - 1) == row_head

    q_rot = all_heads(qrot_ref)
    n_rows = N_KV * past_len
    ej = lax.broadcasted_iota(jnp.int32, (n_sel_pad, n_rows), 0)
    et = lax.broadcasted_iota(jnp.int32, (n_sel_pad, n_rows), 1)
    expand = jnp.where(jnp.right_shift(et, sel_shift + 1) == ej, 1.0, 0.0).astype(BF16)
    selm = jnp.dot(sel.astype(BF16), expand, preferred_element_type=F32)
    sel_bias = jnp.where((selm > 0.5) & own_head(n_rows), 0.0, NEG)
    new_blk = past_len // SEL_LEN
    new_bias = jnp.where(sel[:, new_blk:new_blk + 1] > 0.5, 0.0, NEG)
    new_kv = lambda ref, c: per_head(ref[:, c * A_KV:c * A_KV + HEAD_DIM], ref[:, c * A_KV + HEAD_DIM:(c + 1) * A_KV])
    o_sel = attend(q_rot, sbuf[slot, 0].astype(BF16), sbuf[slot, 1].astype(BF16),
                   new_kv(knew_ref, 0), new_kv(knew_ref, 1), sel_bias, new_bias)
    win_bias = jnp.where(own_head(wk_ref.shape[0]), 0.0, NEG)
    o_win = attend(q_rot, wk_ref[...].astype(BF16), wv_ref[...].astype(BF16),
                   new_kv(wnew_ref, 0), new_kv(wnew_ref, 1), win_bias, 0.0)

    gts = jax.nn.sigmoid(gate_ref[...])
    for h in range(N_KV):
        for g in range(GROUP):
            j, c0 = h * GROUP + g, h * HEAD_DIM + 3 * g
            o = (gts[:, c0:c0 + 1] * o_cmp[j:j + 1] + gts[:, c0 + 1:c0 + 2] * o_sel[j:j + 1]
                 + gts[:, c0 + 2:c0 + 3] * o_win[j:j + 1])
            o_ref[:, j * HEAD_DIM:(j + 1) * HEAD_DIM] = o.astype(o_ref.dtype)


def _nsa_sample(zs, page_table, zk_pool, zv_pool, sel_k, sel_v, win_k, win_v, w2):
    bs, n_pages = page_table.shape
    past_len = n_pages * PAGE_SIZE
    wb = win_k.shape[1]
    assert wb <= WINDOW and past_len % SEL_LEN == 0 and past_len // SEL_LEN < HEAD_DIM
    assert N_KV == 2
    zs3 = zs.reshape(bs, 1, NZ)
    zspec = lambda w, off: pl.BlockSpec((None, 1, w), lambda b, pt: (b, 0, off // w))
    anyspec = pl.BlockSpec(memory_space=pl.ANY)
    wspec = pl.BlockSpec((None, N_KV * wb, HEAD_DIM), lambda b, pt: (b, 0, 0))
    rows2d = lambda a: a.reshape(a.shape[0], a.shape[1] * N_KV, HEAD_DIM)
    out = pl.pallas_call(
        functools.partial(_nsa_sample_kernel, n_pages=n_pages, past_len=past_len),
        out_shape=jax.ShapeDtypeStruct((bs, 1, A_Q), BF16),
        grid_spec=pltpu.PrefetchScalarGridSpec(
            num_scalar_prefetch=1, grid=(bs,),
            in_specs=[zspec(A_Q, Z_QRAW), zspec(A_Q, Z_QROT), zspec(2 * A_KV, Z_KS), zspec(2 * A_KV, Z_KW),
                      zspec(2 * HEAD_DIM, Z_GNSA), pl.BlockSpec(w2.shape, lambda b, pt: (0, 0, 0)), wspec, wspec,
                      anyspec, anyspec, anyspec, anyspec],
            out_specs=pl.BlockSpec((None, 1, A_Q), lambda b, pt: (b, 0, 0)),
            scratch_shapes=[pltpu.VMEM((2, 2, n_pages * ROWS_PER_PAGE, ZW), F32),
                            pltpu.VMEM((2, 2, N_KV * past_len, HEAD_DIM), F32), pltpu.SemaphoreType.DMA((2,))]),
        compiler_params=pltpu.CompilerParams(dimension_semantics=("arbitrary",), vmem_limit_bytes=VMEM_LIMIT),
        name="nsa_sample",
    )(page_table, zs3, zs3, zs3, zs3, zs3, w2, rows2d(win_k), rows2d(win_v), zk_pool, zv_pool,
      rows2d(sel_k), rows2d(sel_v))
    return out.reshape(bs, A_Q)


def _mlstm_sample_kernel(q_ref, k_ref, v_ref, op_ref, g_ref, gain_ref, c_ref, n_ref, m_ref,
                         h_ref, co_ref, no_ref, mo_ref):
    eye = (lax.broadcasted_iota(jnp.int32, (DQK_B, DQK_B), 0) == lax.broadcasted_iota(jnp.int32, (DQK_B, DQK_B), 1))
    col = lambda row: jnp.sum(jnp.where(eye, row, 0.0), axis=1, keepdims=True)
    lane = lax.broadcasted_iota(jnp.int32, (1, N_HEADS_B), 1)
    m_out = jnp.zeros((1, N_HEADS_B), F32)
    for h in range(N_HEADS_B):
        q = q_ref[:, h * DQK_B:(h + 1) * DQK_B]
        k = k_ref[:, h * DQK_B:(h + 1) * DQK_B] * DQK_B ** -0.5
        v = v_ref[:, h * DV_B:(h + 1) * DV_B]
        log_i, log_f = _gate_logs(g_ref[:, h:h + 1], g_ref[:, N_HEADS_B + h:N_HEADS_B + h + 1])
        c_old, n_old, m_old = c_ref[h], n_ref[h:h + 1, :], m_ref[:, h:h + 1]
        inter = log_f + m_old
        m_new = jnp.maximum(inter, log_i)
        decay = jnp.exp(inter - m_new)
        w = jnp.exp(log_i - m_new)
        co_ref[h] = decay * c_old + col(w * k) * v
        no_ref[h:h + 1, :] = decay * n_old + w * k
        m_out = jnp.where(lane == h, m_new, m_out)
        w_intra = w * jnp.sum(q * k, axis=1, keepdims=True)
        num = decay * jnp.sum(c_old * col(q), axis=0, keepdims=True) + w_intra * v
        den = decay * jnp.sum(q * n_old, axis=1, keepdims=True) + w_intra
        hh = num / jnp.maximum(jnp.abs(den), jnp.exp(-m_new))
        out = _rms(hh, gain_ref[h]) * jax.nn.sigmoid(op_ref[:, h * DV_B:(h + 1) * DV_B])
        h_ref[:, h * DV_B:(h + 1) * DV_B] = out.astype(h_ref.dtype)
    mo_ref[...] = m_out


def _mlstm_sample(zs, gain, c0, n0, m0):
    bs, H = zs.shape[0], N_HEADS_B
    zs3 = zs.reshape(bs, 1, NZ)
    zspec = lambda w, off: pl.BlockSpec((None, 1, w), lambda b: (b, 0, off // w))
    cspec = pl.BlockSpec((None, H, DQK_B, DV_B), lambda b: (b, 0, 0, 0))
    nspec = pl.BlockSpec((None, H, DQK_B), lambda b: (b, 0, 0))
    mspec = pl.BlockSpec((None, 1, H), lambda b: (b, 0, 0))
    h, c, n, m = pl.pallas_call(
        _mlstm_sample_kernel,
        out_shape=(jax.ShapeDtypeStruct((bs, 1, B_V), BF16), jax.ShapeDtypeStruct(c0.shape, F32),
                   jax.ShapeDtypeStruct(n0.shape, F32), jax.ShapeDtypeStruct((bs, 1, H), F32)),
        grid=(bs,),
        in_specs=[zspec(B_QK, Z_QB), zspec(B_QK, Z_KB), zspec(B_V, Z_VB), zspec(B_V, Z_OB), zspec(HEAD_DIM, Z_IB),
                  pl.BlockSpec((H, 1, DV_B), lambda b: (0, 0, 0)), cspec, nspec, mspec],
        out_specs=(pl.BlockSpec((None, 1, B_V), lambda b: (b, 0, 0)), cspec, nspec, mspec),
        compiler_params=pltpu.CompilerParams(dimension_semantics=("parallel",), vmem_limit_bytes=VMEM_LIMIT),
        name="mlstm_sample",
    )(zs3, zs3, zs3, zs3, zs3, gain.reshape(H, 1, DV_B), c0, n0, m0.reshape(bs, 1, H))
    return h.reshape(bs, B_V), c, n, m.reshape(bs, H)


def _split_kv(z, B, T):
    z = z.reshape(B, T, NZ)
    kv = lambda o: z[..., o:o + A_KV].reshape(B, T, N_KV, HEAD_DIM)
    return kv(Z_KC), kv(Z_VC), kv(Z_KS), kv(Z_VS), kv(Z_KW), kv(Z_VW)


def kernel(x_prompt, x_sample, cache_cmp_k, cache_cmp_v, cache_sel_k, cache_sel_v, cache_win_k, cache_win_v,
           state_mlstm_c, state_mlstm_n, state_mlstm_m, page_table, norm_mix, w_in, b_in, cmp_pos, cmp_w1, cmp_w2,
           mlstm_norm, w_up_a, w_up_b, w_out, norm_ffn, w_router_grp, b_router_grp, w_router_exp, b_router_exp,
           w_gate, w_up, w_down, norm_final):
    assert w_in.shape[0] == 1, "single layer"
    Bp, Tp = x_prompt.shape[:2]
    Bs, Ts = x_sample.shape[:2]
    assert Ts == 1
    n_p, n_s = Bp * Tp, Bs * Ts
    past_len = page_table.shape[1] * PAGE_SIZE
    pos_p = jnp.arange(Tp, dtype=jnp.int32)
    pos_s = past_len + jnp.arange(Ts, dtype=jnp.int32)
    l = 0

    w_perm, b_perm = _permute_in_proj(w_in[l], b_in[l])
    cw = _compress_weights(cmp_pos[l], cmp_w1[l], cmp_w2[l])
    mw = _merge_weights(w_up_a[l], w_up_b[l], w_out[l], norm_ffn[l], w_router_grp[l], b_router_grp[l],
                        w_router_exp[l], b_router_exp[l])

    xp2d = x_prompt.reshape(n_p, D_MODEL)
    zp = _project(xp2d, jnp.tile(pos_p, Bp), norm_mix[l], w_perm, b_perm, 1024)
    kc, vc, ks, vs, kw, vw = _split_kv(zp, Bp, Tp)
    o_a = _nsa_prompt(zp, _compress_prompt(kc, vc, cw), Bp, Tp)
    h_b, c_p, n_p_state, m_p = _mlstm_prompt(zp, mlstm_norm[l], Bp, Tp)
    bufs = _merge(xp2d, o_a, h_b, zp, mw, n_p + n_s, 0, 256)
    wl = min(WINDOW, Tp)
    prompt_new = (kc, vc, ks, vs, kw[:, -wl:], vw[:, -wl:], c_p, n_p_state, m_p)

    xs2d = x_sample.reshape(n_s, D_MODEL)
    zs = _project(xs2d, jnp.repeat(pos_s, Bs), norm_mix[l], w_perm, b_perm, 128)
    kc, vc, ks, vs, kw, vw = _split_kv(zs, Bs, Ts)
    assert (past_len + Ts - CMP_LEN) // CMP_STRIDE + 1 == past_len // CMP_STRIDE - 1
    zk_pool, zv_pool = _compress_pool(cache_cmp_k[l], cache_cmp_v[l], cw)
    o_a_s = _nsa_sample(zs, page_table, zk_pool, zv_pool, cache_sel_k[l], cache_sel_v[l],
                        cache_win_k[l], cache_win_v[l], cw[2])
    h_s, c_s, n_s_state, m_s = _mlstm_sample(zs, mlstm_norm[l], state_mlstm_c[l], state_mlstm_n[l],
                                             state_mlstm_m[l])
    x2, xn, route = _merge(xs2d, o_a_s, h_s, zs, mw, n_p + n_s, n_p, 128, bufs=bufs)
    wk = jnp.concatenate([cache_win_k[l], kw], axis=1)
    wv = jnp.concatenate([cache_win_v[l], vw], axis=1)
    wl = min(WINDOW, past_len + Ts)
    sample_new = (kc, vc, ks, vs, wk[:, -wl:], wv[:, -wl:], c_s, n_s_state, m_s)

    y_p, y_s = _moe(x2, xn, route, w_gate[l], w_up[l], w_down[l], norm_final, n_p)
    return ((y_p.reshape(Bp, Tp, D_MODEL), y_s.reshape(Bs, Ts, D_MODEL))
            + tuple(a[None] for a in prompt_new) + tuple(a[None] for a in sample_new))
```

```python
import functools

import numpy as np
import jax
import jax.numpy as jnp
from jax import lax
from jax.experimental import pallas as pl
from jax.experimental.pallas import tpu as pltpu

D_MODEL = 2048
PAGE_SIZE = 128
N_HEADS_A = 8
N_KV = 2
GROUP = N_HEADS_A // N_KV
HEAD_DIM = 128
ROT_DIM = HEAD_DIM // 4
ROPE_THETA = 500000.0
CMP_LEN = 32
CMP_STRIDE = 16
SEL_LEN = 64
SEL_TOP = 16
FORCE_BONUS = 100.0
WINDOW = 512
Q_BLOCK = 128
N_HEADS_B = 4
DQK_B = 128
DV_B = 256
MLSTM_CHUNK = 64
GATE_CAP = 15.0
N_GROUPS = 4
EXPERTS_PER_GROUP = 8
N_EXPERTS = N_GROUPS * EXPERTS_PER_GROUP
TOP_K = 2
D_EXPERT = 1024
EPS = 1e-6
NEG = -1e30

A_Q = N_HEADS_A * HEAD_DIM
A_KV = N_KV * HEAD_DIM
B_QK = N_HEADS_B * DQK_B
B_V = N_HEADS_B * DV_B
IN_SPLITS = (A_Q, A_KV, A_KV, A_KV, A_KV, A_KV, A_KV, 3 * N_HEADS_A,
             B_QK, B_QK, B_V, N_HEADS_B, N_HEADS_B, B_V, D_MODEL, D_MODEL)

F32 = jnp.float32
BF16 = jnp.bfloat16
VMEM_LIMIT = 48 * 1024 * 1024
VMEM_LIMIT_MERGE = 56 * 1024 * 1024

_CUTS = np.concatenate([[0], np.cumsum(IN_SPLITS)]).tolist()
_SRC = dict(zip(('q_a', 'k_c', 'v_c', 'k_s', 'v_s', 'k_w', 'v_w', 'g_nsa', 'q_b', 'k_b', 'v_b', 'i_b', 'f_b',
                 'o_b', 'g_ma', 'g_mb'), zip(_CUTS[:-1], _CUTS[1:])))
_Z_ORDER = ('g_ma', 'g_mb', 'q_a', 'q_a', 'v_b', 'o_b', 'k_c', 'v_c', 'k_s', 'v_s', 'k_w', 'v_w', 'q_b', 'k_b')
PROJ_TN = 512
Z_GMA, Z_GMB, Z_QRAW, Z_QROT, Z_VB, Z_OB = 0, 2048, 4096, 5120, 6144, 7168
Z_KC, Z_VC, Z_KS, Z_VS, Z_KW, Z_VW = 8192, 8448, 8704, 8960, 9216, 9472
Z_QB, Z_KB, Z_SMALL = 9728, 10240, 10752
N_GATE = 3 * GROUP
Z_GNSA, Z_IB, Z_FB = Z_SMALL, Z_SMALL + 2 * HEAD_DIM, Z_SMALL + 2 * HEAD_DIM + N_HEADS_B
NZ = 11264
_ROPE_ALL_TILES = (Z_QROT // PROJ_TN, Z_QROT // PROJ_TN + 1)
_ROPE_HALF_TILES = (Z_KS // PROJ_TN, Z_KW // PROJ_TN)

ROUTE_W = 128
MOE_TILE = 512
MOE_SUB = 256
MOE_FC = 512


def _rms(x, g):
    return x * lax.rsqrt(jnp.mean(x * x, axis=-1, keepdims=True) + EPS) * g


def _proj_kernel(x_ref, g_ref, w_ref, b_ref, cos_ref, sa_ref, sb_ref, z_ref, xn_ref):
    j = pl.program_id(1)

    @pl.when(j == 0)
    def _():
        xn_ref[...] = _rms(x_ref[...], g_ref[...]).astype(BF16)

    acc = jnp.dot(xn_ref[...], w_ref[...], preferred_element_type=F32) + b_ref[...]

    def rope(blk):
        return (blk * cos_ref[...] + pltpu.roll(blk, HEAD_DIM - ROT_DIM // 2, 1) * sa_ref[...]
                + pltpu.roll(blk, ROT_DIM // 2, 1) * sb_ref[...])

    is_all = (j == _ROPE_ALL_TILES[0]) | (j == _ROPE_ALL_TILES[1])
    is_half = (j == _ROPE_HALF_TILES[0]) | (j == _ROPE_HALF_TILES[1])
    n_blk = PROJ_TN // HEAD_DIM

    @pl.when(is_all)
    def _():
        for c in range(n_blk):
            z_ref[:, c * HEAD_DIM:(c + 1) * HEAD_DIM] = rope(acc[:, c * HEAD_DIM:(c + 1) * HEAD_DIM])

    @pl.when(is_half)
    def _():
        for c in range(n_blk // 2):
            z_ref[:, c * HEAD_DIM:(c + 1) * HEAD_DIM] = rope(acc[:, c * HEAD_DIM:(c + 1) * HEAD_DIM])
        z_ref[:, PROJ_TN // 2:] = acc[:, PROJ_TN // 2:]

    @pl.when(jnp.logical_not(is_all | is_half))
    def _():
        z_ref[...] = acc


def _rope_tables(pos):
    half = ROT_DIM // 2
    inv = ROPE_THETA ** (-jnp.arange(half, dtype=F32) / half)
    ang = pos.astype(F32)[:, None] * inv[None, :]
    cos, sin = jnp.cos(ang), jnp.sin(ang)
    n = pos.shape[0]
    ones = jnp.ones((n, HEAD_DIM - ROT_DIM), F32)
    zeros = jnp.zeros((n, HEAD_DIM - half), F32)
    cos_t = jnp.concatenate([cos, cos, ones], axis=1)
    sa_t = jnp.concatenate([-sin, zeros], axis=1)
    sb_t = jnp.concatenate([jnp.zeros((n, half), F32), sin, jnp.zeros((n, HEAD_DIM - ROT_DIM), F32)], axis=1)
    return cos_t, sa_t, sb_t


def _permute_in_proj(w_in, b_in):
    g0 = _SRC['g_nsa'][0]
    spans = [_SRC[k] for k in _Z_ORDER]
    spans += [(g0, g0 + N_GATE), HEAD_DIM - N_GATE, (g0 + N_GATE, g0 + 2 * N_GATE), HEAD_DIM - N_GATE,
              _SRC['i_b'], _SRC['f_b'], NZ - Z_FB - N_HEADS_B]

    def permute(a, dtype):
        parts = [jnp.zeros((a.shape[0], s), dtype) if isinstance(s, int) else a[:, s[0]:s[1]].astype(dtype)
                 for s in spans]
        return jnp.concatenate(parts, axis=1)

    return permute(w_in, BF16), permute(b_in[None, :], F32)


def _project(x2d, pos_rows, norm_g, w_perm, b_perm, tm):
    m = x2d.shape[0]
    cos_t, sa_t, sb_t = _rope_tables(pos_rows)
    rows = lambda w: pl.BlockSpec((tm, w), lambda i, j: (i, 0), pipeline_mode=pl.Buffered(1))
    return pl.pallas_call(
        _proj_kernel,
        out_shape=jax.ShapeDtypeStruct((m, NZ), F32),
        grid=(m // tm, NZ // PROJ_TN),
        in_specs=[rows(D_MODEL),
                  pl.BlockSpec((1, D_MODEL), lambda i, j: (0, 0)),
                  pl.BlockSpec((D_MODEL, PROJ_TN), lambda i, j: (0, j)),
                  pl.BlockSpec((1, PROJ_TN), lambda i, j: (0, j)),
                  rows(HEAD_DIM), rows(HEAD_DIM), rows(HEAD_DIM)],
        out_specs=pl.BlockSpec((tm, PROJ_TN), lambda i, j: (i, j)),
        scratch_shapes=[pltpu.VMEM((tm, D_MODEL), BF16)],
        compiler_params=pltpu.CompilerParams(dimension_semantics=("parallel", "arbitrary"),
                                             vmem_limit_bytes=VMEM_LIMIT),
        name="proj",
    )(x2d, norm_g.reshape(1, D_MODEL), w_perm, b_perm, cos_t, sa_t, sb_t)


def _compress_body(x_ref, w1, pb, w2, r):
    bias = pb[0:1, :HEAD_DIM] + pb[1:2, HEAD_DIM:]
    outs = []
    for h in range(N_KV):
        xh = jnp.concatenate([x_ref[:, (N_KV * s + h) * HEAD_DIM:(N_KV * s + h + 1) * HEAD_DIM]
                              for s in range(CMP_STRIDE)], axis=1).astype(BF16)
        zz = jnp.dot(xh, w1, preferred_element_type=F32)
        pre = zz[:, :HEAD_DIM] + pltpu.roll(zz[:, HEAD_DIM:], r - 1, 0) + bias
        mid = pre * jax.nn.sigmoid(pre)
        outs.append(jnp.dot(mid.astype(BF16), w2, preferred_element_type=F32))
    return outs


def _compress_kernel(xk_ref, xv_ref, w1_ref, p_ref, w2_ref, o_ref):
    r = xk_ref.shape[0]
    for which, x_ref in enumerate((xk_ref, xv_ref)):
        w1 = w1_ref[which]
        pb = jnp.dot(p_ref[which], w1, preferred_element_type=F32)
        for h, o in enumerate(_compress_body(x_ref, w1, pb, w2_ref[which], r)):
            o_ref[which, h] = o


def _compress_weights(cmp_pos, cmp_w1, cmp_w2):
    assert CMP_LEN == 2 * CMP_STRIDE
    half = CMP_STRIDE * HEAD_DIM
    w1 = jnp.concatenate([cmp_w1[:, :half], cmp_w1[:, half:]], axis=2).astype(BF16)
    p = cmp_pos.reshape(2, 2, half)
    p = jnp.concatenate([p, jnp.zeros((2, 6, half), p.dtype)], axis=1).astype(BF16)
    return w1, p, cmp_w2.astype(BF16)


def _compress_prompt(kc, vc, cw):
    B, T = kc.shape[:2]
    r = T // CMP_STRIDE
    width = CMP_STRIDE * N_KV * HEAD_DIM
    w1, p, w2 = cw
    full = lambda a: pl.BlockSpec(a.shape, lambda b: (0,) * a.ndim)
    xspec = pl.BlockSpec((None, r, width), lambda b: (b, 0, 0))
    return pl.pallas_call(
        _compress_kernel,
        out_shape=jax.ShapeDtypeStruct((B, 2, N_KV, r, HEAD_DIM), F32),
        grid=(B,),
        in_specs=[xspec, xspec, full(w1), full(p), full(w2)],
        out_specs=pl.BlockSpec((None, 2, N_KV, r, HEAD_DIM), lambda b: (b, 0, 0, 0, 0)),
        compiler_params=pltpu.CompilerParams(dimension_semantics=("parallel",), vmem_limit_bytes=VMEM_LIMIT),
        name="compress_prompt",
    )(kc.reshape(B, r, width), vc.reshape(B, r, width), w1, p, w2)


def _group_scores(q, k, tq):
    s = lax.dot_general(q, k, (((1,), (1,)), ((), ())), preferred_element_type=F32) * HEAD_DIM ** -0.5
    return s.reshape(GROUP, tq, k.shape[0])


def _stack_heads(ref):
    return jnp.concatenate([ref[:, g * HEAD_DIM:(g + 1) * HEAD_DIM] for g in range(GROUP)], axis=0).astype(BF16)


def _importance(psum, n_sel):
    r = psum.shape[1]
    ci = lax.broadcasted_iota(jnp.int32, (r, n_sel), 0) * CMP_STRIDE
    cj = lax.broadcasted_iota(jnp.int32, (r, n_sel), 1) * SEL_LEN
    cover_t = ((ci < cj + SEL_LEN) & (ci + (CMP_LEN - 1) >= cj)).astype(BF16)
    p_hi = psum.astype(BF16)
    p_lo = (psum - p_hi.astype(F32)).astype(BF16)
    return jnp.dot(p_hi, cover_t, preferred_element_type=F32) + jnp.dot(p_lo, cover_t, preferred_element_type=F32)


def _block_scores(imp, blk, qpos):
    cur = jnp.right_shift(qpos, SEL_LEN.bit_length() - 1)
    forced = (blk == 0) | (blk == cur) | (blk == cur - 1)
    return jnp.where(blk * SEL_LEN <= qpos, imp + jnp.where(forced, FORCE_BONUS, 0.0), -1.0)


def _top_rank_mask(score, axis, n_cand):
    idx = lax.broadcasted_iota(jnp.int32, score.shape, axis)
    rank = jnp.zeros(score.shape, F32)
    for j in range(n_cand):
        cand = score[j:j + 1, :] if axis == 0 else score[:, j:j + 1]
        rank = rank + jnp.where((cand > score) | ((cand == score) & (idx > j)), 1.0, 0.0)
    return jnp.where(rank < float(SEL_TOP), 1.0, 0.0)


def _nsa_prompt_kernel(qraw_ref, qrot_ref, ks_ref, vs_ref, kw_ref, vw_ref, kc_ref, vc_ref, gate_ref, o_ref,
                       m_ref, l_ref, acc_ref, *, tq, tk, seq_len):
    q0 = pl.program_id(2) * tq
    r = kc_ref.shape[0]
    n_sel = seq_len // SEL_LEN
    sel_shift = SEL_LEN.bit_length() - 1
    qpos = q0 + lax.broadcasted_iota(jnp.int32, (tq, 1), 0)

    s = _group_scores(_stack_heads(qraw_ref), kc_ref[...].astype(BF16), tq)
    c_end = lax.broadcasted_iota(jnp.int32, (tq, r), 1) * CMP_STRIDE + (CMP_LEN - 1)
    cmask = c_end <= qpos
    s = s + jnp.where(cmask, 0.0, NEG)[None]
    p = jnp.exp(s - jnp.max(s, axis=-1, keepdims=True)) * jnp.where(cmask, 1.0, 0.0)[None]
    p = p / jnp.maximum(jnp.sum(p, axis=-1, keepdims=True), 1e-30)
    o_cmp = jnp.dot(p.reshape(GROUP * tq, r).astype(BF16), vc_ref[...].astype(BF16), preferred_element_type=F32)
    psum = p[0]
    for g in range(1, GROUP):
        psum = psum + p[g]
    imp_t = _importance(psum, HEAD_DIM).T[:n_sel]
    blk_t = lax.broadcasted_iota(jnp.int32, (n_sel, tq), 0)
    qpos_t = q0 + lax.broadcasted_iota(jnp.int32, (n_sel, tq), 1)
    sel_b = _top_rank_mask(_block_scores(imp_t, blk_t, qpos_t), 0, n_sel).astype(BF16)

    q_rot = _stack_heads(qrot_ref)
    m_ref[...] = jnp.full(m_ref.shape, NEG, F32)
    l_ref[...] = jnp.zeros(l_ref.shape, F32)
    acc_ref[...] = jnp.zeros(acc_ref.shape, F32)

    def sel_tile(kt, carry):
        k0 = pl.multiple_of(kt * tk, tk)
        k = ks_ref[pl.ds(k0, tk), :].astype(BF16)
        v = vs_ref[pl.ds(k0, tk), :].astype(BF16)
        s2 = _group_scores(q_rot, k, tq)
        ej = lax.broadcasted_iota(jnp.int32, (n_sel, tk), 0)
        et = k0 + lax.broadcasted_iota(jnp.int32, (n_sel, tk), 1)
        expand = jnp.where(jnp.right_shift(et, sel_shift) == ej, 1.0, 0.0).astype(BF16)
        selm = lax.dot_general(sel_b, expand, (((0,), (0,)), ((), ())), preferred_element_type=F32)
        kpos = k0 + lax.broadcasted_iota(jnp.int32, (tq, tk), 1)
        ok = (selm > 0.5) & (kpos <= qpos)
        s2 = s2 + jnp.where(ok, 0.0, NEG)[None]
        m_prev = m_ref[...]
        m_new = jnp.maximum(m_prev, jnp.max(s2, axis=-1, keepdims=True))
        alpha = jnp.exp(m_prev - m_new)
        p2 = jnp.exp(s2 - m_new)
        l_ref[...] = alpha * l_ref[...] + jnp.sum(p2, axis=-1, keepdims=True)
        pv = jnp.dot(p2.reshape(GROUP * tq, tk).astype(BF16), v, preferred_element_type=F32)
        acc_ref[...] = alpha.reshape(GROUP * tq, 1) * acc_ref[...] + pv
        m_ref[...] = m_new
        return carry

    lax.fori_loop(0, lax.div(q0 + tq + tk - 1, tk), sel_tile, 0)
    o_sel = acc_ref[...] / jnp.maximum(l_ref[...].reshape(GROUP * tq, 1), 1e-30)

    span = WINDOW + tq
    w0 = pl.multiple_of(jnp.maximum(q0 - WINDOW, 0), tq)
    s3 = _group_scores(q_rot, kw_ref[pl.ds(w0, span), :].astype(BF16), tq)
    kp = w0 + lax.broadcasted_iota(jnp.int32, (tq, span), 1)
    okw = (kp <= qpos) & (kp >= qpos - WINDOW)
    s3 = s3 + jnp.where(okw, 0.0, NEG)[None]
    p3 = jnp.exp(s3 - jnp.max(s3, axis=-1, keepdims=True))
    p3 = p3 / jnp.sum(p3, axis=-1, keepdims=True)
    o_win = jnp.dot(p3.reshape(GROUP * tq, span).astype(BF16), vw_ref[pl.ds(w0, span), :].astype(BF16),
                    preferred_element_type=F32)

    gts = jax.nn.sigmoid(gate_ref[...])
    for g in range(GROUP):
        rows = slice(g * tq, (g + 1) * tq)
        o_ref[:, g * HEAD_DIM:(g + 1) * HEAD_DIM] = (gts[:, 3 * g:3 * g + 1] * o_cmp[rows]
                                                     + gts[:, 3 * g + 1:3 * g + 2] * o_sel[rows]
                                                     + gts[:, 3 * g + 2:3 * g + 3] * o_win[rows]).astype(o_ref.dtype)


def _nsa_prompt(z, cmp, B, T, tq=256, tk=512):
    assert T % tk == 0 and tk % tq == 0 and tq % HEAD_DIM == 0 and T >= WINDOW + tq
    assert SEL_LEN & (SEL_LEN - 1) == 0 and tk % SEL_LEN == 0
    assert (T // SEL_LEN) % 8 == 0 and T // SEL_LEN <= HEAD_DIM
    nq = T // tq
    r = cmp.shape[3]
    gw = GROUP * HEAD_DIM
    qspec = lambda off: pl.BlockSpec((tq, gw), lambda b, h, q: (b * nq + q, off // gw + h))
    kvspec = lambda off: pl.BlockSpec((T, HEAD_DIM), lambda b, h, q: (b, off // HEAD_DIM + h))
    cspec = lambda which: pl.BlockSpec((None, None, None, r, HEAD_DIM), lambda b, h, q: (b, which, h, 0, 0))
    return pl.pallas_call(
        functools.partial(_nsa_prompt_kernel, tq=tq, tk=tk, seq_len=T),
        out_shape=jax.ShapeDtypeStruct((B * T, A_Q), BF16),
        grid=(B, N_KV, nq),
        in_specs=[qspec(Z_QRAW), qspec(Z_QROT), kvspec(Z_KS), kvspec(Z_VS), kvspec(Z_KW), kvspec(Z_VW),
                  cspec(0), cspec(1),
                  pl.BlockSpec((tq, HEAD_DIM), lambda b, h, q: (b * nq + q, Z_GNSA // HEAD_DIM + h))],
        out_specs=pl.BlockSpec((tq, gw), lambda b, h, q: (b * nq + q, h)),
        scratch_shapes=[pltpu.VMEM((GROUP, tq, 1), F32), pltpu.VMEM((GROUP, tq, 1), F32),
                        pltpu.VMEM((GROUP * tq, HEAD_DIM), F32)],
        compiler_params=pltpu.CompilerParams(dimension_semantics=("parallel", "parallel", "arbitrary"),
                                             vmem_limit_bytes=VMEM_LIMIT),
        name="nsa_prompt",
    )(z, z, z, z, z, z, cmp, cmp, z)


def _gate_logs(i_pre, f_pre):
    log_i = GATE_CAP * jnp.tanh(i_pre / GATE_CAP)
    fc = GATE_CAP * jnp.tanh(f_pre / GATE_CAP)
    log_f = jnp.minimum(fc, 0.0) - jnp.log1p(jnp.exp(-jnp.abs(fc)))
    return log_i, log_f


def _mlstm_prompt_kernel(q_ref, k_ref, v_ref, op_ref, grow_ref, gcol_ref, gain_ref,
                         h_ref, c_ref, n_ref, m_ref):
    @pl.when(pl.program_id(1) == 0)
    def _():
        c_ref[...] = jnp.zeros(c_ref.shape, F32)
        n_ref[...] = jnp.zeros(n_ref.shape, F32)
        m_ref[...] = jnp.zeros(m_ref.shape, F32)

    for h in range(N_HEADS_B):
        qk_cols = slice(h * DQK_B, (h + 1) * DQK_B)
        v_cols = slice(h * DV_B, (h + 1) * DV_B)
        _mlstm_chunk(q_ref[:, qk_cols], k_ref[:, qk_cols], v_ref[:, v_cols], op_ref[:, v_cols],
                     grow_ref[h:h + 1, :], grow_ref[N_HEADS_B + h:N_HEADS_B + h + 1, :],
                     gcol_ref[:, h:h + 1], gcol_ref[:, N_HEADS_B + h:N_HEADS_B + h + 1], gain_ref[h],
                     h_ref.at[:, v_cols], c_ref.at[h], n_ref.at[h], m_ref.at[h])


def _mlstm_chunk(q, k, v, o_pre, i_row, f_row, i_col, f_col, gain, h_ref, c_ref, n_ref, m_ref):
    L = q.shape[0]
    qb = q.astype(BF16)
    k = k * DQK_B ** -0.5
    vb = v.astype(BF16)
    i_r, f_r = _gate_logs(i_row, f_row)
    i_c, f_c = _gate_logs(i_col, f_col)
    li = lax.broadcasted_iota(jnp.int32, (L, L), 0)
    si = lax.broadcasted_iota(jnp.int32, (L, L), 1)
    tri = si <= li
    bcum_c = jnp.sum(jnp.where(tri, f_r, 0.0), axis=1, keepdims=True)
    bcum_r = jnp.sum(jnp.where(li <= si, f_c, 0.0), axis=0, keepdims=True)
    btot = jnp.sum(f_r, axis=1, keepdims=True)
    c_old, n_old, m_old = c_ref[...], n_ref[...], m_ref[0:1, 0:1]

    dlog = jnp.where(tri, bcum_c - bcum_r + i_r, NEG)
    inter = bcum_c + m_old
    m_t = jnp.maximum(inter, jnp.max(dlog, axis=1, keepdims=True))
    qk = lax.dot_general(qb, k.astype(BF16), (((1,), (1,)), ((), ())), preferred_element_type=F32)
    w_intra = jnp.exp(dlog - m_t) * qk
    w_inter = jnp.exp(inter - m_t)
    num = (w_inter * jnp.dot(qb, c_old.astype(BF16), preferred_element_type=F32)
           + jnp.dot(w_intra.astype(BF16), vb, preferred_element_type=F32))
    den = w_inter * jnp.sum(q * n_old, axis=1, keepdims=True) + jnp.sum(w_intra, axis=1, keepdims=True)
    h = num / jnp.maximum(jnp.abs(den), jnp.exp(-m_t))
    h_ref[...] = (_rms(h, gain) * jax.nn.sigmoid(o_pre)).astype(h_ref.dtype)

    wlog_c = btot - bcum_c + i_c
    m_new = jnp.maximum(btot + m_old, jnp.max(btot - bcum_r + i_r, axis=1, keepdims=True))
    decay = jnp.exp(btot + m_old - m_new)
    kw = k * jnp.exp(wlog_c - m_new)
    c_ref[...] = decay * c_old + lax.dot_general(kw.astype(BF16), vb, (((0,), (0,)), ((), ())),
                                                 preferred_element_type=F32)
    n_ref[...] = decay * n_old + jnp.sum(kw, axis=0, keepdims=True)
    m_ref[...] = jnp.broadcast_to(m_new, m_ref.shape)


def _mlstm_prompt(z, gain, B, T):
    L = MLSTM_CHUNK
    assert T % L == 0
    nc, H = T // L, N_HEADS_B
    gcol = z[:, Z_IB:Z_IB + 2 * H].reshape(B * nc, L, 2 * H)
    grow = gcol.transpose(0, 2, 1)
    rows = lambda w, off: pl.BlockSpec((L, w), lambda b, c: (b * nc + c, off // w))
    gidx = lambda b, c: (b * nc + c, 0, 0)
    state = lambda *s: pl.BlockSpec((None, H) + s, lambda b, c: (b, 0, 0, 0))
    h, c, n, m = pl.pallas_call(
        _mlstm_prompt_kernel,
        out_shape=(jax.ShapeDtypeStruct((B * T, B_V), BF16), jax.ShapeDtypeStruct((B, H, DQK_B, DV_B), F32),
                   jax.ShapeDtypeStruct((B, H, 1, DQK_B), F32), jax.ShapeDtypeStruct((B, H, 1, DQK_B), F32)),
        grid=(B, nc),
        in_specs=[rows(B_QK, Z_QB), rows(B_QK, Z_KB), rows(B_V, Z_VB), rows(B_V, Z_OB),
                  pl.BlockSpec((None, 2 * H, L), gidx), pl.BlockSpec((None, L, 2 * H), gidx),
                  pl.BlockSpec((H, 1, DV_B), lambda b, c: (0, 0, 0))],
        out_specs=(rows(B_V, 0), state(DQK_B, DV_B), state(1, DQK_B), state(1, DQK_B)),
        compiler_params=pltpu.CompilerParams(dimension_semantics=("parallel", "arbitrary"),
                                             vmem_limit_bytes=VMEM_LIMIT),
        name="mlstm_prompt",
    )(z, z, z, z, grow, gcol, gain.reshape(H, 1, DV_B))
    return h, c, n[:, :, 0], m[:, :, 0, 0]


def _merge_kernel(x_ref, oa_ref, hb_ref, gma_ref, gmb_ref, wa_ref, wb_ref, wo_ref, nf_ref, wrh_ref, wrl_ref, br_ref,
                  *refs):
    x2_ref, xn_ref, route_ref = refs[-3:]
    ua = jnp.dot(oa_ref[...], wa_ref[...], preferred_element_type=F32)
    ub = jnp.dot(hb_ref[...], wb_ref[...], preferred_element_type=F32)
    merged = jax.nn.sigmoid(gma_ref[...]) * ua + jax.nn.sigmoid(gmb_ref[...]) * ub
    x2 = x_ref[...] + jnp.dot(merged.astype(BF16), wo_ref[...], preferred_element_type=F32)
    x2_ref[...] = x2
    xn = _rms(x2, nf_ref[...])
    xn_ref[...] = xn
    hi = xn.astype(BF16)
    lo = (xn - hi.astype(F32)).astype(BF16)
    lg = (jnp.dot(hi, wrh_ref[...], preferred_element_type=F32) + jnp.dot(lo, wrh_ref[...], preferred_element_type=F32)
          + jnp.dot(hi, wrl_ref[...], preferred_element_type=F32)) + br_ref[...]
    lane = lax.broadcasted_iota(jnp.int32, lg.shape, 1)
    lanef = lane.astype(F32)
    ninf = -jnp.inf
    gmask = (lane >= N_EXPERTS) & (lane < N_EXPERTS + N_GROUPS)
    gl = jnp.where(gmask, lg, ninf)
    gmax = jnp.max(gl, axis=1, keepdims=True)
    gidx = jnp.min(jnp.where(gl == gmax, lanef, 1e9), axis=1, keepdims=True) - N_EXPERTS
    gprob = 1.0 / jnp.sum(jnp.where(gmask, jnp.exp(lg - gmax), 0.0), axis=1, keepdims=True)
    e_lo = gidx * EXPERTS_PER_GROUP
    el = jnp.where((lanef >= e_lo) & (lanef < e_lo + EXPERTS_PER_GROUP), lg, ninf)
    v1 = jnp.max(el, axis=1, keepdims=True)
    i1 = jnp.min(jnp.where(el == v1, lanef, 1e9), axis=1, keepdims=True)
    el2 = jnp.where(lanef == i1, ninf, el)
    v2 = jnp.max(el2, axis=1, keepdims=True)
    i2 = jnp.min(jnp.where(el2 == v2, lanef, 1e9), axis=1, keepdims=True)
    e2 = jnp.exp(v2 - v1)
    g1 = gprob / (1.0 + e2)
    g2 = gprob * e2 / (1.0 + e2)
    route_ref[...] = jnp.where(lane == 0, i1, jnp.where(lane == 1, i2, jnp.where(lane == 2, g1,
                               jnp.where(lane == 3, g2, 0.0))))


def _merge_weights(w_up_a, w_up_b, w_out, norm_ffn, w_rg, b_rg, w_re, b_re):
    assert TOP_K == 2
    pad = ROUTE_W - N_EXPERTS - N_GROUPS
    w = jnp.concatenate([w_re, w_rg, jnp.zeros((D_MODEL, pad), F32)], axis=1)
    b = jnp.concatenate([b_re, b_rg, jnp.zeros((pad,), F32)]).reshape(1, ROUTE_W)
    hi = w.astype(BF16)
    return (w_up_a.astype(BF16), w_up_b.astype(BF16), w_out.astype(BF16), norm_ffn.reshape(1, D_MODEL).astype(F32),
            hi, (w - hi.astype(F32)).astype(BF16), b)


def _merge(x2d, o_a, h_b, z, mw, n_total, row0, tm, bufs=None):
    m = x2d.shape[0]
    i0 = row0 // tm
    row = lambda w, c=0: pl.BlockSpec((tm, w), lambda i: (i, c))
    const = lambda a: pl.BlockSpec(a.shape, lambda i: (0,) * a.ndim, pipeline_mode=pl.Buffered(1))
    outw = (D_MODEL, D_MODEL, ROUTE_W)
    in_specs = [row(D_MODEL), row(A_Q), row(B_V), row(D_MODEL, Z_GMA // D_MODEL), row(D_MODEL, Z_GMB // D_MODEL)]
    in_specs += [const(a) for a in mw]
    args = [x2d, o_a, h_b, z, z] + list(mw)
    aliases = {}
    if bufs is not None:
        in_specs += [pl.BlockSpec(memory_space=pl.ANY)] * 3
        aliases = {len(args) + k: k for k in range(3)}
        args += list(bufs)
    return pl.pallas_call(
        _merge_kernel,
        out_shape=tuple(jax.ShapeDtypeStruct((n_total, w), F32) for w in outw),
        grid=(m // tm,),
        in_specs=in_specs,
        out_specs=tuple(pl.BlockSpec((tm, w), lambda i: (i0 + i, 0)) for w in outw),
        input_output_aliases=aliases,
        compiler_params=pltpu.CompilerParams(dimension_semantics=("parallel",), vmem_limit_bytes=VMEM_LIMIT_MERGE),
        name="merge_route",
    )(*args)


def _moe_plan(route, n_tiles):
    n = route.shape[0]
    a = n * TOP_K
    flat_e = route[:, 0:TOP_K].astype(jnp.int32).reshape(a)
    onehot = (flat_e[:, None] == jnp.arange(N_EXPERTS, dtype=jnp.int32)[None, :]).astype(jnp.int32)
    rank = jnp.take_along_axis(jnp.cumsum(onehot, axis=0) - onehot, flat_e[:, None], axis=1)[:, 0]
    counts = jnp.sum(onehot, axis=0)
    ntile = (counts + MOE_TILE - 1) // MOE_TILE
    tile_end = jnp.cumsum(ntile)
    dest = ((tile_end - ntile)[flat_e] * MOE_TILE + rank).astype(jnp.int32)
    n_used = tile_end[-1].astype(jnp.int32)
    t = jnp.minimum(jnp.arange(n_tiles, dtype=jnp.int32), n_used - 1)
    tile_expert = jnp.minimum(jnp.searchsorted(tile_end, t, side='right'), N_EXPERTS - 1).astype(jnp.int32)
    first = (tile_end - ntile)[tile_expert]
    tile_rows = jnp.clip(counts[tile_expert] - (t - first) * MOE_TILE, 0, MOE_TILE)
    tile_rows = jnp.where(jnp.arange(n_tiles) < n_used, tile_rows, 0).astype(jnp.int32)
    return dest, tile_expert, tile_rows, n_used.reshape(1)


DMA_UNROLL = 8


def _dispatch_kernel(dest_ref, x_ref, o_hbm, sem):
    i = pl.program_id(0)
    tm = x_ref.shape[0]

    def copy(r, k):
        d = dest_ref[(i * tm + r) * TOP_K + k]
        return pltpu.make_async_copy(x_ref.at[pl.ds(r, 1)], o_hbm.at[pl.ds(d, 1)], sem)

    def start(r, c):
        for k in range(TOP_K):
            copy(r, k).start()
        return c

    def wait(r, c):
        for k in range(TOP_K):
            copy(r, k).wait()
        return c

    lax.fori_loop(0, tm, start, 0, unroll=DMA_UNROLL)
    lax.fori_loop(0, tm, wait, 0, unroll=DMA_UNROLL)


def _moe_dispatch(xn, dest, n_tiles, tm=128):
    n, d = xn.shape
    return pl.pallas_call(
        _dispatch_kernel,
        out_shape=jax.ShapeDtypeStruct((n_tiles * MOE_TILE, d), xn.dtype),
        grid_spec=pltpu.PrefetchScalarGridSpec(
            num_scalar_prefetch=1, grid=(n // tm,),
            in_specs=[pl.BlockSpec((tm, d), lambda i, ds: (i, 0))],
            out_specs=pl.BlockSpec(memory_space=pl.ANY),
            scratch_shapes=[pltpu.SemaphoreType.DMA(())]),
        compiler_params=pltpu.CompilerParams(dimension_semantics=("arbitrary",), vmem_limit_bytes=VMEM_LIMIT),
        name="moe_dispatch",
    )(dest, xn)


def _expert_kernel(te_ref, cnt_ref, nu_ref, x_ref, wg_ref, wu_ref, wd_ref, o_ref, xb_ref):
    t, j = pl.program_id(0), pl.program_id(1)
    cnt = cnt_ref[t]

    @pl.when(cnt > 0)
    def _():
        wg, wu, wd = wg_ref[...].astype(BF16), wu_ref[...].astype(BF16), wd_ref[...].astype(BF16)
        for s in range(MOE_TILE // MOE_SUB):
            rows = slice(s * MOE_SUB, (s + 1) * MOE_SUB)

            @pl.when(s * MOE_SUB < cnt)
            def _():
                @pl.when(j == 0)
                def _():
                    xb_ref[rows] = x_ref[rows].astype(BF16)

                xb = xb_ref[rows]
                g = jnp.dot(xb, wg, preferred_element_type=F32)
                u = jnp.dot(xb, wu, preferred_element_type=F32)
                h = (g * jax.nn.sigmoid(g) * u).astype(BF16)
                y = jnp.dot(h, wd, preferred_element_type=F32)

                @pl.when(j == 0)
                def _():
                    o_ref[rows] = y

                @pl.when(j > 0)
                def _():
                    o_ref[rows] += y


def _moe_experts(xs, tile_expert, tile_rows, n_used, w_gate, w_up, w_down):
    n_tiles = xs.shape[0] // MOE_TILE
    nj = D_EXPERT // MOE_FC
    tix = lambda t, j, te, cnt, nu: (jnp.minimum(t, nu[0] - 1), 0)
    jj = lambda t, j, nu: jnp.where(t < nu[0], j, nj - 1)
    return pl.pallas_call(
        _expert_kernel,
        out_shape=jax.ShapeDtypeStruct(xs.shape, F32),
        grid_spec=pltpu.PrefetchScalarGridSpec(
            num_scalar_prefetch=3, grid=(n_tiles, nj),
            in_specs=[pl.BlockSpec((MOE_TILE, D_MODEL), tix),
                      pl.BlockSpec((None, D_MODEL, MOE_FC), lambda t, j, te, cnt, nu: (te[t], 0, jj(t, j, nu))),
                      pl.BlockSpec((None, D_MODEL, MOE_FC), lambda t, j, te, cnt, nu: (te[t], 0, jj(t, j, nu))),
                      pl.BlockSpec((None, MOE_FC, D_MODEL), lambda t, j, te, cnt, nu: (te[t], jj(t, j, nu), 0))],
            out_specs=pl.BlockSpec((MOE_TILE, D_MODEL), tix),
            scratch_shapes=[pltpu.VMEM((MOE_TILE, D_MODEL), BF16)]),
        compiler_params=pltpu.CompilerParams(dimension_semantics=("arbitrary", "arbitrary"),
                                             vmem_limit_bytes=VMEM_LIMIT_MERGE),
        name="moe_experts",
    )(tile_expert, tile_rows, n_used, xs, w_gate, w_up, w_down)


def _combine_kernel(dest_ref, x2_ref, route_ref, g_ref, y_hbm, op_ref, os_ref, buf_ref, sem, *, n_prompt_tiles):
    i = pl.program_id(0)
    tm = x2_ref.shape[0]

    def copy(r, k):
        d = dest_ref[(i * tm + r) * TOP_K + k]
        return pltpu.make_async_copy(y_hbm.at[pl.ds(d, 1)], buf_ref.at[k, pl.ds(r, 1)], sem)

    def start(r, c):
        for k in range(TOP_K):
            copy(r, k).start()
        return c

    def wait(r, c):
        for k in range(TOP_K):
            copy(r, k).wait()
        return c

    lax.fori_loop(0, tm, start, 0, unroll=DMA_UNROLL)
    lax.fori_loop(0, tm, wait, 0, unroll=DMA_UNROLL)
    ffn = buf_ref[0] * route_ref[:, TOP_K:TOP_K + 1]
    for k in range(1, TOP_K):
        ffn = ffn + buf_ref[k] * route_ref[:, TOP_K + k:TOP_K + k + 1]
    y = _rms(x2_ref[...] + ffn, g_ref[...])

    @pl.when(i < n_prompt_tiles)
    def _():
        op_ref[...] = y

    @pl.when(i >= n_prompt_tiles)
    def _():
        os_ref[...] = y


def _moe_combine(x2, route, dest, ys, norm_final, n_prompt, tm=128):
    n, d = x2.shape
    npt = n_prompt // tm
    nst = (n - n_prompt) // tm
    return pl.pallas_call(
        functools.partial(_combine_kernel, n_prompt_tiles=npt),
        out_shape=(jax.ShapeDtypeStruct((n_prompt, d), F32), jax.ShapeDtypeStruct((n - n_prompt, d), F32)),
        grid_spec=pltpu.PrefetchScalarGridSpec(
            num_scalar_prefetch=1, grid=(n // tm,),
            in_specs=[pl.BlockSpec((tm, d), lambda i, ds: (i, 0)), pl.BlockSpec((tm, ROUTE_W), lambda i, ds: (i, 0)),
                      pl.BlockSpec((1, d), lambda i, ds: (0, 0)), pl.BlockSpec(memory_space=pl.ANY)],
            out_specs=(pl.BlockSpec((tm, d), lambda i, ds: (jnp.minimum(i, npt - 1), 0)),
                       pl.BlockSpec((tm, d), lambda i, ds: (jnp.clip(i - npt, 0, nst - 1), 0))),
            scratch_shapes=[pltpu.VMEM((TOP_K, tm, d), F32), pltpu.SemaphoreType.DMA(())]),
        compiler_params=pltpu.CompilerParams(dimension_semantics=("arbitrary",), vmem_limit_bytes=VMEM_LIMIT),
        name="moe_combine",
    )(dest, x2, route, norm_final.reshape(1, d), ys)


def _moe(x2, xn, route, w_gate, w_up, w_down, norm_final, n_prompt):
    n = x2.shape[0]
    n_tiles = (n * TOP_K + N_EXPERTS * (MOE_TILE - 1) + MOE_TILE - 1) // MOE_TILE
    dest, tile_expert, tile_rows, n_used = _moe_plan(route, n_tiles)
    xs = _moe_dispatch(xn, dest, n_tiles)
    ys = _moe_experts(xs, tile_expert, tile_rows, n_used, w_gate, w_up, w_down)
    return _moe_combine(x2, route, dest, ys, norm_final, n_prompt)


POOL_ROWS = 256
ROWS_PER_PAGE = PAGE_SIZE // CMP_STRIDE
ZW = 2 * N_KV * HEAD_DIM


def _compress_pool_kernel(xk_ref, xv_ref, w1_ref, p_ref, zk_ref, zv_ref):
    for which, (x_ref, z_ref) in enumerate(((xk_ref, zk_ref), (xv_ref, zv_ref))):
        w1 = w1_ref[which]
        pb = jnp.dot(p_ref[which], w1, preferred_element_type=F32)
        bias = pb[0:1, :HEAD_DIM] + pb[1:2, HEAD_DIM:]
        for h in range(N_KV):
            xh = jnp.concatenate([x_ref[pl.ds(N_KV * s + h, POOL_ROWS, stride=N_KV * CMP_STRIDE), :]
                                  for s in range(CMP_STRIDE)], axis=1).astype(BF16)
            zz = jnp.dot(xh, w1, preferred_element_type=F32)
            z_ref[:, 2 * h * HEAD_DIM:(2 * h + 1) * HEAD_DIM] = zz[:, :HEAD_DIM] + bias
            z_ref[:, (2 * h + 1) * HEAD_DIM:(2 * h + 2) * HEAD_DIM] = zz[:, HEAD_DIM:]


def _compress_pool(pool_k, pool_v, cw):
    n_phys = pool_k.shape[0]
    rows = n_phys * ROWS_PER_PAGE
    assert rows % POOL_ROWS == 0
    in_rows = POOL_ROWS * CMP_STRIDE * N_KV
    w1, p, _ = cw
    full = lambda a: pl.BlockSpec(a.shape, lambda i: (0,) * a.ndim)
    flat = lambda a: a.reshape(n_phys * PAGE_SIZE * N_KV, HEAD_DIM)
    zk, zv = pl.pallas_call(
        _compress_pool_kernel,
        out_shape=(jax.ShapeDtypeStruct((rows, ZW), F32),) * 2,
        grid=(rows // POOL_ROWS,),
        in_specs=[pl.BlockSpec((in_rows, HEAD_DIM), lambda i: (i, 0))] * 2 + [full(w1), full(p)],
        out_specs=(pl.BlockSpec((POOL_ROWS, ZW), lambda i: (i, 0)),) * 2,
        compiler_params=pltpu.CompilerParams(dimension_semantics=("parallel",), vmem_limit_bytes=VMEM_LIMIT),
        name="compress_pool",
    )(flat(pool_k), flat(pool_v), w1, p)
    return zk.reshape(n_phys, ROWS_PER_PAGE, ZW), zv.reshape(n_phys, ROWS_PER_PAGE, ZW)


def _nsa_sample_kernel(pt_ref, qraw_ref, qrot_ref, knew_ref, wnew_ref, gate_ref, w2_ref, wk_ref, wv_ref,
                       zk_hbm, zv_hbm, ks_hbm, vs_hbm, o_ref, zbuf, sbuf, sem, *, n_pages, past_len):
    b = pl.program_id(0)
    nb = pl.num_programs(0)
    slot = lax.rem(b, 2)
    page_rows = N_KV * PAGE_SIZE

    def copies(seq, sl):
        out = []
        for pg in range(n_pages):
            page = pt_ref[seq, pg]
            for which, (zsrc, ssrc) in enumerate(((zk_hbm, ks_hbm), (zv_hbm, vs_hbm))):
                out.append(pltpu.make_async_copy(
                    zsrc.at[page], zbuf.at[sl, which, pl.ds(pg * ROWS_PER_PAGE, ROWS_PER_PAGE)], sem.at[sl]))
                out.append(pltpu.make_async_copy(
                    ssrc.at[page], sbuf.at[sl, which, pl.ds(pg * page_rows, page_rows)], sem.at[sl]))
        return out

    @pl.when(b == 0)
    def _():
        for cp in copies(0, 0):
            cp.start()

    @pl.when(b + 1 < nb)
    def _():
        for cp in copies(b + 1, 1 - slot):
            cp.start()

    for cp in copies(b, slot):
        cp.wait()

    scale = HEAD_DIM ** -0.5
    nq = N_KV * GROUP
    r = n_pages * ROWS_PER_PAGE
    qpos = jnp.full((nq, 1), past_len, jnp.int32)
    n_sel_pad = HEAD_DIM
    sel_shift = SEL_LEN.bit_length() - 1
    row_head = lax.broadcasted_iota(jnp.int32, (nq, 1), 0) // GROUP
    per_head = lambda a0, a1: jnp.where(row_head == 0, a0, a1)

    def all_heads(ref):
        return jnp.concatenate([ref[:, j * HEAD_DIM:(j + 1) * HEAD_DIM] for j in range(nq)], axis=0).astype(BF16)

    def scores(q, k):
        return lax.dot_general(q, k, (((1,), (1,)), ((), ())), preferred_element_type=F32) * scale

    cmp = []
    for which in range(2):
        heads = []
        for h in range(N_KV):
            zh = zbuf[slot, which, :, 2 * h * HEAD_DIM:(2 * h + 2) * HEAD_DIM]
            pre = zh[:, :HEAD_DIM] + pltpu.roll(zh[:, HEAD_DIM:], r - 1, 0)
            mid = pre * jax.nn.sigmoid(pre)
            heads.append(jnp.dot(mid.astype(BF16), w2_ref[which], preferred_element_type=F32).astype(BF16))
        cmp.append(heads)
    q_raw = all_heads(qraw_ref)
    c_end = lax.broadcasted_iota(jnp.int32, (nq, r), 1) * CMP_STRIDE + (CMP_LEN - 1)
    cmask = c_end <= qpos
    s = per_head(scores(q_raw, cmp[0][0]), scores(q_raw, cmp[0][1])) + jnp.where(cmask, 0.0, NEG)
    p = jnp.exp(s - jnp.max(s, axis=-1, keepdims=True)) * jnp.where(cmask, 1.0, 0.0)
    p = p / jnp.maximum(jnp.sum(p, axis=-1, keepdims=True), 1e-30)
    pb = p.astype(BF16)
    o_cmp = per_head(jnp.dot(pb, cmp[1][0], preferred_element_type=F32),
                     jnp.dot(pb, cmp[1][1], preferred_element_type=F32))
    psum = per_head(jnp.sum(jnp.where(row_head == 0, p, 0.0), axis=0, keepdims=True),
                    jnp.sum(jnp.where(row_head == 1, p, 0.0), axis=0, keepdims=True))
    blk = lax.broadcasted_iota(jnp.int32, (nq, n_sel_pad), 1)
    sel = _top_rank_mask(_block_scores(_importance(psum, n_sel_pad), blk, qpos), 1,
                         past_len // SEL_LEN + 1)

    def attend(q, k, v, k_new, v_new, bias, new_bias):
        s = scores(q, k) + bias
        s_new = jnp.sum(q.astype(F32) * k_new.astype(BF16).astype(F32), axis=1, keepdims=True) * scale + new_bias
        m = jnp.maximum(jnp.max(s, axis=-1, keepdims=True), s_new)
        p = jnp.exp(s - m)
        p_new = jnp.exp(s_new - m)
        l = jnp.sum(p, axis=-1, keepdims=True) + p_new
        o = (jnp.dot(p.astype(BF16), v, preferred_element_type=F32)
             + p_new.astype(BF16).astype(F32) * v_new.astype(BF16).astype(F32))
        return o / jnp.maximum(l, 1e-30)

    def own_head(n_rows):
        col = lax.broadcasted_iota(jnp.int32, (nq, n_rows), 1)
        return jnp.bitwise_and(col, N_KV - 1) == row_head

    q_rot = all_heads(qrot_ref)
    n_rows = N_KV * past_len
    ej = lax.broadcasted_iota(jnp.int32, (n_sel_pad, n_rows), 0)
    et = lax.broadcasted_iota(jnp.int32, (n_sel_pad, n_rows), 1)
    expand = jnp.where(jnp.right_shift(et, sel_shift + 1) == ej, 1.0, 0.0).astype(BF16)
    selm = jnp.dot(sel.astype(BF16), expand, preferred_element_type=F32)
    sel_bias = jnp.where((selm > 0.5) & own_head(n_rows), 0.0, NEG)
    new_blk = past_len // SEL_LEN
    new_bias = jnp.where(sel[:, new_blk:new_blk + 1] > 0.5, 0.0, NEG)
    new_kv = lambda ref, c: per_head(ref[:, c * A_KV:c * A_KV + HEAD_DIM], ref[:, c * A_KV + HEAD_DIM:(c + 1) * A_KV])
    o_sel = attend(q_rot, sbuf[slot, 0].astype(BF16), sbuf[slot, 1].astype(BF16),
                   new_kv(knew_ref, 0), new_kv(knew_ref, 1), sel_bias, new_bias)
    win_bias = jnp.where(own_head(wk_ref.shape[0]), 0.0, NEG)
    o_win = attend(q_rot, wk_ref[...].astype(BF16), wv_ref[...].astype(BF16),
                   new_kv(wnew_ref, 0), new_kv(wnew_ref, 1), win_bias, 0.0)

    gts = jax.nn.sigmoid(gate_ref[...])
    for h in range(N_KV):
        for g in range(GROUP):
            j, c0 = h * GROUP + g, h * HEAD_DIM + 3 * g
            o = (gts[:, c0:c0 + 1] * o_cmp[j:j + 1] + gts[:, c0 + 1:c0 + 2] * o_sel[j:j + 1]
                 + gts[:, c0 + 2:c0 + 3] * o_win[j:j + 1])
            o_ref[:, j * HEAD_DIM:(j + 1) * HEAD_DIM] = o.astype(o_ref.dtype)


def _nsa_sample(zs, page_table, zk_pool, zv_pool, sel_k, sel_v, win_k, win_v, w2):
    bs, n_pages = page_table.shape
    past_len = n_pages * PAGE_SIZE
    wb = win_k.shape[1]
    assert wb <= WINDOW and past_len % SEL_LEN == 0 and past_len // SEL_LEN < HEAD_DIM
    assert N_KV == 2
    zs3 = zs.reshape(bs, 1, NZ)
    zspec = lambda w, off: pl.BlockSpec((None, 1, w), lambda b, pt: (b, 0, off // w))
    anyspec = pl.BlockSpec(memory_space=pl.ANY)
    wspec = pl.BlockSpec((None, N_KV * wb, HEAD_DIM), lambda b, pt: (b, 0, 0))
    rows2d = lambda a: a.reshape(a.shape[0], a.shape[1] * N_KV, HEAD_DIM)
    out = pl.pallas_call(
        functools.partial(_nsa_sample_kernel, n_pages=n_pages, past_len=past_len),
        out_shape=jax.ShapeDtypeStruct((bs, 1, A_Q), BF16),
        grid_spec=pltpu.PrefetchScalarGridSpec(
            num_scalar_prefetch=1, grid=(bs,),
            in_specs=[zspec(A_Q, Z_QRAW), zspec(A_Q, Z_QROT), zspec(2 * A_KV, Z_KS), zspec(2 * A_KV, Z_KW),
                      zspec(2 * HEAD_DIM, Z_GNSA), pl.BlockSpec(w2.shape, lambda b, pt: (0, 0, 0)), wspec, wspec,
                      anyspec, anyspec, anyspec, anyspec],
            out_specs=pl.BlockSpec((None, 1, A_Q), lambda b, pt: (b, 0, 0)),
            scratch_shapes=[pltpu.VMEM((2, 2, n_pages * ROWS_PER_PAGE, ZW), F32),
                            pltpu.VMEM((2, 2, N_KV * past_len, HEAD_DIM), F32), pltpu.SemaphoreType.DMA((2,))]),
        compiler_params=pltpu.CompilerParams(dimension_semantics=("arbitrary",), vmem_limit_bytes=VMEM_LIMIT),
        name="nsa_sample",
    )(page_table, zs3, zs3, zs3, zs3, zs3, w2, rows2d(win_k), rows2d(win_v), zk_pool, zv_pool,
      rows2d(sel_k), rows2d(sel_v))
    return out.reshape(bs, A_Q)


def _mlstm_sample_kernel(q_ref, k_ref, v_ref, op_ref, g_ref, gain_ref, c_ref, n_ref, m_ref,
                         h_ref, co_ref, no_ref, mo_ref):
    eye = (lax.broadcasted_iota(jnp.int32, (DQK_B, DQK_B), 0) == lax.broadcasted_iota(jnp.int32, (DQK_B, DQK_B), 1))
    col = lambda row: jnp.sum(jnp.where(eye, row, 0.0), axis=1, keepdims=True)
    lane = lax.broadcasted_iota(jnp.int32, (1, N_HEADS_B), 1)
    m_out = jnp.zeros((1, N_HEADS_B), F32)
    for h in range(N_HEADS_B):
        q = q_ref[:, h * DQK_B:(h + 1) * DQK_B]
        k = k_ref[:, h * DQK_B:(h + 1) * DQK_B] * DQK_B ** -0.5
        v = v_ref[:, h * DV_B:(h + 1) * DV_B]
        log_i, log_f = _gate_logs(g_ref[:, h:h + 1], g_ref[:, N_HEADS_B + h:N_HEADS_B + h + 1])
        c_old, n_old, m_old = c_ref[h], n_ref[h:h + 1, :], m_ref[:, h:h + 1]
        inter = log_f + m_old
        m_new = jnp.maximum(inter, log_i)
        decay = jnp.exp(inter - m_new)
        w = jnp.exp(log_i - m_new)
        co_ref[h] = decay * c_old + col(w * k) * v
        no_ref[h:h + 1, :] = decay * n_old + w * k
        m_out = jnp.where(lane == h, m_new, m_out)
        w_intra = w * jnp.sum(q * k, axis=1, keepdims=True)
        num = decay * jnp.sum(c_old * col(q), axis=0, keepdims=True) + w_intra * v
        den = decay * jnp.sum(q * n_old, axis=1, keepdims=True) + w_intra
        hh = num / jnp.maximum(jnp.abs(den), jnp.exp(-m_new))
        out = _rms(hh, gain_ref[h]) * jax.nn.sigmoid(op_ref[:, h * DV_B:(h + 1) * DV_B])
        h_ref[:, h * DV_B:(h + 1) * DV_B] = out.astype(h_ref.dtype)
    mo_ref[...] = m_out


def _mlstm_sample(zs, gain, c0, n0, m0):
    bs, H = zs.shape[0], N_HEADS_B
    zs3 = zs.reshape(bs, 1, NZ)
    zspec = lambda w, off: pl.BlockSpec((None, 1, w), lambda b: (b, 0, off // w))
    cspec = pl.BlockSpec((None, H, DQK_B, DV_B), lambda b: (b, 0, 0, 0))
    nspec = pl.BlockSpec((None, H, DQK_B), lambda b: (b, 0, 0))
    mspec = pl.BlockSpec((None, 1, H), lambda b: (b, 0, 0))
    h, c, n, m = pl.pallas_call(
        _mlstm_sample_kernel,
        out_shape=(jax.ShapeDtypeStruct((bs, 1, B_V), BF16), jax.ShapeDtypeStruct(c0.shape, F32),
                   jax.ShapeDtypeStruct(n0.shape, F32), jax.ShapeDtypeStruct((bs, 1, H), F32)),
        grid=(bs,),
        in_specs=[zspec(B_QK, Z_QB), zspec(B_QK, Z_KB), zspec(B_V, Z_VB), zspec(B_V, Z_OB), zspec(HEAD_DIM, Z_IB),
                  pl.BlockSpec((H, 1, DV_B), lambda b: (0, 0, 0)), cspec, nspec, mspec],
        out_specs=(pl.BlockSpec((None, 1, B_V), lambda b: (b, 0, 0)), cspec, nspec, mspec),
        compiler_params=pltpu.CompilerParams(dimension_semantics=("parallel",), vmem_limit_bytes=VMEM_LIMIT),
        name="mlstm_sample",
    )(zs3, zs3, zs3, zs3, zs3, gain.reshape(H, 1, DV_B), c0, n0, m0.reshape(bs, 1, H))
    return h.reshape(bs, B_V), c, n, m.reshape(bs, H)


def _split_kv(z, B, T):
    z = z.reshape(B, T, NZ)
    kv = lambda o: z[..., o:o + A_KV].reshape(B, T, N_KV, HEAD_DIM)
    return kv(Z_KC), kv(Z_VC), kv(Z_KS), kv(Z_VS), kv(Z_KW), kv(Z_VW)


def kernel(x_prompt, x_sample, cache_cmp_k, cache_cmp_v, cache_sel_k, cache_sel_v, cache_win_k, cache_win_v,
           state_mlstm_c, state_mlstm_n, state_mlstm_m, page_table, norm_mix, w_in, b_in, cmp_pos, cmp_w1, cmp_w2,
           mlstm_norm, w_up_a, w_up_b, w_out, norm_ffn, w_router_grp, b_router_grp, w_router_exp, b_router_exp,
           w_gate, w_up, w_down, norm_final):
    assert w_in.shape[0] == 1, "single layer"
    Bp, Tp = x_prompt.shape[:2]
    Bs, Ts = x_sample.shape[:2]
    assert Ts == 1
    n_p, n_s = Bp * Tp, Bs * Ts
    past_len = page_table.shape[1] * PAGE_SIZE
    pos_p = jnp.arange(Tp, dtype=jnp.int32)
    pos_s = past_len + jnp.arange(Ts, dtype=jnp.int32)
    l = 0

    w_perm, b_perm = _permute_in_proj(w_in[l], b_in[l])
    cw = _compress_weights(cmp_pos[l], cmp_w1[l], cmp_w2[l])
    mw = _merge_weights(w_up_a[l], w_up_b[l], w_out[l], norm_ffn[l], w_router_grp[l], b_router_grp[l],
                        w_router_exp[l], b_router_exp[l])

    xp2d = x_prompt.reshape(n_p, D_MODEL)
    zp = _project(xp2d, jnp.tile(pos_p, Bp), norm_mix[l], w_perm, b_perm, 2048)
    kc, vc, ks, vs, kw, vw = _split_kv(zp, Bp, Tp)
    o_a = _nsa_prompt(zp, _compress_prompt(kc, vc, cw), Bp, Tp)
    h_b, c_p, n_p_state, m_p = _mlstm_prompt(zp, mlstm_norm[l], Bp, Tp)
    bufs = _merge(xp2d, o_a, h_b, zp, mw, n_p + n_s, 0, 256)
    wl = min(WINDOW, Tp)
    prompt_new = (kc, vc, ks, vs, kw[:, -wl:], vw[:, -wl:], c_p, n_p_state, m_p)

    xs2d = x_sample.reshape(n_s, D_MODEL)
    zs = _project(xs2d, jnp.repeat(pos_s, Bs), norm_mix[l], w_perm, b_perm, 128)
    kc, vc, ks, vs, kw, vw = _split_kv(zs, Bs, Ts)
    assert (past_len + Ts - CMP_LEN) // CMP_STRIDE + 1 == past_len // CMP_STRIDE - 1
    zk_pool, zv_pool = _compress_pool(cache_cmp_k[l], cache_cmp_v[l], cw)
    o_a_s = _nsa_sample(zs, page_table, zk_pool, zv_pool, cache_sel_k[l], cache_sel_v[l],
                        cache_win_k[l], cache_win_v[l], cw[2])
    h_s, c_s, n_s_state, m_s = _mlstm_sample(zs, mlstm_norm[l], state_mlstm_c[l], state_mlstm_n[l],
                                             state_mlstm_m[l])
    x2, xn, route = _merge(xs2d, o_a_s, h_s, zs, mw, n_p + n_s, n_p, 128, bufs=bufs)
    wk = jnp.concatenate([cache_win_k[l], kw], axis=1)
    wv = jnp.concatenate([cache_win_v[l], vw], axis=1)
    wl = min(WINDOW, past_len + Ts)
    sample_new = (kc, vc, ks, vs, wk[:, -wl:], wv[:, -wl:], c_s, n_s_state, m_s)

    y_p, y_s = _moe(x2, xn, route, w_gate[l], w_up[l], w_down[l], norm_final, n_p)
    return ((y_p.reshape(Bp, Tp, D_MODEL), y_s.reshape(Bs, Ts, D_MODEL))
            + tuple(a[None] for a in prompt_new) + tuple(a[None] for a in sample_new))
```

```python
import functools

import numpy as np
import jax
import jax.numpy as jnp
from jax import lax
from jax.experimental import pallas as pl
from jax.experimental.pallas import tpu as pltpu

D_MODEL = 2048
PAGE_SIZE = 128
N_HEADS_A = 8
N_KV = 2
GROUP = N_HEADS_A // N_KV
HEAD_DIM = 128
ROT_DIM = HEAD_DIM // 4
ROPE_THETA = 500000.0
CMP_LEN = 32
CMP_STRIDE = 16
SEL_LEN = 64
SEL_TOP = 16
FORCE_BONUS = 100.0
WINDOW = 512
Q_BLOCK = 128
N_HEADS_B = 4
DQK_B = 128
DV_B = 256
MLSTM_CHUNK = 64
GATE_CAP = 15.0
N_GROUPS = 4
EXPERTS_PER_GROUP = 8
N_EXPERTS = N_GROUPS * EXPERTS_PER_GROUP
TOP_K = 2
D_EXPERT = 1024
EPS = 1e-6
NEG = -1e30

A_Q = N_HEADS_A * HEAD_DIM
A_KV = N_KV * HEAD_DIM
B_QK = N_HEADS_B * DQK_B
B_V = N_HEADS_B * DV_B
IN_SPLITS = (A_Q, A_KV, A_KV, A_KV, A_KV, A_KV, A_KV, 3 * N_HEADS_A,
             B_QK, B_QK, B_V, N_HEADS_B, N_HEADS_B, B_V, D_MODEL, D_MODEL)

F32 = jnp.float32
BF16 = jnp.bfloat16
VMEM_LIMIT = 48 * 1024 * 1024
VMEM_LIMIT_MERGE = 56 * 1024 * 1024

_CUTS = np.concatenate([[0], np.cumsum(IN_SPLITS)]).tolist()
_SRC = dict(zip(('q_a', 'k_c', 'v_c', 'k_s', 'v_s', 'k_w', 'v_w', 'g_nsa', 'q_b', 'k_b', 'v_b', 'i_b', 'f_b',
                 'o_b', 'g_ma', 'g_mb'), zip(_CUTS[:-1], _CUTS[1:])))
_Z_ORDER = ('g_ma', 'g_mb', 'q_a', 'q_a', 'v_b', 'o_b', 'k_c', 'v_c', 'k_s', 'v_s', 'k_w', 'v_w', 'q_b', 'k_b')
PROJ_TN = 512
Z_GMA, Z_GMB, Z_QRAW, Z_QROT, Z_VB, Z_OB = 0, 2048, 4096, 5120, 6144, 7168
Z_KC, Z_VC, Z_KS, Z_VS, Z_KW, Z_VW = 8192, 8448, 8704, 8960, 9216, 9472
Z_QB, Z_KB, Z_SMALL = 9728, 10240, 10752
N_GATE = 3 * GROUP
Z_GNSA, Z_IB, Z_FB = Z_SMALL, Z_SMALL + 2 * HEAD_DIM, Z_SMALL + 2 * HEAD_DIM + N_HEADS_B
NZ = 11264
_ROPE_ALL_TILES = (Z_QROT // PROJ_TN, Z_QROT // PROJ_TN + 1)
_ROPE_HALF_TILES = (Z_KS // PROJ_TN, Z_KW // PROJ_TN)

ROUTE_W = 128
MOE_TILE = 512
MOE_SUB = 256
MOE_FC = 512


def _rms(x, g):
    return x * lax.rsqrt(jnp.mean(x * x, axis=-1, keepdims=True) + EPS) * g


def _proj_kernel(x_ref, g_ref, w_ref, b_ref, cos_ref, sa_ref, sb_ref, z_ref, xn_ref):
    j = pl.program_id(1)

    @pl.when(j == 0)
    def _():
        xn_ref[...] = _rms(x_ref[...], g_ref[...]).astype(BF16)

    acc = jnp.dot(xn_ref[...], w_ref[...], preferred_element_type=F32) + b_ref[...]

    def rope(blk):
        return (blk * cos_ref[...] + pltpu.roll(blk, HEAD_DIM - ROT_DIM // 2, 1) * sa_ref[...]
                + pltpu.roll(blk, ROT_DIM // 2, 1) * sb_ref[...])

    is_all = (j == _ROPE_ALL_TILES[0]) | (j == _ROPE_ALL_TILES[1])
    is_half = (j == _ROPE_HALF_TILES[0]) | (j == _ROPE_HALF_TILES[1])
    n_blk = PROJ_TN // HEAD_DIM

    @pl.when(is_all)
    def _():
        for c in range(n_blk):
            z_ref[:, c * HEAD_DIM:(c + 1) * HEAD_DIM] = rope(acc[:, c * HEAD_DIM:(c + 1) * HEAD_DIM])

    @pl.when(is_half)
    def _():
        for c in range(n_blk // 2):
            z_ref[:, c * HEAD_DIM:(c + 1) * HEAD_DIM] = rope(acc[:, c * HEAD_DIM:(c + 1) * HEAD_DIM])
        z_ref[:, PROJ_TN // 2:] = acc[:, PROJ_TN // 2:]

    @pl.when(jnp.logical_not(is_all | is_half))
    def _():
        z_ref[...] = acc


def _rope_tables(pos):
    half = ROT_DIM // 2
    inv = ROPE_THETA ** (-jnp.arange(half, dtype=F32) / half)
    ang = pos.astype(F32)[:, None] * inv[None, :]
    cos, sin = jnp.cos(ang), jnp.sin(ang)
    n = pos.shape[0]
    ones = jnp.ones((n, HEAD_DIM - ROT_DIM), F32)
    zeros = jnp.zeros((n, HEAD_DIM - half), F32)
    cos_t = jnp.concatenate([cos, cos, ones], axis=1)
    sa_t = jnp.concatenate([-sin, zeros], axis=1)
    sb_t = jnp.concatenate([jnp.zeros((n, half), F32), sin, jnp.zeros((n, HEAD_DIM - ROT_DIM), F32)], axis=1)
    return cos_t, sa_t, sb_t


def _permute_in_proj(w_in, b_in):
    g0 = _SRC['g_nsa'][0]
    spans = [_SRC[k] for k in _Z_ORDER]
    spans += [(g0, g0 + N_GATE), HEAD_DIM - N_GATE, (g0 + N_GATE, g0 + 2 * N_GATE), HEAD_DIM - N_GATE,
              _SRC['i_b'], _SRC['f_b'], NZ - Z_FB - N_HEADS_B]

    def permute(a, dtype):
        parts = [jnp.zeros((a.shape[0], s), dtype) if isinstance(s, int) else a[:, s[0]:s[1]].astype(dtype)
                 for s in spans]
        return jnp.concatenate(parts, axis=1)

    return permute(w_in, BF16), permute(b_in[None, :], F32)


def _project(x2d, pos_rows, norm_g, w_perm, b_perm, tm):
    m = x2d.shape[0]
    cos_t, sa_t, sb_t = _rope_tables(pos_rows)
    rows = lambda w: pl.BlockSpec((tm, w), lambda i, j: (i, 0), pipeline_mode=pl.Buffered(1))
    return pl.pallas_call(
        _proj_kernel,
        out_shape=jax.ShapeDtypeStruct((m, NZ), F32),
        grid=(m // tm, NZ // PROJ_TN),
        in_specs=[rows(D_MODEL),
                  pl.BlockSpec((1, D_MODEL), lambda i, j: (0, 0)),
                  pl.BlockSpec((D_MODEL, PROJ_TN), lambda i, j: (0, j)),
                  pl.BlockSpec((1, PROJ_TN), lambda i, j: (0, j)),
                  rows(HEAD_DIM), rows(HEAD_DIM), rows(HEAD_DIM)],
        out_specs=pl.BlockSpec((tm, PROJ_TN), lambda i, j: (i, j)),
        scratch_shapes=[pltpu.VMEM((tm, D_MODEL), BF16)],
        compiler_params=pltpu.CompilerParams(dimension_semantics=("parallel", "arbitrary"),
                                             vmem_limit_bytes=VMEM_LIMIT),
        name="proj",
    )(x2d, norm_g.reshape(1, D_MODEL), w_perm, b_perm, cos_t, sa_t, sb_t)


def _compress_body(x_ref, w1, pb, w2, r):
    bias = pb[0:1, :HEAD_DIM] + pb[1:2, HEAD_DIM:]
    outs = []
    for h in range(N_KV):
        xh = jnp.concatenate([x_ref[:, (N_KV * s + h) * HEAD_DIM:(N_KV * s + h + 1) * HEAD_DIM]
                              for s in range(CMP_STRIDE)], axis=1).astype(BF16)
        zz = jnp.dot(xh, w1, preferred_element_type=F32)
        pre = zz[:, :HEAD_DIM] + pltpu.roll(zz[:, HEAD_DIM:], r - 1, 0) + bias
        mid = pre * jax.nn.sigmoid(pre)
        outs.append(jnp.dot(mid.astype(BF16), w2, preferred_element_type=F32))
    return outs


def _compress_kernel(xk_ref, xv_ref, w1_ref, p_ref, w2_ref, o_ref):
    r = xk_ref.shape[0]
    for which, x_ref in enumerate((xk_ref, xv_ref)):
        w1 = w1_ref[which]
        pb = jnp.dot(p_ref[which], w1, preferred_element_type=F32)
        for h, o in enumerate(_compress_body(x_ref, w1, pb, w2_ref[which], r)):
            o_ref[which, h] = o


def _compress_weights(cmp_pos, cmp_w1, cmp_w2):
    assert CMP_LEN == 2 * CMP_STRIDE
    half = CMP_STRIDE * HEAD_DIM
    w1 = jnp.concatenate([cmp_w1[:, :half], cmp_w1[:, half:]], axis=2).astype(BF16)
    p = cmp_pos.reshape(2, 2, half)
    p = jnp.concatenate([p, jnp.zeros((2, 6, half), p.dtype)], axis=1).astype(BF16)
    return w1, p, cmp_w2.astype(BF16)


def _compress_prompt(kc, vc, cw):
    B, T = kc.shape[:2]
    r = T // CMP_STRIDE
    width = CMP_STRIDE * N_KV * HEAD_DIM
    w1, p, w2 = cw
    full = lambda a: pl.BlockSpec(a.shape, lambda b: (0,) * a.ndim)
    xspec = pl.BlockSpec((None, r, width), lambda b: (b, 0, 0))
    return pl.pallas_call(
        _compress_kernel,
        out_shape=jax.ShapeDtypeStruct((B, 2, N_KV, r, HEAD_DIM), F32),
        grid=(B,),
        in_specs=[xspec, xspec, full(w1), full(p), full(w2)],
        out_specs=pl.BlockSpec((None, 2, N_KV, r, HEAD_DIM), lambda b: (b, 0, 0, 0, 0)),
        compiler_params=pltpu.CompilerParams(dimension_semantics=("parallel",), vmem_limit_bytes=VMEM_LIMIT),
        name="compress_prompt",
    )(kc.reshape(B, r, width), vc.reshape(B, r, width), w1, p, w2)


def _group_scores(q, k, tq):
    s = lax.dot_general(q, k, (((1,), (1,)), ((), ())), preferred_element_type=F32) * HEAD_DIM ** -0.5
    return s.reshape(GROUP, tq, k.shape[0])


def _stack_heads(ref):
    return jnp.concatenate([ref[:, g * HEAD_DIM:(g + 1) * HEAD_DIM] for g in range(GROUP)], axis=0).astype(BF16)


def _importance(psum, n_sel):
    r = psum.shape[1]
    ci = lax.broadcasted_iota(jnp.int32, (r, n_sel), 0) * CMP_STRIDE
    cj = lax.broadcasted_iota(jnp.int32, (r, n_sel), 1) * SEL_LEN
    cover_t = ((ci < cj + SEL_LEN) & (ci + (CMP_LEN - 1) >= cj)).astype(BF16)
    p_hi = psum.astype(BF16)
    p_lo = (psum - p_hi.astype(F32)).astype(BF16)
    return jnp.dot(p_hi, cover_t, preferred_element_type=F32) + jnp.dot(p_lo, cover_t, preferred_element_type=F32)


def _block_scores(imp, blk, qpos):
    cur = jnp.right_shift(qpos, SEL_LEN.bit_length() - 1)
    forced = (blk == 0) | (blk == cur) | (blk == cur - 1)
    return jnp.where(blk * SEL_LEN <= qpos, imp + jnp.where(forced, FORCE_BONUS, 0.0), -1.0)


def _top_rank_mask(score, axis, n_cand):
    idx = lax.broadcasted_iota(jnp.int32, score.shape, axis)
    rank = jnp.zeros(score.shape, F32)
    for j in range(n_cand):
        cand = score[j:j + 1, :] if axis == 0 else score[:, j:j + 1]
        rank = rank + jnp.where((cand > score) | ((cand == score) & (idx > j)), 1.0, 0.0)
    return jnp.where(rank < float(SEL_TOP), 1.0, 0.0)


def _nsa_prompt_kernel(qraw_ref, qrot_ref, ks_ref, vs_ref, kw_ref, vw_ref, kc_ref, vc_ref, gate_ref, o_ref,
                       m_ref, l_ref, acc_ref, *, tq, tk, seq_len):
    q0 = pl.program_id(2) * tq
    r = kc_ref.shape[0]
    n_sel = seq_len // SEL_LEN
    sel_shift = SEL_LEN.bit_length() - 1
    qpos = q0 + lax.broadcasted_iota(jnp.int32, (tq, 1), 0)

    s = _group_scores(_stack_heads(qraw_ref), kc_ref[...].astype(BF16), tq)
    c_end = lax.broadcasted_iota(jnp.int32, (tq, r), 1) * CMP_STRIDE + (CMP_LEN - 1)
    cmask = c_end <= qpos
    s = s + jnp.where(cmask, 0.0, NEG)[None]
    p = jnp.exp(s - jnp.max(s, axis=-1, keepdims=True)) * jnp.where(cmask, 1.0, 0.0)[None]
    p = p / jnp.maximum(jnp.sum(p, axis=-1, keepdims=True), 1e-30)
    o_cmp = jnp.dot(p.reshape(GROUP * tq, r).astype(BF16), vc_ref[...].astype(BF16), preferred_element_type=F32)
    psum = p[0]
    for g in range(1, GROUP):
        psum = psum + p[g]
    imp_t = _importance(psum, HEAD_DIM).T[:n_sel]
    blk_t = lax.broadcasted_iota(jnp.int32, (n_sel, tq), 0)
    qpos_t = q0 + lax.broadcasted_iota(jnp.int32, (n_sel, tq), 1)
    sel_b = _top_rank_mask(_block_scores(imp_t, blk_t, qpos_t), 0, n_sel).astype(BF16)

    q_rot = _stack_heads(qrot_ref)
    m_ref[...] = jnp.full(m_ref.shape, NEG, F32)
    l_ref[...] = jnp.zeros(l_ref.shape, F32)
    acc_ref[...] = jnp.zeros(acc_ref.shape, F32)

    def sel_tile(kt, carry):
        k0 = pl.multiple_of(kt * tk, tk)
        k = ks_ref[pl.ds(k0, tk), :].astype(BF16)
        v = vs_ref[pl.ds(k0, tk), :].astype(BF16)
        s2 = _group_scores(q_rot, k, tq)
        ej = lax.broadcasted_iota(jnp.int32, (n_sel, tk), 0)
        et = k0 + lax.broadcasted_iota(jnp.int32, (n_sel, tk), 1)
        expand = jnp.where(jnp.right_shift(et, sel_shift) == ej, 1.0, 0.0).astype(BF16)
        selm = lax.dot_general(sel_b, expand, (((0,), (0,)), ((), ())), preferred_element_type=F32)
        kpos = k0 + lax.broadcasted_iota(jnp.int32, (tq, tk), 1)
        ok = (selm > 0.5) & (kpos <= qpos)
        s2 = s2 + jnp.where(ok, 0.0, NEG)[None]
        m_prev = m_ref[...]
        m_new = jnp.maximum(m_prev, jnp.max(s2, axis=-1, keepdims=True))
        alpha = jnp.exp(m_prev - m_new)
        p2 = jnp.exp(s2 - m_new)
        l_ref[...] = alpha * l_ref[...] + jnp.sum(p2, axis=-1, keepdims=True)
        pv = jnp.dot(p2.reshape(GROUP * tq, tk).astype(BF16), v, preferred_element_type=F32)
        acc_ref[...] = alpha.reshape(GROUP * tq, 1) * acc_ref[...] + pv
        m_ref[...] = m_new
        return carry

    lax.fori_loop(0, lax.div(q0 + tq + tk - 1, tk), sel_tile, 0)
    o_sel = acc_ref[...] / jnp.maximum(l_ref[...].reshape(GROUP * tq, 1), 1e-30)

    span = WINDOW + tq
    w0 = pl.multiple_of(jnp.maximum(q0 - WINDOW, 0), tq)
    s3 = _group_scores(q_rot, kw_ref[pl.ds(w0, span), :].astype(BF16), tq)
    kp = w0 + lax.broadcasted_iota(jnp.int32, (tq, span), 1)
    okw = (kp <= qpos) & (kp >= qpos - WINDOW)
    s3 = s3 + jnp.where(okw, 0.0, NEG)[None]
    p3 = jnp.exp(s3 - jnp.max(s3, axis=-1, keepdims=True))
    p3 = p3 / jnp.sum(p3, axis=-1, keepdims=True)
    o_win = jnp.dot(p3.reshape(GROUP * tq, span).astype(BF16), vw_ref[pl.ds(w0, span), :].astype(BF16),
                    preferred_element_type=F32)

    gts = jax.nn.sigmoid(gate_ref[...])
    for g in range(GROUP):
        rows = slice(g * tq, (g + 1) * tq)
        o_ref[:, g * HEAD_DIM:(g + 1) * HEAD_DIM] = (gts[:, 3 * g:3 * g + 1] * o_cmp[rows]
                                                     + gts[:, 3 * g + 1:3 * g + 2] * o_sel[rows]
                                                     + gts[:, 3 * g + 2:3 * g + 3] * o_win[rows]).astype(o_ref.dtype)


def _nsa_prompt(z, cmp, B, T, tq=256, tk=512):
    assert T % tk == 0 and tk % tq == 0 and tq % HEAD_DIM == 0 and T >= WINDOW + tq
    assert SEL_LEN & (SEL_LEN - 1) == 0 and tk % SEL_LEN == 0
    assert (T // SEL_LEN) % 8 == 0 and T // SEL_LEN <= HEAD_DIM
    nq = T // tq
    r = cmp.shape[3]
    gw = GROUP * HEAD_DIM
    qspec = lambda off: pl.BlockSpec((tq, gw), lambda b, h, q: (b * nq + q, off // gw + h))
    kvspec = lambda off: pl.BlockSpec((T, HEAD_DIM), lambda b, h, q: (b, off // HEAD_DIM + h))
    cspec = lambda which: pl.BlockSpec((None, None, None, r, HEAD_DIM), lambda b, h, q: (b, which, h, 0, 0))
    return pl.pallas_call(
        functools.partial(_nsa_prompt_kernel, tq=tq, tk=tk, seq_len=T),
        out_shape=jax.ShapeDtypeStruct((B * T, A_Q), BF16),
        grid=(B, N_KV, nq),
        in_specs=[qspec(Z_QRAW), qspec(Z_QROT), kvspec(Z_KS), kvspec(Z_VS), kvspec(Z_KW), kvspec(Z_VW),
                  cspec(0), cspec(1),
                  pl.BlockSpec((tq, HEAD_DIM), lambda b, h, q: (b * nq + q, Z_GNSA // HEAD_DIM + h))],
        out_specs=pl.BlockSpec((tq, gw), lambda b, h, q: (b * nq + q, h)),
        scratch_shapes=[pltpu.VMEM((GROUP, tq, 1), F32), pltpu.VMEM((GROUP, tq, 1), F32),
                        pltpu.VMEM((GROUP * tq, HEAD_DIM), F32)],
        compiler_params=pltpu.CompilerParams(dimension_semantics=("parallel", "parallel", "arbitrary"),
                                             vmem_limit_bytes=VMEM_LIMIT),
        name="nsa_prompt",
    )(z, z, z, z, z, z, cmp, cmp, z)


def _gate_logs(i_pre, f_pre):
    log_i = GATE_CAP * jnp.tanh(i_pre / GATE_CAP)
    fc = GATE_CAP * jnp.tanh(f_pre / GATE_CAP)
    log_f = jnp.minimum(fc, 0.0) - jnp.log1p(jnp.exp(-jnp.abs(fc)))
    return log_i, log_f


def _mlstm_prompt_kernel(q_ref, k_ref, v_ref, op_ref, grow_ref, gcol_ref, gain_ref,
                         h_ref, c_ref, n_ref, m_ref):
    @pl.when(pl.program_id(1) == 0)
    def _():
        c_ref[...] = jnp.zeros(c_ref.shape, F32)
        n_ref[...] = jnp.zeros(n_ref.shape, F32)
        m_ref[...] = jnp.zeros(m_ref.shape, F32)

    for h in range(N_HEADS_B):
        qk_cols = slice(h * DQK_B, (h + 1) * DQK_B)
        v_cols = slice(h * DV_B, (h + 1) * DV_B)
        _mlstm_chunk(q_ref[:, qk_cols], k_ref[:, qk_cols], v_ref[:, v_cols], op_ref[:, v_cols],
                     grow_ref[h:h + 1, :], grow_ref[N_HEADS_B + h:N_HEADS_B + h + 1, :],
                     gcol_ref[:, h:h + 1], gcol_ref[:, N_HEADS_B + h:N_HEADS_B + h + 1], gain_ref[h],
                     h_ref.at[:, v_cols], c_ref.at[h], n_ref.at[h], m_ref.at[h])


def _mlstm_chunk(q, k, v, o_pre, i_row, f_row, i_col, f_col, gain, h_ref, c_ref, n_ref, m_ref):
    L = q.shape[0]
    qb = q.astype(BF16)
    k = k * DQK_B ** -0.5
    vb = v.astype(BF16)
    i_r, f_r = _gate_logs(i_row, f_row)
    i_c, f_c = _gate_logs(i_col, f_col)
    li = lax.broadcasted_iota(jnp.int32, (L, L), 0)
    si = lax.broadcasted_iota(jnp.int32, (L, L), 1)
    tri = si <= li
    bcum_c = jnp.sum(jnp.where(tri, f_r, 0.0), axis=1, keepdims=True)
    bcum_r = jnp.sum(jnp.where(li <= si, f_c, 0.0), axis=0, keepdims=True)
    btot = jnp.sum(f_r, axis=1, keepdims=True)
    c_old, n_old, m_old = c_ref[...], n_ref[...], m_ref[0:1, 0:1]

    dlog = jnp.where(tri, bcum_c - bcum_r + i_r, NEG)
    inter = bcum_c + m_old
    m_t = jnp.maximum(inter, jnp.max(dlog, axis=1, keepdims=True))
    qk = lax.dot_general(qb, k.astype(BF16), (((1,), (1,)), ((), ())), preferred_element_type=F32)
    w_intra = jnp.exp(dlog - m_t) * qk
    w_inter = jnp.exp(inter - m_t)
    num = (w_inter * jnp.dot(qb, c_old.astype(BF16), preferred_element_type=F32)
           + jnp.dot(w_intra.astype(BF16), vb, preferred_element_type=F32))
    den = w_inter * jnp.sum(q * n_old, axis=1, keepdims=True) + jnp.sum(w_intra, axis=1, keepdims=True)
    h = num / jnp.maximum(jnp.abs(den), jnp.exp(-m_t))
    h_ref[...] = (_rms(h, gain) * jax.nn.sigmoid(o_pre)).astype(h_ref.dtype)

    wlog_c = btot - bcum_c + i_c
    m_new = jnp.maximum(btot + m_old, jnp.max(btot - bcum_r + i_r, axis=1, keepdims=True))
    decay = jnp.exp(btot + m_old - m_new)
    kw = k * jnp.exp(wlog_c - m_new)
    c_ref[...] = decay * c_old + lax.dot_general(kw.astype(BF16), vb, (((0,), (0,)), ((), ())),
                                                 preferred_element_type=F32)
    n_ref[...] = decay * n_old + jnp.sum(kw, axis=0, keepdims=True)
    m_ref[...] = jnp.broadcast_to(m_new, m_ref.shape)


def _mlstm_prompt(z, gain, B, T):
    L = MLSTM_CHUNK
    assert T % L == 0
    nc, H = T // L, N_HEADS_B
    gcol = z[:, Z_IB:Z_IB + 2 * H].reshape(B * nc, L, 2 * H)
    grow = gcol.transpose(0, 2, 1)
    rows = lambda w, off: pl.BlockSpec((L, w), lambda b, c: (b * nc + c, off // w))
    gidx = lambda b, c: (b * nc + c, 0, 0)
    state = lambda *s: pl.BlockSpec((None, H) + s, lambda b, c: (b, 0, 0, 0))
    h, c, n, m = pl.pallas_call(
        _mlstm_prompt_kernel,
        out_shape=(jax.ShapeDtypeStruct((B * T, B_V), BF16), jax.ShapeDtypeStruct((B, H, DQK_B, DV_B), F32),
                   jax.ShapeDtypeStruct((B, H, 1, DQK_B), F32), jax.ShapeDtypeStruct((B, H, 1, DQK_B), F32)),
        grid=(B, nc),
        in_specs=[rows(B_QK, Z_QB), rows(B_QK, Z_KB), rows(B_V, Z_VB), rows(B_V, Z_OB),
                  pl.BlockSpec((None, 2 * H, L), gidx), pl.BlockSpec((None, L, 2 * H), gidx),
                  pl.BlockSpec((H, 1, DV_B), lambda b, c: (0, 0, 0))],
        out_specs=(rows(B_V, 0), state(DQK_B, DV_B), state(1, DQK_B), state(1, DQK_B)),
        compiler_params=pltpu.CompilerParams(dimension_semantics=("parallel", "arbitrary"),
                                             vmem_limit_bytes=VMEM_LIMIT),
        name="mlstm_prompt",
    )(z, z, z, z, grow, gcol, gain.reshape(H, 1, DV_B))
    return h, c, n[:, :, 0], m[:, :, 0, 0]


def _merge_kernel(x_ref, oa_ref, hb_ref, gma_ref, gmb_ref, wa_ref, wb_ref, wo_ref, nf_ref, wrh_ref, wrl_ref, br_ref,
                  *refs):
    x2_ref, xn_ref, route_ref = refs[-3:]
    ua = jnp.dot(oa_ref[...], wa_ref[...], preferred_element_type=F32)
    ub = jnp.dot(hb_ref[...], wb_ref[...], preferred_element_type=F32)
    merged = jax.nn.sigmoid(gma_ref[...]) * ua + jax.nn.sigmoid(gmb_ref[...]) * ub
    x2 = x_ref[...] + jnp.dot(merged.astype(BF16), wo_ref[...], preferred_element_type=F32)
    x2_ref[...] = x2
    xn = _rms(x2, nf_ref[...])
    xn_ref[...] = xn
    hi = xn.astype(BF16)
    lo = (xn - hi.astype(F32)).astype(BF16)
    lg = (jnp.dot(hi, wrh_ref[...], preferred_element_type=F32) + jnp.dot(lo, wrh_ref[...], preferred_element_type=F32)
          + jnp.dot(hi, wrl_ref[...], preferred_element_type=F32)) + br_ref[...]
    lane = lax.broadcasted_iota(jnp.int32, lg.shape, 1)
    lanef = lane.astype(F32)
    ninf = -jnp.inf
    gmask = (lane >= N_EXPERTS) & (lane < N_EXPERTS + N_GROUPS)
    gl = jnp.where(gmask, lg, ninf)
    gmax = jnp.max(gl, axis=1, keepdims=True)
    gidx = jnp.min(jnp.where(gl == gmax, lanef, 1e9), axis=1, keepdims=True) - N_EXPERTS
    gprob = 1.0 / jnp.sum(jnp.where(gmask, jnp.exp(lg - gmax), 0.0), axis=1, keepdims=True)
    e_lo = gidx * EXPERTS_PER_GROUP
    el = jnp.where((lanef >= e_lo) & (lanef < e_lo + EXPERTS_PER_GROUP), lg, ninf)
    v1 = jnp.max(el, axis=1, keepdims=True)
    i1 = jnp.min(jnp.where(el == v1, lanef, 1e9), axis=1, keepdims=True)
    el2 = jnp.where(lanef == i1, ninf, el)
    v2 = jnp.max(el2, axis=1, keepdims=True)
    i2 = jnp.min(jnp.where(el2 == v2, lanef, 1e9), axis=1, keepdims=True)
    e2 = jnp.exp(v2 - v1)
    g1 = gprob / (1.0 + e2)
    g2 = gprob * e2 / (1.0 + e2)
    route_ref[...] = jnp.where(lane == 0, i1, jnp.where(lane == 1, i2, jnp.where(lane == 2, g1,
                               jnp.where(lane == 3, g2, 0.0))))


def _merge_weights(w_up_a, w_up_b, w_out, norm_ffn, w_rg, b_rg, w_re, b_re):
    assert TOP_K == 2
    pad = ROUTE_W - N_EXPERTS - N_GROUPS
    w = jnp.concatenate([w_re, w_rg, jnp.zeros((D_MODEL, pad), F32)], axis=1)
    b = jnp.concatenate([b_re, b_rg, jnp.zeros((pad,), F32)]).reshape(1, ROUTE_W)
    hi = w.astype(BF16)
    return (w_up_a.astype(BF16), w_up_b.astype(BF16), w_out.astype(BF16), norm_ffn.reshape(1, D_MODEL).astype(F32),
            hi, (w - hi.astype(F32)).astype(BF16), b)


def _merge(x2d, o_a, h_b, z, mw, n_total, row0, tm, bufs=None):
    m = x2d.shape[0]
    i0 = row0 // tm
    row = lambda w, c=0: pl.BlockSpec((tm, w), lambda i: (i, c))
    const = lambda a: pl.BlockSpec(a.shape, lambda i: (0,) * a.ndim, pipeline_mode=pl.Buffered(1))
    outw = (D_MODEL, D_MODEL, ROUTE_W)
    in_specs = [row(D_MODEL), row(A_Q), row(B_V), row(D_MODEL, Z_GMA // D_MODEL), row(D_MODEL, Z_GMB // D_MODEL)]
    in_specs += [const(a) for a in mw]
    args = [x2d, o_a, h_b, z, z] + list(mw)
    aliases = {}
    if bufs is not None:
        in_specs += [pl.BlockSpec(memory_space=pl.ANY)] * 3
        aliases = {len(args) + k: k for k in range(3)}
        args += list(bufs)
    return pl.pallas_call(
        _merge_kernel,
        out_shape=tuple(jax.ShapeDtypeStruct((n_total, w), F32) for w in outw),
        grid=(m // tm,),
        in_specs=in_specs,
        out_specs=tuple(pl.BlockSpec((tm, w), lambda i: (i0 + i, 0)) for w in outw),
        input_output_aliases=aliases,
        compiler_params=pltpu.CompilerParams(dimension_semantics=("parallel",), vmem_limit_bytes=VMEM_LIMIT_MERGE),
        name="merge_route",
    )(*args)


def _moe_plan(route, n_tiles):
    n = route.shape[0]
    a = n * TOP_K
    flat_e = route[:, 0:TOP_K].astype(jnp.int32).reshape(a)
    onehot = (flat_e[:, None] == jnp.arange(N_EXPERTS, dtype=jnp.int32)[None, :]).astype(jnp.int32)
    rank = jnp.take_along_axis(jnp.cumsum(onehot, axis=0) - onehot, flat_e[:, None], axis=1)[:, 0]
    counts = jnp.sum(onehot, axis=0)
    ntile = (counts + MOE_TILE - 1) // MOE_TILE
    tile_end = jnp.cumsum(ntile)
    dest = ((tile_end - ntile)[flat_e] * MOE_TILE + rank).astype(jnp.int32)
    n_used = tile_end[-1].astype(jnp.int32)
    t = jnp.minimum(jnp.arange(n_tiles, dtype=jnp.int32), n_used - 1)
    tile_expert = jnp.minimum(jnp.searchsorted(tile_end, t, side='right'), N_EXPERTS - 1).astype(jnp.int32)
    first = (tile_end - ntile)[tile_expert]
    tile_rows = jnp.clip(counts[tile_expert] - (t - first) * MOE_TILE, 0, MOE_TILE)
    tile_rows = jnp.where(jnp.arange(n_tiles) < n_used, tile_rows, 0).astype(jnp.int32)
    return dest, tile_expert, tile_rows, n_used.reshape(1)


DMA_UNROLL = 8


def _dispatch_kernel(dest_ref, x_ref, o_hbm, sem):
    i = pl.program_id(0)
    tm = x_ref.shape[0]

    def copy(r, k):
        d = dest_ref[(i * tm + r) * TOP_K + k]
        return pltpu.make_async_copy(x_ref.at[pl.ds(r, 1)], o_hbm.at[pl.ds(d, 1)], sem)

    def start(r, c):
        for k in range(TOP_K):
            copy(r, k).start()
        return c

    def wait(r, c):
        for k in range(TOP_K):
            copy(r, k).wait()
        return c

    lax.fori_loop(0, tm, start, 0, unroll=DMA_UNROLL)
    lax.fori_loop(0, tm, wait, 0, unroll=DMA_UNROLL)


def _moe_dispatch(xn, dest, n_tiles, tm=128):
    n, d = xn.shape
    return pl.pallas_call(
        _dispatch_kernel,
        out_shape=jax.ShapeDtypeStruct((n_tiles * MOE_TILE, d), xn.dtype),
        grid_spec=pltpu.PrefetchScalarGridSpec(
            num_scalar_prefetch=1, grid=(n // tm,),
            in_specs=[pl.BlockSpec((tm, d), lambda i, ds: (i, 0))],
            out_specs=pl.BlockSpec(memory_space=pl.ANY),
            scratch_shapes=[pltpu.SemaphoreType.DMA(())]),
        compiler_params=pltpu.CompilerParams(dimension_semantics=("arbitrary",), vmem_limit_bytes=VMEM_LIMIT),
        name="moe_dispatch",
    )(dest, xn)


def _expert_kernel(te_ref, cnt_ref, nu_ref, x_ref, wg_ref, wu_ref, wd_ref, o_ref, xb_ref):
    t, j = pl.program_id(0), pl.program_id(1)
    cnt = cnt_ref[t]

    def run(rows):
        wg, wu, wd = wg_ref[...].astype(BF16), wu_ref[...].astype(BF16), wd_ref[...].astype(BF16)

        @pl.when(j == 0)
        def _():
            xb_ref[rows] = x_ref[rows].astype(BF16)

        xb = xb_ref[rows]
        g = jnp.dot(xb, wg, preferred_element_type=F32)
        u = jnp.dot(xb, wu, preferred_element_type=F32)
        h = (g * jax.nn.sigmoid(g) * u).astype(BF16)
        y = jnp.dot(h, wd, preferred_element_type=F32)

        @pl.when(j == 0)
        def _():
            o_ref[rows] = y

        @pl.when(j > 0)
        def _():
            o_ref[rows] += y

    assert MOE_TILE == 2 * MOE_SUB

    @pl.when((cnt > 0) & (cnt <= MOE_SUB))
    def _():
        run(slice(0, MOE_SUB))

    @pl.when(cnt > MOE_SUB)
    def _():
        run(slice(0, MOE_TILE))


def _moe_experts(xs, tile_expert, tile_rows, n_used, w_gate, w_up, w_down):
    n_tiles = xs.shape[0] // MOE_TILE
    nj = D_EXPERT // MOE_FC
    tix = lambda t, j, te, cnt, nu: (jnp.minimum(t, nu[0] - 1), 0)
    jj = lambda t, j, nu: jnp.where(t < nu[0], j, nj - 1)
    return pl.pallas_call(
        _expert_kernel,
        out_shape=jax.ShapeDtypeStruct(xs.shape, F32),
        grid_spec=pltpu.PrefetchScalarGridSpec(
            num_scalar_prefetch=3, grid=(n_tiles, nj),
            in_specs=[pl.BlockSpec((MOE_TILE, D_MODEL), tix),
                      pl.BlockSpec((None, D_MODEL, MOE_FC), lambda t, j, te, cnt, nu: (te[t], 0, jj(t, j, nu))),
                      pl.BlockSpec((None, D_MODEL, MOE_FC), lambda t, j, te, cnt, nu: (te[t], 0, jj(t, j, nu))),
                      pl.BlockSpec((None, MOE_FC, D_MODEL), lambda t, j, te, cnt, nu: (te[t], jj(t, j, nu), 0))],
            out_specs=pl.BlockSpec((MOE_TILE, D_MODEL), tix),
            scratch_shapes=[pltpu.VMEM((MOE_TILE, D_MODEL), BF16)]),
        compiler_params=pltpu.CompilerParams(dimension_semantics=("arbitrary", "arbitrary"),
                                             vmem_limit_bytes=VMEM_LIMIT_MERGE),
        name="moe_experts",
    )(tile_expert, tile_rows, n_used, xs, w_gate, w_up, w_down)


def _combine_kernel(dest_ref, x2_ref, route_ref, g_ref, y_hbm, op_ref, os_ref, buf_ref, sem, *, n_prompt_tiles):
    i = pl.program_id(0)
    tm = x2_ref.shape[0]

    def copy(r, k):
        d = dest_ref[(i * tm + r) * TOP_K + k]
        return pltpu.make_async_copy(y_hbm.at[pl.ds(d, 1)], buf_ref.at[k, pl.ds(r, 1)], sem)

    def start(r, c):
        for k in range(TOP_K):
            copy(r, k).start()
        return c

    def wait(r, c):
        for k in range(TOP_K):
            copy(r, k).wait()
        return c

    lax.fori_loop(0, tm, start, 0, unroll=DMA_UNROLL)
    lax.fori_loop(0, tm, wait, 0, unroll=DMA_UNROLL)
    ffn = buf_ref[0] * route_ref[:, TOP_K:TOP_K + 1]
    for k in range(1, TOP_K):
        ffn = ffn + buf_ref[k] * route_ref[:, TOP_K + k:TOP_K + k + 1]
    y = _rms(x2_ref[...] + ffn, g_ref[...])

    @pl.when(i < n_prompt_tiles)
    def _():
        op_ref[...] = y

    @pl.when(i >= n_prompt_tiles)
    def _():
        os_ref[...] = y


def _moe_combine(x2, route, dest, ys, norm_final, n_prompt, tm=128):
    n, d = x2.shape
    npt = n_prompt // tm
    nst = (n - n_prompt) // tm
    return pl.pallas_call(
        functools.partial(_combine_kernel, n_prompt_tiles=npt),
        out_shape=(jax.ShapeDtypeStruct((n_prompt, d), F32), jax.ShapeDtypeStruct((n - n_prompt, d), F32)),
        grid_spec=pltpu.PrefetchScalarGridSpec(
            num_scalar_prefetch=1, grid=(n // tm,),
            in_specs=[pl.BlockSpec((tm, d), lambda i, ds: (i, 0)), pl.BlockSpec((tm, ROUTE_W), lambda i, ds: (i, 0)),
                      pl.BlockSpec((1, d), lambda i, ds: (0, 0)), pl.BlockSpec(memory_space=pl.ANY)],
            out_specs=(pl.BlockSpec((tm, d), lambda i, ds: (jnp.minimum(i, npt - 1), 0)),
                       pl.BlockSpec((tm, d), lambda i, ds: (jnp.clip(i - npt, 0, nst - 1), 0))),
            scratch_shapes=[pltpu.VMEM((TOP_K, tm, d), F32), pltpu.SemaphoreType.DMA(())]),
        compiler_params=pltpu.CompilerParams(dimension_semantics=("arbitrary",), vmem_limit_bytes=VMEM_LIMIT),
        name="moe_combine",
    )(dest, x2, route, norm_final.reshape(1, d), ys)


def _moe(x2, xn, route, w_gate, w_up, w_down, norm_final, n_prompt):
    n = x2.shape[0]
    n_tiles = (n * TOP_K + N_EXPERTS * (MOE_TILE - 1) + MOE_TILE - 1) // MOE_TILE
    dest, tile_expert, tile_rows, n_used = _moe_plan(route, n_tiles)
    xs = _moe_dispatch(xn, dest, n_tiles)
    ys = _moe_experts(xs, tile_expert, tile_rows, n_used, w_gate, w_up, w_down)
    return _moe_combine(x2, route, dest, ys, norm_final, n_prompt)


POOL_ROWS = 256
ROWS_PER_PAGE = PAGE_SIZE // CMP_STRIDE
ZW = 2 * N_KV * HEAD_DIM


def _compress_pool_kernel(xk_ref, xv_ref, w1_ref, p_ref, zk_ref, zv_ref):
    for which, (x_ref, z_ref) in enumerate(((xk_ref, zk_ref), (xv_ref, zv_ref))):
        w1 = w1_ref[which]
        pb = jnp.dot(p_ref[which], w1, preferred_element_type=F32)
        bias = pb[0:1, :HEAD_DIM] + pb[1:2, HEAD_DIM:]
        for h in range(N_KV):
            xh = jnp.concatenate([x_ref[pl.ds(N_KV * s + h, POOL_ROWS, stride=N_KV * CMP_STRIDE), :]
                                  for s in range(CMP_STRIDE)], axis=1).astype(BF16)
            zz = jnp.dot(xh, w1, preferred_element_type=F32)
            z_ref[:, 2 * h * HEAD_DIM:(2 * h + 1) * HEAD_DIM] = zz[:, :HEAD_DIM] + bias
            z_ref[:, (2 * h + 1) * HEAD_DIM:(2 * h + 2) * HEAD_DIM] = zz[:, HEAD_DIM:]


def _compress_pool(pool_k, pool_v, cw):
    n_phys = pool_k.shape[0]
    rows = n_phys * ROWS_PER_PAGE
    assert rows % POOL_ROWS == 0
    in_rows = POOL_ROWS * CMP_STRIDE * N_KV
    w1, p, _ = cw
    full = lambda a: pl.BlockSpec(a.shape, lambda i: (0,) * a.ndim)
    flat = lambda a: a.reshape(n_phys * PAGE_SIZE * N_KV, HEAD_DIM)
    zk, zv = pl.pallas_call(
        _compress_pool_kernel,
        out_shape=(jax.ShapeDtypeStruct((rows, ZW), F32),) * 2,
        grid=(rows // POOL_ROWS,),
        in_specs=[pl.BlockSpec((in_rows, HEAD_DIM), lambda i: (i, 0))] * 2 + [full(w1), full(p)],
        out_specs=(pl.BlockSpec((POOL_ROWS, ZW), lambda i: (i, 0)),) * 2,
        compiler_params=pltpu.CompilerParams(dimension_semantics=("parallel",), vmem_limit_bytes=VMEM_LIMIT),
        name="compress_pool",
    )(flat(pool_k), flat(pool_v), w1, p)
    return zk.reshape(n_phys, ROWS_PER_PAGE, ZW), zv.reshape(n_phys, ROWS_PER_PAGE, ZW)


def _nsa_sample_kernel(pt_ref, qraw_ref, qrot_ref, knew_ref, wnew_ref, gate_ref, w2_ref, wk_ref, wv_ref,
                       zk_hbm, zv_hbm, ks_hbm, vs_hbm, o_ref, zbuf, sbuf, sem, *, n_pages, past_len):
    b = pl.program_id(0)
    nb = pl.num_programs(0)
    slot = lax.rem(b, 2)
    page_rows = N_KV * PAGE_SIZE

    def copies(seq, sl):
        out = []
        for pg in range(n_pages):
            page = pt_ref[seq, pg]
            for which, (zsrc, ssrc) in enumerate(((zk_hbm, ks_hbm), (zv_hbm, vs_hbm))):
                out.append(pltpu.make_async_copy(
                    zsrc.at[page], zbuf.at[sl, which, pl.ds(pg * ROWS_PER_PAGE, ROWS_PER_PAGE)], sem.at[sl]))
                out.append(pltpu.make_async_copy(
                    ssrc.at[page], sbuf.at[sl, which, pl.ds(pg * page_rows, page_rows)], sem.at[sl]))
        return out

    @pl.when(b == 0)
    def _():
        for cp in copies(0, 0):
            cp.start()

    @pl.when(b + 1 < nb)
    def _():
        for cp in copies(b + 1, 1 - slot):
            cp.start()

    for cp in copies(b, slot):
        cp.wait()

    scale = HEAD_DIM ** -0.5
    nq = N_KV * GROUP
    r = n_pages * ROWS_PER_PAGE
    qpos = jnp.full((nq, 1), past_len, jnp.int32)
    n_sel_pad = HEAD_DIM
    sel_shift = SEL_LEN.bit_length() - 1
    row_head = lax.broadcasted_iota(jnp.int32, (nq, 1), 0) // GROUP
    per_head = lambda a0, a1: jnp.where(row_head == 0, a0, a1)

    def all_heads(ref):
        return jnp.concatenate([ref[:, j * HEAD_DIM:(j + 1) * HEAD_DIM] for j in range(nq)], axis=0).astype(BF16)

    def scores(q, k):
        return lax.dot_general(q, k, (((1,), (1,)), ((), ())), preferred_element_type=F32) * scale

    cmp = []
    for which in range(2):
        heads = []
        for h in range(N_KV):
            zh = zbuf[slot, which, :, 2 * h * HEAD_DIM:(2 * h + 2) * HEAD_DIM]
            pre = zh[:, :HEAD_DIM] + pltpu.roll(zh[:, HEAD_DIM:], r - 1, 0)
            mid = pre * jax.nn.sigmoid(pre)
            heads.append(jnp.dot(mid.astype(BF16), w2_ref[which], preferred_element_type=F32).astype(BF16))
        cmp.append(heads)
    q_raw = all_heads(qraw_ref)
    c_end = lax.broadcasted_iota(jnp.int32, (nq, r), 1) * CMP_STRIDE + (CMP_LEN - 1)
    cmask = c_end <= qpos
    s = per_head(scores(q_raw, cmp[0][0]), scores(q_raw, cmp[0][1])) + jnp.where(cmask, 0.0, NEG)
    p = jnp.exp(s - jnp.max(s, axis=-1, keepdims=True)) * jnp.where(cmask, 1.0, 0.0)
    p = p / jnp.maximum(jnp.sum(p, axis=-1, keepdims=True), 1e-30)
    pb = p.astype(BF16)
    o_cmp = per_head(jnp.dot(pb, cmp[1][0], preferred_element_type=F32),
                     jnp.dot(pb, cmp[1][1], preferred_element_type=F32))
    psum = per_head(jnp.sum(jnp.where(row_head == 0, p, 0.0), axis=0, keepdims=True),
                    jnp.sum(jnp.where(row_head == 1, p, 0.0), axis=0, keepdims=True))
    blk = lax.broadcasted_iota(jnp.int32, (nq, n_sel_pad), 1)
    sel = _top_rank_mask(_block_scores(_importance(psum, n_sel_pad), blk, qpos), 1,
                         past_len // SEL_LEN + 1)

    def attend(q, k, v, k_new, v_new, bias, new_bias):
        s = scores(q, k) + bias
        s_new = jnp.sum(q.astype(F32) * k_new.astype(BF16).astype(F32), axis=1, keepdims=True) * scale + new_bias
        m = jnp.maximum(jnp.max(s, axis=-1, keepdims=True), s_new)
        p = jnp.exp(s - m)
        p_new = jnp.exp(s_new - m)
        l = jnp.sum(p, axis=-1, keepdims=True) + p_new
        o = (jnp.dot(p.astype(BF16), v, preferred_element_type=F32)
             + p_new.astype(BF16).astype(F32) * v_new.astype(BF16).astype(F32))
        return o / jnp.maximum(l, 1e-30)

    def own_head(n_rows):
        col = lax.broadcasted_iota(jnp.int32, (nq, n_rows), 1)
        return jnp.bitwise_and(col, N_KV - 1) == row_head

    q_rot = all_heads(qrot_ref)
    n_rows = N_KV * past_len
    ej = lax.broadcasted_iota(jnp.int32, (n_sel_pad, n_rows), 0)
    et = lax.broadcasted_iota(jnp.int32, (n_sel_pad, n_rows), 1)
    expand = jnp.where(jnp.right_shift(et, sel_shift + 1) == ej, 1.0, 0.0).astype(BF16)
    selm = jnp.dot(sel.astype(BF16), expand, preferred_element_type=F32)
    sel_bias = jnp.where((selm > 0.5) & own_head(n_rows), 0.0, NEG)
    new_blk = past_len // SEL_LEN
    new_bias = jnp.where(sel[:, new_blk:new_blk + 1] > 0.5, 0.0, NEG)
    new_kv = lambda ref, c: per_head(ref[:, c * A_KV:c * A_KV + HEAD_DIM], ref[:, c * A_KV + HEAD_DIM:(c + 1) * A_KV])
    o_sel = attend(q_rot, sbuf[slot, 0].astype(BF16), sbuf[slot, 1].astype(BF16),
                   new_kv(knew_ref, 0), new_kv(knew_ref, 1), sel_bias, new_bias)
    win_bias = jnp.where(own_head(wk_ref.shape[0]), 0.0, NEG)
    o_win = attend(q_rot, wk_ref[...].astype(BF16), wv_ref[...].astype(BF16),
                   new_kv(wnew_ref, 0), new_kv(wnew_ref, 1), win_bias, 0.0)

    gts = jax.nn.sigmoid(gate_ref[...])
    for h in range(N_KV):
        for g in range(GROUP):
            j, c0 = h * GROUP + g, h * HEAD_DIM + 3 * g
            o = (gts[:, c0:c0 + 1] * o_cmp[j:j + 1] + gts[:, c0 + 1:c0 + 2] * o_sel[j:j + 1]
                 + gts[:, c0 + 2:c0 + 3] * o_win[j:j + 1])
            o_ref[:, j * HEAD_DIM:(j + 1) * HEAD_DIM] = o.astype(o_ref.dtype)


def _nsa_sample(zs, page_table, zk_pool, zv_pool, sel_k, sel_v, win_k, win_v, w2):
    bs, n_pages = page_table.shape
    past_len = n_pages * PAGE_SIZE
    wb = win_k.shape[1]
    assert wb <= WINDOW and past_len % SEL_LEN == 0 and past_len // SEL_LEN < HEAD_DIM
    assert N_KV == 2
    zs3 = zs.reshape(bs, 1, NZ)
    zspec = lambda w, off: pl.BlockSpec((None, 1, w), lambda b, pt: (b, 0, off // w))
    anyspec = pl.BlockSpec(memory_space=pl.ANY)
    wspec = pl.BlockSpec((None, N_KV * wb, HEAD_DIM), lambda b, pt: (b, 0, 0))
    rows2d = lambda a: a.reshape(a.shape[0], a.shape[1] * N_KV, HEAD_DIM)
    out = pl.pallas_call(
        functools.partial(_nsa_sample_kernel, n_pages=n_pages, past_len=past_len),
        out_shape=jax.ShapeDtypeStruct((bs, 1, A_Q), BF16),
        grid_spec=pltpu.PrefetchScalarGridSpec(
            num_scalar_prefetch=1, grid=(bs,),
            in_specs=[zspec(A_Q, Z_QRAW), zspec(A_Q, Z_QROT), zspec(2 * A_KV, Z_KS), zspec(2 * A_KV, Z_KW),
                      zspec(2 * HEAD_DIM, Z_GNSA), pl.BlockSpec(w2.shape, lambda b, pt: (0, 0, 0)), wspec, wspec,
                      anyspec, anyspec, anyspec, anyspec],
            out_specs=pl.BlockSpec((None, 1, A_Q), lambda b, pt: (b, 0, 0)),
            scratch_shapes=[pltpu.VMEM((2, 2, n_pages * ROWS_PER_PAGE, ZW), F32),
                            pltpu.VMEM((2, 2, N_KV * past_len, HEAD_DIM), F32), pltpu.SemaphoreType.DMA((2,))]),
        compiler_params=pltpu.CompilerParams(dimension_semantics=("arbitrary",), vmem_limit_bytes=VMEM_LIMIT),
        name="nsa_sample",
    )(page_table, zs3, zs3, zs3, zs3, zs3, w2, rows2d(win_k), rows2d(win_v), zk_pool, zv_pool,
      rows2d(sel_k), rows2d(sel_v))
    return out.reshape(bs, A_Q)


def _mlstm_sample_kernel(q_ref, k_ref, v_ref, op_ref, g_ref, gain_ref, c_ref, n_ref, m_ref,
                         h_ref, co_ref, no_ref, mo_ref):
    eye = (lax.broadcasted_iota(jnp.int32, (DQK_B, DQK_B), 0) == lax.broadcasted_iota(jnp.int32, (DQK_B, DQK_B), 1))
    col = lambda row: jnp.sum(jnp.where(eye, row, 0.0), axis=1, keepdims=True)
    lane = lax.broadcasted_iota(jnp.int32, (1, N_HEADS_B), 1)
    for sq in range(q_ref.shape[0]):
        m_out = jnp.zeros((1, N_HEADS_B), F32)
        for h in range(N_HEADS_B):
            q = q_ref[sq, :, h * DQK_B:(h + 1) * DQK_B]
            k = k_ref[sq, :, h * DQK_B:(h + 1) * DQK_B] * DQK_B ** -0.5
            v = v_ref[sq, :, h * DV_B:(h + 1) * DV_B]
            log_i, log_f = _gate_logs(g_ref[sq, :, h:h + 1], g_ref[sq, :, N_HEADS_B + h:N_HEADS_B + h + 1])
            c_old, n_old, m_old = c_ref[sq, h], n_ref[sq, h:h + 1, :], m_ref[sq, :, h:h + 1]
            inter = log_f + m_old
            m_new = jnp.maximum(inter, log_i)
            decay = jnp.exp(inter - m_new)
            w = jnp.exp(log_i - m_new)
            co_ref[sq, h] = decay * c_old + col(w * k) * v
            no_ref[sq, h:h + 1, :] = decay * n_old + w * k
            m_out = jnp.where(lane == h, m_new, m_out)
            w_intra = w * jnp.sum(q * k, axis=1, keepdims=True)
            num = decay * jnp.sum(c_old * col(q), axis=0, keepdims=True) + w_intra * v
            den = decay * jnp.sum(q * n_old, axis=1, keepdims=True) + w_intra
            hh = num / jnp.maximum(jnp.abs(den), jnp.exp(-m_new))
            out = _rms(hh, gain_ref[h]) * jax.nn.sigmoid(op_ref[sq, :, h * DV_B:(h + 1) * DV_B])
            h_ref[sq, :, h * DV_B:(h + 1) * DV_B] = out.astype(h_ref.dtype)
        mo_ref[sq] = m_out


MLSTM_SAMPLE_SEQS = 4


def _mlstm_sample(zs, gain, c0, n0, m0):
    bs, H, sb = zs.shape[0], N_HEADS_B, MLSTM_SAMPLE_SEQS
    assert bs % sb == 0
    zs3 = zs.reshape(bs, 1, NZ)
    zspec = lambda w, off: pl.BlockSpec((sb, 1, w), lambda b: (b, 0, off // w))
    cspec = pl.BlockSpec((sb, H, DQK_B, DV_B), lambda b: (b, 0, 0, 0))
    nspec = pl.BlockSpec((sb, H, DQK_B), lambda b: (b, 0, 0))
    mspec = pl.BlockSpec((sb, 1, H), lambda b: (b, 0, 0))
    h, c, n, m = pl.pallas_call(
        _mlstm_sample_kernel,
        out_shape=(jax.ShapeDtypeStruct((bs, 1, B_V), BF16), jax.ShapeDtypeStruct(c0.shape, F32),
                   jax.ShapeDtypeStruct(n0.shape, F32), jax.ShapeDtypeStruct((bs, 1, H), F32)),
        grid=(bs // sb,),
        in_specs=[zspec(B_QK, Z_QB), zspec(B_QK, Z_KB), zspec(B_V, Z_VB), zspec(B_V, Z_OB), zspec(HEAD_DIM, Z_IB),
                  pl.BlockSpec((H, 1, DV_B), lambda b: (0, 0, 0)), cspec, nspec, mspec],
        out_specs=(pl.BlockSpec((sb, 1, B_V), lambda b: (b, 0, 0)), cspec, nspec, mspec),
        compiler_params=pltpu.CompilerParams(dimension_semantics=("parallel",), vmem_limit_bytes=VMEM_LIMIT),
        name="mlstm_sample",
    )(zs3, zs3, zs3, zs3, zs3, gain.reshape(H, 1, DV_B), c0, n0, m0.reshape(bs, 1, H))
    return h.reshape(bs, B_V), c, n, m.reshape(bs, H)


def _split_kv(z, B, T):
    z = z.reshape(B, T, NZ)
    kv = lambda o: z[..., o:o + A_KV].reshape(B, T, N_KV, HEAD_DIM)
    return kv(Z_KC), kv(Z_VC), kv(Z_KS), kv(Z_VS), kv(Z_KW), kv(Z_VW)


def kernel(x_prompt, x_sample, cache_cmp_k, cache_cmp_v, cache_sel_k, cache_sel_v, cache_win_k, cache_win_v,
           state_mlstm_c, state_mlstm_n, state_mlstm_m, page_table, norm_mix, w_in, b_in, cmp_pos, cmp_w1, cmp_w2,
           mlstm_norm, w_up_a, w_up_b, w_out, norm_ffn, w_router_grp, b_router_grp, w_router_exp, b_router_exp,
           w_gate, w_up, w_down, norm_final):
    assert w_in.shape[0] == 1, "single layer"
    Bp, Tp = x_prompt.shape[:2]
    Bs, Ts = x_sample.shape[:2]
    assert Ts == 1
    n_p, n_s = Bp * Tp, Bs * Ts
    past_len = page_table.shape[1] * PAGE_SIZE
    pos_p = jnp.arange(Tp, dtype=jnp.int32)
    pos_s = past_len + jnp.arange(Ts, dtype=jnp.int32)
    l = 0

    w_perm, b_perm = _permute_in_proj(w_in[l], b_in[l])
    cw = _compress_weights(cmp_pos[l], cmp_w1[l], cmp_w2[l])
    mw = _merge_weights(w_up_a[l], w_up_b[l], w_out[l], norm_ffn[l], w_router_grp[l], b_router_grp[l],
                        w_router_exp[l], b_router_exp[l])

    xp2d = x_prompt.reshape(n_p, D_MODEL)
    zp = _project(xp2d, jnp.tile(pos_p, Bp), norm_mix[l], w_perm, b_perm, 2048)
    kc, vc, ks, vs, kw, vw = _split_kv(zp, Bp, Tp)
    o_a = _nsa_prompt(zp, _compress_prompt(kc, vc, cw), Bp, Tp)
    h_b, c_p, n_p_state, m_p = _mlstm_prompt(zp, mlstm_norm[l], Bp, Tp)
    bufs = _merge(xp2d, o_a, h_b, zp, mw, n_p + n_s, 0, 256)
    wl = min(WINDOW, Tp)
    prompt_new = (kc, vc, ks, vs, kw[:, -wl:], vw[:, -wl:], c_p, n_p_state, m_p)

    xs2d = x_sample.reshape(n_s, D_MODEL)
    zs = _project(xs2d, jnp.repeat(pos_s, Bs), norm_mix[l], w_perm, b_perm, 128)
    kc, vc, ks, vs, kw, vw = _split_kv(zs, Bs, Ts)
    assert (past_len + Ts - CMP_LEN) // CMP_STRIDE + 1 == past_len // CMP_STRIDE - 1
    zk_pool, zv_pool = _compress_pool(cache_cmp_k[l], cache_cmp_v[l], cw)
    o_a_s = _nsa_sample(zs, page_table, zk_pool, zv_pool, cache_sel_k[l], cache_sel_v[l],
                        cache_win_k[l], cache_win_v[l], cw[2])
    h_s, c_s, n_s_state, m_s = _mlstm_sample(zs, mlstm_norm[l], state_mlstm_c[l], state_mlstm_n[l],
                                             state_mlstm_m[l])
    x2, xn, route = _merge(xs2d, o_a_s, h_s, zs, mw, n_p + n_s, n_p, 128, bufs=bufs)
    wk = jnp.concatenate([cache_win_k[l], kw], axis=1)
    wv = jnp.concatenate([cache_win_v[l], vw], axis=1)
    wl = min(WINDOW, past_len + Ts)
    sample_new = (kc, vc, ks, vs, wk[:, -wl:], wv[:, -wl:], c_s, n_s_state, m_s)

    y_p, y_s = _moe(x2, xn, route, w_gate[l], w_up[l], w_down[l], norm_final, n_p)
    return ((y_p.reshape(Bp, Tp, D_MODEL), y_s.reshape(Bs, Ts, D_MODEL))
            + tuple(a[None] for a in prompt_new) + tuple(a[None] for a in sample_new))
```

```python
import functools

import numpy as np
import jax
import jax.numpy as jnp
from jax import lax
from jax.experimental import pallas as pl
from jax.experimental.pallas import tpu as pltpu

D_MODEL = 2048
PAGE_SIZE = 128
N_HEADS_A = 8
N_KV = 2
GROUP = N_HEADS_A // N_KV
HEAD_DIM = 128
ROT_DIM = HEAD_DIM // 4
ROPE_THETA = 500000.0
CMP_LEN = 32
CMP_STRIDE = 16
SEL_LEN = 64
SEL_TOP = 16
FORCE_BONUS = 100.0
WINDOW = 512
Q_BLOCK = 128
N_HEADS_B = 4
DQK_B = 128
DV_B = 256
MLSTM_CHUNK = 64
GATE_CAP = 15.0
N_GROUPS = 4
EXPERTS_PER_GROUP = 8
N_EXPERTS = N_GROUPS * EXPERTS_PER_GROUP
TOP_K = 2
D_EXPERT = 1024
EPS = 1e-6
NEG = -1e30

A_Q = N_HEADS_A * HEAD_DIM
A_KV = N_KV * HEAD_DIM
B_QK = N_HEADS_B * DQK_B
B_V = N_HEADS_B * DV_B
IN_SPLITS = (A_Q, A_KV, A_KV, A_KV, A_KV, A_KV, A_KV, 3 * N_HEADS_A,
             B_QK, B_QK, B_V, N_HEADS_B, N_HEADS_B, B_V, D_MODEL, D_MODEL)

F32 = jnp.float32
BF16 = jnp.bfloat16
VMEM_LIMIT = 48 * 1024 * 1024
VMEM_LIMIT_MERGE = 56 * 1024 * 1024

_CUTS = np.concatenate([[0], np.cumsum(IN_SPLITS)]).tolist()
_SRC = dict(zip(('q_a', 'k_c', 'v_c', 'k_s', 'v_s', 'k_w', 'v_w', 'g_nsa', 'q_b', 'k_b', 'v_b', 'i_b', 'f_b',
                 'o_b', 'g_ma', 'g_mb'), zip(_CUTS[:-1], _CUTS[1:])))
_Z_ORDER = ('g_ma', 'g_mb', 'q_a', 'q_a', 'v_b', 'o_b', 'k_c', 'v_c', 'k_s', 'v_s', 'k_w', 'v_w', 'q_b', 'k_b')
PROJ_TN = 512
Z_GMA, Z_GMB, Z_QRAW, Z_QROT, Z_VB, Z_OB = 0, 2048, 4096, 5120, 6144, 7168
Z_KC, Z_VC, Z_KS, Z_VS, Z_KW, Z_VW = 8192, 8448, 8704, 8960, 9216, 9472
Z_QB, Z_KB, Z_SMALL = 9728, 10240, 10752
N_GATE = 3 * GROUP
Z_GNSA, Z_IB, Z_FB = Z_SMALL, Z_SMALL + 2 * HEAD_DIM, Z_SMALL + 2 * HEAD_DIM + N_HEADS_B
NZ = 11264
_ROPE_ALL_TILES = (Z_QROT // PROJ_TN, Z_QROT // PROJ_TN + 1)
_ROPE_HALF_TILES = (Z_KS // PROJ_TN, Z_KW // PROJ_TN)

ROUTE_W = 128
MOE_TILE = 768
MOE_SUB = 256
MOE_FC = 256


def _rms(x, g):
    return x * lax.rsqrt(jnp.mean(x * x, axis=-1, keepdims=True) + EPS) * g


def _proj_kernel(x_ref, g_ref, w_ref, b_ref, cos_ref, sa_ref, sb_ref, z_ref, xn_ref):
    j = pl.program_id(1)

    @pl.when(j == 0)
    def _():
        xn_ref[...] = _rms(x_ref[...], g_ref[...]).astype(BF16)

    acc = jnp.dot(xn_ref[...], w_ref[...], preferred_element_type=F32) + b_ref[...]

    def rope(blk):
        return (blk * cos_ref[...] + pltpu.roll(blk, HEAD_DIM - ROT_DIM // 2, 1) * sa_ref[...]
                + pltpu.roll(blk, ROT_DIM // 2, 1) * sb_ref[...])

    is_all = (j == _ROPE_ALL_TILES[0]) | (j == _ROPE_ALL_TILES[1])
    is_half = (j == _ROPE_HALF_TILES[0]) | (j == _ROPE_HALF_TILES[1])
    n_blk = PROJ_TN // HEAD_DIM

    @pl.when(is_all)
    def _():
        for c in range(n_blk):
            z_ref[:, c * HEAD_DIM:(c + 1) * HEAD_DIM] = rope(acc[:, c * HEAD_DIM:(c + 1) * HEAD_DIM])

    @pl.when(is_half)
    def _():
        for c in range(n_blk // 2):
            z_ref[:, c * HEAD_DIM:(c + 1) * HEAD_DIM] = rope(acc[:, c * HEAD_DIM:(c + 1) * HEAD_DIM])
        z_ref[:, PROJ_TN // 2:] = acc[:, PROJ_TN // 2:]

    @pl.when(jnp.logical_not(is_all | is_half))
    def _():
        z_ref[...] = acc


def _rope_tables(pos):
    half = ROT_DIM // 2
    inv = ROPE_THETA ** (-jnp.arange(half, dtype=F32) / half)
    ang = pos.astype(F32)[:, None] * inv[None, :]
    cos, sin = jnp.cos(ang), jnp.sin(ang)
    n = pos.shape[0]
    ones = jnp.ones((n, HEAD_DIM - ROT_DIM), F32)
    zeros = jnp.zeros((n, HEAD_DIM - half), F32)
    cos_t = jnp.concatenate([cos, cos, ones], axis=1)
    sa_t = jnp.concatenate([-sin, zeros], axis=1)
    sb_t = jnp.concatenate([jnp.zeros((n, half), F32), sin, jnp.zeros((n, HEAD_DIM - ROT_DIM), F32)], axis=1)
    return cos_t, sa_t, sb_t


def _permute_in_proj(w_in, b_in):
    g0 = _SRC['g_nsa'][0]
    spans = [_SRC[k] for k in _Z_ORDER]
    spans += [(g0, g0 + N_GATE), HEAD_DIM - N_GATE, (g0 + N_GATE, g0 + 2 * N_GATE), HEAD_DIM - N_GATE,
              _SRC['i_b'], _SRC['f_b'], NZ - Z_FB - N_HEADS_B]

    def permute(a, dtype):
        parts = [jnp.zeros((a.shape[0], s), dtype) if isinstance(s, int) else a[:, s[0]:s[1]].astype(dtype)
                 for s in spans]
        return jnp.concatenate(parts, axis=1)

    return permute(w_in, BF16), permute(b_in[None, :], F32)


def _project(x2d, pos_rows, norm_g, w_perm, b_perm, tm):
    m = x2d.shape[0]
    cos_t, sa_t, sb_t = _rope_tables(pos_rows)
    rows = lambda w: pl.BlockSpec((tm, w), lambda i, j: (i, 0), pipeline_mode=pl.Buffered(1))
    return pl.pallas_call(
        _proj_kernel,
        out_shape=jax.ShapeDtypeStruct((m, NZ), F32),
        grid=(m // tm, NZ // PROJ_TN),
        in_specs=[rows(D_MODEL),
                  pl.BlockSpec((1, D_MODEL), lambda i, j: (0, 0)),
                  pl.BlockSpec((D_MODEL, PROJ_TN), lambda i, j: (0, j)),
                  pl.BlockSpec((1, PROJ_TN), lambda i, j: (0, j)),
                  rows(HEAD_DIM), rows(HEAD_DIM), rows(HEAD_DIM)],
        out_specs=pl.BlockSpec((tm, PROJ_TN), lambda i, j: (i, j)),
        scratch_shapes=[pltpu.VMEM((tm, D_MODEL), BF16)],
        compiler_params=pltpu.CompilerParams(dimension_semantics=("parallel", "arbitrary"),
                                             vmem_limit_bytes=VMEM_LIMIT),
        name="proj",
    )(x2d, norm_g.reshape(1, D_MODEL), w_perm, b_perm, cos_t, sa_t, sb_t)


def _compress_body(x_ref, w1, pb, w2, r):
    bias = pb[0:1, :HEAD_DIM] + pb[1:2, HEAD_DIM:]
    outs = []
    for h in range(N_KV):
        xh = jnp.concatenate([x_ref[:, (N_KV * s + h) * HEAD_DIM:(N_KV * s + h + 1) * HEAD_DIM]
                              for s in range(CMP_STRIDE)], axis=1).astype(BF16)
        zz = jnp.dot(xh, w1, preferred_element_type=F32)
        pre = zz[:, :HEAD_DIM] + pltpu.roll(zz[:, HEAD_DIM:], r - 1, 0) + bias
        mid = pre * jax.nn.sigmoid(pre)
        outs.append(jnp.dot(mid.astype(BF16), w2, preferred_element_type=F32))
    return outs


def _compress_kernel(xk_ref, xv_ref, w1_ref, p_ref, w2_ref, o_ref):
    r = xk_ref.shape[0]
    for which, x_ref in enumerate((xk_ref, xv_ref)):
        w1 = w1_ref[which]
        pb = jnp.dot(p_ref[which], w1, preferred_element_type=F32)
        for h, o in enumerate(_compress_body(x_ref, w1, pb, w2_ref[which], r)):
            o_ref[which, h] = o


def _compress_weights(cmp_pos, cmp_w1, cmp_w2):
    assert CMP_LEN == 2 * CMP_STRIDE
    half = CMP_STRIDE * HEAD_DIM
    w1 = jnp.concatenate([cmp_w1[:, :half], cmp_w1[:, half:]], axis=2).astype(BF16)
    p = cmp_pos.reshape(2, 2, half)
    p = jnp.concatenate([p, jnp.zeros((2, 6, half), p.dtype)], axis=1).astype(BF16)
    return w1, p, cmp_w2.astype(BF16)


def _compress_prompt(kc, vc, cw):
    B, T = kc.shape[:2]
    r = T // CMP_STRIDE
    width = CMP_STRIDE * N_KV * HEAD_DIM
    w1, p, w2 = cw
    full = lambda a: pl.BlockSpec(a.shape, lambda b: (0,) * a.ndim)
    xspec = pl.BlockSpec((None, r, width), lambda b: (b, 0, 0))
    return pl.pallas_call(
        _compress_kernel,
        out_shape=jax.ShapeDtypeStruct((B, 2, N_KV, r, HEAD_DIM), F32),
        grid=(B,),
        in_specs=[xspec, xspec, full(w1), full(p), full(w2)],
        out_specs=pl.BlockSpec((None, 2, N_KV, r, HEAD_DIM), lambda b: (b, 0, 0, 0, 0)),
        compiler_params=pltpu.CompilerParams(dimension_semantics=("parallel",), vmem_limit_bytes=VMEM_LIMIT),
        name="compress_prompt",
    )(kc.reshape(B, r, width), vc.reshape(B, r, width), w1, p, w2)


def _group_scores(q, k, tq):
    s = lax.dot_general(q, k, (((1,), (1,)), ((), ())), preferred_element_type=F32) * HEAD_DIM ** -0.5
    return s.reshape(GROUP, tq, k.shape[0])


def _stack_heads(ref):
    return jnp.concatenate([ref[:, g * HEAD_DIM:(g + 1) * HEAD_DIM] for g in range(GROUP)], axis=0).astype(BF16)


def _importance(psum, n_sel):
    r = psum.shape[1]
    ci = lax.broadcasted_iota(jnp.int32, (r, n_sel), 0) * CMP_STRIDE
    cj = lax.broadcasted_iota(jnp.int32, (r, n_sel), 1) * SEL_LEN
    cover_t = ((ci < cj + SEL_LEN) & (ci + (CMP_LEN - 1) >= cj)).astype(BF16)
    p_hi = psum.astype(BF16)
    p_lo = (psum - p_hi.astype(F32)).astype(BF16)
    return jnp.dot(p_hi, cover_t, preferred_element_type=F32) + jnp.dot(p_lo, cover_t, preferred_element_type=F32)


def _block_scores(imp, blk, qpos):
    cur = jnp.right_shift(qpos, SEL_LEN.bit_length() - 1)
    forced = (blk == 0) | (blk == cur) | (blk == cur - 1)
    return jnp.where(blk * SEL_LEN <= qpos, imp + jnp.where(forced, FORCE_BONUS, 0.0), -1.0)


def _top_rank_mask(score, axis, n_cand):
    idx = lax.broadcasted_iota(jnp.int32, score.shape, axis)
    rank = jnp.zeros(score.shape, F32)
    for j in range(n_cand):
        cand = score[j:j + 1, :] if axis == 0 else score[:, j:j + 1]
        rank = rank + jnp.where((cand > score) | ((cand == score) & (idx > j)), 1.0, 0.0)
    return jnp.where(rank < float(SEL_TOP), 1.0, 0.0)


def _nsa_prompt_kernel(qraw_ref, qrot_ref, ks_ref, vs_ref, kw_ref, vw_ref, kc_ref, vc_ref, gate_ref, o_ref,
                       m_ref, l_ref, acc_ref, *, tq, tk, seq_len):
    q0 = pl.program_id(2) * tq
    r = kc_ref.shape[0]
    n_sel = seq_len // SEL_LEN
    sel_shift = SEL_LEN.bit_length() - 1
    qpos = q0 + lax.broadcasted_iota(jnp.int32, (tq, 1), 0)

    s = _group_scores(_stack_heads(qraw_ref), kc_ref[...].astype(BF16), tq)
    c_end = lax.broadcasted_iota(jnp.int32, (tq, r), 1) * CMP_STRIDE + (CMP_LEN - 1)
    cmask = c_end <= qpos
    s = s + jnp.where(cmask, 0.0, NEG)[None]
    p = jnp.exp(s - jnp.max(s, axis=-1, keepdims=True)) * jnp.where(cmask, 1.0, 0.0)[None]
    p = p / jnp.maximum(jnp.sum(p, axis=-1, keepdims=True), 1e-30)
    o_cmp = jnp.dot(p.reshape(GROUP * tq, r).astype(BF16), vc_ref[...].astype(BF16), preferred_element_type=F32)
    psum = p[0]
    for g in range(1, GROUP):
        psum = psum + p[g]
    imp_t = _importance(psum, HEAD_DIM).T[:n_sel]
    blk_t = lax.broadcasted_iota(jnp.int32, (n_sel, tq), 0)
    qpos_t = q0 + lax.broadcasted_iota(jnp.int32, (n_sel, tq), 1)
    sel_b = _top_rank_mask(_block_scores(imp_t, blk_t, qpos_t), 0, n_sel).astype(BF16)

    q_rot = _stack_heads(qrot_ref)
    m_ref[...] = jnp.full(m_ref.shape, NEG, F32)
    l_ref[...] = jnp.zeros(l_ref.shape, F32)
    acc_ref[...] = jnp.zeros(acc_ref.shape, F32)

    def sel_tile(kt, carry):
        k0 = pl.multiple_of(kt * tk, tk)
        k = ks_ref[pl.ds(k0, tk), :].astype(BF16)
        v = vs_ref[pl.ds(k0, tk), :].astype(BF16)
        s2 = _group_scores(q_rot, k, tq)
        ej = lax.broadcasted_iota(jnp.int32, (n_sel, tk), 0)
        et = k0 + lax.broadcasted_iota(jnp.int32, (n_sel, tk), 1)
        expand = jnp.where(jnp.right_shift(et, sel_shift) == ej, 1.0, 0.0).astype(BF16)
        selm = lax.dot_general(sel_b, expand, (((0,), (0,)), ((), ())), preferred_element_type=F32)
        kpos = k0 + lax.broadcasted_iota(jnp.int32, (tq, tk), 1)
        ok = (selm > 0.5) & (kpos <= qpos)
        s2 = s2 + jnp.where(ok, 0.0, NEG)[None]
        m_prev = m_ref[...]
        m_new = jnp.maximum(m_prev, jnp.max(s2, axis=-1, keepdims=True))
        alpha = jnp.exp(m_prev - m_new)
        p2 = jnp.exp(s2 - m_new)
        l_ref[...] = alpha * l_ref[...] + jnp.sum(p2, axis=-1, keepdims=True)
        pv = jnp.dot(p2.reshape(GROUP * tq, tk).astype(BF16), v, preferred_element_type=F32)
        acc_ref[...] = alpha.reshape(GROUP * tq, 1) * acc_ref[...] + pv
        m_ref[...] = m_new
        return carry

    lax.fori_loop(0, lax.div(q0 + tq + tk - 1, tk), sel_tile, 0)
    o_sel = acc_ref[...] / jnp.maximum(l_ref[...].reshape(GROUP * tq, 1), 1e-30)

    span = WINDOW + tq
    w0 = pl.multiple_of(jnp.maximum(q0 - WINDOW, 0), tq)
    s3 = _group_scores(q_rot, kw_ref[pl.ds(w0, span), :].astype(BF16), tq)
    kp = w0 + lax.broadcasted_iota(jnp.int32, (tq, span), 1)
    okw = (kp <= qpos) & (kp >= qpos - WINDOW)
    s3 = s3 + jnp.where(okw, 0.0, NEG)[None]
    p3 = jnp.exp(s3 - jnp.max(s3, axis=-1, keepdims=True))
    p3 = p3 / jnp.sum(p3, axis=-1, keepdims=True)
    o_win = jnp.dot(p3.reshape(GROUP * tq, span).astype(BF16), vw_ref[pl.ds(w0, span), :].astype(BF16),
                    preferred_element_type=F32)

    gts = jax.nn.sigmoid(gate_ref[...])
    for g in range(GROUP):
        rows = slice(g * tq, (g + 1) * tq)
        o_ref[:, g * HEAD_DIM:(g + 1) * HEAD_DIM] = (gts[:, 3 * g:3 * g + 1] * o_cmp[rows]
                                                     + gts[:, 3 * g + 1:3 * g + 2] * o_sel[rows]
                                                     + gts[:, 3 * g + 2:3 * g + 3] * o_win[rows]).astype(o_ref.dtype)


def _nsa_prompt(z, cmp, B, T, tq=256, tk=512):
    assert T % tk == 0 and tk % tq == 0 and tq % HEAD_DIM == 0 and T >= WINDOW + tq
    assert SEL_LEN & (SEL_LEN - 1) == 0 and tk % SEL_LEN == 0
    assert (T // SEL_LEN) % 8 == 0 and T // SEL_LEN <= HEAD_DIM
    nq = T // tq
    r = cmp.shape[3]
    gw = GROUP * HEAD_DIM
    qspec = lambda off: pl.BlockSpec((tq, gw), lambda b, h, q: (b * nq + q, off // gw + h))
    kvspec = lambda off: pl.BlockSpec((T, HEAD_DIM), lambda b, h, q: (b, off // HEAD_DIM + h))
    cspec = lambda which: pl.BlockSpec((None, None, None, r, HEAD_DIM), lambda b, h, q: (b, which, h, 0, 0))
    return pl.pallas_call(
        functools.partial(_nsa_prompt_kernel, tq=tq, tk=tk, seq_len=T),
        out_shape=jax.ShapeDtypeStruct((B * T, A_Q), BF16),
        grid=(B, N_KV, nq),
        in_specs=[qspec(Z_QRAW), qspec(Z_QROT), kvspec(Z_KS), kvspec(Z_VS), kvspec(Z_KW), kvspec(Z_VW),
                  cspec(0), cspec(1),
                  pl.BlockSpec((tq, HEAD_DIM), lambda b, h, q: (b * nq + q, Z_GNSA // HEAD_DIM + h))],
        out_specs=pl.BlockSpec((tq, gw), lambda b, h, q: (b * nq + q, h)),
        scratch_shapes=[pltpu.VMEM((GROUP, tq, 1), F32), pltpu.VMEM((GROUP, tq, 1), F32),
                        pltpu.VMEM((GROUP * tq, HEAD_DIM), F32)],
        compiler_params=pltpu.CompilerParams(dimension_semantics=("parallel", "parallel", "arbitrary"),
                                             vmem_limit_bytes=VMEM_LIMIT),
        name="nsa_prompt",
    )(z, z, z, z, z, z, cmp, cmp, z)


def _gate_logs(i_pre, f_pre):
    log_i = GATE_CAP * jnp.tanh(i_pre / GATE_CAP)
    fc = GATE_CAP * jnp.tanh(f_pre / GATE_CAP)
    log_f = jnp.minimum(fc, 0.0) - jnp.log1p(jnp.exp(-jnp.abs(fc)))
    return log_i, log_f


def _mlstm_prompt_kernel(q_ref, k_ref, v_ref, op_ref, grow_ref, gcol_ref, gain_ref,
                         h_ref, c_ref, n_ref, m_ref):
    @pl.when(pl.program_id(1) == 0)
    def _():
        c_ref[...] = jnp.zeros(c_ref.shape, F32)
        n_ref[...] = jnp.zeros(n_ref.shape, F32)
        m_ref[...] = jnp.zeros(m_ref.shape, F32)

    for h in range(N_HEADS_B):
        qk_cols = slice(h * DQK_B, (h + 1) * DQK_B)
        v_cols = slice(h * DV_B, (h + 1) * DV_B)
        _mlstm_chunk(q_ref[:, qk_cols], k_ref[:, qk_cols], v_ref[:, v_cols], op_ref[:, v_cols],
                     grow_ref[h:h + 1, :], grow_ref[N_HEADS_B + h:N_HEADS_B + h + 1, :],
                     gcol_ref[:, h:h + 1], gcol_ref[:, N_HEADS_B + h:N_HEADS_B + h + 1], gain_ref[h],
                     h_ref.at[:, v_cols], c_ref.at[h], n_ref.at[h], m_ref.at[h])


def _mlstm_chunk(q, k, v, o_pre, i_row, f_row, i_col, f_col, gain, h_ref, c_ref, n_ref, m_ref):
    L = q.shape[0]
    qb = q.astype(BF16)
    k = k * DQK_B ** -0.5
    vb = v.astype(BF16)
    i_r, f_r = _gate_logs(i_row, f_row)
    i_c, f_c = _gate_logs(i_col, f_col)
    li = lax.broadcasted_iota(jnp.int32, (L, L), 0)
    si = lax.broadcasted_iota(jnp.int32, (L, L), 1)
    tri = si <= li
    bcum_c = jnp.sum(jnp.where(tri, f_r, 0.0), axis=1, keepdims=True)
    bcum_r = jnp.sum(jnp.where(li <= si, f_c, 0.0), axis=0, keepdims=True)
    btot = jnp.sum(f_r, axis=1, keepdims=True)
    c_old, n_old, m_old = c_ref[...], n_ref[...], m_ref[0:1, 0:1]

    dlog = jnp.where(tri, bcum_c - bcum_r + i_r, NEG)
    inter = bcum_c + m_old
    m_t = jnp.maximum(inter, jnp.max(dlog, axis=1, keepdims=True))
    qk = lax.dot_general(qb, k.astype(BF16), (((1,), (1,)), ((), ())), preferred_element_type=F32)
    w_intra = jnp.exp(dlog - m_t) * qk
    w_inter = jnp.exp(inter - m_t)
    num = (w_inter * jnp.dot(qb, c_old.astype(BF16), preferred_element_type=F32)
           + jnp.dot(w_intra.astype(BF16), vb, preferred_element_type=F32))
    den = w_inter * jnp.sum(q * n_old, axis=1, keepdims=True) + jnp.sum(w_intra, axis=1, keepdims=True)
    h = num / jnp.maximum(jnp.abs(den), jnp.exp(-m_t))
    h_ref[...] = (_rms(h, gain) * jax.nn.sigmoid(o_pre)).astype(h_ref.dtype)

    wlog_c = btot - bcum_c + i_c
    m_new = jnp.maximum(btot + m_old, jnp.max(btot - bcum_r + i_r, axis=1, keepdims=True))
    decay = jnp.exp(btot + m_old - m_new)
    kw = k * jnp.exp(wlog_c - m_new)
    c_ref[...] = decay * c_old + lax.dot_general(kw.astype(BF16), vb, (((0,), (0,)), ((), ())),
                                                 preferred_element_type=F32)
    n_ref[...] = decay * n_old + jnp.sum(kw, axis=0, keepdims=True)
    m_ref[...] = jnp.broadcast_to(m_new, m_ref.shape)


def _mlstm_prompt(z, gain, B, T):
    L = MLSTM_CHUNK
    assert T % L == 0
    nc, H = T // L, N_HEADS_B
    gcol = z[:, Z_IB:Z_IB + 2 * H].reshape(B * nc, L, 2 * H)
    grow = gcol.transpose(0, 2, 1)
    rows = lambda w, off: pl.BlockSpec((L, w), lambda b, c: (b * nc + c, off // w))
    gidx = lambda b, c: (b * nc + c, 0, 0)
    state = lambda *s: pl.BlockSpec((None, H) + s, lambda b, c: (b, 0, 0, 0))
    h, c, n, m = pl.pallas_call(
        _mlstm_prompt_kernel,
        out_shape=(jax.ShapeDtypeStruct((B * T, B_V), BF16), jax.ShapeDtypeStruct((B, H, DQK_B, DV_B), F32),
                   jax.ShapeDtypeStruct((B, H, 1, DQK_B), F32), jax.ShapeDtypeStruct((B, H, 1, DQK_B), F32)),
        grid=(B, nc),
        in_specs=[rows(B_QK, Z_QB), rows(B_QK, Z_KB), rows(B_V, Z_VB), rows(B_V, Z_OB),
                  pl.BlockSpec((None, 2 * H, L), gidx), pl.BlockSpec((None, L, 2 * H), gidx),
                  pl.BlockSpec((H, 1, DV_B), lambda b, c: (0, 0, 0))],
        out_specs=(rows(B_V, 0), state(DQK_B, DV_B), state(1, DQK_B), state(1, DQK_B)),
        compiler_params=pltpu.CompilerParams(dimension_semantics=("parallel", "arbitrary"),
                                             vmem_limit_bytes=VMEM_LIMIT),
        name="mlstm_prompt",
    )(z, z, z, z, grow, gcol, gain.reshape(H, 1, DV_B))
    return h, c, n[:, :, 0], m[:, :, 0, 0]


def _merge_kernel(x_ref, oa_ref, hb_ref, gma_ref, gmb_ref, wa_ref, wb_ref, wo_ref, nf_ref, wrh_ref, wrl_ref, br_ref,
                  *refs):
    x2_ref, xn_ref, route_ref = refs[-3:]
    ua = jnp.dot(oa_ref[...], wa_ref[...], preferred_element_type=F32)
    ub = jnp.dot(hb_ref[...], wb_ref[...], preferred_element_type=F32)
    merged = jax.nn.sigmoid(gma_ref[...]) * ua + jax.nn.sigmoid(gmb_ref[...]) * ub
    x2 = x_ref[...] + jnp.dot(merged.astype(BF16), wo_ref[...], preferred_element_type=F32)
    x2_ref[...] = x2
    xn = _rms(x2, nf_ref[...])
    xn_ref[...] = xn
    hi = xn.astype(BF16)
    lo = (xn - hi.astype(F32)).astype(BF16)
    lg = (jnp.dot(hi, wrh_ref[...], preferred_element_type=F32) + jnp.dot(lo, wrh_ref[...], preferred_element_type=F32)
          + jnp.dot(hi, wrl_ref[...], preferred_element_type=F32)) + br_ref[...]
    lane = lax.broadcasted_iota(jnp.int32, lg.shape, 1)
    lanef = lane.astype(F32)
    ninf = -jnp.inf
    gmask = (lane >= N_EXPERTS) & (lane < N_EXPERTS + N_GROUPS)
    gl = jnp.where(gmask, lg, ninf)
    gmax = jnp.max(gl, axis=1, keepdims=True)
    gidx = jnp.min(jnp.where(gl == gmax, lanef, 1e9), axis=1, keepdims=True) - N_EXPERTS
    gprob = 1.0 / jnp.sum(jnp.where(gmask, jnp.exp(lg - gmax), 0.0), axis=1, keepdims=True)
    e_lo = gidx * EXPERTS_PER_GROUP
    el = jnp.where((lanef >= e_lo) & (lanef < e_lo + EXPERTS_PER_GROUP), lg, ninf)
    v1 = jnp.max(el, axis=1, keepdims=True)
    i1 = jnp.min(jnp.where(el == v1, lanef, 1e9), axis=1, keepdims=True)
    el2 = jnp.where(lanef == i1, ninf, el)
    v2 = jnp.max(el2, axis=1, keepdims=True)
    i2 = jnp.min(jnp.where(el2 == v2, lanef, 1e9), axis=1, keepdims=True)
    e2 = jnp.exp(v2 - v1)
    g1 = gprob / (1.0 + e2)
    g2 = gprob * e2 / (1.0 + e2)
    route_ref[...] = jnp.where(lane == 0, i1, jnp.where(lane == 1, i2, jnp.where(lane == 2, g1,
                               jnp.where(lane == 3, g2, 0.0))))


def _merge_weights(w_up_a, w_up_b, w_out, norm_ffn, w_rg, b_rg, w_re, b_re):
    assert TOP_K == 2
    pad = ROUTE_W - N_EXPERTS - N_GROUPS
    w = jnp.concatenate([w_re, w_rg, jnp.zeros((D_MODEL, pad), F32)], axis=1)
    b = jnp.concatenate([b_re, b_rg, jnp.zeros((pad,), F32)]).reshape(1, ROUTE_W)
    hi = w.astype(BF16)
    return (w_up_a.astype(BF16), w_up_b.astype(BF16), w_out.astype(BF16), norm_ffn.reshape(1, D_MODEL).astype(F32),
            hi, (w - hi.astype(F32)).astype(BF16), b)


def _merge(x2d, o_a, h_b, z, mw, n_total, row0, tm, bufs=None):
    m = x2d.shape[0]
    i0 = row0 // tm
    row = lambda w, c=0: pl.BlockSpec((tm, w), lambda i: (i, c))
    const = lambda a: pl.BlockSpec(a.shape, lambda i: (0,) * a.ndim, pipeline_mode=pl.Buffered(1))
    outw = (D_MODEL, D_MODEL, ROUTE_W)
    in_specs = [row(D_MODEL), row(A_Q), row(B_V), row(D_MODEL, Z_GMA // D_MODEL), row(D_MODEL, Z_GMB // D_MODEL)]
    in_specs += [const(a) for a in mw]
    args = [x2d, o_a, h_b, z, z] + list(mw)
    aliases = {}
    if bufs is not None:
        in_specs += [pl.BlockSpec(memory_space=pl.ANY)] * 3
        aliases = {len(args) + k: k for k in range(3)}
        args += list(bufs)
    return pl.pallas_call(
        _merge_kernel,
        out_shape=tuple(jax.ShapeDtypeStruct((n_total, w), F32) for w in outw),
        grid=(m // tm,),
        in_specs=in_specs,
        out_specs=tuple(pl.BlockSpec((tm, w), lambda i: (i0 + i, 0)) for w in outw),
        input_output_aliases=aliases,
        compiler_params=pltpu.CompilerParams(dimension_semantics=("parallel",), vmem_limit_bytes=VMEM_LIMIT_MERGE),
        name="merge_route",
    )(*args)


def _moe_plan(route, n_tiles):
    n = route.shape[0]
    a = n * TOP_K
    flat_e = route[:, 0:TOP_K].astype(jnp.int32).reshape(a)
    onehot = (flat_e[:, None] == jnp.arange(N_EXPERTS, dtype=jnp.int32)[None, :]).astype(jnp.int32)
    rank = jnp.take_along_axis(jnp.cumsum(onehot, axis=0) - onehot, flat_e[:, None], axis=1)[:, 0]
    counts = jnp.sum(onehot, axis=0)
    ntile = (counts + MOE_TILE - 1) // MOE_TILE
    tile_end = jnp.cumsum(ntile)
    dest = ((tile_end - ntile)[flat_e] * MOE_TILE + rank).astype(jnp.int32)
    n_used = tile_end[-1].astype(jnp.int32)
    t = jnp.minimum(jnp.arange(n_tiles, dtype=jnp.int32), n_used - 1)
    tile_expert = jnp.minimum(jnp.searchsorted(tile_end, t, side='right'), N_EXPERTS - 1).astype(jnp.int32)
    first = (tile_end - ntile)[tile_expert]
    tile_rows = jnp.clip(counts[tile_expert] - (t - first) * MOE_TILE, 0, MOE_TILE)
    tile_rows = jnp.where(jnp.arange(n_tiles) < n_used, tile_rows, 0).astype(jnp.int32)
    return dest, tile_expert, tile_rows, n_used.reshape(1)


DMA_UNROLL = 8


def _dispatch_kernel(dest_ref, x_ref, o_hbm, sem):
    i = pl.program_id(0)
    tm = x_ref.shape[0]

    def copy(r, k):
        d = dest_ref[(i * tm + r) * TOP_K + k]
        return pltpu.make_async_copy(x_ref.at[pl.ds(r, 1)], o_hbm.at[pl.ds(d, 1)], sem)

    def start(r, c):
        for k in range(TOP_K):
            copy(r, k).start()
        return c

    def wait(r, c):
        for k in range(TOP_K):
            copy(r, k).wait()
        return c

    lax.fori_loop(0, tm, start, 0, unroll=DMA_UNROLL)
    lax.fori_loop(0, tm, wait, 0, unroll=DMA_UNROLL)


def _moe_dispatch(xn, dest, n_tiles, tm=128):
    n, d = xn.shape
    return pl.pallas_call(
        _dispatch_kernel,
        out_shape=jax.ShapeDtypeStruct((n_tiles * MOE_TILE, d), xn.dtype),
        grid_spec=pltpu.PrefetchScalarGridSpec(
            num_scalar_prefetch=1, grid=(n // tm,),
            in_specs=[pl.BlockSpec((tm, d), lambda i, ds: (i, 0))],
            out_specs=pl.BlockSpec(memory_space=pl.ANY),
            scratch_shapes=[pltpu.SemaphoreType.DMA(())]),
        compiler_params=pltpu.CompilerParams(dimension_semantics=("arbitrary",), vmem_limit_bytes=VMEM_LIMIT),
        name="moe_dispatch",
    )(dest, xn)


def _expert_kernel(te_ref, cnt_ref, nu_ref, x_ref, wg_ref, wu_ref, wd_ref, o_ref, xb_ref):
    t, j = pl.program_id(0), pl.program_id(1)
    cnt = cnt_ref[t]

    @pl.when(cnt > 0)
    def _():
        wg, wu, wd = wg_ref[...].astype(BF16), wu_ref[...].astype(BF16), wd_ref[...].astype(BF16)
        for s in range(MOE_TILE // MOE_SUB):
            rows = slice(s * MOE_SUB, (s + 1) * MOE_SUB)

            @pl.when(s * MOE_SUB < cnt)
            def _():
                @pl.when(j == 0)
                def _():
                    xb_ref[rows] = x_ref[rows].astype(BF16)

                xb = xb_ref[rows]
                g = jnp.dot(xb, wg, preferred_element_type=F32)
                u = jnp.dot(xb, wu, preferred_element_type=F32)
                h = (g * jax.nn.sigmoid(g) * u).astype(BF16)
                y = jnp.dot(h, wd, preferred_element_type=F32)

                @pl.when(j == 0)
                def _():
                    o_ref[rows] = y

                @pl.when(j > 0)
                def _():
                    o_ref[rows] += y


def _moe_experts(xs, tile_expert, tile_rows, n_used, w_gate, w_up, w_down):
    n_tiles = xs.shape[0] // MOE_TILE
    nj = D_EXPERT // MOE_FC
    tix = lambda t, j, te, cnt, nu: (jnp.minimum(t, nu[0] - 1), 0)
    jj = lambda t, j, nu: jnp.where(t < nu[0], j, nj - 1)
    return pl.pallas_call(
        _expert_kernel,
        out_shape=jax.ShapeDtypeStruct(xs.shape, F32),
        grid_spec=pltpu.PrefetchScalarGridSpec(
            num_scalar_prefetch=3, grid=(n_tiles, nj),
            in_specs=[pl.BlockSpec((MOE_TILE, D_MODEL), tix),
                      pl.BlockSpec((None, D_MODEL, MOE_FC), lambda t, j, te, cnt, nu: (te[t], 0, jj(t, j, nu))),
                      pl.BlockSpec((None, D_MODEL, MOE_FC), lambda t, j, te, cnt, nu: (te[t], 0, jj(t, j, nu))),
                      pl.BlockSpec((None, MOE_FC, D_MODEL), lambda t, j, te, cnt, nu: (te[t], jj(t, j, nu), 0))],
            out_specs=pl.BlockSpec((MOE_TILE, D_MODEL), tix),
            scratch_shapes=[pltpu.VMEM((MOE_TILE, D_MODEL), BF16)]),
        compiler_params=pltpu.CompilerParams(dimension_semantics=("arbitrary", "arbitrary"),
                                             vmem_limit_bytes=VMEM_LIMIT_MERGE),
        name="moe_experts",
    )(tile_expert, tile_rows, n_used, xs, w_gate, w_up, w_down)


def _combine_kernel(dest_ref, x2_ref, route_ref, g_ref, y_hbm, op_ref, os_ref, buf_ref, sem, *, n_prompt_tiles):
    i = pl.program_id(0)
    tm = x2_ref.shape[0]

    def copy(r, k):
        d = dest_ref[(i * tm + r) * TOP_K + k]
        return pltpu.make_async_copy(y_hbm.at[pl.ds(d, 1)], buf_ref.at[k, pl.ds(r, 1)], sem)

    def start(r, c):
        for k in range(TOP_K):
            copy(r, k).start()
        return c

    def wait(r, c):
        for k in range(TOP_K):
            copy(r, k).wait()
        return c

    lax.fori_loop(0, tm, start, 0, unroll=DMA_UNROLL)
    lax.fori_loop(0, tm, wait, 0, unroll=DMA_UNROLL)
    ffn = buf_ref[0] * route_ref[:, TOP_K:TOP_K + 1]
    for k in range(1, TOP_K):
        ffn = ffn + buf_ref[k] * route_ref[:, TOP_K + k:TOP_K + k + 1]
    y = _rms(x2_ref[...] + ffn, g_ref[...])

    @pl.when(i < n_prompt_tiles)
    def _():
        op_ref[...] = y

    @pl.when(i >= n_prompt_tiles)
    def _():
        os_ref[...] = y


def _moe_combine(x2, route, dest, ys, norm_final, n_prompt, tm=128):
    n, d = x2.shape
    npt = n_prompt // tm
    nst = (n - n_prompt) // tm
    return pl.pallas_call(
        functools.partial(_combine_kernel, n_prompt_tiles=npt),
        out_shape=(jax.ShapeDtypeStruct((n_prompt, d), F32), jax.ShapeDtypeStruct((n - n_prompt, d), F32)),
        grid_spec=pltpu.PrefetchScalarGridSpec(
            num_scalar_prefetch=1, grid=(n // tm,),
            in_specs=[pl.BlockSpec((tm, d), lambda i, ds: (i, 0)), pl.BlockSpec((tm, ROUTE_W), lambda i, ds: (i, 0)),
                      pl.BlockSpec((1, d), lambda i, ds: (0, 0)), pl.BlockSpec(memory_space=pl.ANY)],
            out_specs=(pl.BlockSpec((tm, d), lambda i, ds: (jnp.minimum(i, npt - 1), 0)),
                       pl.BlockSpec((tm, d), lambda i, ds: (jnp.clip(i - npt, 0, nst - 1), 0))),
            scratch_shapes=[pltpu.VMEM((TOP_K, tm, d), F32), pltpu.SemaphoreType.DMA(())]),
        compiler_params=pltpu.CompilerParams(dimension_semantics=("arbitrary",), vmem_limit_bytes=VMEM_LIMIT),
        name="moe_combine",
    )(dest, x2, route, norm_final.reshape(1, d), ys)


def _moe(x2, xn, route, w_gate, w_up, w_down, norm_final, n_prompt):
    n = x2.shape[0]
    n_tiles = (n * TOP_K + N_EXPERTS * (MOE_TILE - 1) + MOE_TILE - 1) // MOE_TILE
    dest, tile_expert, tile_rows, n_used = _moe_plan(route, n_tiles)
    xs = _moe_dispatch(xn, dest, n_tiles)
    ys = _moe_experts(xs, tile_expert, tile_rows, n_used, w_gate, w_up, w_down)
    return _moe_combine(x2, route, dest, ys, norm_final, n_prompt)


POOL_ROWS = 256
ROWS_PER_PAGE = PAGE_SIZE // CMP_STRIDE
ZW = 2 * N_KV * HEAD_DIM


GROUP_ROWS = N_KV * CMP_STRIDE
GROUP_PITCH = GROUP_ROWS + 8


def _compress_pool_kernel(w1_ref, p_ref, xk_hbm, xv_hbm, zk_ref, zv_ref, xbuf, sem):
    i = pl.program_id(0)
    n = pl.num_programs(0)
    slot = lax.rem(i, 2)

    def copies(step, sl):
        return [pltpu.make_async_copy(src.at[pl.ds(step * POOL_ROWS, POOL_ROWS)],
                                      xbuf.at[sl, which, :, pl.ds(0, GROUP_ROWS), :], sem.at[sl, which])
                for which, src in enumerate((xk_hbm, xv_hbm))]

    @pl.when(i == 0)
    def _():
        for cp in copies(0, 0):
            cp.start()

    @pl.when(i + 1 < n)
    def _():
        for cp in copies(i + 1, 1 - slot):
            cp.start()

    for cp in copies(i, slot):
        cp.wait()

    for which, z_ref in enumerate((zk_ref, zv_ref)):
        w1 = w1_ref[which]
        pb = jnp.dot(p_ref[which], w1, preferred_element_type=F32)
        bias = pb[0:1, :HEAD_DIM] + pb[1:2, HEAD_DIM:]
        rows2d = xbuf.at[slot, which].reshape(POOL_ROWS * GROUP_PITCH, HEAD_DIM)
        for h in range(N_KV):
            xh = jnp.concatenate([rows2d[pl.ds(N_KV * s + h, POOL_ROWS, stride=GROUP_PITCH), :]
                                  for s in range(CMP_STRIDE)], axis=1).astype(BF16)
            zz = jnp.dot(xh, w1, preferred_element_type=F32)
            z_ref[:, 2 * h * HEAD_DIM:(2 * h + 1) * HEAD_DIM] = zz[:, :HEAD_DIM] + bias
            z_ref[:, (2 * h + 1) * HEAD_DIM:(2 * h + 2) * HEAD_DIM] = zz[:, HEAD_DIM:]


def _compress_pool(pool_k, pool_v, cw):
    n_phys = pool_k.shape[0]
    rows = n_phys * ROWS_PER_PAGE
    assert rows % POOL_ROWS == 0
    w1, p, _ = cw
    full = lambda a: pl.BlockSpec(a.shape, lambda i: (0,) * a.ndim)
    groups = lambda a: a.reshape(rows, GROUP_ROWS, HEAD_DIM)
    zk, zv = pl.pallas_call(
        _compress_pool_kernel,
        out_shape=(jax.ShapeDtypeStruct((rows, ZW), F32),) * 2,
        grid=(rows // POOL_ROWS,),
        in_specs=[full(w1), full(p), pl.BlockSpec(memory_space=pl.ANY), pl.BlockSpec(memory_space=pl.ANY)],
        out_specs=(pl.BlockSpec((POOL_ROWS, ZW), lambda i: (i, 0)),) * 2,
        scratch_shapes=[pltpu.VMEM((2, 2, POOL_ROWS, GROUP_PITCH, HEAD_DIM), F32), pltpu.SemaphoreType.DMA((2, 2))],
        compiler_params=pltpu.CompilerParams(dimension_semantics=("arbitrary",), vmem_limit_bytes=VMEM_LIMIT),
        name="compress_pool",
    )(w1, p, groups(pool_k), groups(pool_v))
    return zk.reshape(n_phys, ROWS_PER_PAGE, ZW), zv.reshape(n_phys, ROWS_PER_PAGE, ZW)


def _nsa_sample_kernel(pt_ref, qraw_ref, qrot_ref, knew_ref, wnew_ref, gate_ref, w2_ref, wk_ref, wv_ref,
                       zk_hbm, zv_hbm, ks_hbm, vs_hbm, o_ref, zbuf, sbuf, sem, *, n_pages, past_len):
    b = pl.program_id(0)
    nb = pl.num_programs(0)
    slot = lax.rem(b, 2)
    page_rows = N_KV * PAGE_SIZE

    def copies(seq, sl):
        out = []
        for pg in range(n_pages):
            page = pt_ref[seq, pg]
            for which, (zsrc, ssrc) in enumerate(((zk_hbm, ks_hbm), (zv_hbm, vs_hbm))):
                out.append(pltpu.make_async_copy(
                    zsrc.at[page], zbuf.at[sl, which, pl.ds(pg * ROWS_PER_PAGE, ROWS_PER_PAGE)], sem.at[sl]))
                out.append(pltpu.make_async_copy(
                    ssrc.at[page], sbuf.at[sl, which, pl.ds(pg * page_rows, page_rows)], sem.at[sl]))
        return out

    @pl.when(b == 0)
    def _():
        for cp in copies(0, 0):
            cp.start()

    @pl.when(b + 1 < nb)
    def _():
        for cp in copies(b + 1, 1 - slot):
            cp.start()

    for cp in copies(b, slot):
        cp.wait()

    scale = HEAD_DIM ** -0.5
    nq = N_KV * GROUP
    r = n_pages * ROWS_PER_PAGE
    qpos = jnp.full((nq, 1), past_len, jnp.int32)
    n_sel_pad = HEAD_DIM
    sel_shift = SEL_LEN.bit_length() - 1
    row_head = lax.broadcasted_iota(jnp.int32, (nq, 1), 0) // GROUP
    per_head = lambda a0, a1: jnp.where(row_head == 0, a0, a1)

    def all_heads(ref):
        return jnp.concatenate([ref[:, j * HEAD_DIM:(j + 1) * HEAD_DIM] for j in range(nq)], axis=0).astype(BF16)

    def scores(q, k):
        return lax.dot_general(q, k, (((1,), (1,)), ((), ())), preferred_element_type=F32) * scale

    cmp = []
    for which in range(2):
        heads = []
        for h in range(N_KV):
            zh = zbuf[slot, which, :, 2 * h * HEAD_DIM:(2 * h + 2) * HEAD_DIM]
            pre = zh[:, :HEAD_DIM] + pltpu.roll(zh[:, HEAD_DIM:], r - 1, 0)
            mid = pre * jax.nn.sigmoid(pre)
            heads.append(jnp.dot(mid.astype(BF16), w2_ref[which], preferred_element_type=F32).astype(BF16))
        cmp.append(heads)
    q_raw = all_heads(qraw_ref)
    c_end = lax.broadcasted_iota(jnp.int32, (nq, r), 1) * CMP_STRIDE + (CMP_LEN - 1)
    cmask = c_end <= qpos
    s = per_head(scores(q_raw, cmp[0][0]), scores(q_raw, cmp[0][1])) + jnp.where(cmask, 0.0, NEG)
    p = jnp.exp(s - jnp.max(s, axis=-1, keepdims=True)) * jnp.where(cmask, 1.0, 0.0)
    p = p / jnp.maximum(jnp.sum(p, axis=-1, keepdims=True), 1e-30)
    pb = p.astype(BF16)
    o_cmp = per_head(jnp.dot(pb, cmp[1][0], preferred_element_type=F32),
                     jnp.dot(pb, cmp[1][1], preferred_element_type=F32))
    psum = per_head(jnp.sum(jnp.where(row_head == 0, p, 0.0), axis=0, keepdims=True),
                    jnp.sum(jnp.where(row_head == 1, p, 0.0), axis=0, keepdims=True))
    blk = lax.broadcasted_iota(jnp.int32, (nq, n_sel_pad), 1)
    sel = _top_rank_mask(_block_scores(_importance(psum, n_sel_pad), blk, qpos), 1,
                         past_len // SEL_LEN + 1)

    def attend(q, k, v, k_new, v_new, bias, new_bias):
        s = scores(q, k) + bias
        s_new = jnp.sum(q.astype(F32) * k_new.astype(BF16).astype(F32), axis=1, keepdims=True) * scale + new_bias
        m = jnp.maximum(jnp.max(s, axis=-1, keepdims=True), s_new)
        p = jnp.exp(s - m)
        p_new = jnp.exp(s_new - m)
        l = jnp.sum(p, axis=-1, keepdims=True) + p_new
        o = (jnp.dot(p.astype(BF16), v, preferred_element_type=F32)
             + p_new.astype(BF16).astype(F32) * v_new.astype(BF16).astype(F32))
        return o / jnp.maximum(l, 1e-30)

    def own_head(n_rows):
        col = lax.broadcasted_iota(jnp.int32, (nq, n_rows), 1)
        return jnp.bitwise_and(col, N_KV - 1) == row_head

    q_rot = all_heads(qrot_ref)
    n_rows = N_KV * past_len
    ej = lax.broadcasted_iota(jnp.int32, (n_sel_pad, n_rows), 0)
    et = lax.broadcasted_iota(jnp.int32, (n_sel_pad, n_rows), 1)
    expand = jnp.where(jnp.right_shift(et, sel_shift + 1) == ej, 1.0, 0.0).astype(BF16)
    selm = jnp.dot(sel.astype(BF16), expand, preferred_element_type=F32)
    sel_bias = jnp.where((selm > 0.5) & own_head(n_rows), 0.0, NEG)
    new_blk = past_len // SEL_LEN
    new_bias = jnp.where(sel[:, new_blk:new_blk + 1] > 0.5, 0.0, NEG)
    new_kv = lambda ref, c: per_head(ref[:, c * A_KV:c * A_KV + HEAD_DIM], ref[:, c * A_KV + HEAD_DIM:(c + 1) * A_KV])
    o_sel = attend(q_rot, sbuf[slot, 0].astype(BF16), sbuf[slot, 1].astype(BF16),
                   new_kv(knew_ref, 0), new_kv(knew_ref, 1), sel_bias, new_bias)
    win_bias = jnp.where(own_head(wk_ref.shape[0]), 0.0, NEG)
    o_win = attend(q_rot, wk_ref[...].astype(BF16), wv_ref[...].astype(BF16),
                   new_kv(wnew_ref, 0), new_kv(wnew_ref, 1), win_bias, 0.0)

    gts = jax.nn.sigmoid(gate_ref[...])
    for h in range(N_KV):
        for g in range(GROUP):
            j, c0 = h * GROUP + g, h * HEAD_DIM + 3 * g
            o = (gts[:, c0:c0 + 1] * o_cmp[j:j + 1] + gts[:, c0 + 1:c0 + 2] * o_sel[j:j + 1]
                 + gts[:, c0 + 2:c0 + 3] * o_win[j:j + 1])
            o_ref[:, j * HEAD_DIM:(j + 1) * HEAD_DIM] = o.astype(o_ref.dtype)


def _nsa_sample(zs, page_table, zk_pool, zv_pool, sel_k, sel_v, win_k, win_v, w2):
    bs, n_pages = page_table.shape
    past_len = n_pages * PAGE_SIZE
    wb = win_k.shape[1]
    assert wb <= WINDOW and past_len % SEL_LEN == 0 and past_len // SEL_LEN < HEAD_DIM
    assert N_KV == 2
    zs3 = zs.reshape(bs, 1, NZ)
    zspec = lambda w, off: pl.BlockSpec((None, 1, w), lambda b, pt: (b, 0, off // w))
    anyspec = pl.BlockSpec(memory_space=pl.ANY)
    wspec = pl.BlockSpec((None, N_KV * wb, HEAD_DIM), lambda b, pt: (b, 0, 0))
    rows2d = lambda a: a.reshape(a.shape[0], a.shape[1] * N_KV, HEAD_DIM)
    out = pl.pallas_call(
        functools.partial(_nsa_sample_kernel, n_pages=n_pages, past_len=past_len),
        out_shape=jax.ShapeDtypeStruct((bs, 1, A_Q), BF16),
        grid_spec=pltpu.PrefetchScalarGridSpec(
            num_scalar_prefetch=1, grid=(bs,),
            in_specs=[zspec(A_Q, Z_QRAW), zspec(A_Q, Z_QROT), zspec(2 * A_KV, Z_KS), zspec(2 * A_KV, Z_KW),
                      zspec(2 * HEAD_DIM, Z_GNSA), pl.BlockSpec(w2.shape, lambda b, pt: (0, 0, 0)), wspec, wspec,
                      anyspec, anyspec, anyspec, anyspec],
            out_specs=pl.BlockSpec((None, 1, A_Q), lambda b, pt: (b, 0, 0)),
            scratch_shapes=[pltpu.VMEM((2, 2, n_pages * ROWS_PER_PAGE, ZW), F32),
                            pltpu.VMEM((2, 2, N_KV * past_len, HEAD_DIM), F32), pltpu.SemaphoreType.DMA((2,))]),
        compiler_params=pltpu.CompilerParams(dimension_semantics=("arbitrary",), vmem_limit_bytes=VMEM_LIMIT),
        name="nsa_sample",
    )(page_table, zs3, zs3, zs3, zs3, zs3, w2, rows2d(win_k), rows2d(win_v), zk_pool, zv_pool,
      rows2d(sel_k), rows2d(sel_v))
    return out.reshape(bs, A_Q)


def _mlstm_sample_kernel(q_ref, k_ref, v_ref, op_ref, g_ref, gain_ref, c_ref, n_ref, m_ref,
                         h_ref, co_ref, no_ref, mo_ref):
    eye = (lax.broadcasted_iota(jnp.int32, (DQK_B, DQK_B), 0) == lax.broadcasted_iota(jnp.int32, (DQK_B, DQK_B), 1))
    col = lambda row: jnp.sum(jnp.where(eye, row, 0.0), axis=1, keepdims=True)
    lane = lax.broadcasted_iota(jnp.int32, (1, N_HEADS_B), 1)
    m_out = jnp.zeros((1, N_HEADS_B), F32)
    for h in range(N_HEADS_B):
        q = q_ref[:, h * DQK_B:(h + 1) * DQK_B]
        k = k_ref[:, h * DQK_B:(h + 1) * DQK_B] * DQK_B ** -0.5
        v = v_ref[:, h * DV_B:(h + 1) * DV_B]
        log_i, log_f = _gate_logs(g_ref[:, h:h + 1], g_ref[:, N_HEADS_B + h:N_HEADS_B + h + 1])
        c_old, n_old, m_old = c_ref[h], n_ref[h:h + 1, :], m_ref[:, h:h + 1]
        inter = log_f + m_old
        m_new = jnp.maximum(inter, log_i)
        decay = jnp.exp(inter - m_new)
        w = jnp.exp(log_i - m_new)
        co_ref[h] = decay * c_old + col(w * k) * v
        no_ref[h:h + 1, :] = decay * n_old + w * k
        m_out = jnp.where(lane == h, m_new, m_out)
        w_intra = w * jnp.sum(q * k, axis=1, keepdims=True)
        num = decay * jnp.sum(c_old * col(q), axis=0, keepdims=True) + w_intra * v
        den = decay * jnp.sum(q * n_old, axis=1, keepdims=True) + w_intra
        hh = num / jnp.maximum(jnp.abs(den), jnp.exp(-m_new))
        out = _rms(hh, gain_ref[h]) * jax.nn.sigmoid(op_ref[:, h * DV_B:(h + 1) * DV_B])
        h_ref[:, h * DV_B:(h + 1) * DV_B] = out.astype(h_ref.dtype)
    mo_ref[...] = m_out


def _mlstm_sample(zs, gain, c0, n0, m0):
    bs, H = zs.shape[0], N_HEADS_B
    zs3 = zs.reshape(bs, 1, NZ)
    zspec = lambda w, off: pl.BlockSpec((None, 1, w), lambda b: (b, 0, off // w))
    cspec = pl.BlockSpec((None, H, DQK_B, DV_B), lambda b: (b, 0, 0, 0))
    nspec = pl.BlockSpec((None, H, DQK_B), lambda b: (b, 0, 0))
    mspec = pl.BlockSpec((None, 1, H), lambda b: (b, 0, 0))
    h, c, n, m = pl.pallas_call(
        _mlstm_sample_kernel,
        out_shape=(jax.ShapeDtypeStruct((bs, 1, B_V), BF16), jax.ShapeDtypeStruct(c0.shape, F32),
                   jax.ShapeDtypeStruct(n0.shape, F32), jax.ShapeDtypeStruct((bs, 1, H), F32)),
        grid=(bs,),
        in_specs=[zspec(B_QK, Z_QB), zspec(B_QK, Z_KB), zspec(B_V, Z_VB), zspec(B_V, Z_OB), zspec(HEAD_DIM, Z_IB),
                  pl.BlockSpec((H, 1, DV_B), lambda b: (0, 0, 0)), cspec, nspec, mspec],
        out_specs=(pl.BlockSpec((None, 1, B_V), lambda b: (b, 0, 0)), cspec, nspec, mspec),
        compiler_params=pltpu.CompilerParams(dimension_semantics=("parallel",), vmem_limit_bytes=VMEM_LIMIT),
        name="mlstm_sample",
    )(zs3, zs3, zs3, zs3, zs3, gain.reshape(H, 1, DV_B), c0, n0, m0.reshape(bs, 1, H))
    return h.reshape(bs, B_V), c, n, m.reshape(bs, H)


def _split_kv(z, B, T):
    z = z.reshape(B, T, NZ)
    kv = lambda o: z[..., o:o + A_KV].reshape(B, T, N_KV, HEAD_DIM)
    return kv(Z_KC), kv(Z_VC), kv(Z_KS), kv(Z_VS), kv(Z_KW), kv(Z_VW)


def kernel(x_prompt, x_sample, cache_cmp_k, cache_cmp_v, cache_sel_k, cache_sel_v, cache_win_k, cache_win_v,
           state_mlstm_c, state_mlstm_n, state_mlstm_m, page_table, norm_mix, w_in, b_in, cmp_pos, cmp_w1, cmp_w2,
           mlstm_norm, w_up_a, w_up_b, w_out, norm_ffn, w_router_grp, b_router_grp, w_router_exp, b_router_exp,
           w_gate, w_up, w_down, norm_final):
    assert w_in.shape[0] == 1, "single layer"
    Bp, Tp = x_prompt.shape[:2]
    Bs, Ts = x_sample.shape[:2]
    assert Ts == 1
    n_p, n_s = Bp * Tp, Bs * Ts
    past_len = page_table.shape[1] * PAGE_SIZE
    pos_p = jnp.arange(Tp, dtype=jnp.int32)
    pos_s = past_len + jnp.arange(Ts, dtype=jnp.int32)
    l = 0

    w_perm, b_perm = _permute_in_proj(w_in[l], b_in[l])
    cw = _compress_weights(cmp_pos[l], cmp_w1[l], cmp_w2[l])
    mw = _merge_weights(w_up_a[l], w_up_b[l], w_out[l], norm_ffn[l], w_router_grp[l], b_router_grp[l],
                        w_router_exp[l], b_router_exp[l])

    xp2d = x_prompt.reshape(n_p, D_MODEL)
    zp = _project(xp2d, jnp.tile(pos_p, Bp), norm_mix[l], w_perm, b_perm, 2048)
    kc, vc, ks, vs, kw, vw = _split_kv(zp, Bp, Tp)
    o_a = _nsa_prompt(zp, _compress_prompt(kc, vc, cw), Bp, Tp)
    h_b, c_p, n_p_state, m_p = _mlstm_prompt(zp, mlstm_norm[l], Bp, Tp)
    bufs = _merge(xp2d, o_a, h_b, zp, mw, n_p + n_s, 0, 256)
    wl = min(WINDOW, Tp)
    prompt_new = (kc, vc, ks, vs, kw[:, -wl:], vw[:, -wl:], c_p, n_p_state, m_p)

    xs2d = x_sample.reshape(n_s, D_MODEL)
    zs = _project(xs2d, jnp.repeat(pos_s, Bs), norm_mix[l], w_perm, b_perm, 128)
    kc, vc, ks, vs, kw, vw = _split_kv(zs, Bs, Ts)
    assert (past_len + Ts - CMP_LEN) // CMP_STRIDE + 1 == past_len // CMP_STRIDE - 1
    zk_pool, zv_pool = _compress_pool(cache_cmp_k[l], cache_cmp_v[l], cw)
    o_a_s = _nsa_sample(zs, page_table, zk_pool, zv_pool, cache_sel_k[l], cache_sel_v[l],
                        cache_win_k[l], cache_win_v[l], cw[2])
    h_s, c_s, n_s_state, m_s = _mlstm_sample(zs, mlstm_norm[l], state_mlstm_c[l], state_mlstm_n[l],
                                             state_mlstm_m[l])
    x2, xn, route = _merge(xs2d, o_a_s, h_s, zs, mw, n_p + n_s, n_p, 128, bufs=bufs)
    wk = jnp.concatenate([cache_win_k[l], kw], axis=1)
    wv = jnp.concatenate([cache_win_v[l], vw], axis=1)
    wl = min(WINDOW, past_len + Ts)
    sample_new = (kc, vc, ks, vs, wk[:, -wl:], wv[:, -wl:], c_s, n_s_state, m_s)

    y_p, y_s = _moe(x2, xn, route, w_gate[l], w_up[l], w_down[l], norm_final, n_p)
    return ((y_p.reshape(Bp, Tp, D_MODEL), y_s.reshape(Bs, Ts, D_MODEL))
            + tuple(a[None] for a in prompt_new) + tuple(a[None] for a in sample_new))
```

```python
import functools

import numpy as np
import jax
import jax.numpy as jnp
from jax import lax
from jax.experimental import pallas as pl
from jax.experimental.pallas import tpu as pltpu

D_MODEL = 2048
PAGE_SIZE = 128
N_HEADS_A = 8
N_KV = 2
GROUP = N_HEADS_A // N_KV
HEAD_DIM = 128
ROT_DIM = HEAD_DIM // 4
ROPE_THETA = 500000.0
CMP_LEN = 32
CMP_STRIDE = 16
SEL_LEN = 64
SEL_TOP = 16
FORCE_BONUS = 100.0
WINDOW = 512
Q_BLOCK = 128
N_HEADS_B = 4
DQK_B = 128
DV_B = 256
MLSTM_CHUNK = 64
GATE_CAP = 15.0
N_GROUPS = 4
EXPERTS_PER_GROUP = 8
N_EXPERTS = N_GROUPS * EXPERTS_PER_GROUP
TOP_K = 2
D_EXPERT = 1024
EPS = 1e-6
NEG = -1e30

A_Q = N_HEADS_A * HEAD_DIM
A_KV = N_KV * HEAD_DIM
B_QK = N_HEADS_B * DQK_B
B_V = N_HEADS_B * DV_B
IN_SPLITS = (A_Q, A_KV, A_KV, A_KV, A_KV, A_KV, A_KV, 3 * N_HEADS_A,
             B_QK, B_QK, B_V, N_HEADS_B, N_HEADS_B, B_V, D_MODEL, D_MODEL)

F32 = jnp.float32
BF16 = jnp.bfloat16
VMEM_LIMIT = 48 * 1024 * 1024
VMEM_LIMIT_MERGE = 56 * 1024 * 1024

_CUTS = np.concatenate([[0], np.cumsum(IN_SPLITS)]).tolist()
_SRC = dict(zip(('q_a', 'k_c', 'v_c', 'k_s', 'v_s', 'k_w', 'v_w', 'g_nsa', 'q_b', 'k_b', 'v_b', 'i_b', 'f_b',
                 'o_b', 'g_ma', 'g_mb'), zip(_CUTS[:-1], _CUTS[1:])))
_Z_ORDER = ('g_ma', 'g_mb', 'q_a', 'q_a', 'v_b', 'o_b', 'k_c', 'v_c', 'k_s', 'v_s', 'k_w', 'v_w', 'q_b', 'k_b')
PROJ_TN = 512
Z_GMA, Z_GMB, Z_QRAW, Z_QROT, Z_VB, Z_OB = 0, 2048, 4096, 5120, 6144, 7168
Z_KC, Z_VC, Z_KS, Z_VS, Z_KW, Z_VW = 8192, 8448, 8704, 8960, 9216, 9472
Z_QB, Z_KB, Z_SMALL = 9728, 10240, 10752
N_GATE = 3 * GROUP
Z_GNSA, Z_IB, Z_FB = Z_SMALL, Z_SMALL + 2 * HEAD_DIM, Z_SMALL + 2 * HEAD_DIM + N_HEADS_B
NZ = 11264
_ROPE_ALL_TILES = (Z_QROT // PROJ_TN, Z_QROT // PROJ_TN + 1)
_ROPE_HALF_TILES = (Z_KS // PROJ_TN, Z_KW // PROJ_TN)

ROUTE_W = 128
MOE_TILE = 768
MOE_SUB = 256
MOE_FC = 256


def _rms(x, g):
    return x * lax.rsqrt(jnp.mean(x * x, axis=-1, keepdims=True) + EPS) * g


def _proj_kernel(x_ref, g_ref, w_ref, b_ref, cos_ref, sa_ref, sb_ref, z_ref, xn_ref):
    j = pl.program_id(1)

    @pl.when(j == 0)
    def _():
        xn_ref[...] = _rms(x_ref[...], g_ref[...]).astype(BF16)

    acc = jnp.dot(xn_ref[...], w_ref[...], preferred_element_type=F32) + b_ref[...]

    def rope(blk):
        return (blk * cos_ref[...] + pltpu.roll(blk, HEAD_DIM - ROT_DIM // 2, 1) * sa_ref[...]
                + pltpu.roll(blk, ROT_DIM // 2, 1) * sb_ref[...])

    is_all = (j == _ROPE_ALL_TILES[0]) | (j == _ROPE_ALL_TILES[1])
    is_half = (j == _ROPE_HALF_TILES[0]) | (j == _ROPE_HALF_TILES[1])
    n_blk = PROJ_TN // HEAD_DIM

    @pl.when(is_all)
    def _():
        for c in range(n_blk):
            z_ref[:, c * HEAD_DIM:(c + 1) * HEAD_DIM] = rope(acc[:, c * HEAD_DIM:(c + 1) * HEAD_DIM])

    @pl.when(is_half)
    def _():
        for c in range(n_blk // 2):
            z_ref[:, c * HEAD_DIM:(c + 1) * HEAD_DIM] = rope(acc[:, c * HEAD_DIM:(c + 1) * HEAD_DIM])
        z_ref[:, PROJ_TN // 2:] = acc[:, PROJ_TN // 2:]

    @pl.when(jnp.logical_not(is_all | is_half))
    def _():
        z_ref[...] = acc


def _rope_tables(pos):
    half = ROT_DIM // 2
    inv = ROPE_THETA ** (-jnp.arange(half, dtype=F32) / half)
    ang = pos.astype(F32)[:, None] * inv[None, :]
    cos, sin = jnp.cos(ang), jnp.sin(ang)
    n = pos.shape[0]
    ones = jnp.ones((n, HEAD_DIM - ROT_DIM), F32)
    zeros = jnp.zeros((n, HEAD_DIM - half), F32)
    cos_t = jnp.concatenate([cos, cos, ones], axis=1)
    sa_t = jnp.concatenate([-sin, zeros], axis=1)
    sb_t = jnp.concatenate([jnp.zeros((n, half), F32), sin, jnp.zeros((n, HEAD_DIM - ROT_DIM), F32)], axis=1)
    return cos_t, sa_t, sb_t


def _permute_in_proj(w_in, b_in):
    g0 = _SRC['g_nsa'][0]
    spans = [_SRC[k] for k in _Z_ORDER]
    spans += [(g0, g0 + N_GATE), HEAD_DIM - N_GATE, (g0 + N_GATE, g0 + 2 * N_GATE), HEAD_DIM - N_GATE,
              _SRC['i_b'], _SRC['f_b'], NZ - Z_FB - N_HEADS_B]

    def permute(a, dtype):
        parts = [jnp.zeros((a.shape[0], s), dtype) if isinstance(s, int) else a[:, s[0]:s[1]].astype(dtype)
                 for s in spans]
        return jnp.concatenate(parts, axis=1)

    return permute(w_in, BF16), permute(b_in[None, :], F32)


def _project(x2d, pos_rows, norm_g, w_perm, b_perm, tm):
    m = x2d.shape[0]
    cos_t, sa_t, sb_t = _rope_tables(pos_rows)
    rows = lambda w: pl.BlockSpec((tm, w), lambda i, j: (i, 0), pipeline_mode=pl.Buffered(1))
    return pl.pallas_call(
        _proj_kernel,
        out_shape=jax.ShapeDtypeStruct((m, NZ), F32),
        grid=(m // tm, NZ // PROJ_TN),
        in_specs=[rows(D_MODEL),
                  pl.BlockSpec((1, D_MODEL), lambda i, j: (0, 0)),
                  pl.BlockSpec((D_MODEL, PROJ_TN), lambda i, j: (0, j)),
                  pl.BlockSpec((1, PROJ_TN), lambda i, j: (0, j)),
                  rows(HEAD_DIM), rows(HEAD_DIM), rows(HEAD_DIM)],
        out_specs=pl.BlockSpec((tm, PROJ_TN), lambda i, j: (i, j)),
        scratch_shapes=[pltpu.VMEM((tm, D_MODEL), BF16)],
        compiler_params=pltpu.CompilerParams(dimension_semantics=("parallel", "arbitrary"),
                                             vmem_limit_bytes=VMEM_LIMIT),
        name="proj",
    )(x2d, norm_g.reshape(1, D_MODEL), w_perm, b_perm, cos_t, sa_t, sb_t)


def _compress_body(x_ref, w1, pb, w2, r):
    bias = pb[0:1, :HEAD_DIM] + pb[1:2, HEAD_DIM:]
    outs = []
    for h in range(N_KV):
        xh = jnp.concatenate([x_ref[:, (N_KV * s + h) * HEAD_DIM:(N_KV * s + h + 1) * HEAD_DIM]
                              for s in range(CMP_STRIDE)], axis=1).astype(BF16)
        zz = jnp.dot(xh, w1, preferred_element_type=F32)
        pre = zz[:, :HEAD_DIM] + pltpu.roll(zz[:, HEAD_DIM:], r - 1, 0) + bias
        mid = pre * jax.nn.sigmoid(pre)
        outs.append(jnp.dot(mid.astype(BF16), w2, preferred_element_type=F32))
    return outs


def _compress_kernel(xk_ref, xv_ref, w1_ref, p_ref, w2_ref, o_ref):
    r = xk_ref.shape[0]
    for which, x_ref in enumerate((xk_ref, xv_ref)):
        w1 = w1_ref[which]
        pb = jnp.dot(p_ref[which], w1, preferred_element_type=F32)
        for h, o in enumerate(_compress_body(x_ref, w1, pb, w2_ref[which], r)):
            o_ref[which, h] = o


def _compress_weights(cmp_pos, cmp_w1, cmp_w2):
    assert CMP_LEN == 2 * CMP_STRIDE
    half = CMP_STRIDE * HEAD_DIM
    w1 = jnp.concatenate([cmp_w1[:, :half], cmp_w1[:, half:]], axis=2).astype(BF16)
    p = cmp_pos.reshape(2, 2, half)
    p = jnp.concatenate([p, jnp.zeros((2, 6, half), p.dtype)], axis=1).astype(BF16)
    return w1, p, cmp_w2.astype(BF16)


def _compress_prompt(kc, vc, cw):
    B, T = kc.shape[:2]
    r = T // CMP_STRIDE
    width = CMP_STRIDE * N_KV * HEAD_DIM
    w1, p, w2 = cw
    full = lambda a: pl.BlockSpec(a.shape, lambda b: (0,) * a.ndim)
    xspec = pl.BlockSpec((None, r, width), lambda b: (b, 0, 0))
    return pl.pallas_call(
        _compress_kernel,
        out_shape=jax.ShapeDtypeStruct((B, 2, N_KV, r, HEAD_DIM), F32),
        grid=(B,),
        in_specs=[xspec, xspec, full(w1), full(p), full(w2)],
        out_specs=pl.BlockSpec((None, 2, N_KV, r, HEAD_DIM), lambda b: (b, 0, 0, 0, 0)),
        compiler_params=pltpu.CompilerParams(dimension_semantics=("parallel",), vmem_limit_bytes=VMEM_LIMIT),
        name="compress_prompt",
    )(kc.reshape(B, r, width), vc.reshape(B, r, width), w1, p, w2)


def _group_scores(q, k, tq):
    s = lax.dot_general(q, k, (((1,), (1,)), ((), ())), preferred_element_type=F32) * HEAD_DIM ** -0.5
    return s.reshape(GROUP, tq, k.shape[0])


def _stack_heads(ref):
    return jnp.concatenate([ref[:, g * HEAD_DIM:(g + 1) * HEAD_DIM] for g in range(GROUP)], axis=0).astype(BF16)


def _importance(psum, n_sel):
    r = psum.shape[1]
    ci = lax.broadcasted_iota(jnp.int32, (r, n_sel), 0) * CMP_STRIDE
    cj = lax.broadcasted_iota(jnp.int32, (r, n_sel), 1) * SEL_LEN
    cover_t = ((ci < cj + SEL_LEN) & (ci + (CMP_LEN - 1) >= cj)).astype(BF16)
    p_hi = psum.astype(BF16)
    p_lo = (psum - p_hi.astype(F32)).astype(BF16)
    return jnp.dot(p_hi, cover_t, preferred_element_type=F32) + jnp.dot(p_lo, cover_t, preferred_element_type=F32)


def _block_scores(imp, blk, qpos):
    cur = jnp.right_shift(qpos, SEL_LEN.bit_length() - 1)
    forced = (blk == 0) | (blk == cur) | (blk == cur - 1)
    return jnp.where(blk * SEL_LEN <= qpos, imp + jnp.where(forced, FORCE_BONUS, 0.0), -1.0)


def _top_rank_mask(score, axis, n_cand):
    idx = lax.broadcasted_iota(jnp.int32, score.shape, axis)
    rank = jnp.zeros(score.shape, F32)
    for j in range(n_cand):
        cand = score[j:j + 1, :] if axis == 0 else score[:, j:j + 1]
        rank = rank + jnp.where((cand > score) | ((cand == score) & (idx > j)), 1.0, 0.0)
    return jnp.where(rank < float(SEL_TOP), 1.0, 0.0)


def _nsa_prompt_kernel(qraw_ref, qrot_ref, ks_ref, vs_ref, kw_ref, vw_ref, kc_ref, vc_ref, gate_ref, o_ref,
                       m_ref, l_ref, acc_ref, *, tq, tk, seq_len):
    q0 = pl.program_id(2) * tq
    r = kc_ref.shape[0]
    n_sel = seq_len // SEL_LEN
    sel_shift = SEL_LEN.bit_length() - 1
    qpos = q0 + lax.broadcasted_iota(jnp.int32, (tq, 1), 0)

    s = _group_scores(_stack_heads(qraw_ref), kc_ref[...].astype(BF16), tq)
    c_end = lax.broadcasted_iota(jnp.int32, (tq, r), 1) * CMP_STRIDE + (CMP_LEN - 1)
    cmask = c_end <= qpos
    s = s + jnp.where(cmask, 0.0, NEG)[None]
    p = jnp.exp(s - jnp.max(s, axis=-1, keepdims=True)) * jnp.where(cmask, 1.0, 0.0)[None]
    p = p / jnp.maximum(jnp.sum(p, axis=-1, keepdims=True), 1e-30)
    o_cmp = jnp.dot(p.reshape(GROUP * tq, r).astype(BF16), vc_ref[...].astype(BF16), preferred_element_type=F32)
    psum = p[0]
    for g in range(1, GROUP):
        psum = psum + p[g]
    imp_t = _importance(psum, HEAD_DIM).T[:n_sel]
    blk_t = lax.broadcasted_iota(jnp.int32, (n_sel, tq), 0)
    qpos_t = q0 + lax.broadcasted_iota(jnp.int32, (n_sel, tq), 1)
    sel_b = _top_rank_mask(_block_scores(imp_t, blk_t, qpos_t), 0, n_sel).astype(BF16)

    q_rot = _stack_heads(qrot_ref)
    m_ref[...] = jnp.full(m_ref.shape, NEG, F32)
    l_ref[...] = jnp.zeros(l_ref.shape, F32)
    acc_ref[...] = jnp.zeros(acc_ref.shape, F32)

    def sel_tile(kt, carry):
        k0 = pl.multiple_of(kt * tk, tk)
        k = ks_ref[pl.ds(k0, tk), :].astype(BF16)
        v = vs_ref[pl.ds(k0, tk), :].astype(BF16)
        s2 = _group_scores(q_rot, k, tq)
        ej = lax.broadcasted_iota(jnp.int32, (n_sel, tk), 0)
        et = k0 + lax.broadcasted_iota(jnp.int32, (n_sel, tk), 1)
        expand = jnp.where(jnp.right_shift(et, sel_shift) == ej, 1.0, 0.0).astype(BF16)
        selm = lax.dot_general(sel_b, expand, (((0,), (0,)), ((), ())), preferred_element_type=F32)
        kpos = k0 + lax.broadcasted_iota(jnp.int32, (tq, tk), 1)
        ok = (selm > 0.5) & (kpos <= qpos)
        s2 = s2 + jnp.where(ok, 0.0, NEG)[None]
        m_prev = m_ref[...]
        m_new = jnp.maximum(m_prev, jnp.max(s2, axis=-1, keepdims=True))
        alpha = jnp.exp(m_prev - m_new)
        p2 = jnp.exp(s2 - m_new)
        l_ref[...] = alpha * l_ref[...] + jnp.sum(p2, axis=-1, keepdims=True)
        pv = jnp.dot(p2.reshape(GROUP * tq, tk).astype(BF16), v, preferred_element_type=F32)
        acc_ref[...] = alpha.reshape(GROUP * tq, 1) * acc_ref[...] + pv
        m_ref[...] = m_new
        return carry

    lax.fori_loop(0, lax.div(q0 + tq + tk - 1, tk), sel_tile, 0)
    o_sel = acc_ref[...] / jnp.maximum(l_ref[...].reshape(GROUP * tq, 1), 1e-30)

    span = WINDOW + tq
    w0 = pl.multiple_of(jnp.maximum(q0 - WINDOW, 0), tq)
    s3 = _group_scores(q_rot, kw_ref[pl.ds(w0, span), :].astype(BF16), tq)
    kp = w0 + lax.broadcasted_iota(jnp.int32, (tq, span), 1)
    okw = (kp <= qpos) & (kp >= qpos - WINDOW)
    s3 = s3 + jnp.where(okw, 0.0, NEG)[None]
    p3 = jnp.exp(s3 - jnp.max(s3, axis=-1, keepdims=True))
    p3 = p3 / jnp.sum(p3, axis=-1, keepdims=True)
    o_win = jnp.dot(p3.reshape(GROUP * tq, span).astype(BF16), vw_ref[pl.ds(w0, span), :].astype(BF16),
                    preferred_element_type=F32)

    gts = jax.nn.sigmoid(gate_ref[...])
    for g in range(GROUP):
        rows = slice(g * tq, (g + 1) * tq)
        o_ref[:, g * HEAD_DIM:(g + 1) * HEAD_DIM] = (gts[:, 3 * g:3 * g + 1] * o_cmp[rows]
                                                     + gts[:, 3 * g + 1:3 * g + 2] * o_sel[rows]
                                                     + gts[:, 3 * g + 2:3 * g + 3] * o_win[rows]).astype(o_ref.dtype)


def _nsa_prompt(z, cmp, B, T, tq=256, tk=512):
    assert T % tk == 0 and tk % tq == 0 and tq % HEAD_DIM == 0 and T >= WINDOW + tq
    assert SEL_LEN & (SEL_LEN - 1) == 0 and tk % SEL_LEN == 0
    assert (T // SEL_LEN) % 8 == 0 and T // SEL_LEN <= HEAD_DIM
    nq = T // tq
    r = cmp.shape[3]
    gw = GROUP * HEAD_DIM
    qspec = lambda off: pl.BlockSpec((tq, gw), lambda b, h, q: (b * nq + q, off // gw + h))
    kvspec = lambda off: pl.BlockSpec((T, HEAD_DIM), lambda b, h, q: (b, off // HEAD_DIM + h))
    cspec = lambda which: pl.BlockSpec((None, None, None, r, HEAD_DIM), lambda b, h, q: (b, which, h, 0, 0))
    return pl.pallas_call(
        functools.partial(_nsa_prompt_kernel, tq=tq, tk=tk, seq_len=T),
        out_shape=jax.ShapeDtypeStruct((B * T, A_Q), BF16),
        grid=(B, N_KV, nq),
        in_specs=[qspec(Z_QRAW), qspec(Z_QROT), kvspec(Z_KS), kvspec(Z_VS), kvspec(Z_KW), kvspec(Z_VW),
                  cspec(0), cspec(1),
                  pl.BlockSpec((tq, HEAD_DIM), lambda b, h, q: (b * nq + q, Z_GNSA // HEAD_DIM + h))],
        out_specs=pl.BlockSpec((tq, gw), lambda b, h, q: (b * nq + q, h)),
        scratch_shapes=[pltpu.VMEM((GROUP, tq, 1), F32), pltpu.VMEM((GROUP, tq, 1), F32),
                        pltpu.VMEM((GROUP * tq, HEAD_DIM), F32)],
        compiler_params=pltpu.CompilerParams(dimension_semantics=("parallel", "parallel", "arbitrary"),
                                             vmem_limit_bytes=VMEM_LIMIT),
        name="nsa_prompt",
    )(z, z, z, z, z, z, cmp, cmp, z)


def _gate_logs(i_pre, f_pre):
    log_i = GATE_CAP * jnp.tanh(i_pre / GATE_CAP)
    fc = GATE_CAP * jnp.tanh(f_pre / GATE_CAP)
    log_f = jnp.minimum(fc, 0.0) - jnp.log1p(jnp.exp(-jnp.abs(fc)))
    return log_i, log_f


def _mlstm_prompt_kernel(q_ref, k_ref, v_ref, op_ref, grow_ref, gcol_ref, gain_ref,
                         h_ref, c_ref, n_ref, m_ref):
    @pl.when(pl.program_id(1) == 0)
    def _():
        c_ref[...] = jnp.zeros(c_ref.shape, F32)
        n_ref[...] = jnp.zeros(n_ref.shape, F32)
        m_ref[...] = jnp.zeros(m_ref.shape, F32)

    for h in range(N_HEADS_B):
        qk_cols = slice(h * DQK_B, (h + 1) * DQK_B)
        v_cols = slice(h * DV_B, (h + 1) * DV_B)
        _mlstm_chunk(q_ref[:, qk_cols], k_ref[:, qk_cols], v_ref[:, v_cols], op_ref[:, v_cols],
                     grow_ref[h:h + 1, :], grow_ref[N_HEADS_B + h:N_HEADS_B + h + 1, :],
                     gcol_ref[:, h:h + 1], gcol_ref[:, N_HEADS_B + h:N_HEADS_B + h + 1], gain_ref[h],
                     h_ref.at[:, v_cols], c_ref.at[h], n_ref.at[h], m_ref.at[h])


def _mlstm_chunk(q, k, v, o_pre, i_row, f_row, i_col, f_col, gain, h_ref, c_ref, n_ref, m_ref):
    L = q.shape[0]
    qb = q.astype(BF16)
    k = k * DQK_B ** -0.5
    vb = v.astype(BF16)
    i_r, f_r = _gate_logs(i_row, f_row)
    i_c, f_c = _gate_logs(i_col, f_col)
    li = lax.broadcasted_iota(jnp.int32, (L, L), 0)
    si = lax.broadcasted_iota(jnp.int32, (L, L), 1)
    tri = si <= li
    bcum_c = jnp.sum(jnp.where(tri, f_r, 0.0), axis=1, keepdims=True)
    bcum_r = jnp.sum(jnp.where(li <= si, f_c, 0.0), axis=0, keepdims=True)
    btot = jnp.sum(f_r, axis=1, keepdims=True)
    c_old, n_old, m_old = c_ref[...], n_ref[...], m_ref[0:1, 0:1]

    dlog = jnp.where(tri, bcum_c - bcum_r + i_r, NEG)
    inter = bcum_c + m_old
    m_t = jnp.maximum(inter, jnp.max(dlog, axis=1, keepdims=True))
    qk = lax.dot_general(qb, k.astype(BF16), (((1,), (1,)), ((), ())), preferred_element_type=F32)
    w_intra = jnp.exp(dlog - m_t) * qk
    w_inter = jnp.exp(inter - m_t)
    num = (w_inter * jnp.dot(qb, c_old.astype(BF16), preferred_element_type=F32)
           + jnp.dot(w_intra.astype(BF16), vb, preferred_element_type=F32))
    den = w_inter * jnp.sum(q * n_old, axis=1, keepdims=True) + jnp.sum(w_intra, axis=1, keepdims=True)
    h = num / jnp.maximum(jnp.abs(den), jnp.exp(-m_t))
    h_ref[...] = (_rms(h, gain) * jax.nn.sigmoid(o_pre)).astype(h_ref.dtype)

    wlog_c = btot - bcum_c + i_c
    m_new = jnp.maximum(btot + m_old, jnp.max(btot - bcum_r + i_r, axis=1, keepdims=True))
    decay = jnp.exp(btot + m_old - m_new)
    kw = k * jnp.exp(wlog_c - m_new)
    c_ref[...] = decay * c_old + lax.dot_general(kw.astype(BF16), vb, (((0,), (0,)), ((), ())),
                                                 preferred_element_type=F32)
    n_ref[...] = decay * n_old + jnp.sum(kw, axis=0, keepdims=True)
    m_ref[...] = jnp.broadcast_to(m_new, m_ref.shape)


def _mlstm_prompt(z, gain, B, T):
    L = MLSTM_CHUNK
    assert T % L == 0
    nc, H = T // L, N_HEADS_B
    gcol = z[:, Z_IB:Z_IB + 2 * H].reshape(B * nc, L, 2 * H)
    grow = gcol.transpose(0, 2, 1)
    rows = lambda w, off: pl.BlockSpec((L, w), lambda b, c: (b * nc + c, off // w))
    gidx = lambda b, c: (b * nc + c, 0, 0)
    state = lambda *s: pl.BlockSpec((None, H) + s, lambda b, c: (b, 0, 0, 0))
    h, c, n, m = pl.pallas_call(
        _mlstm_prompt_kernel,
        out_shape=(jax.ShapeDtypeStruct((B * T, B_V), BF16), jax.ShapeDtypeStruct((B, H, DQK_B, DV_B), F32),
                   jax.ShapeDtypeStruct((B, H, 1, DQK_B), F32), jax.ShapeDtypeStruct((B, H, 1, DQK_B), F32)),
        grid=(B, nc),
        in_specs=[rows(B_QK, Z_QB), rows(B_QK, Z_KB), rows(B_V, Z_VB), rows(B_V, Z_OB),
                  pl.BlockSpec((None, 2 * H, L), gidx), pl.BlockSpec((None, L, 2 * H), gidx),
                  pl.BlockSpec((H, 1, DV_B), lambda b, c: (0, 0, 0))],
        out_specs=(rows(B_V, 0), state(DQK_B, DV_B), state(1, DQK_B), state(1, DQK_B)),
        compiler_params=pltpu.CompilerParams(dimension_semantics=("parallel", "arbitrary"),
                                             vmem_limit_bytes=VMEM_LIMIT),
        name="mlstm_prompt",
    )(z, z, z, z, grow, gcol, gain.reshape(H, 1, DV_B))
    return h, c, n[:, :, 0], m[:, :, 0, 0]


def _merge_kernel(x_ref, oa_ref, hb_ref, gma_ref, gmb_ref, wa_ref, wb_ref, wo_ref, nf_ref, wrh_ref, wrl_ref, br_ref,
                  *refs):
    x2_ref, xn_ref, route_ref = refs[-3:]
    ua = jnp.dot(oa_ref[...], wa_ref[...], preferred_element_type=F32)
    ub = jnp.dot(hb_ref[...], wb_ref[...], preferred_element_type=F32)
    merged = jax.nn.sigmoid(gma_ref[...]) * ua + jax.nn.sigmoid(gmb_ref[...]) * ub
    x2 = x_ref[...] + jnp.dot(merged.astype(BF16), wo_ref[...], preferred_element_type=F32)
    x2_ref[...] = x2
    xn = _rms(x2, nf_ref[...])
    xn_ref[...] = xn
    hi = xn.astype(BF16)
    lo = (xn - hi.astype(F32)).astype(BF16)
    lg = (jnp.dot(hi, wrh_ref[...], preferred_element_type=F32) + jnp.dot(lo, wrh_ref[...], preferred_element_type=F32)
          + jnp.dot(hi, wrl_ref[...], preferred_element_type=F32)) + br_ref[...]
    lane = lax.broadcasted_iota(jnp.int32, lg.shape, 1)
    lanef = lane.astype(F32)
    ninf = -jnp.inf
    gmask = (lane >= N_EXPERTS) & (lane < N_EXPERTS + N_GROUPS)
    gl = jnp.where(gmask, lg, ninf)
    gmax = jnp.max(gl, axis=1, keepdims=True)
    gidx = jnp.min(jnp.where(gl == gmax, lanef, 1e9), axis=1, keepdims=True) - N_EXPERTS
    gprob = 1.0 / jnp.sum(jnp.where(gmask, jnp.exp(lg - gmax), 0.0), axis=1, keepdims=True)
    e_lo = gidx * EXPERTS_PER_GROUP
    el = jnp.where((lanef >= e_lo) & (lanef < e_lo + EXPERTS_PER_GROUP), lg, ninf)
    v1 = jnp.max(el, axis=1, keepdims=True)
    i1 = jnp.min(jnp.where(el == v1, lanef, 1e9), axis=1, keepdims=True)
    el2 = jnp.where(lanef == i1, ninf, el)
    v2 = jnp.max(el2, axis=1, keepdims=True)
    i2 = jnp.min(jnp.where(el2 == v2, lanef, 1e9), axis=1, keepdims=True)
    e2 = jnp.exp(v2 - v1)
    g1 = gprob / (1.0 + e2)
    g2 = gprob * e2 / (1.0 + e2)
    route_ref[...] = jnp.where(lane == 0, i1, jnp.where(lane == 1, i2, jnp.where(lane == 2, g1,
                               jnp.where(lane == 3, g2, 0.0))))


def _merge_weights(w_up_a, w_up_b, w_out, norm_ffn, w_rg, b_rg, w_re, b_re):
    assert TOP_K == 2
    pad = ROUTE_W - N_EXPERTS - N_GROUPS
    w = jnp.concatenate([w_re, w_rg, jnp.zeros((D_MODEL, pad), F32)], axis=1)
    b = jnp.concatenate([b_re, b_rg, jnp.zeros((pad,), F32)]).reshape(1, ROUTE_W)
    hi = w.astype(BF16)
    return (w_up_a.astype(BF16), w_up_b.astype(BF16), w_out.astype(BF16), norm_ffn.reshape(1, D_MODEL).astype(F32),
            hi, (w - hi.astype(F32)).astype(BF16), b)


def _merge(x2d, o_a, h_b, z, mw, n_total, row0, tm, bufs=None):
    m = x2d.shape[0]
    i0 = row0 // tm
    row = lambda w, c=0: pl.BlockSpec((tm, w), lambda i: (i, c))
    const = lambda a: pl.BlockSpec(a.shape, lambda i: (0,) * a.ndim, pipeline_mode=pl.Buffered(1))
    outw = (D_MODEL, D_MODEL, ROUTE_W)
    in_specs = [row(D_MODEL), row(A_Q), row(B_V), row(D_MODEL, Z_GMA // D_MODEL), row(D_MODEL, Z_GMB // D_MODEL)]
    in_specs += [const(a) for a in mw]
    args = [x2d, o_a, h_b, z, z] + list(mw)
    aliases = {}
    if bufs is not None:
        in_specs += [pl.BlockSpec(memory_space=pl.ANY)] * 3
        aliases = {len(args) + k: k for k in range(3)}
        args += list(bufs)
    return pl.pallas_call(
        _merge_kernel,
        out_shape=tuple(jax.ShapeDtypeStruct((n_total, w), F32) for w in outw),
        grid=(m // tm,),
        in_specs=in_specs,
        out_specs=tuple(pl.BlockSpec((tm, w), lambda i: (i0 + i, 0)) for w in outw),
        input_output_aliases=aliases,
        compiler_params=pltpu.CompilerParams(dimension_semantics=("parallel",), vmem_limit_bytes=VMEM_LIMIT_MERGE),
        name="merge_route",
    )(*args)


def _moe_plan(route, n_tiles):
    n = route.shape[0]
    a = n * TOP_K
    flat_e = route[:, 0:TOP_K].astype(jnp.int32).reshape(a)
    onehot = (flat_e[:, None] == jnp.arange(N_EXPERTS, dtype=jnp.int32)[None, :]).astype(jnp.int32)
    rank = jnp.take_along_axis(jnp.cumsum(onehot, axis=0) - onehot, flat_e[:, None], axis=1)[:, 0]
    counts = jnp.sum(onehot, axis=0)
    ntile = (counts + MOE_TILE - 1) // MOE_TILE
    tile_end = jnp.cumsum(ntile)
    dest = ((tile_end - ntile)[flat_e] * MOE_TILE + rank).astype(jnp.int32)
    n_used = tile_end[-1].astype(jnp.int32)
    t = jnp.minimum(jnp.arange(n_tiles, dtype=jnp.int32), n_used - 1)
    tile_expert = jnp.minimum(jnp.searchsorted(tile_end, t, side='right'), N_EXPERTS - 1).astype(jnp.int32)
    first = (tile_end - ntile)[tile_expert]
    tile_rows = jnp.clip(counts[tile_expert] - (t - first) * MOE_TILE, 0, MOE_TILE)
    tile_rows = jnp.where(jnp.arange(n_tiles) < n_used, tile_rows, 0).astype(jnp.int32)
    return dest, tile_expert, tile_rows, n_used.reshape(1)


DMA_UNROLL = 8


def _dispatch_kernel(dest_ref, x_ref, o_hbm, sem):
    i = pl.program_id(0)
    tm = x_ref.shape[0]

    def copy(r, k):
        d = dest_ref[(i * tm + r) * TOP_K + k]
        return pltpu.make_async_copy(x_ref.at[pl.ds(r, 1)], o_hbm.at[pl.ds(d, 1)], sem)

    def start(r, c):
        for k in range(TOP_K):
            copy(r, k).start()
        return c

    def wait(r, c):
        for k in range(TOP_K):
            copy(r, k).wait()
        return c

    lax.fori_loop(0, tm, start, 0, unroll=DMA_UNROLL)
    lax.fori_loop(0, tm, wait, 0, unroll=DMA_UNROLL)


def _moe_dispatch(xn, dest, n_tiles, tm=128):
    n, d = xn.shape
    return pl.pallas_call(
        _dispatch_kernel,
        out_shape=jax.ShapeDtypeStruct((n_tiles * MOE_TILE, d), xn.dtype),
        grid_spec=pltpu.PrefetchScalarGridSpec(
            num_scalar_prefetch=1, grid=(n // tm,),
            in_specs=[pl.BlockSpec((tm, d), lambda i, ds: (i, 0))],
            out_specs=pl.BlockSpec(memory_space=pl.ANY),
            scratch_shapes=[pltpu.SemaphoreType.DMA(())]),
        compiler_params=pltpu.CompilerParams(dimension_semantics=("arbitrary",), vmem_limit_bytes=VMEM_LIMIT),
        name="moe_dispatch",
    )(dest, xn)


def _expert_kernel(te_ref, cnt_ref, nu_ref, x_ref, wg_ref, wu_ref, wd_ref, o_ref, xb_ref):
    t, j = pl.program_id(0), pl.program_id(1)
    cnt = cnt_ref[t]

    @pl.when(cnt > 0)
    def _():
        wg, wu, wd = wg_ref[...].astype(BF16), wu_ref[...].astype(BF16), wd_ref[...].astype(BF16)
        def run(rows):
            @pl.when(j == 0)
            def _():
                xb_ref[rows] = x_ref[rows].astype(BF16)

            xb = xb_ref[rows]
            g = jnp.dot(xb, wg, preferred_element_type=F32)
            u = jnp.dot(xb, wu, preferred_element_type=F32)
            h = (g * jax.nn.sigmoid(g) * u).astype(BF16)
            y = jnp.dot(h, wd, preferred_element_type=F32)

            @pl.when(j == 0)
            def _():
                o_ref[rows] = y

            @pl.when(j > 0)
            def _():
                o_ref[rows] += y

        for s in range(MOE_TILE // MOE_SUB):
            lo, half = s * MOE_SUB, MOE_SUB // 2

            @pl.when(cnt > lo + half)
            def _():
                run(slice(lo, lo + MOE_SUB))

            @pl.when((cnt > lo) & (cnt <= lo + half))
            def _():
                run(slice(lo, lo + half))


def _moe_experts(xs, tile_expert, tile_rows, n_used, w_gate, w_up, w_down):
    n_tiles = xs.shape[0] // MOE_TILE
    nj = D_EXPERT // MOE_FC
    tix = lambda t, j, te, cnt, nu: (jnp.minimum(t, nu[0] - 1), 0)
    jj = lambda t, j, nu: jnp.where(t < nu[0], j, nj - 1)
    return pl.pallas_call(
        _expert_kernel,
        out_shape=jax.ShapeDtypeStruct(xs.shape, F32),
        grid_spec=pltpu.PrefetchScalarGridSpec(
            num_scalar_prefetch=3, grid=(n_tiles, nj),
            in_specs=[pl.BlockSpec((MOE_TILE, D_MODEL), tix),
                      pl.BlockSpec((None, D_MODEL, MOE_FC), lambda t, j, te, cnt, nu: (te[t], 0, jj(t, j, nu))),
                      pl.BlockSpec((None, D_MODEL, MOE_FC), lambda t, j, te, cnt, nu: (te[t], 0, jj(t, j, nu))),
                      pl.BlockSpec((None, MOE_FC, D_MODEL), lambda t, j, te, cnt, nu: (te[t], jj(t, j, nu), 0))],
            out_specs=pl.BlockSpec((MOE_TILE, D_MODEL), tix),
            scratch_shapes=[pltpu.VMEM((MOE_TILE, D_MODEL), BF16)]),
        compiler_params=pltpu.CompilerParams(dimension_semantics=("arbitrary", "arbitrary"),
                                             vmem_limit_bytes=VMEM_LIMIT_MERGE),
        name="moe_experts",
    )(tile_expert, tile_rows, n_used, xs, w_gate, w_up, w_down)


def _combine_kernel(dest_ref, x2_ref, route_ref, g_ref, y_hbm, op_ref, os_ref, buf_ref, sem, *, n_prompt_tiles):
    i = pl.program_id(0)
    tm = x2_ref.shape[0]

    def copy(r, k):
        d = dest_ref[(i * tm + r) * TOP_K + k]
        return pltpu.make_async_copy(y_hbm.at[pl.ds(d, 1)], buf_ref.at[k, pl.ds(r, 1)], sem)

    def start(r, c):
        for k in range(TOP_K):
            copy(r, k).start()
        return c

    def wait(r, c):
        for k in range(TOP_K):
            copy(r, k).wait()
        return c

    lax.fori_loop(0, tm, start, 0, unroll=DMA_UNROLL)
    lax.fori_loop(0, tm, wait, 0, unroll=DMA_UNROLL)
    ffn = buf_ref[0] * route_ref[:, TOP_K:TOP_K + 1]
    for k in range(1, TOP_K):
        ffn = ffn + buf_ref[k] * route_ref[:, TOP_K + k:TOP_K + k + 1]
    y = _rms(x2_ref[...] + ffn, g_ref[...])

    @pl.when(i < n_prompt_tiles)
    def _():
        op_ref[...] = y

    @pl.when(i >= n_prompt_tiles)
    def _():
        os_ref[...] = y


def _moe_combine(x2, route, dest, ys, norm_final, n_prompt, tm=128):
    n, d = x2.shape
    npt = n_prompt // tm
    nst = (n - n_prompt) // tm
    return pl.pallas_call(
        functools.partial(_combine_kernel, n_prompt_tiles=npt),
        out_shape=(jax.ShapeDtypeStruct((n_prompt, d), F32), jax.ShapeDtypeStruct((n - n_prompt, d), F32)),
        grid_spec=pltpu.PrefetchScalarGridSpec(
            num_scalar_prefetch=1, grid=(n // tm,),
            in_specs=[pl.BlockSpec((tm, d), lambda i, ds: (i, 0)), pl.BlockSpec((tm, ROUTE_W), lambda i, ds: (i, 0)),
                      pl.BlockSpec((1, d), lambda i, ds: (0, 0)), pl.BlockSpec(memory_space=pl.ANY)],
            out_specs=(pl.BlockSpec((tm, d), lambda i, ds: (jnp.minimum(i, npt - 1), 0)),
                       pl.BlockSpec((tm, d), lambda i, ds: (jnp.clip(i - npt, 0, nst - 1), 0))),
            scratch_shapes=[pltpu.VMEM((TOP_K, tm, d), F32), pltpu.SemaphoreType.DMA(())]),
        compiler_params=pltpu.CompilerParams(dimension_semantics=("arbitrary",), vmem_limit_bytes=VMEM_LIMIT),
        name="moe_combine",
    )(dest, x2, route, norm_final.reshape(1, d), ys)


def _moe(x2, xn, route, w_gate, w_up, w_down, norm_final, n_prompt):
    n = x2.shape[0]
    n_tiles = (n * TOP_K + N_EXPERTS * (MOE_TILE - 1) + MOE_TILE - 1) // MOE_TILE
    dest, tile_expert, tile_rows, n_used = _moe_plan(route, n_tiles)
    xs = _moe_dispatch(xn, dest, n_tiles)
    ys = _moe_experts(xs, tile_expert, tile_rows, n_used, w_gate, w_up, w_down)
    return _moe_combine(x2, route, dest, ys, norm_final, n_prompt)


POOL_ROWS = 256
ROWS_PER_PAGE = PAGE_SIZE // CMP_STRIDE
ZW = 2 * N_KV * HEAD_DIM


GROUP_ROWS = N_KV * CMP_STRIDE
GROUP_PITCH = GROUP_ROWS + 8


def _compress_pool_kernel(w1_ref, p_ref, xk_hbm, xv_hbm, zk_ref, zv_ref, xbuf, sem):
    i = pl.program_id(0)
    n = pl.num_programs(0)
    slot = lax.rem(i, 2)

    def copies(step, sl):
        return [pltpu.make_async_copy(src.at[pl.ds(step * POOL_ROWS, POOL_ROWS)],
                                      xbuf.at[sl, which, :, pl.ds(0, GROUP_ROWS), :], sem.at[sl, which])
                for which, src in enumerate((xk_hbm, xv_hbm))]

    @pl.when(i == 0)
    def _():
        for cp in copies(0, 0):
            cp.start()

    @pl.when(i + 1 < n)
    def _():
        for cp in copies(i + 1, 1 - slot):
            cp.start()

    for cp in copies(i, slot):
        cp.wait()

    for which, z_ref in enumerate((zk_ref, zv_ref)):
        w1 = w1_ref[which]
        pb = jnp.dot(p_ref[which], w1, preferred_element_type=F32)
        bias = pb[0:1, :HEAD_DIM] + pb[1:2, HEAD_DIM:]
        rows2d = xbuf.at[slot, which].reshape(POOL_ROWS * GROUP_PITCH, HEAD_DIM)
        for h in range(N_KV):
            xh = jnp.concatenate([rows2d[pl.ds(N_KV * s + h, POOL_ROWS, stride=GROUP_PITCH), :]
                                  for s in range(CMP_STRIDE)], axis=1).astype(BF16)
            zz = jnp.dot(xh, w1, preferred_element_type=F32)
            z_ref[:, 2 * h * HEAD_DIM:(2 * h + 1) * HEAD_DIM] = zz[:, :HEAD_DIM] + bias
            z_ref[:, (2 * h + 1) * HEAD_DIM:(2 * h + 2) * HEAD_DIM] = zz[:, HEAD_DIM:]


def _compress_pool(pool_k, pool_v, cw):
    n_phys = pool_k.shape[0]
    rows = n_phys * ROWS_PER_PAGE
    assert rows % POOL_ROWS == 0
    w1, p, _ = cw
    full = lambda a: pl.BlockSpec(a.shape, lambda i: (0,) * a.ndim)
    groups = lambda a: a.reshape(rows, GROUP_ROWS, HEAD_DIM)
    zk, zv = pl.pallas_call(
        _compress_pool_kernel,
        out_shape=(jax.ShapeDtypeStruct((rows, ZW), F32),) * 2,
        grid=(rows // POOL_ROWS,),
        in_specs=[full(w1), full(p), pl.BlockSpec(memory_space=pl.ANY), pl.BlockSpec(memory_space=pl.ANY)],
        out_specs=(pl.BlockSpec((POOL_ROWS, ZW), lambda i: (i, 0)),) * 2,
        scratch_shapes=[pltpu.VMEM((2, 2, POOL_ROWS, GROUP_PITCH, HEAD_DIM), F32), pltpu.SemaphoreType.DMA((2, 2))],
        compiler_params=pltpu.CompilerParams(dimension_semantics=("arbitrary",), vmem_limit_bytes=VMEM_LIMIT),
        name="compress_pool",
    )(w1, p, groups(pool_k), groups(pool_v))
    return zk.reshape(n_phys, ROWS_PER_PAGE, ZW), zv.reshape(n_phys, ROWS_PER_PAGE, ZW)


def _nsa_sample_kernel(pt_ref, qraw_ref, qrot_ref, knew_ref, wnew_ref, gate_ref, w2_ref, wk_ref, wv_ref,
                       zk_hbm, zv_hbm, ks_hbm, vs_hbm, o_ref, zbuf, sbuf, sem, *, n_pages, past_len):
    b = pl.program_id(0)
    nb = pl.num_programs(0)
    slot = lax.rem(b, 2)
    page_rows = N_KV * PAGE_SIZE

    def copies(seq, sl):
        out = []
        for pg in range(n_pages):
            page = pt_ref[seq, pg]
            for which, (zsrc, ssrc) in enumerate(((zk_hbm, ks_hbm), (zv_hbm, vs_hbm))):
                out.append(pltpu.make_async_copy(
                    zsrc.at[page], zbuf.at[sl, which, pl.ds(pg * ROWS_PER_PAGE, ROWS_PER_PAGE)], sem.at[sl]))
                out.append(pltpu.make_async_copy(
                    ssrc.at[page], sbuf.at[sl, which, pl.ds(pg * page_rows, page_rows)], sem.at[sl]))
        return out

    @pl.when(b == 0)
    def _():
        for cp in copies(0, 0):
            cp.start()

    @pl.when(b + 1 < nb)
    def _():
        for cp in copies(b + 1, 1 - slot):
            cp.start()

    for cp in copies(b, slot):
        cp.wait()

    scale = HEAD_DIM ** -0.5
    nq = N_KV * GROUP
    r = n_pages * ROWS_PER_PAGE
    qpos = jnp.full((nq, 1), past_len, jnp.int32)
    n_sel_pad = HEAD_DIM
    sel_shift = SEL_LEN.bit_length() - 1
    row_head = lax.broadcasted_iota(jnp.int32, (nq, 1), 0) // GROUP
    per_head = lambda a0, a1: jnp.where(row_head == 0, a0, a1)

    def all_heads(ref):
        return jnp.concatenate([ref[:, j * HEAD_DIM:(j + 1) * HEAD_DIM] for j in range(nq)], axis=0).astype(BF16)

    def scores(q, k):
        return lax.dot_general(q, k, (((1,), (1,)), ((), ())), preferred_element_type=F32) * scale

    cmp = []
    for which in range(2):
        heads = []
        for h in range(N_KV):
            zh = zbuf[slot, which, :, 2 * h * HEAD_DIM:(2 * h + 2) * HEAD_DIM]
            pre = zh[:, :HEAD_DIM] + pltpu.roll(zh[:, HEAD_DIM:], r - 1, 0)
            mid = pre * jax.nn.sigmoid(pre)
            heads.append(jnp.dot(mid.astype(BF16), w2_ref[which], preferred_element_type=F32).astype(BF16))
        cmp.append(heads)
    q_raw = all_heads(qraw_ref)
    c_end = lax.broadcasted_iota(jnp.int32, (nq, r), 1) * CMP_STRIDE + (CMP_LEN - 1)
    cmask = c_end <= qpos
    s = per_head(scores(q_raw, cmp[0][0]), scores(q_raw, cmp[0][1])) + jnp.where(cmask, 0.0, NEG)
    p = jnp.exp(s - jnp.max(s, axis=-1, keepdims=True)) * jnp.where(cmask, 1.0, 0.0)
    p = p / jnp.maximum(jnp.sum(p, axis=-1, keepdims=True), 1e-30)
    pb = p.astype(BF16)
    o_cmp = per_head(jnp.dot(pb, cmp[1][0], preferred_element_type=F32),
                     jnp.dot(pb, cmp[1][1], preferred_element_type=F32))
    psum = per_head(jnp.sum(jnp.where(row_head == 0, p, 0.0), axis=0, keepdims=True),
                    jnp.sum(jnp.where(row_head == 1, p, 0.0), axis=0, keepdims=True))
    blk = lax.broadcasted_iota(jnp.int32, (nq, n_sel_pad), 1)
    sel = _top_rank_mask(_block_scores(_importance(psum, n_sel_pad), blk, qpos), 1,
                         past_len // SEL_LEN + 1)

    def attend(q, k, v, k_new, v_new, bias, new_bias):
        s = scores(q, k) + bias
        s_new = jnp.sum(q.astype(F32) * k_new.astype(BF16).astype(F32), axis=1, keepdims=True) * scale + new_bias
        m = jnp.maximum(jnp.max(s, axis=-1, keepdims=True), s_new)
        p = jnp.exp(s - m)
        p_new = jnp.exp(s_new - m)
        l = jnp.sum(p, axis=-1, keepdims=True) + p_new
        o = (jnp.dot(p.astype(BF16), v, preferred_element_type=F32)
             + p_new.astype(BF16).astype(F32) * v_new.astype(BF16).astype(F32))
        return o / jnp.maximum(l, 1e-30)

    def own_head(n_rows):
        col = lax.broadcasted_iota(jnp.int32, (nq, n_rows), 1)
        return jnp.bitwise_and(col, N_KV - 1) == row_head

    q_rot = all_heads(qrot_ref)
    n_rows = N_KV * past_len
    ej = lax.broadcasted_iota(jnp.int32, (n_sel_pad, n_rows), 0)
    et = lax.broadcasted_iota(jnp.int32, (n_sel_pad, n_rows), 1)
    expand = jnp.where(jnp.right_shift(et, sel_shift + 1) == ej, 1.0, 0.0).astype(BF16)
    selm = jnp.dot(sel.astype(BF16), expand, preferred_element_type=F32)
    sel_bias = jnp.where((selm > 0.5) & own_head(n_rows), 0.0, NEG)
    new_blk = past_len // SEL_LEN
    new_bias = jnp.where(sel[:, new_blk:new_blk + 1] > 0.5, 0.0, NEG)
    new_kv = lambda ref, c: per_head(ref[:, c * A_KV:c * A_KV + HEAD_DIM], ref[:, c * A_KV + HEAD_DIM:(c + 1) * A_KV])
    o_sel = attend(q_rot, sbuf[slot, 0].astype(BF16), sbuf[slot, 1].astype(BF16),
                   new_kv(knew_ref, 0), new_kv(knew_ref, 1), sel_bias, new_bias)
    win_bias = jnp.where(own_head(wk_ref.shape[0]), 0.0, NEG)
    o_win = attend(q_rot, wk_ref[...].astype(BF16), wv_ref[...].astype(BF16),
                   new_kv(wnew_ref, 0), new_kv(wnew_ref, 1), win_bias, 0.0)

    gts = jax.nn.sigmoid(gate_ref[...])
    for h in range(N_KV):
        for g in range(GROUP):
            j, c0 = h * GROUP + g, h * HEAD_DIM + 3 * g
            o = (gts[:, c0:c0 + 1] * o_cmp[j:j + 1] + gts[:, c0 + 1:c0 + 2] * o_sel[j:j + 1]
                 + gts[:, c0 + 2:c0 + 3] * o_win[j:j + 1])
            o_ref[:, j * HEAD_DIM:(j + 1) * HEAD_DIM] = o.astype(o_ref.dtype)


def _nsa_sample(zs, page_table, zk_pool, zv_pool, sel_k, sel_v, win_k, win_v, w2):
    bs, n_pages = page_table.shape
    past_len = n_pages * PAGE_SIZE
    wb = win_k.shape[1]
    assert wb <= WINDOW and past_len % SEL_LEN == 0 and past_len // SEL_LEN < HEAD_DIM
    assert N_KV == 2
    zs3 = zs.reshape(bs, 1, NZ)
    zspec = lambda w, off: pl.BlockSpec((None, 1, w), lambda b, pt: (b, 0, off // w))
    anyspec = pl.BlockSpec(memory_space=pl.ANY)
    wspec = pl.BlockSpec((None, N_KV * wb, HEAD_DIM), lambda b, pt: (b, 0, 0))
    rows2d = lambda a: a.reshape(a.shape[0], a.shape[1] * N_KV, HEAD_DIM)
    out = pl.pallas_call(
        functools.partial(_nsa_sample_kernel, n_pages=n_pages, past_len=past_len),
        out_shape=jax.ShapeDtypeStruct((bs, 1, A_Q), BF16),
        grid_spec=pltpu.PrefetchScalarGridSpec(
            num_scalar_prefetch=1, grid=(bs,),
            in_specs=[zspec(A_Q, Z_QRAW), zspec(A_Q, Z_QROT), zspec(2 * A_KV, Z_KS), zspec(2 * A_KV, Z_KW),
                      zspec(2 * HEAD_DIM, Z_GNSA), pl.BlockSpec(w2.shape, lambda b, pt: (0, 0, 0)), wspec, wspec,
                      anyspec, anyspec, anyspec, anyspec],
            out_specs=pl.BlockSpec((None, 1, A_Q), lambda b, pt: (b, 0, 0)),
            scratch_shapes=[pltpu.VMEM((2, 2, n_pages * ROWS_PER_PAGE, ZW), F32),
                            pltpu.VMEM((2, 2, N_KV * past_len, HEAD_DIM), F32), pltpu.SemaphoreType.DMA((2,))]),
        compiler_params=pltpu.CompilerParams(dimension_semantics=("arbitrary",), vmem_limit_bytes=VMEM_LIMIT),
        name="nsa_sample",
    )(page_table, zs3, zs3, zs3, zs3, zs3, w2, rows2d(win_k), rows2d(win_v), zk_pool, zv_pool,
      rows2d(sel_k), rows2d(sel_v))
    return out.reshape(bs, A_Q)


def _mlstm_sample_kernel(q_ref, k_ref, v_ref, op_ref, g_ref, gain_ref, c_ref, n_ref, m_ref,
                         h_ref, co_ref, no_ref, mo_ref):
    eye = (lax.broadcasted_iota(jnp.int32, (DQK_B, DQK_B), 0) == lax.broadcasted_iota(jnp.int32, (DQK_B, DQK_B), 1))
    col = lambda row: jnp.sum(jnp.where(eye, row, 0.0), axis=1, keepdims=True)
    lane = lax.broadcasted_iota(jnp.int32, (1, N_HEADS_B), 1)
    m_out = jnp.zeros((1, N_HEADS_B), F32)
    for h in range(N_HEADS_B):
        q = q_ref[:, h * DQK_B:(h + 1) * DQK_B]
        k = k_ref[:, h * DQK_B:(h + 1) * DQK_B] * DQK_B ** -0.5
        v = v_ref[:, h * DV_B:(h + 1) * DV_B]
        log_i, log_f = _gate_logs(g_ref[:, h:h + 1], g_ref[:, N_HEADS_B + h:N_HEADS_B + h + 1])
        c_old, n_old, m_old = c_ref[h], n_ref[h:h + 1, :], m_ref[:, h:h + 1]
        inter = log_f + m_old
        m_new = jnp.maximum(inter, log_i)
        decay = jnp.exp(inter - m_new)
        w = jnp.exp(log_i - m_new)
        co_ref[h] = decay * c_old + col(w * k) * v
        no_ref[h:h + 1, :] = decay * n_old + w * k
        m_out = jnp.where(lane == h, m_new, m_out)
        w_intra = w * jnp.sum(q * k, axis=1, keepdims=True)
        num = decay * jnp.sum(c_old * col(q), axis=0, keepdims=True) + w_intra * v
        den = decay * jnp.sum(q * n_old, axis=1, keepdims=True) + w_intra
        hh = num / jnp.maximum(jnp.abs(den), jnp.exp(-m_new))
        out = _rms(hh, gain_ref[h]) * jax.nn.sigmoid(op_ref[:, h * DV_B:(h + 1) * DV_B])
        h_ref[:, h * DV_B:(h + 1) * DV_B] = out.astype(h_ref.dtype)
    mo_ref[...] = m_out


def _mlstm_sample(zs, gain, c0, n0, m0):
    bs, H = zs.shape[0], N_HEADS_B
    zs3 = zs.reshape(bs, 1, NZ)
    zspec = lambda w, off: pl.BlockSpec((None, 1, w), lambda b: (b, 0, off // w))
    cspec = pl.BlockSpec((None, H, DQK_B, DV_B), lambda b: (b, 0, 0, 0))
    nspec = pl.BlockSpec((None, H, DQK_B), lambda b: (b, 0, 0))
    mspec = pl.BlockSpec((None, 1, H), lambda b: (b, 0, 0))
    h, c, n, m = pl.pallas_call(
        _mlstm_sample_kernel,
        out_shape=(jax.ShapeDtypeStruct((bs, 1, B_V), BF16), jax.ShapeDtypeStruct(c0.shape, F32),
                   jax.ShapeDtypeStruct(n0.shape, F32), jax.ShapeDtypeStruct((bs, 1, H), F32)),
        grid=(bs,),
        in_specs=[zspec(B_QK, Z_QB), zspec(B_QK, Z_KB), zspec(B_V, Z_VB), zspec(B_V, Z_OB), zspec(HEAD_DIM, Z_IB),
                  pl.BlockSpec((H, 1, DV_B), lambda b: (0, 0, 0)), cspec, nspec, mspec],
        out_specs=(pl.BlockSpec((None, 1, B_V), lambda b: (b, 0, 0)), cspec, nspec, mspec),
        compiler_params=pltpu.CompilerParams(dimension_semantics=("parallel",), vmem_limit_bytes=VMEM_LIMIT),
        name="mlstm_sample",
    )(zs3, zs3, zs3, zs3, zs3, gain.reshape(H, 1, DV_B), c0, n0, m0.reshape(bs, 1, H))
    return h.reshape(bs, B_V), c, n, m.reshape(bs, H)


def _split_kv(z, B, T):
    z = z.reshape(B, T, NZ)
    kv = lambda o: z[..., o:o + A_KV].reshape(B, T, N_KV, HEAD_DIM)
    return kv(Z_KC), kv(Z_VC), kv(Z_KS), kv(Z_VS), kv(Z_KW), kv(Z_VW)


def kernel(x_prompt, x_sample, cache_cmp_k, cache_cmp_v, cache_sel_k, cache_sel_v, cache_win_k, cache_win_v,
           state_mlstm_c, state_mlstm_n, state_mlstm_m, page_table, norm_mix, w_in, b_in, cmp_pos, cmp_w1, cmp_w2,
           mlstm_norm, w_up_a, w_up_b, w_out, norm_ffn, w_router_grp, b_router_grp, w_router_exp, b_router_exp,
           w_gate, w_up, w_down, norm_final):
    assert w_in.shape[0] == 1, "single layer"
    Bp, Tp = x_prompt.shape[:2]
    Bs, Ts = x_sample.shape[:2]
    assert Ts == 1
    n_p, n_s = Bp * Tp, Bs * Ts
    past_len = page_table.shape[1] * PAGE_SIZE
    pos_p = jnp.arange(Tp, dtype=jnp.int32)
    pos_s = past_len + jnp.arange(Ts, dtype=jnp.int32)
    l = 0

    w_perm, b_perm = _permute_in_proj(w_in[l], b_in[l])
    cw = _compress_weights(cmp_pos[l], cmp_w1[l], cmp_w2[l])
    mw = _merge_weights(w_up_a[l], w_up_b[l], w_out[l], norm_ffn[l], w_router_grp[l], b_router_grp[l],
                        w_router_exp[l], b_router_exp[l])

    xp2d = x_prompt.reshape(n_p, D_MODEL)
    zp = _project(xp2d, jnp.tile(pos_p, Bp), norm_mix[l], w_perm, b_perm, 2048)
    kc, vc, ks, vs, kw, vw = _split_kv(zp, Bp, Tp)
    o_a = _nsa_prompt(zp, _compress_prompt(kc, vc, cw), Bp, Tp)
    h_b, c_p, n_p_state, m_p = _mlstm_prompt(zp, mlstm_norm[l], Bp, Tp)
    bufs = _merge(xp2d, o_a, h_b, zp, mw, n_p + n_s, 0, 256)
    wl = min(WINDOW, Tp)
    prompt_new = (kc, vc, ks, vs, kw[:, -wl:], vw[:, -wl:], c_p, n_p_state, m_p)

    xs2d = x_sample.reshape(n_s, D_MODEL)
    zs = _project(xs2d, jnp.repeat(pos_s, Bs), norm_mix[l], w_perm, b_perm, 128)
    kc, vc, ks, vs, kw, vw = _split_kv(zs, Bs, Ts)
    assert (past_len + Ts - CMP_LEN) // CMP_STRIDE + 1 == past_len // CMP_STRIDE - 1
    zk_pool, zv_pool = _compress_pool(cache_cmp_k[l], cache_cmp_v[l], cw)
    o_a_s = _nsa_sample(zs, page_table, zk_pool, zv_pool, cache_sel_k[l], cache_sel_v[l],
                        cache_win_k[l], cache_win_v[l], cw[2])
    h_s, c_s, n_s_state, m_s = _mlstm_sample(zs, mlstm_norm[l], state_mlstm_c[l], state_mlstm_n[l],
                                             state_mlstm_m[l])
    x2, xn, route = _merge(xs2d, o_a_s, h_s, zs, mw, n_p + n_s, n_p, 128, bufs=bufs)
    wk = jnp.concatenate([cache_win_k[l], kw], axis=1)
    wv = jnp.concatenate([cache_win_v[l], vw], axis=1)
    wl = min(WINDOW, past_len + Ts)
    sample_new = (kc, vc, ks, vs, wk[:, -wl:], wv[:, -wl:], c_s, n_s_state, m_s)

    y_p, y_s = _moe(x2, xn, route, w_gate[l], w_up[l], w_down[l], norm_final, n_p)
    return ((y_p.reshape(Bp, Tp, D_MODEL), y_s.reshape(Bs, Ts, D_MODEL))
            + tuple(a[None] for a in prompt_new) + tuple(a[None] for a in sample_new))
```

```python
import functools

import numpy as np
import jax
import jax.numpy as jnp
from jax import lax
from jax.experimental import pallas as pl
from jax.experimental.pallas import tpu as pltpu

D_MODEL = 2048
PAGE_SIZE = 128
N_HEADS_A = 8
N_KV = 2
GROUP = N_HEADS_A // N_KV
HEAD_DIM = 128
ROT_DIM = HEAD_DIM // 4
ROPE_THETA = 500000.0
CMP_LEN = 32
CMP_STRIDE = 16
SEL_LEN = 64
SEL_TOP = 16
FORCE_BONUS = 100.0
WINDOW = 512
Q_BLOCK = 128
N_HEADS_B = 4
DQK_B = 128
DV_B = 256
MLSTM_CHUNK = 64
GATE_CAP = 15.0
N_GROUPS = 4
EXPERTS_PER_GROUP = 8
N_EXPERTS = N_GROUPS * EXPERTS_PER_GROUP
TOP_K = 2
D_EXPERT = 1024
EPS = 1e-6
NEG = -1e30

A_Q = N_HEADS_A * HEAD_DIM
A_KV = N_KV * HEAD_DIM
B_QK = N_HEADS_B * DQK_B
B_V = N_HEADS_B * DV_B
IN_SPLITS = (A_Q, A_KV, A_KV, A_KV, A_KV, A_KV, A_KV, 3 * N_HEADS_A,
             B_QK, B_QK, B_V, N_HEADS_B, N_HEADS_B, B_V, D_MODEL, D_MODEL)

F32 = jnp.float32
BF16 = jnp.bfloat16
VMEM_LIMIT = 48 * 1024 * 1024
VMEM_LIMIT_MERGE = 56 * 1024 * 1024

_CUTS = np.concatenate([[0], np.cumsum(IN_SPLITS)]).tolist()
_SRC = dict(zip(('q_a', 'k_c', 'v_c', 'k_s', 'v_s', 'k_w', 'v_w', 'g_nsa', 'q_b', 'k_b', 'v_b', 'i_b', 'f_b',
                 'o_b', 'g_ma', 'g_mb'), zip(_CUTS[:-1], _CUTS[1:])))
_Z_ORDER = ('g_ma', 'g_mb', 'q_a', 'q_a', 'v_b', 'o_b', 'k_c', 'v_c', 'k_s', 'v_s', 'k_w', 'v_w', 'q_b', 'k_b')
PROJ_TN = 512
Z_GMA, Z_GMB, Z_QRAW, Z_QROT, Z_VB, Z_OB = 0, 2048, 4096, 5120, 6144, 7168
Z_KC, Z_VC, Z_KS, Z_VS, Z_KW, Z_VW = 8192, 8448, 8704, 8960, 9216, 9472
Z_QB, Z_KB, Z_SMALL = 9728, 10240, 10752
N_GATE = 3 * GROUP
Z_GNSA, Z_IB, Z_FB = Z_SMALL, Z_SMALL + 2 * HEAD_DIM, Z_SMALL + 2 * HEAD_DIM + N_HEADS_B
NZ = 11264
_ROPE_ALL_TILES = (Z_QROT // PROJ_TN, Z_QROT // PROJ_TN + 1)
_ROPE_HALF_TILES = (Z_KS // PROJ_TN, Z_KW // PROJ_TN)

ROUTE_W = 128
MOE_TILE = 768
MOE_SUB = 256
MOE_FC = 512


def _rms(x, g):
    return x * lax.rsqrt(jnp.mean(x * x, axis=-1, keepdims=True) + EPS) * g


def _proj_kernel(x_ref, g_ref, w_ref, b_ref, cos_ref, sa_ref, sb_ref, z_ref, xn_ref):
    j = pl.program_id(1)

    @pl.when(j == 0)
    def _():
        xn_ref[...] = _rms(x_ref[...], g_ref[...]).astype(BF16)

    acc = jnp.dot(xn_ref[...], w_ref[...], preferred_element_type=F32) + b_ref[...]

    def rope(blk):
        return (blk * cos_ref[...] + pltpu.roll(blk, HEAD_DIM - ROT_DIM // 2, 1) * sa_ref[...]
                + pltpu.roll(blk, ROT_DIM // 2, 1) * sb_ref[...])

    is_all = (j == _ROPE_ALL_TILES[0]) | (j == _ROPE_ALL_TILES[1])
    is_half = (j == _ROPE_HALF_TILES[0]) | (j == _ROPE_HALF_TILES[1])
    n_blk = PROJ_TN // HEAD_DIM

    @pl.when(is_all)
    def _():
        for c in range(n_blk):
            z_ref[:, c * HEAD_DIM:(c + 1) * HEAD_DIM] = rope(acc[:, c * HEAD_DIM:(c + 1) * HEAD_DIM])

    @pl.when(is_half)
    def _():
        for c in range(n_blk // 2):
            z_ref[:, c * HEAD_DIM:(c + 1) * HEAD_DIM] = rope(acc[:, c * HEAD_DIM:(c + 1) * HEAD_DIM])
        z_ref[:, PROJ_TN // 2:] = acc[:, PROJ_TN // 2:]

    @pl.when(jnp.logical_not(is_all | is_half))
    def _():
        z_ref[...] = acc


def _rope_tables(pos):
    half = ROT_DIM // 2
    inv = ROPE_THETA ** (-jnp.arange(half, dtype=F32) / half)
    ang = pos.astype(F32)[:, None] * inv[None, :]
    cos, sin = jnp.cos(ang), jnp.sin(ang)
    n = pos.shape[0]
    ones = jnp.ones((n, HEAD_DIM - ROT_DIM), F32)
    zeros = jnp.zeros((n, HEAD_DIM - half), F32)
    cos_t = jnp.concatenate([cos, cos, ones], axis=1)
    sa_t = jnp.concatenate([-sin, zeros], axis=1)
    sb_t = jnp.concatenate([jnp.zeros((n, half), F32), sin, jnp.zeros((n, HEAD_DIM - ROT_DIM), F32)], axis=1)
    return cos_t, sa_t, sb_t


def _permute_in_proj(w_in, b_in):
    g0 = _SRC['g_nsa'][0]
    spans = [_SRC[k] for k in _Z_ORDER]
    spans += [(g0, g0 + N_GATE), HEAD_DIM - N_GATE, (g0 + N_GATE, g0 + 2 * N_GATE), HEAD_DIM - N_GATE,
              _SRC['i_b'], _SRC['f_b'], NZ - Z_FB - N_HEADS_B]

    def permute(a, dtype):
        parts = [jnp.zeros((a.shape[0], s), dtype) if isinstance(s, int) else a[:, s[0]:s[1]].astype(dtype)
                 for s in spans]
        return jnp.concatenate(parts, axis=1)

    return permute(w_in, BF16), permute(b_in[None, :], F32)


def _project(x2d, pos_rows, norm_g, w_perm, b_perm, tm):
    m = x2d.shape[0]
    cos_t, sa_t, sb_t = _rope_tables(pos_rows)
    rows = lambda w: pl.BlockSpec((tm, w), lambda i, j: (i, 0), pipeline_mode=pl.Buffered(1))
    return pl.pallas_call(
        _proj_kernel,
        out_shape=jax.ShapeDtypeStruct((m, NZ), F32),
        grid=(m // tm, NZ // PROJ_TN),
        in_specs=[rows(D_MODEL),
                  pl.BlockSpec((1, D_MODEL), lambda i, j: (0, 0)),
                  pl.BlockSpec((D_MODEL, PROJ_TN), lambda i, j: (0, j)),
                  pl.BlockSpec((1, PROJ_TN), lambda i, j: (0, j)),
                  rows(HEAD_DIM), rows(HEAD_DIM), rows(HEAD_DIM)],
        out_specs=pl.BlockSpec((tm, PROJ_TN), lambda i, j: (i, j)),
        scratch_shapes=[pltpu.VMEM((tm, D_MODEL), BF16)],
        compiler_params=pltpu.CompilerParams(dimension_semantics=("parallel", "arbitrary"),
                                             vmem_limit_bytes=VMEM_LIMIT),
        name="proj",
    )(x2d, norm_g.reshape(1, D_MODEL), w_perm, b_perm, cos_t, sa_t, sb_t)


def _compress_body(x_ref, w1, pb, w2, r):
    bias = pb[0:1, :HEAD_DIM] + pb[1:2, HEAD_DIM:]
    outs = []
    for h in range(N_KV):
        xh = jnp.concatenate([x_ref[:, (N_KV * s + h) * HEAD_DIM:(N_KV * s + h + 1) * HEAD_DIM]
                              for s in range(CMP_STRIDE)], axis=1).astype(BF16)
        zz = jnp.dot(xh, w1, preferred_element_type=F32)
        pre = zz[:, :HEAD_DIM] + pltpu.roll(zz[:, HEAD_DIM:], r - 1, 0) + bias
        mid = pre * jax.nn.sigmoid(pre)
        outs.append(jnp.dot(mid.astype(BF16), w2, preferred_element_type=F32))
    return outs


def _compress_kernel(xk_ref, xv_ref, w1_ref, p_ref, w2_ref, o_ref):
    r = xk_ref.shape[0]
    for which, x_ref in enumerate((xk_ref, xv_ref)):
        w1 = w1_ref[which]
        pb = jnp.dot(p_ref[which], w1, preferred_element_type=F32)
        for h, o in enumerate(_compress_body(x_ref, w1, pb, w2_ref[which], r)):
            o_ref[which, h] = o


def _compress_weights(cmp_pos, cmp_w1, cmp_w2):
    assert CMP_LEN == 2 * CMP_STRIDE
    half = CMP_STRIDE * HEAD_DIM
    w1 = jnp.concatenate([cmp_w1[:, :half], cmp_w1[:, half:]], axis=2).astype(BF16)
    p = cmp_pos.reshape(2, 2, half)
    p = jnp.concatenate([p, jnp.zeros((2, 6, half), p.dtype)], axis=1).astype(BF16)
    return w1, p, cmp_w2.astype(BF16)


def _compress_prompt(kc, vc, cw):
    B, T = kc.shape[:2]
    r = T // CMP_STRIDE
    width = CMP_STRIDE * N_KV * HEAD_DIM
    w1, p, w2 = cw
    full = lambda a: pl.BlockSpec(a.shape, lambda b: (0,) * a.ndim)
    xspec = pl.BlockSpec((None, r, width), lambda b: (b, 0, 0))
    return pl.pallas_call(
        _compress_kernel,
        out_shape=jax.ShapeDtypeStruct((B, 2, N_KV, r, HEAD_DIM), F32),
        grid=(B,),
        in_specs=[xspec, xspec, full(w1), full(p), full(w2)],
        out_specs=pl.BlockSpec((None, 2, N_KV, r, HEAD_DIM), lambda b: (b, 0, 0, 0, 0)),
        compiler_params=pltpu.CompilerParams(dimension_semantics=("parallel",), vmem_limit_bytes=VMEM_LIMIT),
        name="compress_prompt",
    )(kc.reshape(B, r, width), vc.reshape(B, r, width), w1, p, w2)


def _group_scores(q, k, tq):
    s = lax.dot_general(q, k, (((1,), (1,)), ((), ())), preferred_element_type=F32) * HEAD_DIM ** -0.5
    return s.reshape(GROUP, tq, k.shape[0])


def _stack_heads(ref):
    return jnp.concatenate([ref[:, g * HEAD_DIM:(g + 1) * HEAD_DIM] for g in range(GROUP)], axis=0).astype(BF16)


def _importance(psum, n_sel):
    r = psum.shape[1]
    ci = lax.broadcasted_iota(jnp.int32, (r, n_sel), 0) * CMP_STRIDE
    cj = lax.broadcasted_iota(jnp.int32, (r, n_sel), 1) * SEL_LEN
    cover_t = ((ci < cj + SEL_LEN) & (ci + (CMP_LEN - 1) >= cj)).astype(BF16)
    p_hi = psum.astype(BF16)
    p_lo = (psum - p_hi.astype(F32)).astype(BF16)
    return jnp.dot(p_hi, cover_t, preferred_element_type=F32) + jnp.dot(p_lo, cover_t, preferred_element_type=F32)


def _block_scores(imp, blk, qpos):
    cur = jnp.right_shift(qpos, SEL_LEN.bit_length() - 1)
    forced = (blk == 0) | (blk == cur) | (blk == cur - 1)
    return jnp.where(blk * SEL_LEN <= qpos, imp + jnp.where(forced, FORCE_BONUS, 0.0), -1.0)


def _top_rank_mask(score, axis, n_cand):
    idx = lax.broadcasted_iota(jnp.int32, score.shape, axis)
    rank = jnp.zeros(score.shape, F32)
    for j in range(n_cand):
        cand = score[j:j + 1, :] if axis == 0 else score[:, j:j + 1]
        rank = rank + jnp.where((cand > score) | ((cand == score) & (idx > j)), 1.0, 0.0)
    return jnp.where(rank < float(SEL_TOP), 1.0, 0.0)


def _nsa_prompt_kernel(qraw_ref, qrot_ref, ks_ref, vs_ref, kw_ref, vw_ref, kc_ref, vc_ref, gate_ref, o_ref,
                       m_ref, l_ref, acc_ref, *, tq, tk, seq_len):
    q0 = pl.program_id(2) * tq
    r = kc_ref.shape[0]
    n_sel = seq_len // SEL_LEN
    sel_shift = SEL_LEN.bit_length() - 1
    qpos = q0 + lax.broadcasted_iota(jnp.int32, (tq, 1), 0)

    s = _group_scores(_stack_heads(qraw_ref), kc_ref[...].astype(BF16), tq)
    c_end = lax.broadcasted_iota(jnp.int32, (tq, r), 1) * CMP_STRIDE + (CMP_LEN - 1)
    cmask = c_end <= qpos
    s = s + jnp.where(cmask, 0.0, NEG)[None]
    p = jnp.exp(s - jnp.max(s, axis=-1, keepdims=True)) * jnp.where(cmask, 1.0, 0.0)[None]
    p = p / jnp.maximum(jnp.sum(p, axis=-1, keepdims=True), 1e-30)
    o_cmp = jnp.dot(p.reshape(GROUP * tq, r).astype(BF16), vc_ref[...].astype(BF16), preferred_element_type=F32)
    psum = p[0]
    for g in range(1, GROUP):
        psum = psum + p[g]
    imp_t = _importance(psum, HEAD_DIM).T[:n_sel]
    blk_t = lax.broadcasted_iota(jnp.int32, (n_sel, tq), 0)
    qpos_t = q0 + lax.broadcasted_iota(jnp.int32, (n_sel, tq), 1)
    sel_b = _top_rank_mask(_block_scores(imp_t, blk_t, qpos_t), 0, n_sel).astype(BF16)

    q_rot = _stack_heads(qrot_ref)
    m_ref[...] = jnp.full(m_ref.shape, NEG, F32)
    l_ref[...] = jnp.zeros(l_ref.shape, F32)
    acc_ref[...] = jnp.zeros(acc_ref.shape, F32)

    def sel_tile(kt, carry):
        k0 = pl.multiple_of(kt * tk, tk)
        k = ks_ref[pl.ds(k0, tk), :].astype(BF16)
        v = vs_ref[pl.ds(k0, tk), :].astype(BF16)
        s2 = _group_scores(q_rot, k, tq)
        ej = lax.broadcasted_iota(jnp.int32, (n_sel, tk), 0)
        et = k0 + lax.broadcasted_iota(jnp.int32, (n_sel, tk), 1)
        expand = jnp.where(jnp.right_shift(et, sel_shift) == ej, 1.0, 0.0).astype(BF16)
        selm = lax.dot_general(sel_b, expand, (((0,), (0,)), ((), ())), preferred_element_type=F32)
        kpos = k0 + lax.broadcasted_iota(jnp.int32, (tq, tk), 1)
        ok = (selm > 0.5) & (kpos <= qpos)
        s2 = s2 + jnp.where(ok, 0.0, NEG)[None]
        m_prev = m_ref[...]
        m_new = jnp.maximum(m_prev, jnp.max(s2, axis=-1, keepdims=True))
        alpha = jnp.exp(m_prev - m_new)
        p2 = jnp.exp(s2 - m_new)
        l_ref[...] = alpha * l_ref[...] + jnp.sum(p2, axis=-1, keepdims=True)
        pv = jnp.dot(p2.reshape(GROUP * tq, tk).astype(BF16), v, preferred_element_type=F32)
        acc_ref[...] = alpha.reshape(GROUP * tq, 1) * acc_ref[...] + pv
        m_ref[...] = m_new
        return carry

    lax.fori_loop(0, lax.div(q0 + tq + tk - 1, tk), sel_tile, 0)
    o_sel = acc_ref[...] / jnp.maximum(l_ref[...].reshape(GROUP * tq, 1), 1e-30)

    span = WINDOW + tq
    w0 = pl.multiple_of(jnp.maximum(q0 - WINDOW, 0), tq)
    s3 = _group_scores(q_rot, kw_ref[pl.ds(w0, span), :].astype(BF16), tq)
    kp = w0 + lax.broadcasted_iota(jnp.int32, (tq, span), 1)
    okw = (kp <= qpos) & (kp >= qpos - WINDOW)
    s3 = s3 + jnp.where(okw, 0.0, NEG)[None]
    p3 = jnp.exp(s3 - jnp.max(s3, axis=-1, keepdims=True))
    p3 = p3 / jnp.sum(p3, axis=-1, keepdims=True)
    o_win = jnp.dot(p3.reshape(GROUP * tq, span).astype(BF16), vw_ref[pl.ds(w0, span), :].astype(BF16),
                    preferred_element_type=F32)

    gts = jax.nn.sigmoid(gate_ref[...])
    for g in range(GROUP):
        rows = slice(g * tq, (g + 1) * tq)
        o_ref[:, g * HEAD_DIM:(g + 1) * HEAD_DIM] = (gts[:, 3 * g:3 * g + 1] * o_cmp[rows]
                                                     + gts[:, 3 * g + 1:3 * g + 2] * o_sel[rows]
                                                     + gts[:, 3 * g + 2:3 * g + 3] * o_win[rows]).astype(o_ref.dtype)


def _nsa_prompt(z, cmp, B, T, tq=256, tk=512):
    assert T % tk == 0 and tk % tq == 0 and tq % HEAD_DIM == 0 and T >= WINDOW + tq
    assert SEL_LEN & (SEL_LEN - 1) == 0 and tk % SEL_LEN == 0
    assert (T // SEL_LEN) % 8 == 0 and T // SEL_LEN <= HEAD_DIM
    nq = T // tq
    r = cmp.shape[3]
    gw = GROUP * HEAD_DIM
    qspec = lambda off: pl.BlockSpec((tq, gw), lambda b, h, q: (b * nq + q, off // gw + h))
    kvspec = lambda off: pl.BlockSpec((T, HEAD_DIM), lambda b, h, q: (b, off // HEAD_DIM + h))
    cspec = lambda which: pl.BlockSpec((None, None, None, r, HEAD_DIM), lambda b, h, q: (b, which, h, 0, 0))
    return pl.pallas_call(
        functools.partial(_nsa_prompt_kernel, tq=tq, tk=tk, seq_len=T),
        out_shape=jax.ShapeDtypeStruct((B * T, A_Q), BF16),
        grid=(B, N_KV, nq),
        in_specs=[qspec(Z_QRAW), qspec(Z_QROT), kvspec(Z_KS), kvspec(Z_VS), kvspec(Z_KW), kvspec(Z_VW),
                  cspec(0), cspec(1),
                  pl.BlockSpec((tq, HEAD_DIM), lambda b, h, q: (b * nq + q, Z_GNSA // HEAD_DIM + h))],
        out_specs=pl.BlockSpec((tq, gw), lambda b, h, q: (b * nq + q, h)),
        scratch_shapes=[pltpu.VMEM((GROUP, tq, 1), F32), pltpu.VMEM((GROUP, tq, 1), F32),
                        pltpu.VMEM((GROUP * tq, HEAD_DIM), F32)],
        compiler_params=pltpu.CompilerParams(dimension_semantics=("parallel", "parallel", "arbitrary"),
                                             vmem_limit_bytes=VMEM_LIMIT),
        name="nsa_prompt",
    )(z, z, z, z, z, z, cmp, cmp, z)


def _gate_logs(i_pre, f_pre):
    log_i = GATE_CAP * jnp.tanh(i_pre / GATE_CAP)
    fc = GATE_CAP * jnp.tanh(f_pre / GATE_CAP)
    log_f = jnp.minimum(fc, 0.0) - jnp.log1p(jnp.exp(-jnp.abs(fc)))
    return log_i, log_f


def _mlstm_prompt_kernel(q_ref, k_ref, v_ref, op_ref, grow_ref, gcol_ref, gain_ref,
                         h_ref, c_ref, n_ref, m_ref):
    @pl.when(pl.program_id(1) == 0)
    def _():
        c_ref[...] = jnp.zeros(c_ref.shape, F32)
        n_ref[...] = jnp.zeros(n_ref.shape, F32)
        m_ref[...] = jnp.zeros(m_ref.shape, F32)

    for h in range(N_HEADS_B):
        qk_cols = slice(h * DQK_B, (h + 1) * DQK_B)
        v_cols = slice(h * DV_B, (h + 1) * DV_B)
        _mlstm_chunk(q_ref[:, qk_cols], k_ref[:, qk_cols], v_ref[:, v_cols], op_ref[:, v_cols],
                     grow_ref[h:h + 1, :], grow_ref[N_HEADS_B + h:N_HEADS_B + h + 1, :],
                     gcol_ref[:, h:h + 1], gcol_ref[:, N_HEADS_B + h:N_HEADS_B + h + 1], gain_ref[h],
                     h_ref.at[:, v_cols], c_ref.at[h], n_ref.at[h], m_ref.at[h])


def _mlstm_chunk(q, k, v, o_pre, i_row, f_row, i_col, f_col, gain, h_ref, c_ref, n_ref, m_ref):
    L = q.shape[0]
    qb = q.astype(BF16)
    k = k * DQK_B ** -0.5
    vb = v.astype(BF16)
    i_r, f_r = _gate_logs(i_row, f_row)
    i_c, f_c = _gate_logs(i_col, f_col)
    li = lax.broadcasted_iota(jnp.int32, (L, L), 0)
    si = lax.broadcasted_iota(jnp.int32, (L, L), 1)
    tri = si <= li
    bcum_c = jnp.sum(jnp.where(tri, f_r, 0.0), axis=1, keepdims=True)
    bcum_r = jnp.sum(jnp.where(li <= si, f_c, 0.0), axis=0, keepdims=True)
    btot = jnp.sum(f_r, axis=1, keepdims=True)
    c_old, n_old, m_old = c_ref[...], n_ref[...], m_ref[0:1, 0:1]

    dlog = jnp.where(tri, bcum_c - bcum_r + i_r, NEG)
    inter = bcum_c + m_old
    m_t = jnp.maximum(inter, jnp.max(dlog, axis=1, keepdims=True))
    qk = lax.dot_general(qb, k.astype(BF16), (((1,), (1,)), ((), ())), preferred_element_type=F32)
    w_intra = jnp.exp(dlog - m_t) * qk
    w_inter = jnp.exp(inter - m_t)
    num = (w_inter * jnp.dot(qb, c_old.astype(BF16), preferred_element_type=F32)
           + jnp.dot(w_intra.astype(BF16), vb, preferred_element_type=F32))
    den = w_inter * jnp.sum(q * n_old, axis=1, keepdims=True) + jnp.sum(w_intra, axis=1, keepdims=True)
    h = num / jnp.maximum(jnp.abs(den), jnp.exp(-m_t))
    h_ref[...] = (_rms(h, gain) * jax.nn.sigmoid(o_pre)).astype(h_ref.dtype)

    wlog_c = btot - bcum_c + i_c
    m_new = jnp.maximum(btot + m_old, jnp.max(btot - bcum_r + i_r, axis=1, keepdims=True))
    decay = jnp.exp(btot + m_old - m_new)
    kw = k * jnp.exp(wlog_c - m_new)
    c_ref[...] = decay * c_old + lax.dot_general(kw.astype(BF16), vb, (((0,), (0,)), ((), ())),
                                                 preferred_element_type=F32)
    n_ref[...] = decay * n_old + jnp.sum(kw, axis=0, keepdims=True)
    m_ref[...] = jnp.broadcast_to(m_new, m_ref.shape)


def _mlstm_prompt(z, gain, B, T):
    L = MLSTM_CHUNK
    assert T % L == 0
    nc, H = T // L, N_HEADS_B
    gcol = z[:, Z_IB:Z_IB + 2 * H].reshape(B * nc, L, 2 * H)
    grow = gcol.transpose(0, 2, 1)
    rows = lambda w, off: pl.BlockSpec((L, w), lambda b, c: (b * nc + c, off // w))
    gidx = lambda b, c: (b * nc + c, 0, 0)
    state = lambda *s: pl.BlockSpec((None, H) + s, lambda b, c: (b, 0, 0, 0))
    h, c, n, m = pl.pallas_call(
        _mlstm_prompt_kernel,
        out_shape=(jax.ShapeDtypeStruct((B * T, B_V), BF16), jax.ShapeDtypeStruct((B, H, DQK_B, DV_B), F32),
                   jax.ShapeDtypeStruct((B, H, 1, DQK_B), F32), jax.ShapeDtypeStruct((B, H, 1, DQK_B), F32)),
        grid=(B, nc),
        in_specs=[rows(B_QK, Z_QB), rows(B_QK, Z_KB), rows(B_V, Z_VB), rows(B_V, Z_OB),
                  pl.BlockSpec((None, 2 * H, L), gidx), pl.BlockSpec((None, L, 2 * H), gidx),
                  pl.BlockSpec((H, 1, DV_B), lambda b, c: (0, 0, 0))],
        out_specs=(rows(B_V, 0), state(DQK_B, DV_B), state(1, DQK_B), state(1, DQK_B)),
        compiler_params=pltpu.CompilerParams(dimension_semantics=("parallel", "arbitrary"),
                                             vmem_limit_bytes=VMEM_LIMIT),
        name="mlstm_prompt",
    )(z, z, z, z, grow, gcol, gain.reshape(H, 1, DV_B))
    return h, c, n[:, :, 0], m[:, :, 0, 0]


def _merge_kernel(x_ref, oa_ref, hb_ref, gma_ref, gmb_ref, wa_ref, wb_ref, wo_ref, nf_ref, wrh_ref, wrl_ref, br_ref,
                  *refs):
    x2_ref, xn_ref, route_ref = refs[-3:]
    ua = jnp.dot(oa_ref[...], wa_ref[...], preferred_element_type=F32)
    ub = jnp.dot(hb_ref[...], wb_ref[...], preferred_element_type=F32)
    merged = jax.nn.sigmoid(gma_ref[...]) * ua + jax.nn.sigmoid(gmb_ref[...]) * ub
    x2 = x_ref[...] + jnp.dot(merged.astype(BF16), wo_ref[...], preferred_element_type=F32)
    x2_ref[...] = x2
    xn = _rms(x2, nf_ref[...])
    xn_ref[...] = xn
    hi = xn.astype(BF16)
    lo = (xn - hi.astype(F32)).astype(BF16)
    lg = (jnp.dot(hi, wrh_ref[...], preferred_element_type=F32) + jnp.dot(lo, wrh_ref[...], preferred_element_type=F32)
          + jnp.dot(hi, wrl_ref[...], preferred_element_type=F32)) + br_ref[...]
    lane = lax.broadcasted_iota(jnp.int32, lg.shape, 1)
    lanef = lane.astype(F32)
    ninf = -jnp.inf
    gmask = (lane >= N_EXPERTS) & (lane < N_EXPERTS + N_GROUPS)
    gl = jnp.where(gmask, lg, ninf)
    gmax = jnp.max(gl, axis=1, keepdims=True)
    gidx = jnp.min(jnp.where(gl == gmax, lanef, 1e9), axis=1, keepdims=True) - N_EXPERTS
    gprob = 1.0 / jnp.sum(jnp.where(gmask, jnp.exp(lg - gmax), 0.0), axis=1, keepdims=True)
    e_lo = gidx * EXPERTS_PER_GROUP
    el = jnp.where((lanef >= e_lo) & (lanef < e_lo + EXPERTS_PER_GROUP), lg, ninf)
    v1 = jnp.max(el, axis=1, keepdims=True)
    i1 = jnp.min(jnp.where(el == v1, lanef, 1e9), axis=1, keepdims=True)
    el2 = jnp.where(lanef == i1, ninf, el)
    v2 = jnp.max(el2, axis=1, keepdims=True)
    i2 = jnp.min(jnp.where(el2 == v2, lanef, 1e9), axis=1, keepdims=True)
    e2 = jnp.exp(v2 - v1)
    g1 = gprob / (1.0 + e2)
    g2 = gprob * e2 / (1.0 + e2)
    route_ref[...] = jnp.where(lane == 0, i1, jnp.where(lane == 1, i2, jnp.where(lane == 2, g1,
                               jnp.where(lane == 3, g2, 0.0))))


def _merge_weights(w_up_a, w_up_b, w_out, norm_ffn, w_rg, b_rg, w_re, b_re):
    assert TOP_K == 2
    pad = ROUTE_W - N_EXPERTS - N_GROUPS
    w = jnp.concatenate([w_re, w_rg, jnp.zeros((D_MODEL, pad), F32)], axis=1)
    b = jnp.concatenate([b_re, b_rg, jnp.zeros((pad,), F32)]).reshape(1, ROUTE_W)
    hi = w.astype(BF16)
    return (w_up_a.astype(BF16), w_up_b.astype(BF16), w_out.astype(BF16), norm_ffn.reshape(1, D_MODEL).astype(F32),
            hi, (w - hi.astype(F32)).astype(BF16), b)


def _merge(x2d, o_a, h_b, z, mw, n_total, row0, tm, bufs=None):
    m = x2d.shape[0]
    i0 = row0 // tm
    row = lambda w, c=0: pl.BlockSpec((tm, w), lambda i: (i, c))
    const = lambda a: pl.BlockSpec(a.shape, lambda i: (0,) * a.ndim, pipeline_mode=pl.Buffered(1))
    outw = (D_MODEL, D_MODEL, ROUTE_W)
    in_specs = [row(D_MODEL), row(A_Q), row(B_V), row(D_MODEL, Z_GMA // D_MODEL), row(D_MODEL, Z_GMB // D_MODEL)]
    in_specs += [const(a) for a in mw]
    args = [x2d, o_a, h_b, z, z] + list(mw)
    aliases = {}
    if bufs is not None:
        in_specs += [pl.BlockSpec(memory_space=pl.ANY)] * 3
        aliases = {len(args) + k: k for k in range(3)}
        args += list(bufs)
    return pl.pallas_call(
        _merge_kernel,
        out_shape=tuple(jax.ShapeDtypeStruct((n_total, w), F32) for w in outw),
        grid=(m // tm,),
        in_specs=in_specs,
        out_specs=tuple(pl.BlockSpec((tm, w), lambda i: (i0 + i, 0)) for w in outw),
        input_output_aliases=aliases,
        compiler_params=pltpu.CompilerParams(dimension_semantics=("parallel",), vmem_limit_bytes=VMEM_LIMIT_MERGE),
        name="merge_route",
    )(*args)


def _moe_plan(route, n_tiles):
    n = route.shape[0]
    a = n * TOP_K
    flat_e = route[:, 0:TOP_K].astype(jnp.int32).reshape(a)
    onehot = (flat_e[:, None] == jnp.arange(N_EXPERTS, dtype=jnp.int32)[None, :]).astype(jnp.int32)
    rank = jnp.take_along_axis(jnp.cumsum(onehot, axis=0) - onehot, flat_e[:, None], axis=1)[:, 0]
    counts = jnp.sum(onehot, axis=0)
    ntile = (counts + MOE_TILE - 1) // MOE_TILE
    tile_end = jnp.cumsum(ntile)
    dest = ((tile_end - ntile)[flat_e] * MOE_TILE + rank).astype(jnp.int32)
    n_used = tile_end[-1].astype(jnp.int32)
    t = jnp.minimum(jnp.arange(n_tiles, dtype=jnp.int32), n_used - 1)
    tile_expert = jnp.minimum(jnp.searchsorted(tile_end, t, side='right'), N_EXPERTS - 1).astype(jnp.int32)
    first = (tile_end - ntile)[tile_expert]
    tile_rows = jnp.clip(counts[tile_expert] - (t - first) * MOE_TILE, 0, MOE_TILE)
    tile_rows = jnp.where(jnp.arange(n_tiles) < n_used, tile_rows, 0).astype(jnp.int32)
    return dest, tile_expert, tile_rows, n_used.reshape(1)


DMA_UNROLL = 8


def _dispatch_kernel(dest_ref, x_ref, o_hbm, sem):
    i = pl.program_id(0)
    tm = x_ref.shape[0]

    def copy(r, k):
        d = dest_ref[(i * tm + r) * TOP_K + k]
        return pltpu.make_async_copy(x_ref.at[pl.ds(r, 1)], o_hbm.at[pl.ds(d, 1)], sem)

    def start(r, c):
        for k in range(TOP_K):
            copy(r, k).start()
        return c

    def wait(r, c):
        for k in range(TOP_K):
            copy(r, k).wait()
        return c

    lax.fori_loop(0, tm, start, 0, unroll=DMA_UNROLL)
    lax.fori_loop(0, tm, wait, 0, unroll=DMA_UNROLL)


def _moe_dispatch(xn, dest, n_tiles, tm=128):
    n, d = xn.shape
    return pl.pallas_call(
        _dispatch_kernel,
        out_shape=jax.ShapeDtypeStruct((n_tiles * MOE_TILE, d), xn.dtype),
        grid_spec=pltpu.PrefetchScalarGridSpec(
            num_scalar_prefetch=1, grid=(n // tm,),
            in_specs=[pl.BlockSpec((tm, d), lambda i, ds: (i, 0))],
            out_specs=pl.BlockSpec(memory_space=pl.ANY),
            scratch_shapes=[pltpu.SemaphoreType.DMA(())]),
        compiler_params=pltpu.CompilerParams(dimension_semantics=("arbitrary",), vmem_limit_bytes=VMEM_LIMIT),
        name="moe_dispatch",
    )(dest, xn)


def _expert_kernel(te_ref, cnt_ref, nu_ref, x_ref, wg_ref, wu_ref, wd_ref, o_ref):
    t, j = pl.program_id(0), pl.program_id(1)
    cnt = cnt_ref[t]

    @pl.when(cnt > 0)
    def _():
        wg, wu, wd = wg_ref[...].astype(BF16), wu_ref[...].astype(BF16), wd_ref[...].astype(BF16)

        def run(rows):
            xb = x_ref[rows].astype(BF16)
            g = jnp.dot(xb, wg, preferred_element_type=F32)
            u = jnp.dot(xb, wu, preferred_element_type=F32)
            h = (g * jax.nn.sigmoid(g) * u).astype(BF16)
            y = jnp.dot(h, wd, preferred_element_type=F32)

            @pl.when(j == 0)
            def _():
                o_ref[rows] = y

            @pl.when(j > 0)
            def _():
                o_ref[rows] += y

        for s in range(MOE_TILE // MOE_SUB):
            lo, half = s * MOE_SUB, MOE_SUB // 2

            @pl.when(cnt > lo + half)
            def _():
                run(slice(lo, lo + MOE_SUB))

            @pl.when((cnt > lo) & (cnt <= lo + half))
            def _():
                run(slice(lo, lo + half))


def _moe_experts(xs, tile_expert, tile_rows, n_used, w_gate, w_up, w_down):
    n_tiles = xs.shape[0] // MOE_TILE
    nj = D_EXPERT // MOE_FC
    tix = lambda t, j, te, cnt, nu: (jnp.minimum(t, nu[0] - 1), 0)
    jj = lambda t, j, nu: jnp.where(t < nu[0], j, nj - 1)
    return pl.pallas_call(
        _expert_kernel,
        out_shape=jax.ShapeDtypeStruct(xs.shape, F32),
        grid_spec=pltpu.PrefetchScalarGridSpec(
            num_scalar_prefetch=3, grid=(n_tiles, nj),
            in_specs=[pl.BlockSpec((MOE_TILE, D_MODEL), tix),
                      pl.BlockSpec((None, D_MODEL, MOE_FC), lambda t, j, te, cnt, nu: (te[t], 0, jj(t, j, nu))),
                      pl.BlockSpec((None, D_MODEL, MOE_FC), lambda t, j, te, cnt, nu: (te[t], 0, jj(t, j, nu))),
                      pl.BlockSpec((None, MOE_FC, D_MODEL), lambda t, j, te, cnt, nu: (te[t], jj(t, j, nu), 0))],
            out_specs=pl.BlockSpec((MOE_TILE, D_MODEL), tix)),
        compiler_params=pltpu.CompilerParams(dimension_semantics=("arbitrary", "arbitrary"),
                                             vmem_limit_bytes=VMEM_LIMIT_MERGE),
        name="moe_experts",
    )(tile_expert, tile_rows, n_used, xs, w_gate, w_up, w_down)


def _combine_kernel(dest_ref, x2_ref, route_ref, g_ref, y_hbm, op_ref, os_ref, buf_ref, sem, *, n_prompt_tiles):
    i = pl.program_id(0)
    tm = x2_ref.shape[0]

    def copy(r, k):
        d = dest_ref[(i * tm + r) * TOP_K + k]
        return pltpu.make_async_copy(y_hbm.at[pl.ds(d, 1)], buf_ref.at[k, pl.ds(r, 1)], sem)

    def start(r, c):
        for k in range(TOP_K):
            copy(r, k).start()
        return c

    def wait(r, c):
        for k in range(TOP_K):
            copy(r, k).wait()
        return c

    lax.fori_loop(0, tm, start, 0, unroll=DMA_UNROLL)
    lax.fori_loop(0, tm, wait, 0, unroll=DMA_UNROLL)
    ffn = buf_ref[0] * route_ref[:, TOP_K:TOP_K + 1]
    for k in range(1, TOP_K):
        ffn = ffn + buf_ref[k] * route_ref[:, TOP_K + k:TOP_K + k + 1]
    y = _rms(x2_ref[...] + ffn, g_ref[...])

    @pl.when(i < n_prompt_tiles)
    def _():
        op_ref[...] = y

    @pl.when(i >= n_prompt_tiles)
    def _():
        os_ref[...] = y


def _moe_combine(x2, route, dest, ys, norm_final, n_prompt, tm=128):
    n, d = x2.shape
    npt = n_prompt // tm
    nst = (n - n_prompt) // tm
    return pl.pallas_call(
        functools.partial(_combine_kernel, n_prompt_tiles=npt),
        out_shape=(jax.ShapeDtypeStruct((n_prompt, d), F32), jax.ShapeDtypeStruct((n - n_prompt, d), F32)),
        grid_spec=pltpu.PrefetchScalarGridSpec(
            num_scalar_prefetch=1, grid=(n // tm,),
            in_specs=[pl.BlockSpec((tm, d), lambda i, ds: (i, 0)), pl.BlockSpec((tm, ROUTE_W), lambda i, ds: (i, 0)),
                      pl.BlockSpec((1, d), lambda i, ds: (0, 0)), pl.BlockSpec(memory_space=pl.ANY)],
            out_specs=(pl.BlockSpec((tm, d), lambda i, ds: (jnp.minimum(i, npt - 1), 0)),
                       pl.BlockSpec((tm, d), lambda i, ds: (jnp.clip(i - npt, 0, nst - 1), 0))),
            scratch_shapes=[pltpu.VMEM((TOP_K, tm, d), F32), pltpu.SemaphoreType.DMA(())]),
        compiler_params=pltpu.CompilerParams(dimension_semantics=("arbitrary",), vmem_limit_bytes=VMEM_LIMIT),
        name="moe_combine",
    )(dest, x2, route, norm_final.reshape(1, d), ys)


def _moe(x2, xn, route, w_gate, w_up, w_down, norm_final, n_prompt):
    n = x2.shape[0]
    n_tiles = (n * TOP_K + N_EXPERTS * (MOE_TILE - 1) + MOE_TILE - 1) // MOE_TILE
    dest, tile_expert, tile_rows, n_used = _moe_plan(route, n_tiles)
    xs = _moe_dispatch(xn, dest, n_tiles)
    ys = _moe_experts(xs, tile_expert, tile_rows, n_used, w_gate, w_up, w_down)
    return _moe_combine(x2, route, dest, ys, norm_final, n_prompt)


POOL_ROWS = 256
ROWS_PER_PAGE = PAGE_SIZE // CMP_STRIDE
ZW = 2 * N_KV * HEAD_DIM


GROUP_ROWS = N_KV * CMP_STRIDE
GROUP_PITCH = GROUP_ROWS + 8


def _compress_pool_kernel(w1_ref, p_ref, xk_hbm, xv_hbm, zk_ref, zv_ref, xbuf, sem):
    i = pl.program_id(0)
    n = pl.num_programs(0)
    slot = lax.rem(i, 2)

    def copies(step, sl):
        return [pltpu.make_async_copy(src.at[pl.ds(step * POOL_ROWS, POOL_ROWS)],
                                      xbuf.at[sl, which, :, pl.ds(0, GROUP_ROWS), :], sem.at[sl, which])
                for which, src in enumerate((xk_hbm, xv_hbm))]

    @pl.when(i == 0)
    def _():
        for cp in copies(0, 0):
            cp.start()

    @pl.when(i + 1 < n)
    def _():
        for cp in copies(i + 1, 1 - slot):
            cp.start()

    for cp in copies(i, slot):
        cp.wait()

    for which, z_ref in enumerate((zk_ref, zv_ref)):
        w1 = w1_ref[which]
        pb = jnp.dot(p_ref[which], w1, preferred_element_type=F32)
        bias = pb[0:1, :HEAD_DIM] + pb[1:2, HEAD_DIM:]
        rows2d = xbuf.at[slot, which].reshape(POOL_ROWS * GROUP_PITCH, HEAD_DIM)
        for h in range(N_KV):
            xh = jnp.concatenate([rows2d[pl.ds(N_KV * s + h, POOL_ROWS, stride=GROUP_PITCH), :]
                                  for s in range(CMP_STRIDE)], axis=1).astype(BF16)
            zz = jnp.dot(xh, w1, preferred_element_type=F32)
            z_ref[:, 2 * h * HEAD_DIM:(2 * h + 1) * HEAD_DIM] = zz[:, :HEAD_DIM] + bias
            z_ref[:, (2 * h + 1) * HEAD_DIM:(2 * h + 2) * HEAD_DIM] = zz[:, HEAD_DIM:]


def _compress_pool(pool_k, pool_v, cw):
    n_phys = pool_k.shape[0]
    rows = n_phys * ROWS_PER_PAGE
    assert rows % POOL_ROWS == 0
    w1, p, _ = cw
    full = lambda a: pl.BlockSpec(a.shape, lambda i: (0,) * a.ndim)
    groups = lambda a: a.reshape(rows, GROUP_ROWS, HEAD_DIM)
    zk, zv = pl.pallas_call(
        _compress_pool_kernel,
        out_shape=(jax.ShapeDtypeStruct((rows, ZW), F32),) * 2,
        grid=(rows // POOL_ROWS,),
        in_specs=[full(w1), full(p), pl.BlockSpec(memory_space=pl.ANY), pl.BlockSpec(memory_space=pl.ANY)],
        out_specs=(pl.BlockSpec((POOL_ROWS, ZW), lambda i: (i, 0)),) * 2,
        scratch_shapes=[pltpu.VMEM((2, 2, POOL_ROWS, GROUP_PITCH, HEAD_DIM), F32), pltpu.SemaphoreType.DMA((2, 2))],
        compiler_params=pltpu.CompilerParams(dimension_semantics=("arbitrary",), vmem_limit_bytes=VMEM_LIMIT),
        name="compress_pool",
    )(w1, p, groups(pool_k), groups(pool_v))
    return zk.reshape(n_phys, ROWS_PER_PAGE, ZW), zv.reshape(n_phys, ROWS_PER_PAGE, ZW)


def _nsa_sample_kernel(pt_ref, qraw_ref, qrot_ref, knew_ref, wnew_ref, gate_ref, w2_ref, wk_ref, wv_ref,
                       zk_hbm, zv_hbm, ks_hbm, vs_hbm, o_ref, zbuf, sbuf, sem, *, n_pages, past_len):
    b = pl.program_id(0)
    nb = pl.num_programs(0)
    slot = lax.rem(b, 2)
    page_rows = N_KV * PAGE_SIZE

    def copies(seq, sl):
        out = []
        for pg in range(n_pages):
            page = pt_ref[seq, pg]
            for which, (zsrc, ssrc) in enumerate(((zk_hbm, ks_hbm), (zv_hbm, vs_hbm))):
                out.append(pltpu.make_async_copy(
                    zsrc.at[page], zbuf.at[sl, which, pl.ds(pg * ROWS_PER_PAGE, ROWS_PER_PAGE)], sem.at[sl]))
                out.append(pltpu.make_async_copy(
                    ssrc.at[page], sbuf.at[sl, which, pl.ds(pg * page_rows, page_rows)], sem.at[sl]))
        return out

    @pl.when(b == 0)
    def _():
        for cp in copies(0, 0):
            cp.start()

    @pl.when(b + 1 < nb)
    def _():
        for cp in copies(b + 1, 1 - slot):
            cp.start()

    for cp in copies(b, slot):
        cp.wait()

    scale = HEAD_DIM ** -0.5
    nq = N_KV * GROUP
    r = n_pages * ROWS_PER_PAGE
    qpos = jnp.full((nq, 1), past_len, jnp.int32)
    n_sel_pad = HEAD_DIM
    sel_shift = SEL_LEN.bit_length() - 1
    row_head = lax.broadcasted_iota(jnp.int32, (nq, 1), 0) // GROUP
    per_head = lambda a0, a1: jnp.where(row_head == 0, a0, a1)

    def all_heads(ref):
        return jnp.concatenate([ref[:, j * HEAD_DIM:(j + 1) * HEAD_DIM] for j in range(nq)], axis=0).astype(BF16)

    def scores(q, k):
        return lax.dot_general(q, k, (((1,), (1,)), ((), ())), preferred_element_type=F32) * scale

    cmp = []
    for which in range(2):
        heads = []
        for h in range(N_KV):
            zh = zbuf[slot, which, :, 2 * h * HEAD_DIM:(2 * h + 2) * HEAD_DIM]
            pre = zh[:, :HEAD_DIM] + pltpu.roll(zh[:, HEAD_DIM:], r - 1, 0)
            mid = pre * jax.nn.sigmoid(pre)
            heads.append(jnp.dot(mid.astype(BF16), w2_ref[which], preferred_element_type=F32).astype(BF16))
        cmp.append(heads)
    q_raw = all_heads(qraw_ref)
    c_end = lax.broadcasted_iota(jnp.int32, (nq, r), 1) * CMP_STRIDE + (CMP_LEN - 1)
    cmask = c_end <= qpos
    s = per_head(scores(q_raw, cmp[0][0]), scores(q_raw, cmp[0][1])) + jnp.where(cmask, 0.0, NEG)
    p = jnp.exp(s - jnp.max(s, axis=-1, keepdims=True)) * jnp.where(cmask, 1.0, 0.0)
    p = p / jnp.maximum(jnp.sum(p, axis=-1, keepdims=True), 1e-30)
    pb = p.astype(BF16)
    o_cmp = per_head(jnp.dot(pb, cmp[1][0], preferred_element_type=F32),
                     jnp.dot(pb, cmp[1][1], preferred_element_type=F32))
    psum = per_head(jnp.sum(jnp.where(row_head == 0, p, 0.0), axis=0, keepdims=True),
                    jnp.sum(jnp.where(row_head == 1, p, 0.0), axis=0, keepdims=True))
    blk = lax.broadcasted_iota(jnp.int32, (nq, n_sel_pad), 1)
    sel = _top_rank_mask(_block_scores(_importance(psum, n_sel_pad), blk, qpos), 1,
                         past_len // SEL_LEN + 1)

    def attend(q, k, v, k_new, v_new, bias, new_bias):
        s = scores(q, k) + bias
        s_new = jnp.sum(q.astype(F32) * k_new.astype(BF16).astype(F32), axis=1, keepdims=True) * scale + new_bias
        m = jnp.maximum(jnp.max(s, axis=-1, keepdims=True), s_new)
        p = jnp.exp(s - m)
        p_new = jnp.exp(s_new - m)
        l = jnp.sum(p, axis=-1, keepdims=True) + p_new
        o = (jnp.dot(p.astype(BF16), v, preferred_element_type=F32)
             + p_new.astype(BF16).astype(F32) * v_new.astype(BF16).astype(F32))
        return o / jnp.maximum(l, 1e-30)

    def own_head(n_rows):
        col = lax.broadcasted_iota(jnp.int32, (nq, n_rows), 1)
        return jnp.bitwise_and(col, N_KV - 1) == row_head

    q_rot = all_heads(qrot_ref)
    n_rows = N_KV * past_len
    ej = lax.broadcasted_iota(jnp.int32, (n_sel_pad, n_rows), 0)
    et = lax.broadcasted_iota(jnp.int32, (n_sel_pad, n_rows), 1)
    expand = jnp.where(jnp.right_shift(et, sel_shift + 1) == ej, 1.0, 0.0).astype(BF16)
    selm = jnp.dot(sel.astype(BF16), expand, preferred_element_type=F32)
    sel_bias = jnp.where((selm > 0.5) & own_head(n_rows), 0.0, NEG)
    new_blk = past_len // SEL_LEN
    new_bias = jnp.where(sel[:, new_blk:new_blk + 1] > 0.5, 0.0, NEG)
    new_kv = lambda ref, c: per_head(ref[:, c * A_KV:c * A_KV + HEAD_DIM], ref[:, c * A_KV + HEAD_DIM:(c + 1) * A_KV])
    o_sel = attend(q_rot, sbuf[slot, 0].astype(BF16), sbuf[slot, 1].astype(BF16),
                   new_kv(knew_ref, 0), new_kv(knew_ref, 1), sel_bias, new_bias)
    win_bias = jnp.where(own_head(wk_ref.shape[0]), 0.0, NEG)
    o_win = attend(q_rot, wk_ref[...].astype(BF16), wv_ref[...].astype(BF16),
                   new_kv(wnew_ref, 0), new_kv(wnew_ref, 1), win_bias, 0.0)

    gts = jax.nn.sigmoid(gate_ref[...])
    for h in range(N_KV):
        for g in range(GROUP):
            j, c0 = h * GROUP + g, h * HEAD_DIM + 3 * g
            o = (gts[:, c0:c0 + 1] * o_cmp[j:j + 1] + gts[:, c0 + 1:c0 + 2] * o_sel[j:j + 1]
                 + gts[:, c0 + 2:c0 + 3] * o_win[j:j + 1])
            o_ref[:, j * HEAD_DIM:(j + 1) * HEAD_DIM] = o.astype(o_ref.dtype)


def _nsa_sample(zs, page_table, zk_pool, zv_pool, sel_k, sel_v, win_k, win_v, w2):
    bs, n_pages = page_table.shape
    past_len = n_pages * PAGE_SIZE
    wb = win_k.shape[1]
    assert wb <= WINDOW and past_len % SEL_LEN == 0 and past_len // SEL_LEN < HEAD_DIM
    assert N_KV == 2
    zs3 = zs.reshape(bs, 1, NZ)
    zspec = lambda w, off: pl.BlockSpec((None, 1, w), lambda b, pt: (b, 0, off // w))
    anyspec = pl.BlockSpec(memory_space=pl.ANY)
    wspec = pl.BlockSpec((None, N_KV * wb, HEAD_DIM), lambda b, pt: (b, 0, 0))
    rows2d = lambda a: a.reshape(a.shape[0], a.shape[1] * N_KV, HEAD_DIM)
    out = pl.pallas_call(
        functools.partial(_nsa_sample_kernel, n_pages=n_pages, past_len=past_len),
        out_shape=jax.ShapeDtypeStruct((bs, 1, A_Q), BF16),
        grid_spec=pltpu.PrefetchScalarGridSpec(
            num_scalar_prefetch=1, grid=(bs,),
            in_specs=[zspec(A_Q, Z_QRAW), zspec(A_Q, Z_QROT), zspec(2 * A_KV, Z_KS), zspec(2 * A_KV, Z_KW),
                      zspec(2 * HEAD_DIM, Z_GNSA), pl.BlockSpec(w2.shape, lambda b, pt: (0, 0, 0)), wspec, wspec,
                      anyspec, anyspec, anyspec, anyspec],
            out_specs=pl.BlockSpec((None, 1, A_Q), lambda b, pt: (b, 0, 0)),
            scratch_shapes=[pltpu.VMEM((2, 2, n_pages * ROWS_PER_PAGE, ZW), F32),
                            pltpu.VMEM((2, 2, N_KV * past_len, HEAD_DIM), F32), pltpu.SemaphoreType.DMA((2,))]),
        compiler_params=pltpu.CompilerParams(dimension_semantics=("arbitrary",), vmem_limit_bytes=VMEM_LIMIT),
        name="nsa_sample",
    )(page_table, zs3, zs3, zs3, zs3, zs3, w2, rows2d(win_k), rows2d(win_v), zk_pool, zv_pool,
      rows2d(sel_k), rows2d(sel_v))
    return out.reshape(bs, A_Q)


def _mlstm_sample_kernel(q_ref, k_ref, v_ref, op_ref, g_ref, gain_ref, c_ref, n_ref, m_ref,
                         h_ref, co_ref, no_ref, mo_ref):
    eye = (lax.broadcasted_iota(jnp.int32, (DQK_B, DQK_B), 0) == lax.broadcasted_iota(jnp.int32, (DQK_B, DQK_B), 1))
    col = lambda row: jnp.sum(jnp.where(eye, row, 0.0), axis=1, keepdims=True)
    lane = lax.broadcasted_iota(jnp.int32, (1, N_HEADS_B), 1)
    m_out = jnp.zeros((1, N_HEADS_B), F32)
    for h in range(N_HEADS_B):
        q = q_ref[:, h * DQK_B:(h + 1) * DQK_B]
        k = k_ref[:, h * DQK_B:(h + 1) * DQK_B] * DQK_B ** -0.5
        v = v_ref[:, h * DV_B:(h + 1) * DV_B]
        log_i, log_f = _gate_logs(g_ref[:, h:h + 1], g_ref[:, N_HEADS_B + h:N_HEADS_B + h + 1])
        c_old, n_old, m_old = c_ref[h], n_ref[h:h + 1, :], m_ref[:, h:h + 1]
        inter = log_f + m_old
        m_new = jnp.maximum(inter, log_i)
        decay = jnp.exp(inter - m_new)
        w = jnp.exp(log_i - m_new)
        co_ref[h] = decay * c_old + col(w * k) * v
        no_ref[h:h + 1, :] = decay * n_old + w * k
        m_out = jnp.where(lane == h, m_new, m_out)
        w_intra = w * jnp.sum(q * k, axis=1, keepdims=True)
        num = decay * jnp.sum(c_old * col(q), axis=0, keepdims=True) + w_intra * v
        den = decay * jnp.sum(q * n_old, axis=1, keepdims=True) + w_intra
        hh = num / jnp.maximum(jnp.abs(den), jnp.exp(-m_new))
        out = _rms(hh, gain_ref[h]) * jax.nn.sigmoid(op_ref[:, h * DV_B:(h + 1) * DV_B])
        h_ref[:, h * DV_B:(h + 1) * DV_B] = out.astype(h_ref.dtype)
    mo_ref[...] = m_out


def _mlstm_sample(zs, gain, c0, n0, m0):
    bs, H = zs.shape[0], N_HEADS_B
    zs3 = zs.reshape(bs, 1, NZ)
    zspec = lambda w, off: pl.BlockSpec((None, 1, w), lambda b: (b, 0, off // w))
    cspec = pl.BlockSpec((None, H, DQK_B, DV_B), lambda b: (b, 0, 0, 0))
    nspec = pl.BlockSpec((None, H, DQK_B), lambda b: (b, 0, 0))
    mspec = pl.BlockSpec((None, 1, H), lambda b: (b, 0, 0))
    h, c, n, m = pl.pallas_call(
        _mlstm_sample_kernel,
        out_shape=(jax.ShapeDtypeStruct((bs, 1, B_V), BF16), jax.ShapeDtypeStruct(c0.shape, F32),
                   jax.ShapeDtypeStruct(n0.shape, F32), jax.ShapeDtypeStruct((bs, 1, H), F32)),
        grid=(bs,),
        in_specs=[zspec(B_QK, Z_QB), zspec(B_QK, Z_KB), zspec(B_V, Z_VB), zspec(B_V, Z_OB), zspec(HEAD_DIM, Z_IB),
                  pl.BlockSpec((H, 1, DV_B), lambda b: (0, 0, 0)), cspec, nspec, mspec],
        out_specs=(pl.BlockSpec((None, 1, B_V), lambda b: (b, 0, 0)), cspec, nspec, mspec),
        compiler_params=pltpu.CompilerParams(dimension_semantics=("parallel",), vmem_limit_bytes=VMEM_LIMIT),
        name="mlstm_sample",
    )(zs3, zs3, zs3, zs3, zs3, gain.reshape(H, 1, DV_B), c0, n0, m0.reshape(bs, 1, H))
    return h.reshape(bs, B_V), c, n, m.reshape(bs, H)


def _split_kv(z, B, T):
    z = z.reshape(B, T, NZ)
    kv = lambda o: z[..., o:o + A_KV].reshape(B, T, N_KV, HEAD_DIM)
    return kv(Z_KC), kv(Z_VC), kv(Z_KS), kv(Z_VS), kv(Z_KW), kv(Z_VW)


def kernel(x_prompt, x_sample, cache_cmp_k, cache_cmp_v, cache_sel_k, cache_sel_v, cache_win_k, cache_win_v,
           state_mlstm_c, state_mlstm_n, state_mlstm_m, page_table, norm_mix, w_in, b_in, cmp_pos, cmp_w1, cmp_w2,
           mlstm_norm, w_up_a, w_up_b, w_out, norm_ffn, w_router_grp, b_router_grp, w_router_exp, b_router_exp,
           w_gate, w_up, w_down, norm_final):
    assert w_in.shape[0] == 1, "single layer"
    Bp, Tp = x_prompt.shape[:2]
    Bs, Ts = x_sample.shape[:2]
    assert Ts == 1
    n_p, n_s = Bp * Tp, Bs * Ts
    past_len = page_table.shape[1] * PAGE_SIZE
    pos_p = jnp.arange(Tp, dtype=jnp.int32)
    pos_s = past_len + jnp.arange(Ts, dtype=jnp.int32)
    l = 0

    w_perm, b_perm = _permute_in_proj(w_in[l], b_in[l])
    cw = _compress_weights(cmp_pos[l], cmp_w1[l], cmp_w2[l])
    mw = _merge_weights(w_up_a[l], w_up_b[l], w_out[l], norm_ffn[l], w_router_grp[l], b_router_grp[l],
                        w_router_exp[l], b_router_exp[l])

    xp2d = x_prompt.reshape(n_p, D_MODEL)
    zp = _project(xp2d, jnp.tile(pos_p, Bp), norm_mix[l], w_perm, b_perm, 2048)
    kc, vc, ks, vs, kw, vw = _split_kv(zp, Bp, Tp)
    o_a = _nsa_prompt(zp, _compress_prompt(kc, vc, cw), Bp, Tp)
    h_b, c_p, n_p_state, m_p = _mlstm_prompt(zp, mlstm_norm[l], Bp, Tp)
    bufs = _merge(xp2d, o_a, h_b, zp, mw, n_p + n_s, 0, 256)
    wl = min(WINDOW, Tp)
    prompt_new = (kc, vc, ks, vs, kw[:, -wl:], vw[:, -wl:], c_p, n_p_state, m_p)

    xs2d = x_sample.reshape(n_s, D_MODEL)
    zs = _project(xs2d, jnp.repeat(pos_s, Bs), norm_mix[l], w_perm, b_perm, 128)
    kc, vc, ks, vs, kw, vw = _split_kv(zs, Bs, Ts)
    assert (past_len + Ts - CMP_LEN) // CMP_STRIDE + 1 == past_len // CMP_STRIDE - 1
    zk_pool, zv_pool = _compress_pool(cache_cmp_k[l], cache_cmp_v[l], cw)
    o_a_s = _nsa_sample(zs, page_table, zk_pool, zv_pool, cache_sel_k[l], cache_sel_v[l],
                        cache_win_k[l], cache_win_v[l], cw[2])
    h_s, c_s, n_s_state, m_s = _mlstm_sample(zs, mlstm_norm[l], state_mlstm_c[l], state_mlstm_n[l],
                                             state_mlstm_m[l])
    x2, xn, route = _merge(xs2d, o_a_s, h_s, zs, mw, n_p + n_s, n_p, 128, bufs=bufs)
    wk = jnp.concatenate([cache_win_k[l], kw], axis=1)
    wv = jnp.concatenate([cache_win_v[l], vw], axis=1)
    wl = min(WINDOW, past_len + Ts)
    sample_new = (kc, vc, ks, vs, wk[:, -wl:], wv[:, -wl:], c_s, n_s_state, m_s)

    y_p, y_s = _moe(x2, xn, route, w_gate[l], w_up[l], w_down[l], norm_final, n_p)
    return ((y_p.reshape(Bp, Tp, D_MODEL), y_s.reshape(Bs, Ts, D_MODEL))
            + tuple(a[None] for a in prompt_new) + tuple(a[None] for a in sample_new))
```

```python
import functools

import numpy as np
import jax
import jax.numpy as jnp
from jax import lax
from jax.experimental import pallas as pl
from jax.experimental.pallas import tpu as pltpu

D_MODEL = 2048
PAGE_SIZE = 128
N_HEADS_A = 8
N_KV = 2
GROUP = N_HEADS_A // N_KV
HEAD_DIM = 128
ROT_DIM = HEAD_DIM // 4
ROPE_THETA = 500000.0
CMP_LEN = 32
CMP_STRIDE = 16
SEL_LEN = 64
SEL_TOP = 16
FORCE_BONUS = 100.0
WINDOW = 512
Q_BLOCK = 128
N_HEADS_B = 4
DQK_B = 128
DV_B = 256
MLSTM_CHUNK = 64
GATE_CAP = 15.0
N_GROUPS = 4
EXPERTS_PER_GROUP = 8
N_EXPERTS = N_GROUPS * EXPERTS_PER_GROUP
TOP_K = 2
D_EXPERT = 1024
EPS = 1e-6
NEG = -1e30

A_Q = N_HEADS_A * HEAD_DIM
A_KV = N_KV * HEAD_DIM
B_QK = N_HEADS_B * DQK_B
B_V = N_HEADS_B * DV_B
IN_SPLITS = (A_Q, A_KV, A_KV, A_KV, A_KV, A_KV, A_KV, 3 * N_HEADS_A,
             B_QK, B_QK, B_V, N_HEADS_B, N_HEADS_B, B_V, D_MODEL, D_MODEL)

F32 = jnp.float32
BF16 = jnp.bfloat16
VMEM_LIMIT = 48 * 1024 * 1024
VMEM_LIMIT_MERGE = 56 * 1024 * 1024

_CUTS = np.concatenate([[0], np.cumsum(IN_SPLITS)]).tolist()
_SRC = dict(zip(('q_a', 'k_c', 'v_c', 'k_s', 'v_s', 'k_w', 'v_w', 'g_nsa', 'q_b', 'k_b', 'v_b', 'i_b', 'f_b',
                 'o_b', 'g_ma', 'g_mb'), zip(_CUTS[:-1], _CUTS[1:])))
_Z_ORDER = ('g_ma', 'g_mb', 'q_a', 'q_a', 'v_b', 'o_b', 'k_c', 'v_c', 'k_s', 'v_s', 'k_w', 'v_w', 'q_b', 'k_b')
PROJ_TN = 512
Z_GMA, Z_GMB, Z_QRAW, Z_QROT, Z_VB, Z_OB = 0, 2048, 4096, 5120, 6144, 7168
Z_KC, Z_VC, Z_KS, Z_VS, Z_KW, Z_VW = 8192, 8448, 8704, 8960, 9216, 9472
Z_QB, Z_KB, Z_SMALL = 9728, 10240, 10752
N_GATE = 3 * GROUP
Z_GNSA, Z_IB, Z_FB = Z_SMALL, Z_SMALL + 2 * HEAD_DIM, Z_SMALL + 2 * HEAD_DIM + N_HEADS_B
NZ = 11264
_ROPE_ALL_TILES = (Z_QROT // PROJ_TN, Z_QROT // PROJ_TN + 1)
_ROPE_HALF_TILES = (Z_KS // PROJ_TN, Z_KW // PROJ_TN)

ROUTE_W = 128
MOE_TILE = 768
MOE_SUB = 256
MOE_FC = 512


def _rms(x, g):
    return x * lax.rsqrt(jnp.mean(x * x, axis=-1, keepdims=True) + EPS) * g


def _proj_kernel(x_ref, g_ref, w_ref, b_ref, cos_ref, sa_ref, sb_ref, z_ref, xn_ref):
    j = pl.program_id(1)

    @pl.when(j == 0)
    def _():
        xn_ref[...] = _rms(x_ref[...], g_ref[...]).astype(BF16)

    acc = jnp.dot(xn_ref[...], w_ref[...], preferred_element_type=F32) + b_ref[...]

    def rope(blk):
        return (blk * cos_ref[...] + pltpu.roll(blk, HEAD_DIM - ROT_DIM // 2, 1) * sa_ref[...]
                + pltpu.roll(blk, ROT_DIM // 2, 1) * sb_ref[...])

    is_all = (j == _ROPE_ALL_TILES[0]) | (j == _ROPE_ALL_TILES[1])
    is_half = (j == _ROPE_HALF_TILES[0]) | (j == _ROPE_HALF_TILES[1])
    n_blk = PROJ_TN // HEAD_DIM

    @pl.when(is_all)
    def _():
        for c in range(n_blk):
            z_ref[:, c * HEAD_DIM:(c + 1) * HEAD_DIM] = rope(acc[:, c * HEAD_DIM:(c + 1) * HEAD_DIM])

    @pl.when(is_half)
    def _():
        for c in range(n_blk // 2):
            z_ref[:, c * HEAD_DIM:(c + 1) * HEAD_DIM] = rope(acc[:, c * HEAD_DIM:(c + 1) * HEAD_DIM])
        z_ref[:, PROJ_TN // 2:] = acc[:, PROJ_TN // 2:]

    @pl.when(jnp.logical_not(is_all | is_half))
    def _():
        z_ref[...] = acc


def _rope_tables(pos):
    half = ROT_DIM // 2
    inv = ROPE_THETA ** (-jnp.arange(half, dtype=F32) / half)
    ang = pos.astype(F32)[:, None] * inv[None, :]
    cos, sin = jnp.cos(ang), jnp.sin(ang)
    n = pos.shape[0]
    ones = jnp.ones((n, HEAD_DIM - ROT_DIM), F32)
    zeros = jnp.zeros((n, HEAD_DIM - half), F32)
    cos_t = jnp.concatenate([cos, cos, ones], axis=1)
    sa_t = jnp.concatenate([-sin, zeros], axis=1)
    sb_t = jnp.concatenate([jnp.zeros((n, half), F32), sin, jnp.zeros((n, HEAD_DIM - ROT_DIM), F32)], axis=1)
    return cos_t, sa_t, sb_t


def _permute_in_proj(w_in, b_in):
    g0 = _SRC['g_nsa'][0]
    spans = [_SRC[k] for k in _Z_ORDER]
    spans += [(g0, g0 + N_GATE), HEAD_DIM - N_GATE, (g0 + N_GATE, g0 + 2 * N_GATE), HEAD_DIM - N_GATE,
              _SRC['i_b'], _SRC['f_b'], NZ - Z_FB - N_HEADS_B]

    def permute(a, dtype):
        parts = [jnp.zeros((a.shape[0], s), dtype) if isinstance(s, int) else a[:, s[0]:s[1]].astype(dtype)
                 for s in spans]
        return jnp.concatenate(parts, axis=1)

    return permute(w_in, BF16), permute(b_in[None, :], F32)


def _project(x2d, pos_rows, norm_g, w_perm, b_perm, tm):
    m = x2d.shape[0]
    cos_t, sa_t, sb_t = _rope_tables(pos_rows)
    rows = lambda w: pl.BlockSpec((tm, w), lambda i, j: (i, 0), pipeline_mode=pl.Buffered(1))
    return pl.pallas_call(
        _proj_kernel,
        out_shape=jax.ShapeDtypeStruct((m, NZ), F32),
        grid=(m // tm, NZ // PROJ_TN),
        in_specs=[rows(D_MODEL),
                  pl.BlockSpec((1, D_MODEL), lambda i, j: (0, 0)),
                  pl.BlockSpec((D_MODEL, PROJ_TN), lambda i, j: (0, j)),
                  pl.BlockSpec((1, PROJ_TN), lambda i, j: (0, j)),
                  rows(HEAD_DIM), rows(HEAD_DIM), rows(HEAD_DIM)],
        out_specs=pl.BlockSpec((tm, PROJ_TN), lambda i, j: (i, j)),
        scratch_shapes=[pltpu.VMEM((tm, D_MODEL), BF16)],
        compiler_params=pltpu.CompilerParams(dimension_semantics=("parallel", "arbitrary"),
                                             vmem_limit_bytes=VMEM_LIMIT),
        name="proj",
    )(x2d, norm_g.reshape(1, D_MODEL), w_perm, b_perm, cos_t, sa_t, sb_t)


def _compress_body(x_ref, w1, pb, w2, r):
    bias = pb[0:1, :HEAD_DIM] + pb[1:2, HEAD_DIM:]
    outs = []
    for h in range(N_KV):
        xh = jnp.concatenate([x_ref[:, (N_KV * s + h) * HEAD_DIM:(N_KV * s + h + 1) * HEAD_DIM]
                              for s in range(CMP_STRIDE)], axis=1).astype(BF16)
        zz = jnp.dot(xh, w1, preferred_element_type=F32)
        pre = zz[:, :HEAD_DIM] + pltpu.roll(zz[:, HEAD_DIM:], r - 1, 0) + bias
        mid = pre * jax.nn.sigmoid(pre)
        outs.append(jnp.dot(mid.astype(BF16), w2, preferred_element_type=F32))
    return outs


def _compress_kernel(xk_ref, xv_ref, w1_ref, p_ref, w2_ref, o_ref):
    r = xk_ref.shape[0]
    for which, x_ref in enumerate((xk_ref, xv_ref)):
        w1 = w1_ref[which]
        pb = jnp.dot(p_ref[which], w1, preferred_element_type=F32)
        for h, o in enumerate(_compress_body(x_ref, w1, pb, w2_ref[which], r)):
            o_ref[which, h] = o


def _compress_weights(cmp_pos, cmp_w1, cmp_w2):
    assert CMP_LEN == 2 * CMP_STRIDE
    half = CMP_STRIDE * HEAD_DIM
    w1 = jnp.concatenate([cmp_w1[:, :half], cmp_w1[:, half:]], axis=2).astype(BF16)
    p = cmp_pos.reshape(2, 2, half)
    p = jnp.concatenate([p, jnp.zeros((2, 6, half), p.dtype)], axis=1).astype(BF16)
    return w1, p, cmp_w2.astype(BF16)


def _compress_prompt(kc, vc, cw):
    B, T = kc.shape[:2]
    r = T // CMP_STRIDE
    width = CMP_STRIDE * N_KV * HEAD_DIM
    w1, p, w2 = cw
    full = lambda a: pl.BlockSpec(a.shape, lambda b: (0,) * a.ndim)
    xspec = pl.BlockSpec((None, r, width), lambda b: (b, 0, 0))
    return pl.pallas_call(
        _compress_kernel,
        out_shape=jax.ShapeDtypeStruct((B, 2, N_KV, r, HEAD_DIM), F32),
        grid=(B,),
        in_specs=[xspec, xspec, full(w1), full(p), full(w2)],
        out_specs=pl.BlockSpec((None, 2, N_KV, r, HEAD_DIM), lambda b: (b, 0, 0, 0, 0)),
        compiler_params=pltpu.CompilerParams(dimension_semantics=("parallel",), vmem_limit_bytes=VMEM_LIMIT),
        name="compress_prompt",
    )(kc.reshape(B, r, width), vc.reshape(B, r, width), w1, p, w2)


def _group_scores(q, k, tq):
    s = lax.dot_general(q, k, (((1,), (1,)), ((), ())), preferred_element_type=F32) * HEAD_DIM ** -0.5
    return s.reshape(GROUP, tq, k.shape[0])


def _stack_heads(ref):
    return jnp.concatenate([ref[:, g * HEAD_DIM:(g + 1) * HEAD_DIM] for g in range(GROUP)], axis=0).astype(BF16)


def _importance(psum, n_sel):
    r = psum.shape[1]
    ci = lax.broadcasted_iota(jnp.int32, (r, n_sel), 0) * CMP_STRIDE
    cj = lax.broadcasted_iota(jnp.int32, (r, n_sel), 1) * SEL_LEN
    cover_t = ((ci < cj + SEL_LEN) & (ci + (CMP_LEN - 1) >= cj)).astype(BF16)
    p_hi = psum.astype(BF16)
    p_lo = (psum - p_hi.astype(F32)).astype(BF16)
    return jnp.dot(p_hi, cover_t, preferred_element_type=F32) + jnp.dot(p_lo, cover_t, preferred_element_type=F32)


def _block_scores(imp, blk, qpos):
    cur = jnp.right_shift(qpos, SEL_LEN.bit_length() - 1)
    forced = (blk == 0) | (blk == cur) | (blk == cur - 1)
    return jnp.where(blk * SEL_LEN <= qpos, imp + jnp.where(forced, FORCE_BONUS, 0.0), -1.0)


def _top_rank_mask(score, axis, n_cand):
    idx = lax.broadcasted_iota(jnp.int32, score.shape, axis)
    rank = jnp.zeros(score.shape, F32)
    for j in range(n_cand):
        cand = score[j:j + 1, :] if axis == 0 else score[:, j:j + 1]
        rank = rank + jnp.where((cand > score) | ((cand == score) & (idx > j)), 1.0, 0.0)
    return jnp.where(rank < float(SEL_TOP), 1.0, 0.0)


def _nsa_prompt_kernel(qraw_ref, qrot_ref, ks_ref, vs_ref, kw_ref, vw_ref, kc_ref, vc_ref, gate_ref, o_ref,
                       m_ref, l_ref, acc_ref, *, tq, tk, seq_len):
    q0 = pl.program_id(2) * tq
    r = kc_ref.shape[0]
    n_sel = seq_len // SEL_LEN
    sel_shift = SEL_LEN.bit_length() - 1
    qpos = q0 + lax.broadcasted_iota(jnp.int32, (tq, 1), 0)

    s = _group_scores(_stack_heads(qraw_ref), kc_ref[...].astype(BF16), tq)
    c_end = lax.broadcasted_iota(jnp.int32, (tq, r), 1) * CMP_STRIDE + (CMP_LEN - 1)
    cmask = c_end <= qpos
    s = s + jnp.where(cmask, 0.0, NEG)[None]
    p = jnp.exp(s - jnp.max(s, axis=-1, keepdims=True)) * jnp.where(cmask, 1.0, 0.0)[None]
    p = p / jnp.maximum(jnp.sum(p, axis=-1, keepdims=True), 1e-30)
    o_cmp = jnp.dot(p.reshape(GROUP * tq, r).astype(BF16), vc_ref[...].astype(BF16), preferred_element_type=F32)
    psum = p[0]
    for g in range(1, GROUP):
        psum = psum + p[g]
    imp_t = _importance(psum, HEAD_DIM).T[:n_sel]
    blk_t = lax.broadcasted_iota(jnp.int32, (n_sel, tq), 0)
    qpos_t = q0 + lax.broadcasted_iota(jnp.int32, (n_sel, tq), 1)
    sel_b = _top_rank_mask(_block_scores(imp_t, blk_t, qpos_t), 0, n_sel).astype(BF16)

    q_rot = _stack_heads(qrot_ref)
    m_ref[...] = jnp.full(m_ref.shape, NEG, F32)
    l_ref[...] = jnp.zeros(l_ref.shape, F32)
    acc_ref[...] = jnp.zeros(acc_ref.shape, F32)

    def sel_tile(kt, carry):
        k0 = pl.multiple_of(kt * tk, tk)
        k = ks_ref[pl.ds(k0, tk), :].astype(BF16)
        v = vs_ref[pl.ds(k0, tk), :].astype(BF16)
        s2 = _group_scores(q_rot, k, tq)
        ej = lax.broadcasted_iota(jnp.int32, (n_sel, tk), 0)
        et = k0 + lax.broadcasted_iota(jnp.int32, (n_sel, tk), 1)
        expand = jnp.where(jnp.right_shift(et, sel_shift) == ej, 1.0, 0.0).astype(BF16)
        selm = lax.dot_general(sel_b, expand, (((0,), (0,)), ((), ())), preferred_element_type=F32)
        kpos = k0 + lax.broadcasted_iota(jnp.int32, (tq, tk), 1)
        ok = (selm > 0.5) & (kpos <= qpos)
        s2 = s2 + jnp.where(ok, 0.0, NEG)[None]
        m_prev = m_ref[...]
        m_new = jnp.maximum(m_prev, jnp.max(s2, axis=-1, keepdims=True))
        alpha = jnp.exp(m_prev - m_new)
        p2 = jnp.exp(s2 - m_new)
        l_ref[...] = alpha * l_ref[...] + jnp.sum(p2, axis=-1, keepdims=True)
        pv = jnp.dot(p2.reshape(GROUP * tq, tk).astype(BF16), v, preferred_element_type=F32)
        acc_ref[...] = alpha.reshape(GROUP * tq, 1) * acc_ref[...] + pv
        m_ref[...] = m_new
        return carry

    lax.fori_loop(0, lax.div(q0 + tq + tk - 1, tk), sel_tile, 0)
    o_sel = acc_ref[...] / jnp.maximum(l_ref[...].reshape(GROUP * tq, 1), 1e-30)

    span = WINDOW + tq
    w0 = pl.multiple_of(jnp.maximum(q0 - WINDOW, 0), tq)
    s3 = _group_scores(q_rot, kw_ref[pl.ds(w0, span), :].astype(BF16), tq)
    kp = w0 + lax.broadcasted_iota(jnp.int32, (tq, span), 1)
    okw = (kp <= qpos) & (kp >= qpos - WINDOW)
    s3 = s3 + jnp.where(okw, 0.0, NEG)[None]
    p3 = jnp.exp(s3 - jnp.max(s3, axis=-1, keepdims=True))
    p3 = p3 / jnp.sum(p3, axis=-1, keepdims=True)
    o_win = jnp.dot(p3.reshape(GROUP * tq, span).astype(BF16), vw_ref[pl.ds(w0, span), :].astype(BF16),
                    preferred_element_type=F32)

    gts = jax.nn.sigmoid(gate_ref[...])
    for g in range(GROUP):
        rows = slice(g * tq, (g + 1) * tq)
        o_ref[:, g * HEAD_DIM:(g + 1) * HEAD_DIM] = (gts[:, 3 * g:3 * g + 1] * o_cmp[rows]
                                                     + gts[:, 3 * g + 1:3 * g + 2] * o_sel[rows]
                                                     + gts[:, 3 * g + 2:3 * g + 3] * o_win[rows]).astype(o_ref.dtype)


def _nsa_prompt(z, cmp, B, T, tq=256, tk=512):
    assert T % tk == 0 and tk % tq == 0 and tq % HEAD_DIM == 0 and T >= WINDOW + tq
    assert SEL_LEN & (SEL_LEN - 1) == 0 and tk % SEL_LEN == 0
    assert (T // SEL_LEN) % 8 == 0 and T // SEL_LEN <= HEAD_DIM
    nq = T // tq
    r = cmp.shape[3]
    gw = GROUP * HEAD_DIM
    qspec = lambda off: pl.BlockSpec((tq, gw), lambda b, h, q: (b * nq + q, off // gw + h))
    kvspec = lambda off: pl.BlockSpec((T, HEAD_DIM), lambda b, h, q: (b, off // HEAD_DIM + h))
    cspec = lambda which: pl.BlockSpec((None, None, None, r, HEAD_DIM), lambda b, h, q: (b, which, h, 0, 0))
    return pl.pallas_call(
        functools.partial(_nsa_prompt_kernel, tq=tq, tk=tk, seq_len=T),
        out_shape=jax.ShapeDtypeStruct((B * T, A_Q), BF16),
        grid=(B, N_KV, nq),
        in_specs=[qspec(Z_QRAW), qspec(Z_QROT), kvspec(Z_KS), kvspec(Z_VS), kvspec(Z_KW), kvspec(Z_VW),
                  cspec(0), cspec(1),
                  pl.BlockSpec((tq, HEAD_DIM), lambda b, h, q: (b * nq + q, Z_GNSA // HEAD_DIM + h))],
        out_specs=pl.BlockSpec((tq, gw), lambda b, h, q: (b * nq + q, h)),
        scratch_shapes=[pltpu.VMEM((GROUP, tq, 1), F32), pltpu.VMEM((GROUP, tq, 1), F32),
                        pltpu.VMEM((GROUP * tq, HEAD_DIM), F32)],
        compiler_params=pltpu.CompilerParams(dimension_semantics=("parallel", "parallel", "arbitrary"),
                                             vmem_limit_bytes=VMEM_LIMIT),
        name="nsa_prompt",
    )(z, z, z, z, z, z, cmp, cmp, z)


def _gate_logs(i_pre, f_pre):
    log_i = GATE_CAP * jnp.tanh(i_pre / GATE_CAP)
    fc = GATE_CAP * jnp.tanh(f_pre / GATE_CAP)
    log_f = jnp.minimum(fc, 0.0) - jnp.log1p(jnp.exp(-jnp.abs(fc)))
    return log_i, log_f


def _mlstm_prompt_kernel(q_ref, k_ref, v_ref, op_ref, grow_ref, gcol_ref, gain_ref,
                         h_ref, c_ref, n_ref, m_ref):
    @pl.when(pl.program_id(1) == 0)
    def _():
        c_ref[...] = jnp.zeros(c_ref.shape, F32)
        n_ref[...] = jnp.zeros(n_ref.shape, F32)
        m_ref[...] = jnp.zeros(m_ref.shape, F32)

    for h in range(N_HEADS_B):
        qk_cols = slice(h * DQK_B, (h + 1) * DQK_B)
        v_cols = slice(h * DV_B, (h + 1) * DV_B)
        _mlstm_chunk(q_ref[:, qk_cols], k_ref[:, qk_cols], v_ref[:, v_cols], op_ref[:, v_cols],
                     grow_ref[h:h + 1, :], grow_ref[N_HEADS_B + h:N_HEADS_B + h + 1, :],
                     gcol_ref[:, h:h + 1], gcol_ref[:, N_HEADS_B + h:N_HEADS_B + h + 1], gain_ref[h],
                     h_ref.at[:, v_cols], c_ref.at[h], n_ref.at[h], m_ref.at[h])


def _mlstm_chunk(q, k, v, o_pre, i_row, f_row, i_col, f_col, gain, h_ref, c_ref, n_ref, m_ref):
    L = q.shape[0]
    qb = q.astype(BF16)
    k = k * DQK_B ** -0.5
    vb = v.astype(BF16)
    i_r, f_r = _gate_logs(i_row, f_row)
    i_c, f_c = _gate_logs(i_col, f_col)
    li = lax.broadcasted_iota(jnp.int32, (L, L), 0)
    si = lax.broadcasted_iota(jnp.int32, (L, L), 1)
    tri = si <= li
    bcum_c = jnp.sum(jnp.where(tri, f_r, 0.0), axis=1, keepdims=True)
    bcum_r = jnp.sum(jnp.where(li <= si, f_c, 0.0), axis=0, keepdims=True)
    btot = jnp.sum(f_r, axis=1, keepdims=True)
    c_old, n_old, m_old = c_ref[...], n_ref[...], m_ref[0:1, 0:1]

    dlog = jnp.where(tri, bcum_c - bcum_r + i_r, NEG)
    inter = bcum_c + m_old
    m_t = jnp.maximum(inter, jnp.max(dlog, axis=1, keepdims=True))
    qk = lax.dot_general(qb, k.astype(BF16), (((1,), (1,)), ((), ())), preferred_element_type=F32)
    w_intra = jnp.exp(dlog - m_t) * qk
    w_inter = jnp.exp(inter - m_t)
    num = (w_inter * jnp.dot(qb, c_old.astype(BF16), preferred_element_type=F32)
           + jnp.dot(w_intra.astype(BF16), vb, preferred_element_type=F32))
    den = w_inter * jnp.sum(q * n_old, axis=1, keepdims=True) + jnp.sum(w_intra, axis=1, keepdims=True)
    h = num / jnp.maximum(jnp.abs(den), jnp.exp(-m_t))
    h_ref[...] = (_rms(h, gain) * jax.nn.sigmoid(o_pre)).astype(h_ref.dtype)

    wlog_c = btot - bcum_c + i_c
    m_new = jnp.maximum(btot + m_old, jnp.max(btot - bcum_r + i_r, axis=1, keepdims=True))
    decay = jnp.exp(btot + m_old - m_new)
    kw = k * jnp.exp(wlog_c - m_new)
    c_ref[...] = decay * c_old + lax.dot_general(kw.astype(BF16), vb, (((0,), (0,)), ((), ())),
                                                 preferred_element_type=F32)
    n_ref[...] = decay * n_old + jnp.sum(kw, axis=0, keepdims=True)
    m_ref[...] = jnp.broadcast_to(m_new, m_ref.shape)


def _mlstm_prompt(z, gain, B, T):
    L = MLSTM_CHUNK
    assert T % L == 0
    nc, H = T // L, N_HEADS_B
    gcol = z[:, Z_IB:Z_IB + 2 * H].reshape(B * nc, L, 2 * H)
    grow = gcol.transpose(0, 2, 1)
    rows = lambda w, off: pl.BlockSpec((L, w), lambda b, c: (b * nc + c, off // w))
    gidx = lambda b, c: (b * nc + c, 0, 0)
    state = lambda *s: pl.BlockSpec((None, H) + s, lambda b, c: (b, 0, 0, 0))
    h, c, n, m = pl.pallas_call(
        _mlstm_prompt_kernel,
        out_shape=(jax.ShapeDtypeStruct((B * T, B_V), BF16), jax.ShapeDtypeStruct((B, H, DQK_B, DV_B), F32),
                   jax.ShapeDtypeStruct((B, H, 1, DQK_B), F32), jax.ShapeDtypeStruct((B, H, 1, DQK_B), F32)),
        grid=(B, nc),
        in_specs=[rows(B_QK, Z_QB), rows(B_QK, Z_KB), rows(B_V, Z_VB), rows(B_V, Z_OB),
                  pl.BlockSpec((None, 2 * H, L), gidx), pl.BlockSpec((None, L, 2 * H), gidx),
                  pl.BlockSpec((H, 1, DV_B), lambda b, c: (0, 0, 0))],
        out_specs=(rows(B_V, 0), state(DQK_B, DV_B), state(1, DQK_B), state(1, DQK_B)),
        compiler_params=pltpu.CompilerParams(dimension_semantics=("parallel", "arbitrary"),
                                             vmem_limit_bytes=VMEM_LIMIT),
        name="mlstm_prompt",
    )(z, z, z, z, grow, gcol, gain.reshape(H, 1, DV_B))
    return h, c, n[:, :, 0], m[:, :, 0, 0]


def _merge_kernel(x_ref, oa_ref, hb_ref, gma_ref, gmb_ref, wa_ref, wb_ref, wo_ref, nf_ref, wrh_ref, wrl_ref, br_ref,
                  *refs):
    x2_ref, xn_ref, route_ref = refs[-3:]
    ua = jnp.dot(oa_ref[...], wa_ref[...], preferred_element_type=F32)
    ub = jnp.dot(hb_ref[...], wb_ref[...], preferred_element_type=F32)
    merged = jax.nn.sigmoid(gma_ref[...]) * ua + jax.nn.sigmoid(gmb_ref[...]) * ub
    x2 = x_ref[...] + jnp.dot(merged.astype(BF16), wo_ref[...], preferred_element_type=F32)
    x2_ref[...] = x2
    xn = _rms(x2, nf_ref[...])
    xn_ref[...] = xn
    hi = xn.astype(BF16)
    lo = (xn - hi.astype(F32)).astype(BF16)
    lg = (jnp.dot(hi, wrh_ref[...], preferred_element_type=F32) + jnp.dot(lo, wrh_ref[...], preferred_element_type=F32)
          + jnp.dot(hi, wrl_ref[...], preferred_element_type=F32)) + br_ref[...]
    lane = lax.broadcasted_iota(jnp.int32, lg.shape, 1)
    lanef = lane.astype(F32)
    ninf = -jnp.inf
    gmask = (lane >= N_EXPERTS) & (lane < N_EXPERTS + N_GROUPS)
    gl = jnp.where(gmask, lg, ninf)
    gmax = jnp.max(gl, axis=1, keepdims=True)
    gidx = jnp.min(jnp.where(gl == gmax, lanef, 1e9), axis=1, keepdims=True) - N_EXPERTS
    gprob = 1.0 / jnp.sum(jnp.where(gmask, jnp.exp(lg - gmax), 0.0), axis=1, keepdims=True)
    e_lo = gidx * EXPERTS_PER_GROUP
    el = jnp.where((lanef >= e_lo) & (lanef < e_lo + EXPERTS_PER_GROUP), lg, ninf)
    v1 = jnp.max(el, axis=1, keepdims=True)
    i1 = jnp.min(jnp.where(el == v1, lanef, 1e9), axis=1, keepdims=True)
    el2 = jnp.where(lanef == i1, ninf, el)
    v2 = jnp.max(el2, axis=1, keepdims=True)
    i2 = jnp.min(jnp.where(el2 == v2, lanef, 1e9), axis=1, keepdims=True)
    e2 = jnp.exp(v2 - v1)
    g1 = gprob / (1.0 + e2)
    g2 = gprob * e2 / (1.0 + e2)
    route_ref[...] = jnp.where(lane == 0, i1, jnp.where(lane == 1, i2, jnp.where(lane == 2, g1,
                               jnp.where(lane == 3, g2, 0.0))))


def _merge_weights(w_up_a, w_up_b, w_out, norm_ffn, w_rg, b_rg, w_re, b_re):
    assert TOP_K == 2
    pad = ROUTE_W - N_EXPERTS - N_GROUPS
    w = jnp.concatenate([w_re, w_rg, jnp.zeros((D_MODEL, pad), F32)], axis=1)
    b = jnp.concatenate([b_re, b_rg, jnp.zeros((pad,), F32)]).reshape(1, ROUTE_W)
    hi = w.astype(BF16)
    return (w_up_a.astype(BF16), w_up_b.astype(BF16), w_out.astype(BF16), norm_ffn.reshape(1, D_MODEL).astype(F32),
            hi, (w - hi.astype(F32)).astype(BF16), b)


def _merge(x2d, o_a, h_b, z, mw, n_total, row0, tm, bufs=None):
    m = x2d.shape[0]
    i0 = row0 // tm
    row = lambda w, c=0: pl.BlockSpec((tm, w), lambda i: (i, c))
    const = lambda a: pl.BlockSpec(a.shape, lambda i: (0,) * a.ndim, pipeline_mode=pl.Buffered(1))
    outw = (D_MODEL, D_MODEL, ROUTE_W)
    in_specs = [row(D_MODEL), row(A_Q), row(B_V), row(D_MODEL, Z_GMA // D_MODEL), row(D_MODEL, Z_GMB // D_MODEL)]
    in_specs += [const(a) for a in mw]
    args = [x2d, o_a, h_b, z, z] + list(mw)
    aliases = {}
    if bufs is not None:
        in_specs += [pl.BlockSpec(memory_space=pl.ANY)] * 3
        aliases = {len(args) + k: k for k in range(3)}
        args += list(bufs)
    return pl.pallas_call(
        _merge_kernel,
        out_shape=tuple(jax.ShapeDtypeStruct((n_total, w), F32) for w in outw),
        grid=(m // tm,),
        in_specs=in_specs,
        out_specs=tuple(pl.BlockSpec((tm, w), lambda i: (i0 + i, 0)) for w in outw),
        input_output_aliases=aliases,
        compiler_params=pltpu.CompilerParams(dimension_semantics=("parallel",), vmem_limit_bytes=VMEM_LIMIT_MERGE),
        name="merge_route",
    )(*args)


def _moe_plan(route, n_tiles):
    n = route.shape[0]
    a = n * TOP_K
    flat_e = route[:, 0:TOP_K].astype(jnp.int32).reshape(a)
    onehot = (flat_e[:, None] == jnp.arange(N_EXPERTS, dtype=jnp.int32)[None, :]).astype(jnp.int32)
    rank = jnp.take_along_axis(jnp.cumsum(onehot, axis=0) - onehot, flat_e[:, None], axis=1)[:, 0]
    counts = jnp.sum(onehot, axis=0)
    ntile = (counts + MOE_TILE - 1) // MOE_TILE
    tile_end = jnp.cumsum(ntile)
    dest = ((tile_end - ntile)[flat_e] * MOE_TILE + rank).astype(jnp.int32)
    n_used = tile_end[-1].astype(jnp.int32)
    t = jnp.minimum(jnp.arange(n_tiles, dtype=jnp.int32), n_used - 1)
    tile_expert = jnp.minimum(jnp.searchsorted(tile_end, t, side='right'), N_EXPERTS - 1).astype(jnp.int32)
    first = (tile_end - ntile)[tile_expert]
    tile_rows = jnp.clip(counts[tile_expert] - (t - first) * MOE_TILE, 0, MOE_TILE)
    tile_rows = jnp.where(jnp.arange(n_tiles) < n_used, tile_rows, 0).astype(jnp.int32)
    return dest, tile_expert, tile_rows, n_used.reshape(1)


DMA_UNROLL = 8


def _dispatch_kernel(dest_ref, x_ref, o_hbm, sem):
    i = pl.program_id(0)
    tm = x_ref.shape[0]

    def copy(r, k):
        d = dest_ref[(i * tm + r) * TOP_K + k]
        return pltpu.make_async_copy(x_ref.at[pl.ds(r, 1)], o_hbm.at[pl.ds(d, 1)], sem)

    def start(r, c):
        for k in range(TOP_K):
            copy(r, k).start()
        return c

    def wait(r, c):
        for k in range(TOP_K):
            copy(r, k).wait()
        return c

    lax.fori_loop(0, tm, start, 0, unroll=DMA_UNROLL)
    lax.fori_loop(0, tm, wait, 0, unroll=DMA_UNROLL)


def _moe_dispatch(xn, dest, n_tiles, tm=128):
    n, d = xn.shape
    return pl.pallas_call(
        _dispatch_kernel,
        out_shape=jax.ShapeDtypeStruct((n_tiles * MOE_TILE, d), xn.dtype),
        grid_spec=pltpu.PrefetchScalarGridSpec(
            num_scalar_prefetch=1, grid=(n // tm,),
            in_specs=[pl.BlockSpec((tm, d), lambda i, ds: (i, 0))],
            out_specs=pl.BlockSpec(memory_space=pl.ANY),
            scratch_shapes=[pltpu.SemaphoreType.DMA(())]),
        compiler_params=pltpu.CompilerParams(dimension_semantics=("arbitrary",), vmem_limit_bytes=VMEM_LIMIT),
        name="moe_dispatch",
    )(dest, xn)


def _expert_kernel(te_ref, cnt_ref, nu_ref, x_ref, wg_ref, wu_ref, wd_ref, o_ref):
    t, j = pl.program_id(0), pl.program_id(1)
    cnt = cnt_ref[t]

    @pl.when(cnt > 0)
    def _():
        wg, wu, wd = wg_ref[...].astype(BF16), wu_ref[...].astype(BF16), wd_ref[...].astype(BF16)

        def run(rows):
            xb = x_ref[rows].astype(BF16)
            g = jnp.dot(xb, wg, preferred_element_type=F32)
            u = jnp.dot(xb, wu, preferred_element_type=F32)
            h = (g * jax.nn.sigmoid(g) * u).astype(BF16)
            y = jnp.dot(h, wd, preferred_element_type=F32)

            @pl.when(j == 0)
            def _():
                o_ref[rows] = y

            @pl.when(j > 0)
            def _():
                o_ref[rows] += y

        for s in range(MOE_TILE // MOE_SUB):
            lo, half = s * MOE_SUB, MOE_SUB // 2

            @pl.when(cnt > lo + half)
            def _():
                run(slice(lo, lo + MOE_SUB))

            @pl.when((cnt > lo) & (cnt <= lo + half))
            def _():
                run(slice(lo, lo + half))


def _moe_experts(xs, tile_expert, tile_rows, n_used, w_gate, w_up, w_down):
    n_tiles = xs.shape[0] // MOE_TILE
    nj = D_EXPERT // MOE_FC
    tix = lambda t, j, te, cnt, nu: (jnp.minimum(t, nu[0] - 1), 0)
    jj = lambda t, j, nu: jnp.where(t < nu[0], j, nj - 1)
    return pl.pallas_call(
        _expert_kernel,
        out_shape=jax.ShapeDtypeStruct(xs.shape, F32),
        grid_spec=pltpu.PrefetchScalarGridSpec(
            num_scalar_prefetch=3, grid=(n_tiles, nj),
            in_specs=[pl.BlockSpec((MOE_TILE, D_MODEL), tix),
                      pl.BlockSpec((None, D_MODEL, MOE_FC), lambda t, j, te, cnt, nu: (te[t], 0, jj(t, j, nu))),
                      pl.BlockSpec((None, D_MODEL, MOE_FC), lambda t, j, te, cnt, nu: (te[t], 0, jj(t, j, nu))),
                      pl.BlockSpec((None, MOE_FC, D_MODEL), lambda t, j, te, cnt, nu: (te[t], jj(t, j, nu), 0))],
            out_specs=pl.BlockSpec((MOE_TILE, D_MODEL), tix)),
        compiler_params=pltpu.CompilerParams(dimension_semantics=("arbitrary", "arbitrary"),
                                             vmem_limit_bytes=VMEM_LIMIT_MERGE),
        name="moe_experts",
    )(tile_expert, tile_rows, n_used, xs, w_gate, w_up, w_down)


def _combine_kernel(dest_ref, x2_ref, route_ref, g_ref, y_hbm, op_ref, os_ref, buf_ref, sem, *, n_prompt_tiles):
    i = pl.program_id(0)
    tm = x2_ref.shape[0]

    def copy(r, k):
        d = dest_ref[(i * tm + r) * TOP_K + k]
        return pltpu.make_async_copy(y_hbm.at[pl.ds(d, 1)], buf_ref.at[k, pl.ds(r, 1)], sem)

    def start(r, c):
        for k in range(TOP_K):
            copy(r, k).start()
        return c

    def wait(r, c):
        for k in range(TOP_K):
            copy(r, k).wait()
        return c

    lax.fori_loop(0, tm, start, 0, unroll=DMA_UNROLL)
    lax.fori_loop(0, tm, wait, 0, unroll=DMA_UNROLL)
    ffn = buf_ref[0] * route_ref[:, TOP_K:TOP_K + 1]
    for k in range(1, TOP_K):
        ffn = ffn + buf_ref[k] * route_ref[:, TOP_K + k:TOP_K + k + 1]
    y = _rms(x2_ref[...] + ffn, g_ref[...])

    @pl.when(i < n_prompt_tiles)
    def _():
        op_ref[...] = y

    @pl.when(i >= n_prompt_tiles)
    def _():
        os_ref[...] = y


def _moe_combine(x2, route, dest, ys, norm_final, n_prompt, tm=128):
    n, d = x2.shape
    npt = n_prompt // tm
    nst = (n - n_prompt) // tm
    return pl.pallas_call(
        functools.partial(_combine_kernel, n_prompt_tiles=npt),
        out_shape=(jax.ShapeDtypeStruct((n_prompt, d), F32), jax.ShapeDtypeStruct((n - n_prompt, d), F32)),
        grid_spec=pltpu.PrefetchScalarGridSpec(
            num_scalar_prefetch=1, grid=(n // tm,),
            in_specs=[pl.BlockSpec((tm, d), lambda i, ds: (i, 0)), pl.BlockSpec((tm, ROUTE_W), lambda i, ds: (i, 0)),
                      pl.BlockSpec((1, d), lambda i, ds: (0, 0)), pl.BlockSpec(memory_space=pl.ANY)],
            out_specs=(pl.BlockSpec((tm, d), lambda i, ds: (jnp.minimum(i, npt - 1), 0)),
                       pl.BlockSpec((tm, d), lambda i, ds: (jnp.clip(i - npt, 0, nst - 1), 0))),
            scratch_shapes=[pltpu.VMEM((TOP_K, tm, d), F32), pltpu.SemaphoreType.DMA(())]),
        compiler_params=pltpu.CompilerParams(dimension_semantics=("arbitrary",), vmem_limit_bytes=VMEM_LIMIT),
        name="moe_combine",
    )(dest, x2, route, norm_final.reshape(1, d), ys)


def _moe(x2, xn, route, w_gate, w_up, w_down, norm_final, n_prompt):
    n = x2.shape[0]
    n_tiles = (n * TOP_K + N_EXPERTS * (MOE_TILE - 1) + MOE_TILE - 1) // MOE_TILE
    dest, tile_expert, tile_rows, n_used = _moe_plan(route, n_tiles)
    xs = _moe_dispatch(xn, dest, n_tiles)
    ys = _moe_experts(xs, tile_expert, tile_rows, n_used, w_gate, w_up, w_down)
    return _moe_combine(x2, route, dest, ys, norm_final, n_prompt)


POOL_ROWS = 256
ROWS_PER_PAGE = PAGE_SIZE // CMP_STRIDE
ZW = 2 * N_KV * HEAD_DIM


GROUP_ROWS = N_KV * CMP_STRIDE
GROUP_PITCH = GROUP_ROWS + 8


def _compress_pool_kernel(pt_ref, w1_ref, p_ref, xk_hbm, xv_hbm, zk_ref, zv_ref, xbuf, sem):
    i = pl.program_id(0)
    n = pl.num_programs(0)
    slot = lax.rem(i, 2)
    pages = POOL_ROWS // ROWS_PER_PAGE

    def copies(step, sl):
        out = []
        for pg in range(pages):
            page = pt_ref[step * pages + pg]
            for which, src in enumerate((xk_hbm, xv_hbm)):
                out.append(pltpu.make_async_copy(
                    src.at[page], xbuf.at[sl, which, pl.ds(pg * ROWS_PER_PAGE, ROWS_PER_PAGE), pl.ds(0, GROUP_ROWS), :],
                    sem.at[sl, which]))
        return out

    @pl.when(i == 0)
    def _():
        for cp in copies(0, 0):
            cp.start()

    @pl.when(i + 1 < n)
    def _():
        for cp in copies(i + 1, 1 - slot):
            cp.start()

    for cp in copies(i, slot):
        cp.wait()

    for which, z_ref in enumerate((zk_ref, zv_ref)):
        w1 = w1_ref[which]
        pb = jnp.dot(p_ref[which], w1, preferred_element_type=F32)
        bias = pb[0:1, :HEAD_DIM] + pb[1:2, HEAD_DIM:]
        rows2d = xbuf.at[slot, which].reshape(POOL_ROWS * GROUP_PITCH, HEAD_DIM)
        for h in range(N_KV):
            xh = jnp.concatenate([rows2d[pl.ds(N_KV * s + h, POOL_ROWS, stride=GROUP_PITCH), :]
                                  for s in range(CMP_STRIDE)], axis=1).astype(BF16)
            zz = jnp.dot(xh, w1, preferred_element_type=F32)
            z_ref[:, 2 * h * HEAD_DIM:(2 * h + 1) * HEAD_DIM] = zz[:, :HEAD_DIM] + bias
            z_ref[:, (2 * h + 1) * HEAD_DIM:(2 * h + 2) * HEAD_DIM] = zz[:, HEAD_DIM:]


def _compress_pool(pool_k, pool_v, page_table, cw):
    n_phys = pool_k.shape[0]
    bs, n_pages = page_table.shape
    rows = bs * n_pages * ROWS_PER_PAGE
    assert rows % POOL_ROWS == 0
    w1, p, _ = cw
    full = lambda a: pl.BlockSpec(a.shape, lambda i, pt: (0,) * a.ndim)
    groups = lambda a: a.reshape(n_phys, ROWS_PER_PAGE, GROUP_ROWS, HEAD_DIM)
    zk, zv = pl.pallas_call(
        _compress_pool_kernel,
        out_shape=(jax.ShapeDtypeStruct((rows, ZW), F32),) * 2,
        grid_spec=pltpu.PrefetchScalarGridSpec(
            num_scalar_prefetch=1, grid=(rows // POOL_ROWS,),
            in_specs=[full(w1), full(p), pl.BlockSpec(memory_space=pl.ANY), pl.BlockSpec(memory_space=pl.ANY)],
            out_specs=(pl.BlockSpec((POOL_ROWS, ZW), lambda i, pt: (i, 0)),) * 2,
            scratch_shapes=[pltpu.VMEM((2, 2, POOL_ROWS, GROUP_PITCH, HEAD_DIM), F32),
                            pltpu.SemaphoreType.DMA((2, 2))]),
        compiler_params=pltpu.CompilerParams(dimension_semantics=("arbitrary",), vmem_limit_bytes=VMEM_LIMIT),
        name="compress_pool",
    )(page_table.reshape(bs * n_pages), w1, p, groups(pool_k), groups(pool_v))
    return zk.reshape(bs, n_pages * ROWS_PER_PAGE, ZW), zv.reshape(bs, n_pages * ROWS_PER_PAGE, ZW)


def _nsa_sample_kernel(pt_ref, qraw_ref, qrot_ref, knew_ref, wnew_ref, gate_ref, w2_ref, wk_ref, wv_ref,
                       zk_ref, zv_ref, ks_hbm, vs_hbm, o_ref, sbuf, sem, *, n_pages, past_len):
    b = pl.program_id(0)
    nb = pl.num_programs(0)
    slot = lax.rem(b, 2)
    page_rows = N_KV * PAGE_SIZE

    def copies(seq, sl):
        out = []
        for pg in range(n_pages):
            page = pt_ref[seq, pg]
            for which, ssrc in enumerate((ks_hbm, vs_hbm)):
                out.append(pltpu.make_async_copy(
                    ssrc.at[page], sbuf.at[sl, which, pl.ds(pg * page_rows, page_rows)], sem.at[sl]))
        return out

    @pl.when(b == 0)
    def _():
        for cp in copies(0, 0):
            cp.start()

    @pl.when(b + 1 < nb)
    def _():
        for cp in copies(b + 1, 1 - slot):
            cp.start()

    for cp in copies(b, slot):
        cp.wait()

    scale = HEAD_DIM ** -0.5
    nq = N_KV * GROUP
    r = n_pages * ROWS_PER_PAGE
    qpos = jnp.full((nq, 1), past_len, jnp.int32)
    n_sel_pad = HEAD_DIM
    sel_shift = SEL_LEN.bit_length() - 1
    row_head = lax.broadcasted_iota(jnp.int32, (nq, 1), 0) // GROUP
    per_head = lambda a0, a1: jnp.where(row_head == 0, a0, a1)

    def all_heads(ref):
        return jnp.concatenate([ref[:, j * HEAD_DIM:(j + 1) * HEAD_DIM] for j in range(nq)], axis=0).astype(BF16)

    def scores(q, k):
        return lax.dot_general(q, k, (((1,), (1,)), ((), ())), preferred_element_type=F32) * scale

    cmp = []
    for which, z_ref in enumerate((zk_ref, zv_ref)):
        heads = []
        for h in range(N_KV):
            zh = z_ref[:, 2 * h * HEAD_DIM:(2 * h + 2) * HEAD_DIM]
            pre = zh[:, :HEAD_DIM] + pltpu.roll(zh[:, HEAD_DIM:], r - 1, 0)
            mid = pre * jax.nn.sigmoid(pre)
            heads.append(jnp.dot(mid.astype(BF16), w2_ref[which], preferred_element_type=F32).astype(BF16))
        cmp.append(heads)
    q_raw = all_heads(qraw_ref)
    c_end = lax.broadcasted_iota(jnp.int32, (nq, r), 1) * CMP_STRIDE + (CMP_LEN - 1)
    cmask = c_end <= qpos
    s = per_head(scores(q_raw, cmp[0][0]), scores(q_raw, cmp[0][1])) + jnp.where(cmask, 0.0, NEG)
    p = jnp.exp(s - jnp.max(s, axis=-1, keepdims=True)) * jnp.where(cmask, 1.0, 0.0)
    p = p / jnp.maximum(jnp.sum(p, axis=-1, keepdims=True), 1e-30)
    pb = p.astype(BF16)
    o_cmp = per_head(jnp.dot(pb, cmp[1][0], preferred_element_type=F32),
                     jnp.dot(pb, cmp[1][1], preferred_element_type=F32))
    psum = per_head(jnp.sum(jnp.where(row_head == 0, p, 0.0), axis=0, keepdims=True),
                    jnp.sum(jnp.where(row_head == 1, p, 0.0), axis=0, keepdims=True))
    blk = lax.broadcasted_iota(jnp.int32, (nq, n_sel_pad), 1)
    sel = _top_rank_mask(_block_scores(_importance(psum, n_sel_pad), blk, qpos), 1,
                         past_len // SEL_LEN + 1)

    def attend(q, k, v, k_new, v_new, bias, new_bias):
        s = scores(q, k) + bias
        s_new = jnp.sum(q.astype(F32) * k_new.astype(BF16).astype(F32), axis=1, keepdims=True) * scale + new_bias
        m = jnp.maximum(jnp.max(s, axis=-1, keepdims=True), s_new)
        p = jnp.exp(s - m)
        p_new = jnp.exp(s_new - m)
        l = jnp.sum(p, axis=-1, keepdims=True) + p_new
        o = (jnp.dot(p.astype(BF16), v, preferred_element_type=F32)
             + p_new.astype(BF16).astype(F32) * v_new.astype(BF16).astype(F32))
        return o / jnp.maximum(l, 1e-30)

    def own_head(n_rows):
        col = lax.broadcasted_iota(jnp.int32, (nq, n_rows), 1)
        return jnp.bitwise_and(col, N_KV - 1) == row_head

    q_rot = all_heads(qrot_ref)
    n_rows = N_KV * past_len
    ej = lax.broadcasted_iota(jnp.int32, (n_sel_pad, n_rows), 0)
    et = lax.broadcasted_iota(jnp.int32, (n_sel_pad, n_rows), 1)
    expand = jnp.where(jnp.right_shift(et, sel_shift + 1) == ej, 1.0, 0.0).astype(BF16)
    selm = jnp.dot(sel.astype(BF16), expand, preferred_element_type=F32)
    sel_bias = jnp.where((selm > 0.5) & own_head(n_rows), 0.0, NEG)
    new_blk = past_len // SEL_LEN
    new_bias = jnp.where(sel[:, new_blk:new_blk + 1] > 0.5, 0.0, NEG)
    new_kv = lambda ref, c: per_head(ref[:, c * A_KV:c * A_KV + HEAD_DIM], ref[:, c * A_KV + HEAD_DIM:(c + 1) * A_KV])
    o_sel = attend(q_rot, sbuf[slot, 0].astype(BF16), sbuf[slot, 1].astype(BF16),
                   new_kv(knew_ref, 0), new_kv(knew_ref, 1), sel_bias, new_bias)
    win_bias = jnp.where(own_head(wk_ref.shape[0]), 0.0, NEG)
    o_win = attend(q_rot, wk_ref[...].astype(BF16), wv_ref[...].astype(BF16),
                   new_kv(wnew_ref, 0), new_kv(wnew_ref, 1), win_bias, 0.0)

    gts = jax.nn.sigmoid(gate_ref[...])
    for h in range(N_KV):
        for g in range(GROUP):
            j, c0 = h * GROUP + g, h * HEAD_DIM + 3 * g
            o = (gts[:, c0:c0 + 1] * o_cmp[j:j + 1] + gts[:, c0 + 1:c0 + 2] * o_sel[j:j + 1]
                 + gts[:, c0 + 2:c0 + 3] * o_win[j:j + 1])
            o_ref[:, j * HEAD_DIM:(j + 1) * HEAD_DIM] = o.astype(o_ref.dtype)


def _nsa_sample(zs, page_table, zk_pool, zv_pool, sel_k, sel_v, win_k, win_v, w2):
    bs, n_pages = page_table.shape
    past_len = n_pages * PAGE_SIZE
    wb = win_k.shape[1]
    assert wb <= WINDOW and past_len % SEL_LEN == 0 and past_len // SEL_LEN < HEAD_DIM
    assert N_KV == 2
    zs3 = zs.reshape(bs, 1, NZ)
    zspec = lambda w, off: pl.BlockSpec((None, 1, w), lambda b, pt: (b, 0, off // w))
    anyspec = pl.BlockSpec(memory_space=pl.ANY)
    wspec = pl.BlockSpec((None, N_KV * wb, HEAD_DIM), lambda b, pt: (b, 0, 0))
    partial_sums = pl.BlockSpec((None, n_pages * ROWS_PER_PAGE, ZW), lambda b, pt: (b, 0, 0))
    rows2d = lambda a: a.reshape(a.shape[0], a.shape[1] * N_KV, HEAD_DIM)
    out = pl.pallas_call(
        functools.partial(_nsa_sample_kernel, n_pages=n_pages, past_len=past_len),
        out_shape=jax.ShapeDtypeStruct((bs, 1, A_Q), BF16),
        grid_spec=pltpu.PrefetchScalarGridSpec(
            num_scalar_prefetch=1, grid=(bs,),
            in_specs=[zspec(A_Q, Z_QRAW), zspec(A_Q, Z_QROT), zspec(2 * A_KV, Z_KS), zspec(2 * A_KV, Z_KW),
                      zspec(2 * HEAD_DIM, Z_GNSA), pl.BlockSpec(w2.shape, lambda b, pt: (0, 0, 0)), wspec, wspec,
                      partial_sums, partial_sums, anyspec, anyspec],
            out_specs=pl.BlockSpec((None, 1, A_Q), lambda b, pt: (b, 0, 0)),
            scratch_shapes=[pltpu.VMEM((2, 2, N_KV * past_len, HEAD_DIM), F32), pltpu.SemaphoreType.DMA((2,))]),
        compiler_params=pltpu.CompilerParams(dimension_semantics=("arbitrary",), vmem_limit_bytes=VMEM_LIMIT),
        name="nsa_sample",
    )(page_table, zs3, zs3, zs3, zs3, zs3, w2, rows2d(win_k), rows2d(win_v), zk_pool, zv_pool,
      rows2d(sel_k), rows2d(sel_v))
    return out.reshape(bs, A_Q)


def _mlstm_sample_kernel(q_ref, k_ref, v_ref, op_ref, g_ref, gain_ref, c_ref, n_ref, m_ref,
                         h_ref, co_ref, no_ref, mo_ref):
    eye = (lax.broadcasted_iota(jnp.int32, (DQK_B, DQK_B), 0) == lax.broadcasted_iota(jnp.int32, (DQK_B, DQK_B), 1))
    col = lambda row: jnp.sum(jnp.where(eye, row, 0.0), axis=1, keepdims=True)
    lane = lax.broadcasted_iota(jnp.int32, (1, N_HEADS_B), 1)
    m_out = jnp.zeros((1, N_HEADS_B), F32)
    for h in range(N_HEADS_B):
        q = q_ref[:, h * DQK_B:(h + 1) * DQK_B]
        k = k_ref[:, h * DQK_B:(h + 1) * DQK_B] * DQK_B ** -0.5
        v = v_ref[:, h * DV_B:(h + 1) * DV_B]
        log_i, log_f = _gate_logs(g_ref[:, h:h + 1], g_ref[:, N_HEADS_B + h:N_HEADS_B + h + 1])
        c_old, n_old, m_old = c_ref[h], n_ref[h:h + 1, :], m_ref[:, h:h + 1]
        inter = log_f + m_old
        m_new = jnp.maximum(inter, log_i)
        decay = jnp.exp(inter - m_new)
        w = jnp.exp(log_i - m_new)
        co_ref[h] = decay * c_old + col(w * k) * v
        no_ref[h:h + 1, :] = decay * n_old + w * k
        m_out = jnp.where(lane == h, m_new, m_out)
        w_intra = w * jnp.sum(q * k, axis=1, keepdims=True)
        num = decay * jnp.sum(c_old * col(q), axis=0, keepdims=True) + w_intra * v
        den = decay * jnp.sum(q * n_old, axis=1, keepdims=True) + w_intra
        hh = num / jnp.maximum(jnp.abs(den), jnp.exp(-m_new))
        out = _rms(hh, gain_ref[h]) * jax.nn.sigmoid(op_ref[:, h * DV_B:(h + 1) * DV_B])
        h_ref[:, h * DV_B:(h + 1) * DV_B] = out.astype(h_ref.dtype)
    mo_ref[...] = m_out


def _mlstm_sample(zs, gain, c0, n0, m0):
    bs, H = zs.shape[0], N_HEADS_B
    zs3 = zs.reshape(bs, 1, NZ)
    zspec = lambda w, off: pl.BlockSpec((None, 1, w), lambda b: (b, 0, off // w))
    cspec = pl.BlockSpec((None, H, DQK_B, DV_B), lambda b: (b, 0, 0, 0))
    nspec = pl.BlockSpec((None, H, DQK_B), lambda b: (b, 0, 0))
    mspec = pl.BlockSpec((None, 1, H), lambda b: (b, 0, 0))
    h, c, n, m = pl.pallas_call(
        _mlstm_sample_kernel,
        out_shape=(jax.ShapeDtypeStruct((bs, 1, B_V), BF16), jax.ShapeDtypeStruct(c0.shape, F32),
                   jax.ShapeDtypeStruct(n0.shape, F32), jax.ShapeDtypeStruct((bs, 1, H), F32)),
        grid=(bs,),
        in_specs=[zspec(B_QK, Z_QB), zspec(B_QK, Z_KB), zspec(B_V, Z_VB), zspec(B_V, Z_OB), zspec(HEAD_DIM, Z_IB),
                  pl.BlockSpec((H, 1, DV_B), lambda b: (0, 0, 0)), cspec, nspec, mspec],
        out_specs=(pl.BlockSpec((None, 1, B_V), lambda b: (b, 0, 0)), cspec, nspec, mspec),
        compiler_params=pltpu.CompilerParams(dimension_semantics=("parallel",), vmem_limit_bytes=VMEM_LIMIT),
        name="mlstm_sample",
    )(zs3, zs3, zs3, zs3, zs3, gain.reshape(H, 1, DV_B), c0, n0, m0.reshape(bs, 1, H))
    return h.reshape(bs, B_V), c, n, m.reshape(bs, H)


def _split_kv(z, B, T):
    z = z.reshape(B, T, NZ)
    kv = lambda o: z[..., o:o + A_KV].reshape(B, T, N_KV, HEAD_DIM)
    return kv(Z_KC), kv(Z_VC), kv(Z_KS), kv(Z_VS), kv(Z_KW), kv(Z_VW)


def kernel(x_prompt, x_sample, cache_cmp_k, cache_cmp_v, cache_sel_k, cache_sel_v, cache_win_k, cache_win_v,
           state_mlstm_c, state_mlstm_n, state_mlstm_m, page_table, norm_mix, w_in, b_in, cmp_pos, cmp_w1, cmp_w2,
           mlstm_norm, w_up_a, w_up_b, w_out, norm_ffn, w_router_grp, b_router_grp, w_router_exp, b_router_exp,
           w_gate, w_up, w_down, norm_final):
    assert w_in.shape[0] == 1, "single layer"
    Bp, Tp = x_prompt.shape[:2]
    Bs, Ts = x_sample.shape[:2]
    assert Ts == 1
    n_p, n_s = Bp * Tp, Bs * Ts
    past_len = page_table.shape[1] * PAGE_SIZE
    pos_p = jnp.arange(Tp, dtype=jnp.int32)
    pos_s = past_len + jnp.arange(Ts, dtype=jnp.int32)
    l = 0

    w_perm, b_perm = _permute_in_proj(w_in[l], b_in[l])
    cw = _compress_weights(cmp_pos[l], cmp_w1[l], cmp_w2[l])
    mw = _merge_weights(w_up_a[l], w_up_b[l], w_out[l], norm_ffn[l], w_router_grp[l], b_router_grp[l],
                        w_router_exp[l], b_router_exp[l])

    xp2d = x_prompt.reshape(n_p, D_MODEL)
    zp = _project(xp2d, jnp.tile(pos_p, Bp), norm_mix[l], w_perm, b_perm, 2048)
    kc, vc, ks, vs, kw, vw = _split_kv(zp, Bp, Tp)
    o_a = _nsa_prompt(zp, _compress_prompt(kc, vc, cw), Bp, Tp)
    h_b, c_p, n_p_state, m_p = _mlstm_prompt(zp, mlstm_norm[l], Bp, Tp)
    bufs = _merge(xp2d, o_a, h_b, zp, mw, n_p + n_s, 0, 256)
    wl = min(WINDOW, Tp)
    prompt_new = (kc, vc, ks, vs, kw[:, -wl:], vw[:, -wl:], c_p, n_p_state, m_p)

    xs2d = x_sample.reshape(n_s, D_MODEL)
    zs = _project(xs2d, jnp.repeat(pos_s, Bs), norm_mix[l], w_perm, b_perm, 128)
    kc, vc, ks, vs, kw, vw = _split_kv(zs, Bs, Ts)
    assert (past_len + Ts - CMP_LEN) // CMP_STRIDE + 1 == past_len // CMP_STRIDE - 1
    zk_pool, zv_pool = _compress_pool(cache_cmp_k[l], cache_cmp_v[l], page_table, cw)
    o_a_s = _nsa_sample(zs, page_table, zk_pool, zv_pool, cache_sel_k[l], cache_sel_v[l],
                        cache_win_k[l], cache_win_v[l], cw[2])
    h_s, c_s, n_s_state, m_s = _mlstm_sample(zs, mlstm_norm[l], state_mlstm_c[l], state_mlstm_n[l],
                                             state_mlstm_m[l])
    x2, xn, route = _merge(xs2d, o_a_s, h_s, zs, mw, n_p + n_s, n_p, 128, bufs=bufs)
    wk = jnp.concatenate([cache_win_k[l], kw], axis=1)
    wv = jnp.concatenate([cache_win_v[l], vw], axis=1)
    wl = min(WINDOW, past_len + Ts)
    sample_new = (kc, vc, ks, vs, wk[:, -wl:], wv[:, -wl:], c_s, n_s_state, m_s)

    y_p, y_s = _moe(x2, xn, route, w_gate[l], w_up[l], w_down[l], norm_final, n_p)
    return ((y_p.reshape(Bp, Tp, D_MODEL), y_s.reshape(Bs, Ts, D_MODEL))
            + tuple(a[None] for a in prompt_new) + tuple(a[None] for a in sample_new))
```

```python
import functools

import numpy as np
import jax
import jax.numpy as jnp
from jax import lax
from jax.experimental import pallas as pl
from jax.experimental.pallas import tpu as pltpu

D_MODEL = 2048
PAGE_SIZE = 128
N_HEADS_A = 8
N_KV = 2
GROUP = N_HEADS_A // N_KV
HEAD_DIM = 128
ROT_DIM = HEAD_DIM // 4
ROPE_THETA = 500000.0
CMP_LEN = 32
CMP_STRIDE = 16
SEL_LEN = 64
SEL_TOP = 16
FORCE_BONUS = 100.0
WINDOW = 512
N_HEADS_B = 4
DQK_B = 128
DV_B = 256
MLSTM_CHUNK = 64
GATE_CAP = 15.0
N_GROUPS = 4
EXPERTS_PER_GROUP = 8
N_EXPERTS = N_GROUPS * EXPERTS_PER_GROUP
TOP_K = 2
D_EXPERT = 1024
EPS = 1e-6
NEG = -1e30

A_Q = N_HEADS_A * HEAD_DIM
A_KV = N_KV * HEAD_DIM
B_QK = N_HEADS_B * DQK_B
B_V = N_HEADS_B * DV_B
IN_SPLITS = (A_Q, A_KV, A_KV, A_KV, A_KV, A_KV, A_KV, 3 * N_HEADS_A,
             B_QK, B_QK, B_V, N_HEADS_B, N_HEADS_B, B_V, D_MODEL, D_MODEL)

F32 = jnp.float32
BF16 = jnp.bfloat16
VMEM_LIMIT = 48 * 1024 * 1024
VMEM_LIMIT_MERGE = 56 * 1024 * 1024

_CUTS = np.concatenate([[0], np.cumsum(IN_SPLITS)]).tolist()
_SRC = dict(zip(('q_a', 'k_c', 'v_c', 'k_s', 'v_s', 'k_w', 'v_w', 'g_nsa', 'q_b', 'k_b', 'v_b', 'i_b', 'f_b',
                 'o_b', 'g_ma', 'g_mb'), zip(_CUTS[:-1], _CUTS[1:])))
_Z_ORDER = ('g_ma', 'g_mb', 'q_a', 'q_a', 'v_b', 'o_b', 'k_c', 'v_c', 'k_s', 'v_s', 'k_w', 'v_w', 'q_b', 'k_b')
PROJ_TN = 512
Z_GMA, Z_GMB, Z_QRAW, Z_QROT, Z_VB, Z_OB = 0, 2048, 4096, 5120, 6144, 7168
Z_KC, Z_VC, Z_KS, Z_VS, Z_KW, Z_VW = 8192, 8448, 8704, 8960, 9216, 9472
Z_QB, Z_KB, Z_SMALL = 9728, 10240, 10752
N_GATE = 3 * GROUP
Z_GNSA, Z_IB, Z_FB = Z_SMALL, Z_SMALL + 2 * HEAD_DIM, Z_SMALL + 2 * HEAD_DIM + N_HEADS_B
NZ = 11264
_ROPE_ALL_TILES = (Z_QROT // PROJ_TN, Z_QROT // PROJ_TN + 1)
_ROPE_HALF_TILES = (Z_KS // PROJ_TN, Z_KW // PROJ_TN)

ROUTE_W = 128
MOE_TILE = 768
MOE_SUB = 256
MOE_FC = 512


def _rms(x, g):
    return x * lax.rsqrt(jnp.mean(x * x, axis=-1, keepdims=True) + EPS) * g


def _proj_kernel(x_ref, g_ref, w_ref, b_ref, cos_ref, sa_ref, sb_ref, z_ref, xn_ref):
    j = pl.program_id(1)

    @pl.when(j == 0)
    def _():
        xn_ref[...] = _rms(x_ref[...], g_ref[...]).astype(BF16)

    acc = jnp.dot(xn_ref[...], w_ref[...], preferred_element_type=F32) + b_ref[...]

    def rope(blk):
        return (blk * cos_ref[...] + pltpu.roll(blk, HEAD_DIM - ROT_DIM // 2, 1) * sa_ref[...]
                + pltpu.roll(blk, ROT_DIM // 2, 1) * sb_ref[...])

    is_all = (j == _ROPE_ALL_TILES[0]) | (j == _ROPE_ALL_TILES[1])
    is_half = (j == _ROPE_HALF_TILES[0]) | (j == _ROPE_HALF_TILES[1])
    n_blk = PROJ_TN // HEAD_DIM

    @pl.when(is_all)
    def _():
        for c in range(n_blk):
            z_ref[:, c * HEAD_DIM:(c + 1) * HEAD_DIM] = rope(acc[:, c * HEAD_DIM:(c + 1) * HEAD_DIM])

    @pl.when(is_half)
    def _():
        for c in range(n_blk // 2):
            z_ref[:, c * HEAD_DIM:(c + 1) * HEAD_DIM] = rope(acc[:, c * HEAD_DIM:(c + 1) * HEAD_DIM])
        z_ref[:, PROJ_TN // 2:] = acc[:, PROJ_TN // 2:]

    @pl.when(jnp.logical_not(is_all | is_half))
    def _():
        z_ref[...] = acc


def _rope_tables(pos):
    half = ROT_DIM // 2
    inv = ROPE_THETA ** (-jnp.arange(half, dtype=F32) / half)
    ang = pos.astype(F32)[:, None] * inv[None, :]
    cos, sin = jnp.cos(ang), jnp.sin(ang)
    n = pos.shape[0]
    ones = jnp.ones((n, HEAD_DIM - ROT_DIM), F32)
    zeros = jnp.zeros((n, HEAD_DIM - half), F32)
    cos_t = jnp.concatenate([cos, cos, ones], axis=1)
    sa_t = jnp.concatenate([-sin, zeros], axis=1)
    sb_t = jnp.concatenate([jnp.zeros((n, half), F32), sin, jnp.zeros((n, HEAD_DIM - ROT_DIM), F32)], axis=1)
    return cos_t, sa_t, sb_t


def _permute_in_proj(w_in, b_in):
    g0 = _SRC['g_nsa'][0]
    spans = [_SRC[k] for k in _Z_ORDER]
    spans += [(g0, g0 + N_GATE), HEAD_DIM - N_GATE, (g0 + N_GATE, g0 + 2 * N_GATE), HEAD_DIM - N_GATE,
              _SRC['i_b'], _SRC['f_b'], NZ - Z_FB - N_HEADS_B]

    def permute(a, dtype):
        parts = [jnp.zeros((a.shape[0], s), dtype) if isinstance(s, int) else a[:, s[0]:s[1]].astype(dtype)
                 for s in spans]
        return jnp.concatenate(parts, axis=1)

    return permute(w_in, BF16), permute(b_in[None, :], F32)


def _project(x2d, pos_rows, norm_g, w_perm, b_perm, tm):
    m = x2d.shape[0]
    cos_t, sa_t, sb_t = _rope_tables(pos_rows)
    rows = lambda w: pl.BlockSpec((tm, w), lambda i, j: (i, 0), pipeline_mode=pl.Buffered(1))
    return pl.pallas_call(
        _proj_kernel,
        out_shape=jax.ShapeDtypeStruct((m, NZ), F32),
        grid=(m // tm, NZ // PROJ_TN),
        in_specs=[rows(D_MODEL),
                  pl.BlockSpec((1, D_MODEL), lambda i, j: (0, 0)),
                  pl.BlockSpec((D_MODEL, PROJ_TN), lambda i, j: (0, j)),
                  pl.BlockSpec((1, PROJ_TN), lambda i, j: (0, j)),
                  rows(HEAD_DIM), rows(HEAD_DIM), rows(HEAD_DIM)],
        out_specs=pl.BlockSpec((tm, PROJ_TN), lambda i, j: (i, j)),
        scratch_shapes=[pltpu.VMEM((tm, D_MODEL), BF16)],
        compiler_params=pltpu.CompilerParams(dimension_semantics=("parallel", "arbitrary"),
                                             vmem_limit_bytes=VMEM_LIMIT),
        name="proj",
    )(x2d, norm_g.reshape(1, D_MODEL), w_perm, b_perm, cos_t, sa_t, sb_t)


def _compress_body(x_ref, w1, pb, w2, r):
    bias = pb[0:1, :HEAD_DIM] + pb[1:2, HEAD_DIM:]
    outs = []
    for h in range(N_KV):
        xh = jnp.concatenate([x_ref[:, (N_KV * s + h) * HEAD_DIM:(N_KV * s + h + 1) * HEAD_DIM]
                              for s in range(CMP_STRIDE)], axis=1).astype(BF16)
        zz = jnp.dot(xh, w1, preferred_element_type=F32)
        pre = zz[:, :HEAD_DIM] + pltpu.roll(zz[:, HEAD_DIM:], r - 1, 0) + bias
        mid = pre * jax.nn.sigmoid(pre)
        outs.append(jnp.dot(mid.astype(BF16), w2, preferred_element_type=F32))
    return outs


def _compress_kernel(xk_ref, xv_ref, w1_ref, p_ref, w2_ref, o_ref):
    r = xk_ref.shape[0]
    for which, x_ref in enumerate((xk_ref, xv_ref)):
        w1 = w1_ref[which]
        pb = jnp.dot(p_ref[which], w1, preferred_element_type=F32)
        for h, o in enumerate(_compress_body(x_ref, w1, pb, w2_ref[which], r)):
            o_ref[which, h] = o


def _compress_weights(cmp_pos, cmp_w1, cmp_w2):
    assert CMP_LEN == 2 * CMP_STRIDE
    half = CMP_STRIDE * HEAD_DIM
    w1 = jnp.concatenate([cmp_w1[:, :half], cmp_w1[:, half:]], axis=2).astype(BF16)
    p = cmp_pos.reshape(2, 2, half)
    p = jnp.concatenate([p, jnp.zeros((2, 6, half), p.dtype)], axis=1).astype(BF16)
    return w1, p, cmp_w2.astype(BF16)


def _compress_prompt(kc, vc, cw):
    B, T = kc.shape[:2]
    r = T // CMP_STRIDE
    width = CMP_STRIDE * N_KV * HEAD_DIM
    w1, p, w2 = cw
    full = lambda a: pl.BlockSpec(a.shape, lambda b: (0,) * a.ndim)
    xspec = pl.BlockSpec((None, r, width), lambda b: (b, 0, 0))
    return pl.pallas_call(
        _compress_kernel,
        out_shape=jax.ShapeDtypeStruct((B, 2, N_KV, r, HEAD_DIM), F32),
        grid=(B,),
        in_specs=[xspec, xspec, full(w1), full(p), full(w2)],
        out_specs=pl.BlockSpec((None, 2, N_KV, r, HEAD_DIM), lambda b: (b, 0, 0, 0, 0)),
        compiler_params=pltpu.CompilerParams(dimension_semantics=("parallel",), vmem_limit_bytes=VMEM_LIMIT),
        name="compress_prompt",
    )(kc.reshape(B, r, width), vc.reshape(B, r, width), w1, p, w2)


def _group_scores(q, k, tq):
    s = lax.dot_general(q, k, (((1,), (1,)), ((), ())), preferred_element_type=F32) * HEAD_DIM ** -0.5
    return s.reshape(GROUP, tq, k.shape[0])


def _stack_heads(ref):
    return jnp.concatenate([ref[:, g * HEAD_DIM:(g + 1) * HEAD_DIM] for g in range(GROUP)], axis=0).astype(BF16)


def _importance(psum, n_sel):
    r = psum.shape[1]
    ci = lax.broadcasted_iota(jnp.int32, (r, n_sel), 0) * CMP_STRIDE
    cj = lax.broadcasted_iota(jnp.int32, (r, n_sel), 1) * SEL_LEN
    cover_t = ((ci < cj + SEL_LEN) & (ci + (CMP_LEN - 1) >= cj)).astype(BF16)
    p_hi = psum.astype(BF16)
    p_lo = (psum - p_hi.astype(F32)).astype(BF16)
    return jnp.dot(p_hi, cover_t, preferred_element_type=F32) + jnp.dot(p_lo, cover_t, preferred_element_type=F32)


def _block_scores(imp, blk, qpos):
    cur = jnp.right_shift(qpos, SEL_LEN.bit_length() - 1)
    forced = (blk == 0) | (blk == cur) | (blk == cur - 1)
    return jnp.where(blk * SEL_LEN <= qpos, imp + jnp.where(forced, FORCE_BONUS, 0.0), -1.0)


def _top_rank_mask(score, axis, n_cand):
    idx = lax.broadcasted_iota(jnp.int32, score.shape, axis)
    rank = jnp.zeros(score.shape, F32)
    for j in range(n_cand):
        cand = score[j:j + 1, :] if axis == 0 else score[:, j:j + 1]
        rank = rank + jnp.where((cand > score) | ((cand == score) & (idx > j)), 1.0, 0.0)
    return jnp.where(rank < float(SEL_TOP), 1.0, 0.0)


def _nsa_prompt_kernel(qraw_ref, qrot_ref, ks_ref, vs_ref, kw_ref, vw_ref, kc_ref, vc_ref, gate_ref, o_ref,
                       m_ref, l_ref, acc_ref, *, tq, tk, seq_len):
    q0 = pl.program_id(2) * tq
    r = kc_ref.shape[0]
    n_sel = seq_len // SEL_LEN
    sel_shift = SEL_LEN.bit_length() - 1
    qpos = q0 + lax.broadcasted_iota(jnp.int32, (tq, 1), 0)

    s = _group_scores(_stack_heads(qraw_ref), kc_ref[...].astype(BF16), tq)
    c_end = lax.broadcasted_iota(jnp.int32, (tq, r), 1) * CMP_STRIDE + (CMP_LEN - 1)
    cmask = c_end <= qpos
    s = s + jnp.where(cmask, 0.0, NEG)[None]
    p = jnp.exp(s - jnp.max(s, axis=-1, keepdims=True)) * jnp.where(cmask, 1.0, 0.0)[None]
    p = p / jnp.maximum(jnp.sum(p, axis=-1, keepdims=True), 1e-30)
    o_cmp = jnp.dot(p.reshape(GROUP * tq, r).astype(BF16), vc_ref[...].astype(BF16), preferred_element_type=F32)
    psum = p[0]
    for g in range(1, GROUP):
        psum = psum + p[g]
    imp_t = _importance(psum, HEAD_DIM).T[:n_sel]
    blk_t = lax.broadcasted_iota(jnp.int32, (n_sel, tq), 0)
    qpos_t = q0 + lax.broadcasted_iota(jnp.int32, (n_sel, tq), 1)
    sel_b = _top_rank_mask(_block_scores(imp_t, blk_t, qpos_t), 0, n_sel).astype(BF16)

    q_rot = _stack_heads(qrot_ref)
    m_ref[...] = jnp.full(m_ref.shape, NEG, F32)
    l_ref[...] = jnp.zeros(l_ref.shape, F32)
    acc_ref[...] = jnp.zeros(acc_ref.shape, F32)

    def sel_tile(kt, carry):
        k0 = pl.multiple_of(kt * tk, tk)
        k = ks_ref[pl.ds(k0, tk), :].astype(BF16)
        v = vs_ref[pl.ds(k0, tk), :].astype(BF16)
        s2 = _group_scores(q_rot, k, tq)
        ej = lax.broadcasted_iota(jnp.int32, (n_sel, tk), 0)
        et = k0 + lax.broadcasted_iota(jnp.int32, (n_sel, tk), 1)
        expand = jnp.where(jnp.right_shift(et, sel_shift) == ej, 1.0, 0.0).astype(BF16)
        selm = lax.dot_general(sel_b, expand, (((0,), (0,)), ((), ())), preferred_element_type=F32)
        kpos = k0 + lax.broadcasted_iota(jnp.int32, (tq, tk), 1)
        ok = (selm > 0.5) & (kpos <= qpos)
        s2 = s2 + jnp.where(ok, 0.0, NEG)[None]
        m_prev = m_ref[...]
        m_new = jnp.maximum(m_prev, jnp.max(s2, axis=-1, keepdims=True))
        alpha = jnp.exp(m_prev - m_new)
        p2 = jnp.exp(s2 - m_new)
        l_ref[...] = alpha * l_ref[...] + jnp.sum(p2, axis=-1, keepdims=True)
        pv = jnp.dot(p2.reshape(GROUP * tq, tk).astype(BF16), v, preferred_element_type=F32)
        acc_ref[...] = alpha.reshape(GROUP * tq, 1) * acc_ref[...] + pv
        m_ref[...] = m_new
        return carry

    lax.fori_loop(0, lax.div(q0 + tq + tk - 1, tk), sel_tile, 0)
    o_sel = acc_ref[...] / jnp.maximum(l_ref[...].reshape(GROUP * tq, 1), 1e-30)

    span = WINDOW + tq
    w0 = pl.multiple_of(jnp.maximum(q0 - WINDOW, 0), tq)
    s3 = _group_scores(q_rot, kw_ref[pl.ds(w0, span), :].astype(BF16), tq)
    kp = w0 + lax.broadcasted_iota(jnp.int32, (tq, span), 1)
    okw = (kp <= qpos) & (kp >= qpos - WINDOW)
    s3 = s3 + jnp.where(okw, 0.0, NEG)[None]
    p3 = jnp.exp(s3 - jnp.max(s3, axis=-1, keepdims=True))
    p3 = p3 / jnp.sum(p3, axis=-1, keepdims=True)
    o_win = jnp.dot(p3.reshape(GROUP * tq, span).astype(BF16), vw_ref[pl.ds(w0, span), :].astype(BF16),
                    preferred_element_type=F32)

    gts = jax.nn.sigmoid(gate_ref[...])
    for g in range(GROUP):
        rows = slice(g * tq, (g + 1) * tq)
        o_ref[:, g * HEAD_DIM:(g + 1) * HEAD_DIM] = (gts[:, 3 * g:3 * g + 1] * o_cmp[rows]
                                                     + gts[:, 3 * g + 1:3 * g + 2] * o_sel[rows]
                                                     + gts[:, 3 * g + 2:3 * g + 3] * o_win[rows]).astype(o_ref.dtype)


def _nsa_prompt(z, cmp, B, T, tq=256, tk=512):
    assert T % tk == 0 and tk % tq == 0 and tq % HEAD_DIM == 0 and T >= WINDOW + tq
    assert SEL_LEN & (SEL_LEN - 1) == 0 and tk % SEL_LEN == 0
    assert (T // SEL_LEN) % 8 == 0 and T // SEL_LEN <= HEAD_DIM
    nq = T // tq
    r = cmp.shape[3]
    gw = GROUP * HEAD_DIM
    qspec = lambda off: pl.BlockSpec((tq, gw), lambda b, h, q: (b * nq + q, off // gw + h))
    kvspec = lambda off: pl.BlockSpec((T, HEAD_DIM), lambda b, h, q: (b, off // HEAD_DIM + h))
    cspec = lambda which: pl.BlockSpec((None, None, None, r, HEAD_DIM), lambda b, h, q: (b, which, h, 0, 0))
    return pl.pallas_call(
        functools.partial(_nsa_prompt_kernel, tq=tq, tk=tk, seq_len=T),
        out_shape=jax.ShapeDtypeStruct((B * T, A_Q), BF16),
        grid=(B, N_KV, nq),
        in_specs=[qspec(Z_QRAW), qspec(Z_QROT), kvspec(Z_KS), kvspec(Z_VS), kvspec(Z_KW), kvspec(Z_VW),
                  cspec(0), cspec(1),
                  pl.BlockSpec((tq, HEAD_DIM), lambda b, h, q: (b * nq + q, Z_GNSA // HEAD_DIM + h))],
        out_specs=pl.BlockSpec((tq, gw), lambda b, h, q: (b * nq + q, h)),
        scratch_shapes=[pltpu.VMEM((GROUP, tq, 1), F32), pltpu.VMEM((GROUP, tq, 1), F32),
                        pltpu.VMEM((GROUP * tq, HEAD_DIM), F32)],
        compiler_params=pltpu.CompilerParams(dimension_semantics=("parallel", "parallel", "arbitrary"),
                                             vmem_limit_bytes=VMEM_LIMIT),
        name="nsa_prompt",
    )(z, z, z, z, z, z, cmp, cmp, z)


def _gate_logs(i_pre, f_pre):
    log_i = GATE_CAP * jnp.tanh(i_pre / GATE_CAP)
    fc = GATE_CAP * jnp.tanh(f_pre / GATE_CAP)
    log_f = jnp.minimum(fc, 0.0) - jnp.log1p(jnp.exp(-jnp.abs(fc)))
    return log_i, log_f


def _mlstm_prompt_kernel(q_ref, k_ref, v_ref, op_ref, grow_ref, gcol_ref, gain_ref,
                         h_ref, c_ref, n_ref, m_ref):
    @pl.when(pl.program_id(1) == 0)
    def _():
        c_ref[...] = jnp.zeros(c_ref.shape, F32)
        n_ref[...] = jnp.zeros(n_ref.shape, F32)
        m_ref[...] = jnp.zeros(m_ref.shape, F32)

    for h in range(N_HEADS_B):
        qk_cols = slice(h * DQK_B, (h + 1) * DQK_B)
        v_cols = slice(h * DV_B, (h + 1) * DV_B)
        _mlstm_chunk(q_ref[:, qk_cols], k_ref[:, qk_cols], v_ref[:, v_cols], op_ref[:, v_cols],
                     grow_ref[h:h + 1, :], grow_ref[N_HEADS_B + h:N_HEADS_B + h + 1, :],
                     gcol_ref[:, h:h + 1], gcol_ref[:, N_HEADS_B + h:N_HEADS_B + h + 1], gain_ref[h],
                     h_ref.at[:, v_cols], c_ref.at[h], n_ref.at[h], m_ref.at[h])


def _mlstm_chunk(q, k, v, o_pre, i_row, f_row, i_col, f_col, gain, h_ref, c_ref, n_ref, m_ref):
    L = q.shape[0]
    qb = q.astype(BF16)
    k = k * DQK_B ** -0.5
    vb = v.astype(BF16)
    i_r, f_r = _gate_logs(i_row, f_row)
    i_c, f_c = _gate_logs(i_col, f_col)
    li = lax.broadcasted_iota(jnp.int32, (L, L), 0)
    si = lax.broadcasted_iota(jnp.int32, (L, L), 1)
    tri = si <= li
    bcum_c = jnp.sum(jnp.where(tri, f_r, 0.0), axis=1, keepdims=True)
    bcum_r = jnp.sum(jnp.where(li <= si, f_c, 0.0), axis=0, keepdims=True)
    btot = jnp.sum(f_r, axis=1, keepdims=True)
    c_old, n_old, m_old = c_ref[...], n_ref[...], m_ref[0:1, 0:1]

    dlog = jnp.where(tri, bcum_c - bcum_r + i_r, NEG)
    inter = bcum_c + m_old
    m_t = jnp.maximum(inter, jnp.max(dlog, axis=1, keepdims=True))
    qk = lax.dot_general(qb, k.astype(BF16), (((1,), (1,)), ((), ())), preferred_element_type=F32)
    w_intra = jnp.exp(dlog - m_t) * qk
    w_inter = jnp.exp(inter - m_t)
    num = (w_inter * jnp.dot(qb, c_old.astype(BF16), preferred_element_type=F32)
           + jnp.dot(w_intra.astype(BF16), vb, preferred_element_type=F32))
    den = w_inter * jnp.sum(q * n_old, axis=1, keepdims=True) + jnp.sum(w_intra, axis=1, keepdims=True)
    h = num / jnp.maximum(jnp.abs(den), jnp.exp(-m_t))
    h_ref[...] = (_rms(h, gain) * jax.nn.sigmoid(o_pre)).astype(h_ref.dtype)

    wlog_c = btot - bcum_c + i_c
    m_new = jnp.maximum(btot + m_old, jnp.max(btot - bcum_r + i_r, axis=1, keepdims=True))
    decay = jnp.exp(btot + m_old - m_new)
    kw = k * jnp.exp(wlog_c - m_new)
    c_ref[...] = decay * c_old + lax.dot_general(kw.astype(BF16), vb, (((0,), (0,)), ((), ())),
                                                 preferred_element_type=F32)
    n_ref[...] = decay * n_old + jnp.sum(kw, axis=0, keepdims=True)
    m_ref[...] = jnp.broadcast_to(m_new, m_ref.shape)


def _mlstm_prompt(z, gain, B, T):
    L = MLSTM_CHUNK
    assert T % L == 0
    nc, H = T // L, N_HEADS_B
    gcol = z[:, Z_IB:Z_IB + 2 * H].reshape(B * nc, L, 2 * H)
    grow = gcol.transpose(0, 2, 1)
    rows = lambda w, off: pl.BlockSpec((L, w), lambda b, c: (b * nc + c, off // w))
    gidx = lambda b, c: (b * nc + c, 0, 0)
    state = lambda *s: pl.BlockSpec((None, H) + s, lambda b, c: (b, 0, 0, 0))
    h, c, n, m = pl.pallas_call(
        _mlstm_prompt_kernel,
        out_shape=(jax.ShapeDtypeStruct((B * T, B_V), BF16), jax.ShapeDtypeStruct((B, H, DQK_B, DV_B), F32),
                   jax.ShapeDtypeStruct((B, H, 1, DQK_B), F32), jax.ShapeDtypeStruct((B, H, 1, DQK_B), F32)),
        grid=(B, nc),
        in_specs=[rows(B_QK, Z_QB), rows(B_QK, Z_KB), rows(B_V, Z_VB), rows(B_V, Z_OB),
                  pl.BlockSpec((None, 2 * H, L), gidx), pl.BlockSpec((None, L, 2 * H), gidx),
                  pl.BlockSpec((H, 1, DV_B), lambda b, c: (0, 0, 0))],
        out_specs=(rows(B_V, 0), state(DQK_B, DV_B), state(1, DQK_B), state(1, DQK_B)),
        compiler_params=pltpu.CompilerParams(dimension_semantics=("parallel", "arbitrary"),
                                             vmem_limit_bytes=VMEM_LIMIT),
        name="mlstm_prompt",
    )(z, z, z, z, grow, gcol, gain.reshape(H, 1, DV_B))
    return h, c, n[:, :, 0], m[:, :, 0, 0]


def _merge_kernel(x_ref, oa_ref, hb_ref, gma_ref, gmb_ref, wa_ref, wb_ref, wo_ref, nf_ref, wrh_ref, wrl_ref, br_ref,
                  *refs):
    x2_ref, xn_ref, route_ref = refs[-3:]
    ua = jnp.dot(oa_ref[...], wa_ref[...], preferred_element_type=F32)
    ub = jnp.dot(hb_ref[...], wb_ref[...], preferred_element_type=F32)
    merged = jax.nn.sigmoid(gma_ref[...]) * ua + jax.nn.sigmoid(gmb_ref[...]) * ub
    x2 = x_ref[...] + jnp.dot(merged.astype(BF16), wo_ref[...], preferred_element_type=F32)
    x2_ref[...] = x2
    xn = _rms(x2, nf_ref[...])
    xn_ref[...] = xn
    hi = xn.astype(BF16)
    lo = (xn - hi.astype(F32)).astype(BF16)
    lg = (jnp.dot(hi, wrh_ref[...], preferred_element_type=F32) + jnp.dot(lo, wrh_ref[...], preferred_element_type=F32)
          + jnp.dot(hi, wrl_ref[...], preferred_element_type=F32)) + br_ref[...]
    lane = lax.broadcasted_iota(jnp.int32, lg.shape, 1)
    lanef = lane.astype(F32)
    ninf = -jnp.inf
    gmask = (lane >= N_EXPERTS) & (lane < N_EXPERTS + N_GROUPS)
    gl = jnp.where(gmask, lg, ninf)
    gmax = jnp.max(gl, axis=1, keepdims=True)
    gidx = jnp.min(jnp.where(gl == gmax, lanef, 1e9), axis=1, keepdims=True) - N_EXPERTS
    gprob = 1.0 / jnp.sum(jnp.where(gmask, jnp.exp(lg - gmax), 0.0), axis=1, keepdims=True)
    e_lo = gidx * EXPERTS_PER_GROUP
    el = jnp.where((lanef >= e_lo) & (lanef < e_lo + EXPERTS_PER_GROUP), lg, ninf)
    v1 = jnp.max(el, axis=1, keepdims=True)
    i1 = jnp.min(jnp.where(el == v1, lanef, 1e9), axis=1, keepdims=True)
    el2 = jnp.where(lanef == i1, ninf, el)
    v2 = jnp.max(el2, axis=1, keepdims=True)
    i2 = jnp.min(jnp.where(el2 == v2, lanef, 1e9), axis=1, keepdims=True)
    e2 = jnp.exp(v2 - v1)
    g1 = gprob / (1.0 + e2)
    g2 = gprob * e2 / (1.0 + e2)
    route_ref[...] = jnp.where(lane == 0, i1, jnp.where(lane == 1, i2, jnp.where(lane == 2, g1,
                               jnp.where(lane == 3, g2, 0.0))))


def _merge_weights(w_up_a, w_up_b, w_out, norm_ffn, w_rg, b_rg, w_re, b_re):
    assert TOP_K == 2
    pad = ROUTE_W - N_EXPERTS - N_GROUPS
    w = jnp.concatenate([w_re, w_rg, jnp.zeros((D_MODEL, pad), F32)], axis=1)
    b = jnp.concatenate([b_re, b_rg, jnp.zeros((pad,), F32)]).reshape(1, ROUTE_W)
    hi = w.astype(BF16)
    return (w_up_a.astype(BF16), w_up_b.astype(BF16), w_out.astype(BF16), norm_ffn.reshape(1, D_MODEL).astype(F32),
            hi, (w - hi.astype(F32)).astype(BF16), b)


def _merge(x2d, o_a, h_b, z, mw, n_total, row0, tm, bufs=None):
    m = x2d.shape[0]
    i0 = row0 // tm
    row = lambda w, c=0: pl.BlockSpec((tm, w), lambda i: (i, c))
    const = lambda a: pl.BlockSpec(a.shape, lambda i: (0,) * a.ndim, pipeline_mode=pl.Buffered(1))
    outw = (D_MODEL, D_MODEL, ROUTE_W)
    in_specs = [row(D_MODEL), row(A_Q), row(B_V), row(D_MODEL, Z_GMA // D_MODEL), row(D_MODEL, Z_GMB // D_MODEL)]
    in_specs += [const(a) for a in mw]
    args = [x2d, o_a, h_b, z, z] + list(mw)
    aliases = {}
    if bufs is not None:
        in_specs += [pl.BlockSpec(memory_space=pl.ANY)] * 3
        aliases = {len(args) + k: k for k in range(3)}
        args += list(bufs)
    return pl.pallas_call(
        _merge_kernel,
        out_shape=tuple(jax.ShapeDtypeStruct((n_total, w), F32) for w in outw),
        grid=(m // tm,),
        in_specs=in_specs,
        out_specs=tuple(pl.BlockSpec((tm, w), lambda i: (i0 + i, 0)) for w in outw),
        input_output_aliases=aliases,
        compiler_params=pltpu.CompilerParams(dimension_semantics=("parallel",), vmem_limit_bytes=VMEM_LIMIT_MERGE),
        name="merge_route",
    )(*args)


def _moe_plan(route, n_tiles):
    n = route.shape[0]
    a = n * TOP_K
    flat_e = route[:, 0:TOP_K].astype(jnp.int32).reshape(a)
    onehot = (flat_e[:, None] == jnp.arange(N_EXPERTS, dtype=jnp.int32)[None, :]).astype(jnp.int32)
    rank = jnp.take_along_axis(jnp.cumsum(onehot, axis=0) - onehot, flat_e[:, None], axis=1)[:, 0]
    counts = jnp.sum(onehot, axis=0)
    ntile = (counts + MOE_TILE - 1) // MOE_TILE
    tile_end = jnp.cumsum(ntile)
    dest = ((tile_end - ntile)[flat_e] * MOE_TILE + rank).astype(jnp.int32)
    n_used = tile_end[-1].astype(jnp.int32)
    t = jnp.minimum(jnp.arange(n_tiles, dtype=jnp.int32), n_used - 1)
    tile_expert = jnp.minimum(jnp.searchsorted(tile_end, t, side='right'), N_EXPERTS - 1).astype(jnp.int32)
    first = (tile_end - ntile)[tile_expert]
    tile_rows = jnp.clip(counts[tile_expert] - (t - first) * MOE_TILE, 0, MOE_TILE)
    tile_rows = jnp.where(jnp.arange(n_tiles) < n_used, tile_rows, 0).astype(jnp.int32)
    return dest, tile_expert, tile_rows, n_used.reshape(1)


DMA_UNROLL = 16


def _dispatch_kernel(dest_ref, x_ref, o_hbm, sem):
    i = pl.program_id(0)
    tm = x_ref.shape[0]

    def copy(r, k):
        d = dest_ref[(i * tm + r) * TOP_K + k]
        return pltpu.make_async_copy(x_ref.at[pl.ds(r, 1)], o_hbm.at[pl.ds(d, 1)], sem)

    def start(r, c):
        for k in range(TOP_K):
            copy(r, k).start()
        return c

    def wait(r, c):
        for k in range(TOP_K):
            copy(r, k).wait()
        return c

    lax.fori_loop(0, tm, start, 0, unroll=DMA_UNROLL)
    lax.fori_loop(0, tm, wait, 0, unroll=DMA_UNROLL)


def _moe_dispatch(xn, dest, n_tiles, tm=128):
    n, d = xn.shape
    return pl.pallas_call(
        _dispatch_kernel,
        out_shape=jax.ShapeDtypeStruct((n_tiles * MOE_TILE, d), xn.dtype),
        grid_spec=pltpu.PrefetchScalarGridSpec(
            num_scalar_prefetch=1, grid=(n // tm,),
            in_specs=[pl.BlockSpec((tm, d), lambda i, ds: (i, 0))],
            out_specs=pl.BlockSpec(memory_space=pl.ANY),
            scratch_shapes=[pltpu.SemaphoreType.DMA(())]),
        compiler_params=pltpu.CompilerParams(dimension_semantics=("arbitrary",), vmem_limit_bytes=VMEM_LIMIT),
        name="moe_dispatch",
    )(dest, xn)


def _expert_kernel(te_ref, cnt_ref, nu_ref, x_ref, wg_ref, wu_ref, wd_ref, o_ref):
    t, j = pl.program_id(0), pl.program_id(1)
    cnt = cnt_ref[t]

    @pl.when(cnt > 0)
    def _():
        wg, wu, wd = wg_ref[...].astype(BF16), wu_ref[...].astype(BF16), wd_ref[...].astype(BF16)

        def run(rows):
            xb = x_ref[rows].astype(BF16)
            g = jnp.dot(xb, wg, preferred_element_type=F32)
            u = jnp.dot(xb, wu, preferred_element_type=F32)
            h = (g * jax.nn.sigmoid(g) * u).astype(BF16)
            y = jnp.dot(h, wd, preferred_element_type=F32)

            @pl.when(j == 0)
            def _():
                o_ref[rows] = y

            @pl.when(j > 0)
            def _():
                o_ref[rows] += y

        for s in range(MOE_TILE // MOE_SUB):
            lo, half = s * MOE_SUB, MOE_SUB // 2

            @pl.when(cnt > lo + half)
            def _():
                run(slice(lo, lo + MOE_SUB))

            @pl.when((cnt > lo) & (cnt <= lo + half))
            def _():
                run(slice(lo, lo + half))


def _moe_experts(xs, tile_expert, tile_rows, n_used, w_gate, w_up, w_down):
    n_tiles = xs.shape[0] // MOE_TILE
    nj = D_EXPERT // MOE_FC
    tix = lambda t, j, te, cnt, nu: (jnp.minimum(t, nu[0] - 1), 0)
    jj = lambda t, j, nu: jnp.where(t < nu[0], j, nj - 1)
    return pl.pallas_call(
        _expert_kernel,
        out_shape=jax.ShapeDtypeStruct(xs.shape, F32),
        grid_spec=pltpu.PrefetchScalarGridSpec(
            num_scalar_prefetch=3, grid=(n_tiles, nj),
            in_specs=[pl.BlockSpec((MOE_TILE, D_MODEL), tix),
                      pl.BlockSpec((None, D_MODEL, MOE_FC), lambda t, j, te, cnt, nu: (te[t], 0, jj(t, j, nu))),
                      pl.BlockSpec((None, D_MODEL, MOE_FC), lambda t, j, te, cnt, nu: (te[t], 0, jj(t, j, nu))),
                      pl.BlockSpec((None, MOE_FC, D_MODEL), lambda t, j, te, cnt, nu: (te[t], jj(t, j, nu), 0))],
            out_specs=pl.BlockSpec((MOE_TILE, D_MODEL), tix)),
        compiler_params=pltpu.CompilerParams(dimension_semantics=("arbitrary", "arbitrary"),
                                             vmem_limit_bytes=VMEM_LIMIT_MERGE),
        name="moe_experts",
    )(tile_expert, tile_rows, n_used, xs, w_gate, w_up, w_down)


def _combine_kernel(dest_ref, x2_ref, route_ref, g_ref, y_hbm, op_ref, os_ref, buf_ref, sem, *, n_prompt_tiles):
    i = pl.program_id(0)
    tm = x2_ref.shape[0]

    def copy(r, k):
        d = dest_ref[(i * tm + r) * TOP_K + k]
        return pltpu.make_async_copy(y_hbm.at[pl.ds(d, 1)], buf_ref.at[k, pl.ds(r, 1)], sem)

    def start(r, c):
        for k in range(TOP_K):
            copy(r, k).start()
        return c

    def wait(r, c):
        for k in range(TOP_K):
            copy(r, k).wait()
        return c

    lax.fori_loop(0, tm, start, 0, unroll=DMA_UNROLL)
    lax.fori_loop(0, tm, wait, 0, unroll=DMA_UNROLL)
    ffn = buf_ref[0] * route_ref[:, TOP_K:TOP_K + 1]
    for k in range(1, TOP_K):
        ffn = ffn + buf_ref[k] * route_ref[:, TOP_K + k:TOP_K + k + 1]
    y = _rms(x2_ref[...] + ffn, g_ref[...])

    @pl.when(i < n_prompt_tiles)
    def _():
        op_ref[...] = y

    @pl.when(i >= n_prompt_tiles)
    def _():
        os_ref[...] = y


def _moe_combine(x2, route, dest, ys, norm_final, n_prompt, tm=128):
    n, d = x2.shape
    npt = n_prompt // tm
    nst = (n - n_prompt) // tm
    return pl.pallas_call(
        functools.partial(_combine_kernel, n_prompt_tiles=npt),
        out_shape=(jax.ShapeDtypeStruct((n_prompt, d), F32), jax.ShapeDtypeStruct((n - n_prompt, d), F32)),
        grid_spec=pltpu.PrefetchScalarGridSpec(
            num_scalar_prefetch=1, grid=(n // tm,),
            in_specs=[pl.BlockSpec((tm, d), lambda i, ds: (i, 0)), pl.BlockSpec((tm, ROUTE_W), lambda i, ds: (i, 0)),
                      pl.BlockSpec((1, d), lambda i, ds: (0, 0)), pl.BlockSpec(memory_space=pl.ANY)],
            out_specs=(pl.BlockSpec((tm, d), lambda i, ds: (jnp.minimum(i, npt - 1), 0)),
                       pl.BlockSpec((tm, d), lambda i, ds: (jnp.clip(i - npt, 0, nst - 1), 0))),
            scratch_shapes=[pltpu.VMEM((TOP_K, tm, d), F32), pltpu.SemaphoreType.DMA(())]),
        compiler_params=pltpu.CompilerParams(dimension_semantics=("arbitrary",), vmem_limit_bytes=VMEM_LIMIT),
        name="moe_combine",
    )(dest, x2, route, norm_final.reshape(1, d), ys)


def _moe(x2, xn, route, w_gate, w_up, w_down, norm_final, n_prompt):
    n = x2.shape[0]
    n_tiles = (n * TOP_K + N_EXPERTS * (MOE_TILE - 1) + MOE_TILE - 1) // MOE_TILE
    dest, tile_expert, tile_rows, n_used = _moe_plan(route, n_tiles)
    xs = _moe_dispatch(xn, dest, n_tiles)
    ys = _moe_experts(xs, tile_expert, tile_rows, n_used, w_gate, w_up, w_down)
    return _moe_combine(x2, route, dest, ys, norm_final, n_prompt)


POOL_ROWS = 256
ROWS_PER_PAGE = PAGE_SIZE // CMP_STRIDE
ZW = 2 * N_KV * HEAD_DIM


GROUP_ROWS = N_KV * CMP_STRIDE
GROUP_PITCH = GROUP_ROWS + 8


def _compress_pool_kernel(pt_ref, w1_ref, p_ref, xk_hbm, xv_hbm, zk_ref, zv_ref, xbuf, sem):
    i = pl.program_id(0)
    n = pl.num_programs(0)
    slot = lax.rem(i, 2)
    pages = POOL_ROWS // ROWS_PER_PAGE

    def copies(step, sl):
        out = []
        for pg in range(pages):
            page = pt_ref[step * pages + pg]
            for which, src in enumerate((xk_hbm, xv_hbm)):
                out.append(pltpu.make_async_copy(
                    src.at[page], xbuf.at[sl, which, pl.ds(pg * ROWS_PER_PAGE, ROWS_PER_PAGE), pl.ds(0, GROUP_ROWS), :],
                    sem.at[sl, which]))
        return out

    @pl.when(i == 0)
    def _():
        for cp in copies(0, 0):
            cp.start()

    @pl.when(i + 1 < n)
    def _():
        for cp in copies(i + 1, 1 - slot):
            cp.start()

    for cp in copies(i, slot):
        cp.wait()

    for which, z_ref in enumerate((zk_ref, zv_ref)):
        w1 = w1_ref[which]
        pb = jnp.dot(p_ref[which], w1, preferred_element_type=F32)
        bias = pb[0:1, :HEAD_DIM] + pb[1:2, HEAD_DIM:]
        rows2d = xbuf.at[slot, which].reshape(POOL_ROWS * GROUP_PITCH, HEAD_DIM)
        for h in range(N_KV):
            xh = jnp.concatenate([rows2d[pl.ds(N_KV * s + h, POOL_ROWS, stride=GROUP_PITCH), :]
                                  for s in range(CMP_STRIDE)], axis=1).astype(BF16)
            zz = jnp.dot(xh, w1, preferred_element_type=F32)
            z_ref[:, 2 * h * HEAD_DIM:(2 * h + 1) * HEAD_DIM] = zz[:, :HEAD_DIM] + bias
            z_ref[:, (2 * h + 1) * HEAD_DIM:(2 * h + 2) * HEAD_DIM] = zz[:, HEAD_DIM:]


def _compress_pool(pool_k, pool_v, page_table, cw):
    n_phys = pool_k.shape[0]
    bs, n_pages = page_table.shape
    rows = bs * n_pages * ROWS_PER_PAGE
    assert rows % POOL_ROWS == 0
    w1, p, _ = cw
    full = lambda a: pl.BlockSpec(a.shape, lambda i, pt: (0,) * a.ndim)
    groups = lambda a: a.reshape(n_phys, ROWS_PER_PAGE, GROUP_ROWS, HEAD_DIM)
    zk, zv = pl.pallas_call(
        _compress_pool_kernel,
        out_shape=(jax.ShapeDtypeStruct((rows, ZW), F32),) * 2,
        grid_spec=pltpu.PrefetchScalarGridSpec(
            num_scalar_prefetch=1, grid=(rows // POOL_ROWS,),
            in_specs=[full(w1), full(p), pl.BlockSpec(memory_space=pl.ANY), pl.BlockSpec(memory_space=pl.ANY)],
            out_specs=(pl.BlockSpec((POOL_ROWS, ZW), lambda i, pt: (i, 0)),) * 2,
            scratch_shapes=[pltpu.VMEM((2, 2, POOL_ROWS, GROUP_PITCH, HEAD_DIM), F32),
                            pltpu.SemaphoreType.DMA((2, 2))]),
        compiler_params=pltpu.CompilerParams(dimension_semantics=("arbitrary",), vmem_limit_bytes=VMEM_LIMIT),
        name="compress_pool",
    )(page_table.reshape(bs * n_pages), w1, p, groups(pool_k), groups(pool_v))
    return zk.reshape(bs, n_pages * ROWS_PER_PAGE, ZW), zv.reshape(bs, n_pages * ROWS_PER_PAGE, ZW)


def _nsa_sample_kernel(pt_ref, qraw_ref, qrot_ref, knew_ref, wnew_ref, gate_ref, w2_ref, wk_ref, wv_ref,
                       zk_ref, zv_ref, ks_hbm, vs_hbm, o_ref, sbuf, sem, *, n_pages, past_len):
    b = pl.program_id(0)
    nb = pl.num_programs(0)
    slot = lax.rem(b, 2)
    page_rows = N_KV * PAGE_SIZE

    def copies(seq, sl):
        out = []
        for pg in range(n_pages):
            page = pt_ref[seq, pg]
            for which, ssrc in enumerate((ks_hbm, vs_hbm)):
                out.append(pltpu.make_async_copy(
                    ssrc.at[page], sbuf.at[sl, which, pl.ds(pg * page_rows, page_rows)], sem.at[sl]))
        return out

    @pl.when(b == 0)
    def _():
        for cp in copies(0, 0):
            cp.start()

    @pl.when(b + 1 < nb)
    def _():
        for cp in copies(b + 1, 1 - slot):
            cp.start()

    for cp in copies(b, slot):
        cp.wait()

    scale = HEAD_DIM ** -0.5
    nq = N_KV * GROUP
    r = n_pages * ROWS_PER_PAGE
    qpos = jnp.full((nq, 1), past_len, jnp.int32)
    n_sel_pad = HEAD_DIM
    sel_shift = SEL_LEN.bit_length() - 1
    row_head = lax.broadcasted_iota(jnp.int32, (nq, 1), 0) // GROUP
    per_head = lambda a0, a1: jnp.where(row_head == 0, a0, a1)

    def all_heads(ref):
        return jnp.concatenate([ref[:, j * HEAD_DIM:(j + 1) * HEAD_DIM] for j in range(nq)], axis=0).astype(BF16)

    def scores(q, k):
        return lax.dot_general(q, k, (((1,), (1,)), ((), ())), preferred_element_type=F32) * scale

    cmp = []
    for which, z_ref in enumerate((zk_ref, zv_ref)):
        heads = []
        for h in range(N_KV):
            zh = z_ref[:, 2 * h * HEAD_DIM:(2 * h + 2) * HEAD_DIM]
            pre = zh[:, :HEAD_DIM] + pltpu.roll(zh[:, HEAD_DIM:], r - 1, 0)
            mid = pre * jax.nn.sigmoid(pre)
            heads.append(jnp.dot(mid.astype(BF16), w2_ref[which], preferred_element_type=F32).astype(BF16))
        cmp.append(heads)
    q_raw = all_heads(qraw_ref)
    c_end = lax.broadcasted_iota(jnp.int32, (nq, r), 1) * CMP_STRIDE + (CMP_LEN - 1)
    cmask = c_end <= qpos
    s = per_head(scores(q_raw, cmp[0][0]), scores(q_raw, cmp[0][1])) + jnp.where(cmask, 0.0, NEG)
    p = jnp.exp(s - jnp.max(s, axis=-1, keepdims=True)) * jnp.where(cmask, 1.0, 0.0)
    p = p / jnp.maximum(jnp.sum(p, axis=-1, keepdims=True), 1e-30)
    pb = p.astype(BF16)
    o_cmp = per_head(jnp.dot(pb, cmp[1][0], preferred_element_type=F32),
                     jnp.dot(pb, cmp[1][1], preferred_element_type=F32))
    psum = per_head(jnp.sum(jnp.where(row_head == 0, p, 0.0), axis=0, keepdims=True),
                    jnp.sum(jnp.where(row_head == 1, p, 0.0), axis=0, keepdims=True))
    blk = lax.broadcasted_iota(jnp.int32, (nq, n_sel_pad), 1)
    sel = _top_rank_mask(_block_scores(_importance(psum, n_sel_pad), blk, qpos), 1,
                         past_len // SEL_LEN + 1)

    def attend(q, k, v, k_new, v_new, bias, new_bias):
        s = scores(q, k) + bias
        s_new = jnp.sum(q.astype(F32) * k_new.astype(BF16).astype(F32), axis=1, keepdims=True) * scale + new_bias
        m = jnp.maximum(jnp.max(s, axis=-1, keepdims=True), s_new)
        p = jnp.exp(s - m)
        p_new = jnp.exp(s_new - m)
        l = jnp.sum(p, axis=-1, keepdims=True) + p_new
        o = (jnp.dot(p.astype(BF16), v, preferred_element_type=F32)
             + p_new.astype(BF16).astype(F32) * v_new.astype(BF16).astype(F32))
        return o / jnp.maximum(l, 1e-30)

    def own_head(n_rows):
        col = lax.broadcasted_iota(jnp.int32, (nq, n_rows), 1)
        return jnp.bitwise_and(col, N_KV - 1) == row_head

    q_rot = all_heads(qrot_ref)
    n_rows = N_KV * past_len
    ej = lax.broadcasted_iota(jnp.int32, (n_sel_pad, n_rows), 0)
    et = lax.broadcasted_iota(jnp.int32, (n_sel_pad, n_rows), 1)
    expand = jnp.where(jnp.right_shift(et, sel_shift + 1) == ej, 1.0, 0.0).astype(BF16)
    selm = jnp.dot(sel.astype(BF16), expand, preferred_element_type=F32)
    sel_bias = jnp.where((selm > 0.5) & own_head(n_rows), 0.0, NEG)
    new_blk = past_len // SEL_LEN
    new_bias = jnp.where(sel[:, new_blk:new_blk + 1] > 0.5, 0.0, NEG)
    new_kv = lambda ref, c: per_head(ref[:, c * A_KV:c * A_KV + HEAD_DIM], ref[:, c * A_KV + HEAD_DIM:(c + 1) * A_KV])
    o_sel = attend(q_rot, sbuf[slot, 0].astype(BF16), sbuf[slot, 1].astype(BF16),
                   new_kv(knew_ref, 0), new_kv(knew_ref, 1), sel_bias, new_bias)
    win_bias = jnp.where(own_head(wk_ref.shape[0]), 0.0, NEG)
    o_win = attend(q_rot, wk_ref[...].astype(BF16), wv_ref[...].astype(BF16),
                   new_kv(wnew_ref, 0), new_kv(wnew_ref, 1), win_bias, 0.0)

    gts = jax.nn.sigmoid(gate_ref[...])
    for h in range(N_KV):
        for g in range(GROUP):
            j, c0 = h * GROUP + g, h * HEAD_DIM + 3 * g
            o = (gts[:, c0:c0 + 1] * o_cmp[j:j + 1] + gts[:, c0 + 1:c0 + 2] * o_sel[j:j + 1]
                 + gts[:, c0 + 2:c0 + 3] * o_win[j:j + 1])
            o_ref[:, j * HEAD_DIM:(j + 1) * HEAD_DIM] = o.astype(o_ref.dtype)


def _nsa_sample(zs, page_table, zk_pool, zv_pool, sel_k, sel_v, win_k, win_v, w2):
    bs, n_pages = page_table.shape
    past_len = n_pages * PAGE_SIZE
    wb = win_k.shape[1]
    assert wb <= WINDOW and past_len % SEL_LEN == 0 and past_len // SEL_LEN < HEAD_DIM
    assert N_KV == 2
    zs3 = zs.reshape(bs, 1, NZ)
    zspec = lambda w, off: pl.BlockSpec((None, 1, w), lambda b, pt: (b, 0, off // w))
    anyspec = pl.BlockSpec(memory_space=pl.ANY)
    wspec = pl.BlockSpec((None, N_KV * wb, HEAD_DIM), lambda b, pt: (b, 0, 0))
    partial_sums = pl.BlockSpec((None, n_pages * ROWS_PER_PAGE, ZW), lambda b, pt: (b, 0, 0))
    rows2d = lambda a: a.reshape(a.shape[0], a.shape[1] * N_KV, HEAD_DIM)
    out = pl.pallas_call(
        functools.partial(_nsa_sample_kernel, n_pages=n_pages, past_len=past_len),
        out_shape=jax.ShapeDtypeStruct((bs, 1, A_Q), BF16),
        grid_spec=pltpu.PrefetchScalarGridSpec(
            num_scalar_prefetch=1, grid=(bs,),
            in_specs=[zspec(A_Q, Z_QRAW), zspec(A_Q, Z_QROT), zspec(2 * A_KV, Z_KS), zspec(2 * A_KV, Z_KW),
                      zspec(2 * HEAD_DIM, Z_GNSA), pl.BlockSpec(w2.shape, lambda b, pt: (0, 0, 0)), wspec, wspec,
                      partial_sums, partial_sums, anyspec, anyspec],
            out_specs=pl.BlockSpec((None, 1, A_Q), lambda b, pt: (b, 0, 0)),
            scratch_shapes=[pltpu.VMEM((2, 2, N_KV * past_len, HEAD_DIM), F32), pltpu.SemaphoreType.DMA((2,))]),
        compiler_params=pltpu.CompilerParams(dimension_semantics=("arbitrary",), vmem_limit_bytes=VMEM_LIMIT),
        name="nsa_sample",
    )(page_table, zs3, zs3, zs3, zs3, zs3, w2, rows2d(win_k), rows2d(win_v), zk_pool, zv_pool,
      rows2d(sel_k), rows2d(sel_v))
    return out.reshape(bs, A_Q)


def _mlstm_sample_kernel(q_ref, k_ref, v_ref, op_ref, g_ref, gain_ref, c_ref, n_ref, m_ref,
                         h_ref, co_ref, no_ref, mo_ref):
    eye = (lax.broadcasted_iota(jnp.int32, (DQK_B, DQK_B), 0) == lax.broadcasted_iota(jnp.int32, (DQK_B, DQK_B), 1))
    col = lambda row: jnp.sum(jnp.where(eye, row, 0.0), axis=1, keepdims=True)
    lane = lax.broadcasted_iota(jnp.int32, (1, N_HEADS_B), 1)
    m_out = jnp.zeros((1, N_HEADS_B), F32)
    for h in range(N_HEADS_B):
        q = q_ref[:, h * DQK_B:(h + 1) * DQK_B]
        k = k_ref[:, h * DQK_B:(h + 1) * DQK_B] * DQK_B ** -0.5
        v = v_ref[:, h * DV_B:(h + 1) * DV_B]
        log_i, log_f = _gate_logs(g_ref[:, h:h + 1], g_ref[:, N_HEADS_B + h:N_HEADS_B + h + 1])
        c_old, n_old, m_old = c_ref[h], n_ref[h:h + 1, :], m_ref[:, h:h + 1]
        inter = log_f + m_old
        m_new = jnp.maximum(inter, log_i)
        decay = jnp.exp(inter - m_new)
        w = jnp.exp(log_i - m_new)
        co_ref[h] = decay * c_old + col(w * k) * v
        no_ref[h:h + 1, :] = decay * n_old + w * k
        m_out = jnp.where(lane == h, m_new, m_out)
        w_intra = w * jnp.sum(q * k, axis=1, keepdims=True)
        num = decay * jnp.sum(c_old * col(q), axis=0, keepdims=True) + w_intra * v
        den = decay * jnp.sum(q * n_old, axis=1, keepdims=True) + w_intra
        hh = num / jnp.maximum(jnp.abs(den), jnp.exp(-m_new))
        out = _rms(hh, gain_ref[h]) * jax.nn.sigmoid(op_ref[:, h * DV_B:(h + 1) * DV_B])
        h_ref[:, h * DV_B:(h + 1) * DV_B] = out.astype(h_ref.dtype)
    mo_ref[...] = m_out


def _mlstm_sample(zs, gain, c0, n0, m0):
    bs, H = zs.shape[0], N_HEADS_B
    zs3 = zs.reshape(bs, 1, NZ)
    zspec = lambda w, off: pl.BlockSpec((None, 1, w), lambda b: (b, 0, off // w))
    cspec = pl.BlockSpec((None, H, DQK_B, DV_B), lambda b: (b, 0, 0, 0))
    nspec = pl.BlockSpec((None, H, DQK_B), lambda b: (b, 0, 0))
    mspec = pl.BlockSpec((None, 1, H), lambda b: (b, 0, 0))
    h, c, n, m = pl.pallas_call(
        _mlstm_sample_kernel,
        out_shape=(jax.ShapeDtypeStruct((bs, 1, B_V), BF16), jax.ShapeDtypeStruct(c0.shape, F32),
                   jax.ShapeDtypeStruct(n0.shape, F32), jax.ShapeDtypeStruct((bs, 1, H), F32)),
        grid=(bs,),
        in_specs=[zspec(B_QK, Z_QB), zspec(B_QK, Z_KB), zspec(B_V, Z_VB), zspec(B_V, Z_OB), zspec(HEAD_DIM, Z_IB),
                  pl.BlockSpec((H, 1, DV_B), lambda b: (0, 0, 0)), cspec, nspec, mspec],
        out_specs=(pl.BlockSpec((None, 1, B_V), lambda b: (b, 0, 0)), cspec, nspec, mspec),
        compiler_params=pltpu.CompilerParams(dimension_semantics=("parallel",), vmem_limit_bytes=VMEM_LIMIT),
        name="mlstm_sample",
    )(zs3, zs3, zs3, zs3, zs3, gain.reshape(H, 1, DV_B), c0, n0, m0.reshape(bs, 1, H))
    return h.reshape(bs, B_V), c, n, m.reshape(bs, H)


def _split_kv(z, B, T):
    z = z.reshape(B, T, NZ)
    kv = lambda o: z[..., o:o + A_KV].reshape(B, T, N_KV, HEAD_DIM)
    return kv(Z_KC), kv(Z_VC), kv(Z_KS), kv(Z_VS), kv(Z_KW), kv(Z_VW)


def kernel(x_prompt, x_sample, cache_cmp_k, cache_cmp_v, cache_sel_k, cache_sel_v, cache_win_k, cache_win_v,
           state_mlstm_c, state_mlstm_n, state_mlstm_m, page_table, norm_mix, w_in, b_in, cmp_pos, cmp_w1, cmp_w2,
           mlstm_norm, w_up_a, w_up_b, w_out, norm_ffn, w_router_grp, b_router_grp, w_router_exp, b_router_exp,
           w_gate, w_up, w_down, norm_final):
    assert w_in.shape[0] == 1, "single layer"
    Bp, Tp = x_prompt.shape[:2]
    Bs, Ts = x_sample.shape[:2]
    assert Ts == 1
    n_p, n_s = Bp * Tp, Bs * Ts
    past_len = page_table.shape[1] * PAGE_SIZE
    pos_p = jnp.arange(Tp, dtype=jnp.int32)
    pos_s = past_len + jnp.arange(Ts, dtype=jnp.int32)
    l = 0

    w_perm, b_perm = _permute_in_proj(w_in[l], b_in[l])
    cw = _compress_weights(cmp_pos[l], cmp_w1[l], cmp_w2[l])
    mw = _merge_weights(w_up_a[l], w_up_b[l], w_out[l], norm_ffn[l], w_router_grp[l], b_router_grp[l],
                        w_router_exp[l], b_router_exp[l])

    xp2d = x_prompt.reshape(n_p, D_MODEL)
    zp = _project(xp2d, jnp.tile(pos_p, Bp), norm_mix[l], w_perm, b_perm, 2048)
    kc, vc, ks, vs, kw, vw = _split_kv(zp, Bp, Tp)
    o_a = _nsa_prompt(zp, _compress_prompt(kc, vc, cw), Bp, Tp)
    h_b, c_p, n_p_state, m_p = _mlstm_prompt(zp, mlstm_norm[l], Bp, Tp)
    bufs = _merge(xp2d, o_a, h_b, zp, mw, n_p + n_s, 0, 256)
    wl = min(WINDOW, Tp)
    prompt_new = (kc, vc, ks, vs, kw[:, -wl:], vw[:, -wl:], c_p, n_p_state, m_p)

    xs2d = x_sample.reshape(n_s, D_MODEL)
    zs = _project(xs2d, jnp.repeat(pos_s, Bs), norm_mix[l], w_perm, b_perm, 128)
    kc, vc, ks, vs, kw, vw = _split_kv(zs, Bs, Ts)
    assert (past_len + Ts - CMP_LEN) // CMP_STRIDE + 1 == past_len // CMP_STRIDE - 1
    zk_pool, zv_pool = _compress_pool(cache_cmp_k[l], cache_cmp_v[l], page_table, cw)
    o_a_s = _nsa_sample(zs, page_table, zk_pool, zv_pool, cache_sel_k[l], cache_sel_v[l],
                        cache_win_k[l], cache_win_v[l], cw[2])
    h_s, c_s, n_s_state, m_s = _mlstm_sample(zs, mlstm_norm[l], state_mlstm_c[l], state_mlstm_n[l],
                                             state_mlstm_m[l])
    x2, xn, route = _merge(xs2d, o_a_s, h_s, zs, mw, n_p + n_s, n_p, 128, bufs=bufs)
    wk = jnp.concatenate([cache_win_k[l], kw], axis=1)
    wv = jnp.concatenate([cache_win_v[l], vw], axis=1)
    wl = min(WINDOW, past_len + Ts)
    sample_new = (kc, vc, ks, vs, wk[:, -wl:], wv[:, -wl:], c_s, n_s_state, m_s)

    y_p, y_s = _moe(x2, xn, route, w_gate[l], w_up[l], w_down[l], norm_final, n_p)
    return ((y_p.reshape(Bp, Tp, D_MODEL), y_s.reshape(Bs, Ts, D_MODEL))
            + tuple(a[None] for a in prompt_new) + tuple(a[None] for a in sample_new))
```

```python
import functools

import numpy as np
import jax
import jax.numpy as jnp
from jax import lax
from jax.experimental import pallas as pl
from jax.experimental.pallas import tpu as pltpu

D_MODEL = 2048
PAGE_SIZE = 128
N_HEADS_A = 8
N_KV = 2
GROUP = N_HEADS_A // N_KV
HEAD_DIM = 128
ROT_DIM = HEAD_DIM // 4
ROPE_THETA = 500000.0
CMP_LEN = 32
CMP_STRIDE = 16
SEL_LEN = 64
SEL_TOP = 16
FORCE_BONUS = 100.0
WINDOW = 512
N_HEADS_B = 4
DQK_B = 128
DV_B = 256
MLSTM_CHUNK = 64
GATE_CAP = 15.0
N_GROUPS = 4
EXPERTS_PER_GROUP = 8
N_EXPERTS = N_GROUPS * EXPERTS_PER_GROUP
TOP_K = 2
D_EXPERT = 1024
EPS = 1e-6
NEG = -1e30

A_Q = N_HEADS_A * HEAD_DIM
A_KV = N_KV * HEAD_DIM
B_QK = N_HEADS_B * DQK_B
B_V = N_HEADS_B * DV_B
IN_SPLITS = (A_Q, A_KV, A_KV, A_KV, A_KV, A_KV, A_KV, 3 * N_HEADS_A,
             B_QK, B_QK, B_V, N_HEADS_B, N_HEADS_B, B_V, D_MODEL, D_MODEL)

F32 = jnp.float32
BF16 = jnp.bfloat16
VMEM_LIMIT = 48 * 1024 * 1024
VMEM_LIMIT_MERGE = 56 * 1024 * 1024

_CUTS = np.concatenate([[0], np.cumsum(IN_SPLITS)]).tolist()
_SRC = dict(zip(('q_a', 'k_c', 'v_c', 'k_s', 'v_s', 'k_w', 'v_w', 'g_nsa', 'q_b', 'k_b', 'v_b', 'i_b', 'f_b',
                 'o_b', 'g_ma', 'g_mb'), zip(_CUTS[:-1], _CUTS[1:])))
_Z_ORDER = ('g_ma', 'g_mb', 'q_a', 'q_a', 'v_b', 'o_b', 'k_c', 'v_c', 'k_s', 'v_s', 'k_w', 'v_w', 'q_b', 'k_b')
PROJ_TN = 512
Z_GMA, Z_GMB, Z_QRAW, Z_QROT, Z_VB, Z_OB = 0, 2048, 4096, 5120, 6144, 7168
Z_KC, Z_VC, Z_KS, Z_VS, Z_KW, Z_VW = 8192, 8448, 8704, 8960, 9216, 9472
Z_QB, Z_KB, Z_SMALL = 9728, 10240, 10752
N_GATE = 3 * GROUP
Z_GNSA, Z_IB, Z_FB = Z_SMALL, Z_SMALL + 2 * HEAD_DIM, Z_SMALL + 2 * HEAD_DIM + N_HEADS_B
NZ = 11264
_ROPE_ALL_TILES = (Z_QROT // PROJ_TN, Z_QROT // PROJ_TN + 1)
_ROPE_HALF_TILES = (Z_KS // PROJ_TN, Z_KW // PROJ_TN)

ROUTE_W = 128
MOE_TILE = 768
MOE_SUB = 256
MOE_FC = 512


def _rms(x, g):
    return x * lax.rsqrt(jnp.mean(x * x, axis=-1, keepdims=True) + EPS) * g


def _proj_kernel(x_ref, g_ref, w_ref, b_ref, cos_ref, sa_ref, sb_ref, z_ref, xn_ref):
    j = pl.program_id(1)

    @pl.when(j == 0)
    def _():
        xn_ref[...] = _rms(x_ref[...], g_ref[...]).astype(BF16)

    acc = jnp.dot(xn_ref[...], w_ref[...], preferred_element_type=F32) + b_ref[...]

    def rope(blk):
        return (blk * cos_ref[...] + pltpu.roll(blk, HEAD_DIM - ROT_DIM // 2, 1) * sa_ref[...]
                + pltpu.roll(blk, ROT_DIM // 2, 1) * sb_ref[...])

    is_all = (j == _ROPE_ALL_TILES[0]) | (j == _ROPE_ALL_TILES[1])
    is_half = (j == _ROPE_HALF_TILES[0]) | (j == _ROPE_HALF_TILES[1])
    n_blk = PROJ_TN // HEAD_DIM

    @pl.when(is_all)
    def _():
        for c in range(n_blk):
            z_ref[:, c * HEAD_DIM:(c + 1) * HEAD_DIM] = rope(acc[:, c * HEAD_DIM:(c + 1) * HEAD_DIM])

    @pl.when(is_half)
    def _():
        for c in range(n_blk // 2):
            z_ref[:, c * HEAD_DIM:(c + 1) * HEAD_DIM] = rope(acc[:, c * HEAD_DIM:(c + 1) * HEAD_DIM])
        z_ref[:, PROJ_TN // 2:] = acc[:, PROJ_TN // 2:]

    @pl.when(jnp.logical_not(is_all | is_half))
    def _():
        z_ref[...] = acc


def _rope_tables(pos):
    half = ROT_DIM // 2
    inv = ROPE_THETA ** (-jnp.arange(half, dtype=F32) / half)
    ang = pos.astype(F32)[:, None] * inv[None, :]
    cos, sin = jnp.cos(ang), jnp.sin(ang)
    n = pos.shape[0]
    ones = jnp.ones((n, HEAD_DIM - ROT_DIM), F32)
    zeros = jnp.zeros((n, HEAD_DIM - half), F32)
    cos_t = jnp.concatenate([cos, cos, ones], axis=1)
    sa_t = jnp.concatenate([-sin, zeros], axis=1)
    sb_t = jnp.concatenate([jnp.zeros((n, half), F32), sin, jnp.zeros((n, HEAD_DIM - ROT_DIM), F32)], axis=1)
    return cos_t, sa_t, sb_t


def _permute_in_proj(w_in, b_in):
    g0 = _SRC['g_nsa'][0]
    spans = [_SRC[k] for k in _Z_ORDER]
    spans += [(g0, g0 + N_GATE), HEAD_DIM - N_GATE, (g0 + N_GATE, g0 + 2 * N_GATE), HEAD_DIM - N_GATE,
              _SRC['i_b'], _SRC['f_b'], NZ - Z_FB - N_HEADS_B]

    def permute(a, dtype):
        parts = [jnp.zeros((a.shape[0], s), dtype) if isinstance(s, int) else a[:, s[0]:s[1]].astype(dtype)
                 for s in spans]
        return jnp.concatenate(parts, axis=1)

    return permute(w_in, BF16), permute(b_in[None, :], F32)


def _project(x2d, pos_rows, norm_g, w_perm, b_perm, tm):
    m = x2d.shape[0]
    cos_t, sa_t, sb_t = _rope_tables(pos_rows)
    rows = lambda w: pl.BlockSpec((tm, w), lambda i, j: (i, 0), pipeline_mode=pl.Buffered(1))
    return pl.pallas_call(
        _proj_kernel,
        out_shape=jax.ShapeDtypeStruct((m, NZ), F32),
        grid=(m // tm, NZ // PROJ_TN),
        in_specs=[rows(D_MODEL),
                  pl.BlockSpec((1, D_MODEL), lambda i, j: (0, 0)),
                  pl.BlockSpec((D_MODEL, PROJ_TN), lambda i, j: (0, j)),
                  pl.BlockSpec((1, PROJ_TN), lambda i, j: (0, j)),
                  rows(HEAD_DIM), rows(HEAD_DIM), rows(HEAD_DIM)],
        out_specs=pl.BlockSpec((tm, PROJ_TN), lambda i, j: (i, j)),
        scratch_shapes=[pltpu.VMEM((tm, D_MODEL), BF16)],
        compiler_params=pltpu.CompilerParams(dimension_semantics=("parallel", "arbitrary"),
                                             vmem_limit_bytes=VMEM_LIMIT),
        name="proj",
    )(x2d, norm_g.reshape(1, D_MODEL), w_perm, b_perm, cos_t, sa_t, sb_t)


def _compress_body(x_ref, w1, pb, w2, r):
    bias = pb[0:1, :HEAD_DIM] + pb[1:2, HEAD_DIM:]
    outs = []
    for h in range(N_KV):
        xh = jnp.concatenate([x_ref[:, (N_KV * s + h) * HEAD_DIM:(N_KV * s + h + 1) * HEAD_DIM]
                              for s in range(CMP_STRIDE)], axis=1).astype(BF16)
        zz = jnp.dot(xh, w1, preferred_element_type=F32)
        pre = zz[:, :HEAD_DIM] + pltpu.roll(zz[:, HEAD_DIM:], r - 1, 0) + bias
        mid = pre * jax.nn.sigmoid(pre)
        outs.append(jnp.dot(mid.astype(BF16), w2, preferred_element_type=F32))
    return outs


def _compress_kernel(xk_ref, xv_ref, w1_ref, p_ref, w2_ref, o_ref):
    r = xk_ref.shape[0]
    for which, x_ref in enumerate((xk_ref, xv_ref)):
        w1 = w1_ref[which]
        pb = jnp.dot(p_ref[which], w1, preferred_element_type=F32)
        for h, o in enumerate(_compress_body(x_ref, w1, pb, w2_ref[which], r)):
            o_ref[which, h] = o


def _compress_weights(cmp_pos, cmp_w1, cmp_w2):
    assert CMP_LEN == 2 * CMP_STRIDE
    half = CMP_STRIDE * HEAD_DIM
    w1 = jnp.concatenate([cmp_w1[:, :half], cmp_w1[:, half:]], axis=2).astype(BF16)
    p = cmp_pos.reshape(2, 2, half)
    p = jnp.concatenate([p, jnp.zeros((2, 6, half), p.dtype)], axis=1).astype(BF16)
    return w1, p, cmp_w2.astype(BF16)


def _compress_prompt(kc, vc, cw):
    B, T = kc.shape[:2]
    r = T // CMP_STRIDE
    width = CMP_STRIDE * N_KV * HEAD_DIM
    w1, p, w2 = cw
    full = lambda a: pl.BlockSpec(a.shape, lambda b: (0,) * a.ndim)
    xspec = pl.BlockSpec((None, r, width), lambda b: (b, 0, 0))
    return pl.pallas_call(
        _compress_kernel,
        out_shape=jax.ShapeDtypeStruct((B, 2, N_KV, r, HEAD_DIM), F32),
        grid=(B,),
        in_specs=[xspec, xspec, full(w1), full(p), full(w2)],
        out_specs=pl.BlockSpec((None, 2, N_KV, r, HEAD_DIM), lambda b: (b, 0, 0, 0, 0)),
        compiler_params=pltpu.CompilerParams(dimension_semantics=("parallel",), vmem_limit_bytes=VMEM_LIMIT),
        name="compress_prompt",
    )(kc.reshape(B, r, width), vc.reshape(B, r, width), w1, p, w2)


def _group_scores(q, k, tq):
    s = lax.dot_general(q, k, (((1,), (1,)), ((), ())), preferred_element_type=F32) * HEAD_DIM ** -0.5
    return s.reshape(GROUP, tq, k.shape[0])


def _stack_heads(ref):
    return jnp.concatenate([ref[:, g * HEAD_DIM:(g + 1) * HEAD_DIM] for g in range(GROUP)], axis=0).astype(BF16)


def _importance(psum, n_sel):
    r = psum.shape[1]
    ci = lax.broadcasted_iota(jnp.int32, (r, n_sel), 0) * CMP_STRIDE
    cj = lax.broadcasted_iota(jnp.int32, (r, n_sel), 1) * SEL_LEN
    cover_t = ((ci < cj + SEL_LEN) & (ci + (CMP_LEN - 1) >= cj)).astype(BF16)
    p_hi = psum.astype(BF16)
    p_lo = (psum - p_hi.astype(F32)).astype(BF16)
    return jnp.dot(p_hi, cover_t, preferred_element_type=F32) + jnp.dot(p_lo, cover_t, preferred_element_type=F32)


def _block_scores(imp, blk, qpos):
    cur = jnp.right_shift(qpos, SEL_LEN.bit_length() - 1)
    forced = (blk == 0) | (blk == cur) | (blk == cur - 1)
    return jnp.where(blk * SEL_LEN <= qpos, imp + jnp.where(forced, FORCE_BONUS, 0.0), -1.0)


def _top_rank_mask(score, axis, n_cand):
    idx = lax.broadcasted_iota(jnp.int32, score.shape, axis)
    rank = jnp.zeros(score.shape, F32)
    for j in range(n_cand):
        cand = score[j:j + 1, :] if axis == 0 else score[:, j:j + 1]
        rank = rank + jnp.where((cand > score) | ((cand == score) & (idx > j)), 1.0, 0.0)
    return jnp.where(rank < float(SEL_TOP), 1.0, 0.0)


def _nsa_prompt_kernel(qraw_ref, qrot_ref, ks_ref, vs_ref, kw_ref, vw_ref, kc_ref, vc_ref, gate_ref, o_ref,
                       m_ref, l_ref, acc_ref, *, tq, tk, seq_len):
    q0 = pl.program_id(2) * tq
    r = kc_ref.shape[0]
    n_sel = seq_len // SEL_LEN
    sel_shift = SEL_LEN.bit_length() - 1
    qpos = q0 + lax.broadcasted_iota(jnp.int32, (tq, 1), 0)

    s = _group_scores(_stack_heads(qraw_ref), kc_ref[...].astype(BF16), tq)
    c_end = lax.broadcasted_iota(jnp.int32, (tq, r), 1) * CMP_STRIDE + (CMP_LEN - 1)
    cmask = c_end <= qpos
    s = s + jnp.where(cmask, 0.0, NEG)[None]
    p = jnp.exp(s - jnp.max(s, axis=-1, keepdims=True)) * jnp.where(cmask, 1.0, 0.0)[None]
    p = p / jnp.maximum(jnp.sum(p, axis=-1, keepdims=True), 1e-30)
    o_cmp = jnp.dot(p.reshape(GROUP * tq, r).astype(BF16), vc_ref[...].astype(BF16), preferred_element_type=F32)
    psum = p[0]
    for g in range(1, GROUP):
        psum = psum + p[g]
    imp_t = _importance(psum, HEAD_DIM).T[:n_sel]
    blk_t = lax.broadcasted_iota(jnp.int32, (n_sel, tq), 0)
    qpos_t = q0 + lax.broadcasted_iota(jnp.int32, (n_sel, tq), 1)
    sel_b = _top_rank_mask(_block_scores(imp_t, blk_t, qpos_t), 0, n_sel).astype(BF16)

    q_rot = _stack_heads(qrot_ref)
    m_ref[...] = jnp.full(m_ref.shape, NEG, F32)
    l_ref[...] = jnp.zeros(l_ref.shape, F32)
    acc_ref[...] = jnp.zeros(acc_ref.shape, F32)

    def sel_tile(kt, carry):
        k0 = pl.multiple_of(kt * tk, tk)
        k = ks_ref[pl.ds(k0, tk), :].astype(BF16)
        v = vs_ref[pl.ds(k0, tk), :].astype(BF16)
        s2 = _group_scores(q_rot, k, tq)
        ej = lax.broadcasted_iota(jnp.int32, (n_sel, tk), 0)
        et = k0 + lax.broadcasted_iota(jnp.int32, (n_sel, tk), 1)
        expand = jnp.where(jnp.right_shift(et, sel_shift) == ej, 1.0, 0.0).astype(BF16)
        selm = lax.dot_general(sel_b, expand, (((0,), (0,)), ((), ())), preferred_element_type=F32)
        kpos = k0 + lax.broadcasted_iota(jnp.int32, (tq, tk), 1)
        ok = (selm > 0.5) & (kpos <= qpos)
        s2 = s2 + jnp.where(ok, 0.0, NEG)[None]
        m_prev = m_ref[...]
        m_new = jnp.maximum(m_prev, jnp.max(s2, axis=-1, keepdims=True))
        alpha = jnp.exp(m_prev - m_new)
        p2 = jnp.exp(s2 - m_new)
        l_ref[...] = alpha * l_ref[...] + jnp.sum(p2, axis=-1, keepdims=True)
        pv = jnp.dot(p2.reshape(GROUP * tq, tk).astype(BF16), v, preferred_element_type=F32)
        acc_ref[...] = alpha.reshape(GROUP * tq, 1) * acc_ref[...] + pv
        m_ref[...] = m_new
        return carry

    lax.fori_loop(0, lax.div(q0 + tq + tk - 1, tk), sel_tile, 0)
    o_sel = acc_ref[...] / jnp.maximum(l_ref[...].reshape(GROUP * tq, 1), 1e-30)

    span = WINDOW + tq
    w0 = pl.multiple_of(jnp.maximum(q0 - WINDOW, 0), tq)
    s3 = _group_scores(q_rot, kw_ref[pl.ds(w0, span), :].astype(BF16), tq)
    kp = w0 + lax.broadcasted_iota(jnp.int32, (tq, span), 1)
    okw = (kp <= qpos) & (kp >= qpos - WINDOW)
    s3 = s3 + jnp.where(okw, 0.0, NEG)[None]
    p3 = jnp.exp(s3 - jnp.max(s3, axis=-1, keepdims=True))
    p3 = p3 / jnp.sum(p3, axis=-1, keepdims=True)
    o_win = jnp.dot(p3.reshape(GROUP * tq, span).astype(BF16), vw_ref[pl.ds(w0, span), :].astype(BF16),
                    preferred_element_type=F32)

    gts = jax.nn.sigmoid(gate_ref[...])
    for g in range(GROUP):
        rows = slice(g * tq, (g + 1) * tq)
        o_ref[:, g * HEAD_DIM:(g + 1) * HEAD_DIM] = (gts[:, 3 * g:3 * g + 1] * o_cmp[rows]
                                                     + gts[:, 3 * g + 1:3 * g + 2] * o_sel[rows]
                                                     + gts[:, 3 * g + 2:3 * g + 3] * o_win[rows]).astype(o_ref.dtype)


def _nsa_prompt(z, cmp, B, T, tq=256, tk=512):
    assert T % tk == 0 and tk % tq == 0 and tq % HEAD_DIM == 0 and T >= WINDOW + tq
    assert SEL_LEN & (SEL_LEN - 1) == 0 and tk % SEL_LEN == 0
    assert (T // SEL_LEN) % 8 == 0 and T // SEL_LEN <= HEAD_DIM
    nq = T // tq
    r = cmp.shape[3]
    gw = GROUP * HEAD_DIM
    qspec = lambda off: pl.BlockSpec((tq, gw), lambda b, h, q: (b * nq + q, off // gw + h))
    kvspec = lambda off: pl.BlockSpec((T, HEAD_DIM), lambda b, h, q: (b, off // HEAD_DIM + h))
    cspec = lambda which: pl.BlockSpec((None, None, None, r, HEAD_DIM), lambda b, h, q: (b, which, h, 0, 0))
    return pl.pallas_call(
        functools.partial(_nsa_prompt_kernel, tq=tq, tk=tk, seq_len=T),
        out_shape=jax.ShapeDtypeStruct((B * T, A_Q), BF16),
        grid=(B, N_KV, nq),
        in_specs=[qspec(Z_QRAW), qspec(Z_QROT), kvspec(Z_KS), kvspec(Z_VS), kvspec(Z_KW), kvspec(Z_VW),
                  cspec(0), cspec(1),
                  pl.BlockSpec((tq, HEAD_DIM), lambda b, h, q: (b * nq + q, Z_GNSA // HEAD_DIM + h))],
        out_specs=pl.BlockSpec((tq, gw), lambda b, h, q: (b * nq + q, h)),
        scratch_shapes=[pltpu.VMEM((GROUP, tq, 1), F32), pltpu.VMEM((GROUP, tq, 1), F32),
                        pltpu.VMEM((GROUP * tq, HEAD_DIM), F32)],
        compiler_params=pltpu.CompilerParams(dimension_semantics=("parallel", "parallel", "arbitrary"),
                                             vmem_limit_bytes=VMEM_LIMIT),
        name="nsa_prompt",
    )(z, z, z, z, z, z, cmp, cmp, z)


def _gate_logs(i_pre, f_pre):
    log_i = GATE_CAP * jnp.tanh(i_pre / GATE_CAP)
    fc = GATE_CAP * jnp.tanh(f_pre / GATE_CAP)
    log_f = jnp.minimum(fc, 0.0) - jnp.log1p(jnp.exp(-jnp.abs(fc)))
    return log_i, log_f


def _mlstm_prompt_kernel(q_ref, k_ref, v_ref, op_ref, grow_ref, gcol_ref, gain_ref,
                         h_ref, c_ref, n_ref, m_ref):
    @pl.when(pl.program_id(1) == 0)
    def _():
        c_ref[...] = jnp.zeros(c_ref.shape, F32)
        n_ref[...] = jnp.zeros(n_ref.shape, F32)
        m_ref[...] = jnp.zeros(m_ref.shape, F32)

    for h in range(N_HEADS_B):
        qk_cols = slice(h * DQK_B, (h + 1) * DQK_B)
        v_cols = slice(h * DV_B, (h + 1) * DV_B)
        _mlstm_chunk(q_ref[:, qk_cols], k_ref[:, qk_cols], v_ref[:, v_cols], op_ref[:, v_cols],
                     grow_ref[h:h + 1, :], grow_ref[N_HEADS_B + h:N_HEADS_B + h + 1, :],
                     gcol_ref[:, h:h + 1], gcol_ref[:, N_HEADS_B + h:N_HEADS_B + h + 1], gain_ref[h],
                     h_ref.at[:, v_cols], c_ref.at[h], n_ref.at[h], m_ref.at[h])


def _mlstm_chunk(q, k, v, o_pre, i_row, f_row, i_col, f_col, gain, h_ref, c_ref, n_ref, m_ref):
    L = q.shape[0]
    qb = q.astype(BF16)
    k = k * DQK_B ** -0.5
    vb = v.astype(BF16)
    i_r, f_r = _gate_logs(i_row, f_row)
    i_c, f_c = _gate_logs(i_col, f_col)
    li = lax.broadcasted_iota(jnp.int32, (L, L), 0)
    si = lax.broadcasted_iota(jnp.int32, (L, L), 1)
    tri = si <= li
    bcum_c = jnp.sum(jnp.where(tri, f_r, 0.0), axis=1, keepdims=True)
    bcum_r = jnp.sum(jnp.where(li <= si, f_c, 0.0), axis=0, keepdims=True)
    btot = jnp.sum(f_r, axis=1, keepdims=True)
    c_old, n_old, m_old = c_ref[...], n_ref[...], m_ref[0:1, 0:1]

    dlog = jnp.where(tri, bcum_c - bcum_r + i_r, NEG)
    inter = bcum_c + m_old
    m_t = jnp.maximum(inter, jnp.max(dlog, axis=1, keepdims=True))
    qk = lax.dot_general(qb, k.astype(BF16), (((1,), (1,)), ((), ())), preferred_element_type=F32)
    w_intra = jnp.exp(dlog - m_t) * qk
    w_inter = jnp.exp(inter - m_t)
    num = (w_inter * jnp.dot(qb, c_old.astype(BF16), preferred_element_type=F32)
           + jnp.dot(w_intra.astype(BF16), vb, preferred_element_type=F32))
    den = w_inter * jnp.sum(q * n_old, axis=1, keepdims=True) + jnp.sum(w_intra, axis=1, keepdims=True)
    h = num / jnp.maximum(jnp.abs(den), jnp.exp(-m_t))
    h_ref[...] = (_rms(h, gain) * jax.nn.sigmoid(o_pre)).astype(h_ref.dtype)

    wlog_c = btot - bcum_c + i_c
    m_new = jnp.maximum(btot + m_old, jnp.max(btot - bcum_r + i_r, axis=1, keepdims=True))
    decay = jnp.exp(btot + m_old - m_new)
    kw = k * jnp.exp(wlog_c - m_new)
    c_ref[...] = decay * c_old + lax.dot_general(kw.astype(BF16), vb, (((0,), (0,)), ((), ())),
                                                 preferred_element_type=F32)
    n_ref[...] = decay * n_old + jnp.sum(kw, axis=0, keepdims=True)
    m_ref[...] = jnp.broadcast_to(m_new, m_ref.shape)


def _mlstm_prompt(z, gain, B, T):
    L = MLSTM_CHUNK
    assert T % L == 0
    nc, H = T // L, N_HEADS_B
    gcol = z[:, Z_IB:Z_IB + 2 * H].reshape(B * nc, L, 2 * H)
    grow = gcol.transpose(0, 2, 1)
    rows = lambda w, off: pl.BlockSpec((L, w), lambda b, c: (b * nc + c, off // w))
    gidx = lambda b, c: (b * nc + c, 0, 0)
    state = lambda *s: pl.BlockSpec((None, H) + s, lambda b, c: (b, 0, 0, 0))
    h, c, n, m = pl.pallas_call(
        _mlstm_prompt_kernel,
        out_shape=(jax.ShapeDtypeStruct((B * T, B_V), BF16), jax.ShapeDtypeStruct((B, H, DQK_B, DV_B), F32),
                   jax.ShapeDtypeStruct((B, H, 1, DQK_B), F32), jax.ShapeDtypeStruct((B, H, 1, DQK_B), F32)),
        grid=(B, nc),
        in_specs=[rows(B_QK, Z_QB), rows(B_QK, Z_KB), rows(B_V, Z_VB), rows(B_V, Z_OB),
                  pl.BlockSpec((None, 2 * H, L), gidx), pl.BlockSpec((None, L, 2 * H), gidx),
                  pl.BlockSpec((H, 1, DV_B), lambda b, c: (0, 0, 0))],
        out_specs=(rows(B_V, 0), state(DQK_B, DV_B), state(1, DQK_B), state(1, DQK_B)),
        compiler_params=pltpu.CompilerParams(dimension_semantics=("parallel", "arbitrary"),
                                             vmem_limit_bytes=VMEM_LIMIT),
        name="mlstm_prompt",
    )(z, z, z, z, grow, gcol, gain.reshape(H, 1, DV_B))
    return h, c, n[:, :, 0], m[:, :, 0, 0]


def _merge_kernel(x_ref, oa_ref, hb_ref, gma_ref, gmb_ref, wa_ref, wb_ref, wo_ref, nf_ref, wrh_ref, wrl_ref, br_ref,
                  *refs):
    x2_ref, xn_ref, route_ref = refs[-3:]
    ua = jnp.dot(oa_ref[...], wa_ref[...], preferred_element_type=F32)
    ub = jnp.dot(hb_ref[...], wb_ref[...], preferred_element_type=F32)
    merged = jax.nn.sigmoid(gma_ref[...]) * ua + jax.nn.sigmoid(gmb_ref[...]) * ub
    x2 = x_ref[...] + jnp.dot(merged.astype(BF16), wo_ref[...], preferred_element_type=F32)
    x2_ref[...] = x2
    xn = _rms(x2, nf_ref[...])
    xn_ref[...] = xn
    hi = xn.astype(BF16)
    lo = (xn - hi.astype(F32)).astype(BF16)
    lg = (jnp.dot(hi, wrh_ref[...], preferred_element_type=F32) + jnp.dot(lo, wrh_ref[...], preferred_element_type=F32)
          + jnp.dot(hi, wrl_ref[...], preferred_element_type=F32)) + br_ref[...]
    lane = lax.broadcasted_iota(jnp.int32, lg.shape, 1)
    lanef = lane.astype(F32)
    ninf = -jnp.inf
    gmask = (lane >= N_EXPERTS) & (lane < N_EXPERTS + N_GROUPS)
    gl = jnp.where(gmask, lg, ninf)
    gmax = jnp.max(gl, axis=1, keepdims=True)
    gidx = jnp.min(jnp.where(gl == gmax, lanef, 1e9), axis=1, keepdims=True) - N_EXPERTS
    gprob = 1.0 / jnp.sum(jnp.where(gmask, jnp.exp(lg - gmax), 0.0), axis=1, keepdims=True)
    e_lo = gidx * EXPERTS_PER_GROUP
    el = jnp.where((lanef >= e_lo) & (lanef < e_lo + EXPERTS_PER_GROUP), lg, ninf)
    v1 = jnp.max(el, axis=1, keepdims=True)
    i1 = jnp.min(jnp.where(el == v1, lanef, 1e9), axis=1, keepdims=True)
    el2 = jnp.where(lanef == i1, ninf, el)
    v2 = jnp.max(el2, axis=1, keepdims=True)
    i2 = jnp.min(jnp.where(el2 == v2, lanef, 1e9), axis=1, keepdims=True)
    e2 = jnp.exp(v2 - v1)
    g1 = gprob / (1.0 + e2)
    g2 = gprob * e2 / (1.0 + e2)
    route_ref[...] = jnp.where(lane == 0, i1, jnp.where(lane == 1, i2, jnp.where(lane == 2, g1,
                               jnp.where(lane == 3, g2, 0.0))))


def _merge_weights(w_up_a, w_up_b, w_out, norm_ffn, w_rg, b_rg, w_re, b_re):
    assert TOP_K == 2
    pad = ROUTE_W - N_EXPERTS - N_GROUPS
    w = jnp.concatenate([w_re, w_rg, jnp.zeros((D_MODEL, pad), F32)], axis=1)
    b = jnp.concatenate([b_re, b_rg, jnp.zeros((pad,), F32)]).reshape(1, ROUTE_W)
    hi = w.astype(BF16)
    return (w_up_a.astype(BF16), w_up_b.astype(BF16), w_out.astype(BF16), norm_ffn.reshape(1, D_MODEL).astype(F32),
            hi, (w - hi.astype(F32)).astype(BF16), b)


def _merge(x2d, o_a, h_b, z, mw, n_total, row0, tm, bufs=None):
    m = x2d.shape[0]
    i0 = row0 // tm
    row = lambda w, c=0: pl.BlockSpec((tm, w), lambda i: (i, c))
    const = lambda a: pl.BlockSpec(a.shape, lambda i: (0,) * a.ndim, pipeline_mode=pl.Buffered(1))
    outw = (D_MODEL, D_MODEL, ROUTE_W)
    in_specs = [row(D_MODEL), row(A_Q), row(B_V), row(D_MODEL, Z_GMA // D_MODEL), row(D_MODEL, Z_GMB // D_MODEL)]
    in_specs += [const(a) for a in mw]
    args = [x2d, o_a, h_b, z, z] + list(mw)
    aliases = {}
    if bufs is not None:
        in_specs += [pl.BlockSpec(memory_space=pl.ANY)] * 3
        aliases = {len(args) + k: k for k in range(3)}
        args += list(bufs)
    return pl.pallas_call(
        _merge_kernel,
        out_shape=tuple(jax.ShapeDtypeStruct((n_total, w), F32) for w in outw),
        grid=(m // tm,),
        in_specs=in_specs,
        out_specs=tuple(pl.BlockSpec((tm, w), lambda i: (i0 + i, 0)) for w in outw),
        input_output_aliases=aliases,
        compiler_params=pltpu.CompilerParams(dimension_semantics=("parallel",), vmem_limit_bytes=VMEM_LIMIT_MERGE),
        name="merge_route",
    )(*args)


def _moe_plan(route, n_tiles):
    n = route.shape[0]
    a = n * TOP_K
    flat_e = route[:, 0:TOP_K].astype(jnp.int32).reshape(a)
    onehot = (flat_e[:, None] == jnp.arange(N_EXPERTS, dtype=jnp.int32)[None, :]).astype(jnp.int32)
    rank = jnp.take_along_axis(jnp.cumsum(onehot, axis=0) - onehot, flat_e[:, None], axis=1)[:, 0]
    counts = jnp.sum(onehot, axis=0)
    ntile = (counts + MOE_TILE - 1) // MOE_TILE
    tile_end = jnp.cumsum(ntile)
    dest = ((tile_end - ntile)[flat_e] * MOE_TILE + rank).astype(jnp.int32)
    n_used = tile_end[-1].astype(jnp.int32)
    t = jnp.minimum(jnp.arange(n_tiles, dtype=jnp.int32), n_used - 1)
    tile_expert = jnp.minimum(jnp.searchsorted(tile_end, t, side='right'), N_EXPERTS - 1).astype(jnp.int32)
    first = (tile_end - ntile)[tile_expert]
    tile_rows = jnp.clip(counts[tile_expert] - (t - first) * MOE_TILE, 0, MOE_TILE)
    tile_rows = jnp.where(jnp.arange(n_tiles) < n_used, tile_rows, 0).astype(jnp.int32)
    return dest, tile_expert, tile_rows, n_used.reshape(1)


DMA_UNROLL = 16


def _dispatch_kernel(dest_ref, x_ref, o_hbm, sem):
    i = pl.program_id(0)
    tm = x_ref.shape[0]

    def copy(r, k):
        d = dest_ref[(i * tm + r) * TOP_K + k]
        return pltpu.make_async_copy(x_ref.at[pl.ds(r, 1)], o_hbm.at[pl.ds(d, 1)], sem)

    def start(r, c):
        for k in range(TOP_K):
            copy(r, k).start(priority=k % 2)
        return c

    def wait(r, c):
        for k in range(TOP_K):
            copy(r, k).wait()
        return c

    lax.fori_loop(0, tm, start, 0, unroll=DMA_UNROLL)
    lax.fori_loop(0, tm, wait, 0, unroll=DMA_UNROLL)


def _moe_dispatch(xn, dest, n_tiles, tm=128):
    n, d = xn.shape
    return pl.pallas_call(
        _dispatch_kernel,
        out_shape=jax.ShapeDtypeStruct((n_tiles * MOE_TILE, d), xn.dtype),
        grid_spec=pltpu.PrefetchScalarGridSpec(
            num_scalar_prefetch=1, grid=(n // tm,),
            in_specs=[pl.BlockSpec((tm, d), lambda i, ds: (i, 0))],
            out_specs=pl.BlockSpec(memory_space=pl.ANY),
            scratch_shapes=[pltpu.SemaphoreType.DMA(())]),
        compiler_params=pltpu.CompilerParams(dimension_semantics=("arbitrary",), vmem_limit_bytes=VMEM_LIMIT),
        name="moe_dispatch",
    )(dest, xn)


def _expert_kernel(te_ref, cnt_ref, nu_ref, x_ref, wg_ref, wu_ref, wd_ref, o_ref):
    t, j = pl.program_id(0), pl.program_id(1)
    cnt = cnt_ref[t]

    @pl.when(cnt > 0)
    def _():
        wg, wu, wd = wg_ref[...].astype(BF16), wu_ref[...].astype(BF16), wd_ref[...].astype(BF16)

        def run(rows):
            xb = x_ref[rows].astype(BF16)
            g = jnp.dot(xb, wg, preferred_element_type=F32)
            u = jnp.dot(xb, wu, preferred_element_type=F32)
            h = (g * jax.nn.sigmoid(g) * u).astype(BF16)
            y = jnp.dot(h, wd, preferred_element_type=F32)

            @pl.when(j == 0)
            def _():
                o_ref[rows] = y

            @pl.when(j > 0)
            def _():
                o_ref[rows] += y

        for s in range(MOE_TILE // MOE_SUB):
            lo, half = s * MOE_SUB, MOE_SUB // 2

            @pl.when(cnt > lo + half)
            def _():
                run(slice(lo, lo + MOE_SUB))

            @pl.when((cnt > lo) & (cnt <= lo + half))
            def _():
                run(slice(lo, lo + half))


def _moe_experts(xs, tile_expert, tile_rows, n_used, w_gate, w_up, w_down):
    n_tiles = xs.shape[0] // MOE_TILE
    nj = D_EXPERT // MOE_FC
    tix = lambda t, j, te, cnt, nu: (jnp.minimum(t, nu[0] - 1), 0)
    jj = lambda t, j, nu: jnp.where(t < nu[0], j, nj - 1)
    return pl.pallas_call(
        _expert_kernel,
        out_shape=jax.ShapeDtypeStruct(xs.shape, F32),
        grid_spec=pltpu.PrefetchScalarGridSpec(
            num_scalar_prefetch=3, grid=(n_tiles, nj),
            in_specs=[pl.BlockSpec((MOE_TILE, D_MODEL), tix),
                      pl.BlockSpec((None, D_MODEL, MOE_FC), lambda t, j, te, cnt, nu: (te[t], 0, jj(t, j, nu))),
                      pl.BlockSpec((None, D_MODEL, MOE_FC), lambda t, j, te, cnt, nu: (te[t], 0, jj(t, j, nu))),
                      pl.BlockSpec((None, MOE_FC, D_MODEL), lambda t, j, te, cnt, nu: (te[t], jj(t, j, nu), 0))],
            out_specs=pl.BlockSpec((MOE_TILE, D_MODEL), tix)),
        compiler_params=pltpu.CompilerParams(dimension_semantics=("arbitrary", "arbitrary"),
                                             vmem_limit_bytes=VMEM_LIMIT_MERGE),
        name="moe_experts",
    )(tile_expert, tile_rows, n_used, xs, w_gate, w_up, w_down)


def _combine_kernel(dest_ref, x2_ref, route_ref, g_ref, y_hbm, op_ref, os_ref, buf_ref, sem, *, n_prompt_tiles):
    i = pl.program_id(0)
    tm = x2_ref.shape[0]

    def copy(r, k):
        d = dest_ref[(i * tm + r) * TOP_K + k]
        return pltpu.make_async_copy(y_hbm.at[pl.ds(d, 1)], buf_ref.at[k, pl.ds(r, 1)], sem)

    def start(r, c):
        for k in range(TOP_K):
            copy(r, k).start(priority=k % 2)
        return c

    def wait(r, c):
        for k in range(TOP_K):
            copy(r, k).wait()
        return c

    lax.fori_loop(0, tm, start, 0, unroll=DMA_UNROLL)
    lax.fori_loop(0, tm, wait, 0, unroll=DMA_UNROLL)
    ffn = buf_ref[0] * route_ref[:, TOP_K:TOP_K + 1]
    for k in range(1, TOP_K):
        ffn = ffn + buf_ref[k] * route_ref[:, TOP_K + k:TOP_K + k + 1]
    y = _rms(x2_ref[...] + ffn, g_ref[...])

    @pl.when(i < n_prompt_tiles)
    def _():
        op_ref[...] = y

    @pl.when(i >= n_prompt_tiles)
    def _():
        os_ref[...] = y


def _moe_combine(x2, route, dest, ys, norm_final, n_prompt, tm=128):
    n, d = x2.shape
    npt = n_prompt // tm
    nst = (n - n_prompt) // tm
    return pl.pallas_call(
        functools.partial(_combine_kernel, n_prompt_tiles=npt),
        out_shape=(jax.ShapeDtypeStruct((n_prompt, d), F32), jax.ShapeDtypeStruct((n - n_prompt, d), F32)),
        grid_spec=pltpu.PrefetchScalarGridSpec(
            num_scalar_prefetch=1, grid=(n // tm,),
            in_specs=[pl.BlockSpec((tm, d), lambda i, ds: (i, 0)), pl.BlockSpec((tm, ROUTE_W), lambda i, ds: (i, 0)),
                      pl.BlockSpec((1, d), lambda i, ds: (0, 0)), pl.BlockSpec(memory_space=pl.ANY)],
            out_specs=(pl.BlockSpec((tm, d), lambda i, ds: (jnp.minimum(i, npt - 1), 0)),
                       pl.BlockSpec((tm, d), lambda i, ds: (jnp.clip(i - npt, 0, nst - 1), 0))),
            scratch_shapes=[pltpu.VMEM((TOP_K, tm, d), F32), pltpu.SemaphoreType.DMA(())]),
        compiler_params=pltpu.CompilerParams(dimension_semantics=("arbitrary",), vmem_limit_bytes=VMEM_LIMIT),
        name="moe_combine",
    )(dest, x2, route, norm_final.reshape(1, d), ys)


def _moe(x2, xn, route, w_gate, w_up, w_down, norm_final, n_prompt):
    n = x2.shape[0]
    n_tiles = (n * TOP_K + N_EXPERTS * (MOE_TILE - 1) + MOE_TILE - 1) // MOE_TILE
    dest, tile_expert, tile_rows, n_used = _moe_plan(route, n_tiles)
    xs = _moe_dispatch(xn, dest, n_tiles)
    ys = _moe_experts(xs, tile_expert, tile_rows, n_used, w_gate, w_up, w_down)
    return _moe_combine(x2, route, dest, ys, norm_final, n_prompt)


POOL_ROWS = 256
ROWS_PER_PAGE = PAGE_SIZE // CMP_STRIDE
ZW = 2 * N_KV * HEAD_DIM


GROUP_ROWS = N_KV * CMP_STRIDE
GROUP_PITCH = GROUP_ROWS + 8


def _compress_pool_kernel(pt_ref, w1_ref, p_ref, xk_hbm, xv_hbm, zk_ref, zv_ref, xbuf, sem):
    i = pl.program_id(0)
    n = pl.num_programs(0)
    slot = lax.rem(i, 2)
    pages = POOL_ROWS // ROWS_PER_PAGE

    def copies(step, sl):
        out = []
        for pg in range(pages):
            page = pt_ref[step * pages + pg]
            for which, src in enumerate((xk_hbm, xv_hbm)):
                out.append(pltpu.make_async_copy(
                    src.at[page], xbuf.at[sl, which, pl.ds(pg * ROWS_PER_PAGE, ROWS_PER_PAGE), pl.ds(0, GROUP_ROWS), :],
                    sem.at[sl, which]))
        return out

    @pl.when(i == 0)
    def _():
        for cp in copies(0, 0):
            cp.start()

    @pl.when(i + 1 < n)
    def _():
        for cp in copies(i + 1, 1 - slot):
            cp.start()

    for cp in copies(i, slot):
        cp.wait()

    for which, z_ref in enumerate((zk_ref, zv_ref)):
        w1 = w1_ref[which]
        pb = jnp.dot(p_ref[which], w1, preferred_element_type=F32)
        bias = pb[0:1, :HEAD_DIM] + pb[1:2, HEAD_DIM:]
        rows2d = xbuf.at[slot, which].reshape(POOL_ROWS * GROUP_PITCH, HEAD_DIM)
        for h in range(N_KV):
            xh = jnp.concatenate([rows2d[pl.ds(N_KV * s + h, POOL_ROWS, stride=GROUP_PITCH), :]
                                  for s in range(CMP_STRIDE)], axis=1).astype(BF16)
            zz = jnp.dot(xh, w1, preferred_element_type=F32)
            z_ref[:, 2 * h * HEAD_DIM:(2 * h + 1) * HEAD_DIM] = zz[:, :HEAD_DIM] + bias
            z_ref[:, (2 * h + 1) * HEAD_DIM:(2 * h + 2) * HEAD_DIM] = zz[:, HEAD_DIM:]


def _compress_pool(pool_k, pool_v, page_table, cw):
    n_phys = pool_k.shape[0]
    bs, n_pages = page_table.shape
    rows = bs * n_pages * ROWS_PER_PAGE
    assert rows % POOL_ROWS == 0
    w1, p, _ = cw
    full = lambda a: pl.BlockSpec(a.shape, lambda i, pt: (0,) * a.ndim)
    groups = lambda a: a.reshape(n_phys, ROWS_PER_PAGE, GROUP_ROWS, HEAD_DIM)
    zk, zv = pl.pallas_call(
        _compress_pool_kernel,
        out_shape=(jax.ShapeDtypeStruct((rows, ZW), F32),) * 2,
        grid_spec=pltpu.PrefetchScalarGridSpec(
            num_scalar_prefetch=1, grid=(rows // POOL_ROWS,),
            in_specs=[full(w1), full(p), pl.BlockSpec(memory_space=pl.ANY), pl.BlockSpec(memory_space=pl.ANY)],
            out_specs=(pl.BlockSpec((POOL_ROWS, ZW), lambda i, pt: (i, 0)),) * 2,
            scratch_shapes=[pltpu.VMEM((2, 2, POOL_ROWS, GROUP_PITCH, HEAD_DIM), F32),
                            pltpu.SemaphoreType.DMA((2, 2))]),
        compiler_params=pltpu.CompilerParams(dimension_semantics=("arbitrary",), vmem_limit_bytes=VMEM_LIMIT),
        name="compress_pool",
    )(page_table.reshape(bs * n_pages), w1, p, groups(pool_k), groups(pool_v))
    return zk.reshape(bs, n_pages * ROWS_PER_PAGE, ZW), zv.reshape(bs, n_pages * ROWS_PER_PAGE, ZW)


def _nsa_sample_kernel(pt_ref, qraw_ref, qrot_ref, knew_ref, wnew_ref, gate_ref, w2_ref, wk_ref, wv_ref,
                       zk_ref, zv_ref, ks_hbm, vs_hbm, o_ref, sbuf, sem, *, n_pages, past_len):
    b = pl.program_id(0)
    nb = pl.num_programs(0)
    slot = lax.rem(b, 2)
    page_rows = N_KV * PAGE_SIZE

    def copies(seq, sl):
        out = []
        for pg in range(n_pages):
            page = pt_ref[seq, pg]
            for which, ssrc in enumerate((ks_hbm, vs_hbm)):
                out.append(pltpu.make_async_copy(
                    ssrc.at[page], sbuf.at[sl, which, pl.ds(pg * page_rows, page_rows)], sem.at[sl]))
        return out

    @pl.when(b == 0)
    def _():
        for cp in copies(0, 0):
            cp.start()

    @pl.when(b + 1 < nb)
    def _():
        for cp in copies(b + 1, 1 - slot):
            cp.start()

    for cp in copies(b, slot):
        cp.wait()

    scale = HEAD_DIM ** -0.5
    nq = N_KV * GROUP
    r = n_pages * ROWS_PER_PAGE
    qpos = jnp.full((nq, 1), past_len, jnp.int32)
    n_sel_pad = HEAD_DIM
    sel_shift = SEL_LEN.bit_length() - 1
    row_head = lax.broadcasted_iota(jnp.int32, (nq, 1), 0) // GROUP
    per_head = lambda a0, a1: jnp.where(row_head == 0, a0, a1)

    def all_heads(ref):
        return jnp.concatenate([ref[:, j * HEAD_DIM:(j + 1) * HEAD_DIM] for j in range(nq)], axis=0).astype(BF16)

    def scores(q, k):
        return lax.dot_general(q, k, (((1,), (1,)), ((), ())), preferred_element_type=F32) * scale

    cmp = []
    for which, z_ref in enumerate((zk_ref, zv_ref)):
        heads = []
        for h in range(N_KV):
            zh = z_ref[:, 2 * h * HEAD_DIM:(2 * h + 2) * HEAD_DIM]
            pre = zh[:, :HEAD_DIM] + pltpu.roll(zh[:, HEAD_DIM:], r - 1, 0)
            mid = pre * jax.nn.sigmoid(pre)
            heads.append(jnp.dot(mid.astype(BF16), w2_ref[which], preferred_element_type=F32).astype(BF16))
        cmp.append(heads)
    q_raw = all_heads(qraw_ref)
    c_end = lax.broadcasted_iota(jnp.int32, (nq, r), 1) * CMP_STRIDE + (CMP_LEN - 1)
    cmask = c_end <= qpos
    s = per_head(scores(q_raw, cmp[0][0]), scores(q_raw, cmp[0][1])) + jnp.where(cmask, 0.0, NEG)
    p = jnp.exp(s - jnp.max(s, axis=-1, keepdims=True)) * jnp.where(cmask, 1.0, 0.0)
    p = p / jnp.maximum(jnp.sum(p, axis=-1, keepdims=True), 1e-30)
    pb = p.astype(BF16)
    o_cmp = per_head(jnp.dot(pb, cmp[1][0], preferred_element_type=F32),
                     jnp.dot(pb, cmp[1][1], preferred_element_type=F32))
    psum = per_head(jnp.sum(jnp.where(row_head == 0, p, 0.0), axis=0, keepdims=True),
                    jnp.sum(jnp.where(row_head == 1, p, 0.0), axis=0, keepdims=True))
    blk = lax.broadcasted_iota(jnp.int32, (nq, n_sel_pad), 1)
    sel = _top_rank_mask(_block_scores(_importance(psum, n_sel_pad), blk, qpos), 1,
                         past_len // SEL_LEN + 1)

    def attend(q, k, v, k_new, v_new, bias, new_bias):
        s = scores(q, k) + bias
        s_new = jnp.sum(q.astype(F32) * k_new.astype(BF16).astype(F32), axis=1, keepdims=True) * scale + new_bias
        m = jnp.maximum(jnp.max(s, axis=-1, keepdims=True), s_new)
        p = jnp.exp(s - m)
        p_new = jnp.exp(s_new - m)
        l = jnp.sum(p, axis=-1, keepdims=True) + p_new
        o = (jnp.dot(p.astype(BF16), v, preferred_element_type=F32)
             + p_new.astype(BF16).astype(F32) * v_new.astype(BF16).astype(F32))
        return o / jnp.maximum(l, 1e-30)

    def own_head(n_rows):
        col = lax.broadcasted_iota(jnp.int32, (nq, n_rows), 1)
        return jnp.bitwise_and(col, N_KV - 1) == row_head

    q_rot = all_heads(qrot_ref)
    n_rows = N_KV * past_len
    ej = lax.broadcasted_iota(jnp.int32, (n_sel_pad, n_rows), 0)
    et = lax.broadcasted_iota(jnp.int32, (n_sel_pad, n_rows), 1)
    expand = jnp.where(jnp.right_shift(et, sel_shift + 1) == ej, 1.0, 0.0).astype(BF16)
    selm = jnp.dot(sel.astype(BF16), expand, preferred_element_type=F32)
    sel_bias = jnp.where((selm > 0.5) & own_head(n_rows), 0.0, NEG)
    new_blk = past_len // SEL_LEN
    new_bias = jnp.where(sel[:, new_blk:new_blk + 1] > 0.5, 0.0, NEG)
    new_kv = lambda ref, c: per_head(ref[:, c * A_KV:c * A_KV + HEAD_DIM], ref[:, c * A_KV + HEAD_DIM:(c + 1) * A_KV])
    o_sel = attend(q_rot, sbuf[slot, 0].astype(BF16), sbuf[slot, 1].astype(BF16),
                   new_kv(knew_ref, 0), new_kv(knew_ref, 1), sel_bias, new_bias)
    win_bias = jnp.where(own_head(wk_ref.shape[0]), 0.0, NEG)
    o_win = attend(q_rot, wk_ref[...].astype(BF16), wv_ref[...].astype(BF16),
                   new_kv(wnew_ref, 0), new_kv(wnew_ref, 1), win_bias, 0.0)

    gts = jax.nn.sigmoid(gate_ref[...])
    for h in range(N_KV):
        for g in range(GROUP):
            j, c0 = h * GROUP + g, h * HEAD_DIM + 3 * g
            o = (gts[:, c0:c0 + 1] * o_cmp[j:j + 1] + gts[:, c0 + 1:c0 + 2] * o_sel[j:j + 1]
                 + gts[:, c0 + 2:c0 + 3] * o_win[j:j + 1])
            o_ref[:, j * HEAD_DIM:(j + 1) * HEAD_DIM] = o.astype(o_ref.dtype)


def _nsa_sample(zs, page_table, zk_pool, zv_pool, sel_k, sel_v, win_k, win_v, w2):
    bs, n_pages = page_table.shape
    past_len = n_pages * PAGE_SIZE
    wb = win_k.shape[1]
    assert wb <= WINDOW and past_len % SEL_LEN == 0 and past_len // SEL_LEN < HEAD_DIM
    assert N_KV == 2
    zs3 = zs.reshape(bs, 1, NZ)
    zspec = lambda w, off: pl.BlockSpec((None, 1, w), lambda b, pt: (b, 0, off // w))
    anyspec = pl.BlockSpec(memory_space=pl.ANY)
    wspec = pl.BlockSpec((None, N_KV * wb, HEAD_DIM), lambda b, pt: (b, 0, 0))
    partial_sums = pl.BlockSpec((None, n_pages * ROWS_PER_PAGE, ZW), lambda b, pt: (b, 0, 0))
    rows2d = lambda a: a.reshape(a.shape[0], a.shape[1] * N_KV, HEAD_DIM)
    out = pl.pallas_call(
        functools.partial(_nsa_sample_kernel, n_pages=n_pages, past_len=past_len),
        out_shape=jax.ShapeDtypeStruct((bs, 1, A_Q), BF16),
        grid_spec=pltpu.PrefetchScalarGridSpec(
            num_scalar_prefetch=1, grid=(bs,),
            in_specs=[zspec(A_Q, Z_QRAW), zspec(A_Q, Z_QROT), zspec(2 * A_KV, Z_KS), zspec(2 * A_KV, Z_KW),
                      zspec(2 * HEAD_DIM, Z_GNSA), pl.BlockSpec(w2.shape, lambda b, pt: (0, 0, 0)), wspec, wspec,
                      partial_sums, partial_sums, anyspec, anyspec],
            out_specs=pl.BlockSpec((None, 1, A_Q), lambda b, pt: (b, 0, 0)),
            scratch_shapes=[pltpu.VMEM((2, 2, N_KV * past_len, HEAD_DIM), F32), pltpu.SemaphoreType.DMA((2,))]),
        compiler_params=pltpu.CompilerParams(dimension_semantics=("arbitrary",), vmem_limit_bytes=VMEM_LIMIT),
        name="nsa_sample",
    )(page_table, zs3, zs3, zs3, zs3, zs3, w2, rows2d(win_k), rows2d(win_v), zk_pool, zv_pool,
      rows2d(sel_k), rows2d(sel_v))
    return out.reshape(bs, A_Q)


def _mlstm_sample_kernel(q_ref, k_ref, v_ref, op_ref, g_ref, gain_ref, c_ref, n_ref, m_ref,
                         h_ref, co_ref, no_ref, mo_ref):
    eye = (lax.broadcasted_iota(jnp.int32, (DQK_B, DQK_B), 0) == lax.broadcasted_iota(jnp.int32, (DQK_B, DQK_B), 1))
    col = lambda row: jnp.sum(jnp.where(eye, row, 0.0), axis=1, keepdims=True)
    lane = lax.broadcasted_iota(jnp.int32, (1, N_HEADS_B), 1)
    m_out = jnp.zeros((1, N_HEADS_B), F32)
    for h in range(N_HEADS_B):
        q = q_ref[:, h * DQK_B:(h + 1) * DQK_B]
        k = k_ref[:, h * DQK_B:(h + 1) * DQK_B] * DQK_B ** -0.5
        v = v_ref[:, h * DV_B:(h + 1) * DV_B]
        log_i, log_f = _gate_logs(g_ref[:, h:h + 1], g_ref[:, N_HEADS_B + h:N_HEADS_B + h + 1])
        c_old, n_old, m_old = c_ref[h], n_ref[h:h + 1, :], m_ref[:, h:h + 1]
        inter = log_f + m_old
        m_new = jnp.maximum(inter, log_i)
        decay = jnp.exp(inter - m_new)
        w = jnp.exp(log_i - m_new)
        co_ref[h] = decay * c_old + col(w * k) * v
        no_ref[h:h + 1, :] = decay * n_old + w * k
        m_out = jnp.where(lane == h, m_new, m_out)
        w_intra = w * jnp.sum(q * k, axis=1, keepdims=True)
        num = decay * jnp.sum(c_old * col(q), axis=0, keepdims=True) + w_intra * v
        den = decay * jnp.sum(q * n_old, axis=1, keepdims=True) + w_intra
        hh = num / jnp.maximum(jnp.abs(den), jnp.exp(-m_new))
        out = _rms(hh, gain_ref[h]) * jax.nn.sigmoid(op_ref[:, h * DV_B:(h + 1) * DV_B])
        h_ref[:, h * DV_B:(h + 1) * DV_B] = out.astype(h_ref.dtype)
    mo_ref[...] = m_out


def _mlstm_sample(zs, gain, c0, n0, m0):
    bs, H = zs.shape[0], N_HEADS_B
    zs3 = zs.reshape(bs, 1, NZ)
    zspec = lambda w, off: pl.BlockSpec((None, 1, w), lambda b: (b, 0, off // w))
    cspec = pl.BlockSpec((None, H, DQK_B, DV_B), lambda b: (b, 0, 0, 0))
    nspec = pl.BlockSpec((None, H, DQK_B), lambda b: (b, 0, 0))
    mspec = pl.BlockSpec((None, 1, H), lambda b: (b, 0, 0))
    h, c, n, m = pl.pallas_call(
        _mlstm_sample_kernel,
        out_shape=(jax.ShapeDtypeStruct((bs, 1, B_V), BF16), jax.ShapeDtypeStruct(c0.shape, F32),
                   jax.ShapeDtypeStruct(n0.shape, F32), jax.ShapeDtypeStruct((bs, 1, H), F32)),
        grid=(bs,),
        in_specs=[zspec(B_QK, Z_QB), zspec(B_QK, Z_KB), zspec(B_V, Z_VB), zspec(B_V, Z_OB), zspec(HEAD_DIM, Z_IB),
                  pl.BlockSpec((H, 1, DV_B), lambda b: (0, 0, 0)), cspec, nspec, mspec],
        out_specs=(pl.BlockSpec((None, 1, B_V), lambda b: (b, 0, 0)), cspec, nspec, mspec),
        compiler_params=pltpu.CompilerParams(dimension_semantics=("parallel",), vmem_limit_bytes=VMEM_LIMIT),
        name="mlstm_sample",
    )(zs3, zs3, zs3, zs3, zs3, gain.reshape(H, 1, DV_B), c0, n0, m0.reshape(bs, 1, H))
    return h.reshape(bs, B_V), c, n, m.reshape(bs, H)


def _split_kv(z, B, T):
    z = z.reshape(B, T, NZ)
    kv = lambda o: z[..., o:o + A_KV].reshape(B, T, N_KV, HEAD_DIM)
    return kv(Z_KC), kv(Z_VC), kv(Z_KS), kv(Z_VS), kv(Z_KW), kv(Z_VW)


def kernel(x_prompt, x_sample, cache_cmp_k, cache_cmp_v, cache_sel_k, cache_sel_v, cache_win_k, cache_win_v,
           state_mlstm_c, state_mlstm_n, state_mlstm_m, page_table, norm_mix, w_in, b_in, cmp_pos, cmp_w1, cmp_w2,
           mlstm_norm, w_up_a, w_up_b, w_out, norm_ffn, w_router_grp, b_router_grp, w_router_exp, b_router_exp,
           w_gate, w_up, w_down, norm_final):
    assert w_in.shape[0] == 1, "single layer"
    Bp, Tp = x_prompt.shape[:2]
    Bs, Ts = x_sample.shape[:2]
    assert Ts == 1
    n_p, n_s = Bp * Tp, Bs * Ts
    past_len = page_table.shape[1] * PAGE_SIZE
    pos_p = jnp.arange(Tp, dtype=jnp.int32)
    pos_s = past_len + jnp.arange(Ts, dtype=jnp.int32)
    l = 0

    w_perm, b_perm = _permute_in_proj(w_in[l], b_in[l])
    cw = _compress_weights(cmp_pos[l], cmp_w1[l], cmp_w2[l])
    mw = _merge_weights(w_up_a[l], w_up_b[l], w_out[l], norm_ffn[l], w_router_grp[l], b_router_grp[l],
                        w_router_exp[l], b_router_exp[l])

    xp2d = x_prompt.reshape(n_p, D_MODEL)
    zp = _project(xp2d, jnp.tile(pos_p, Bp), norm_mix[l], w_perm, b_perm, 2048)
    kc, vc, ks, vs, kw, vw = _split_kv(zp, Bp, Tp)
    o_a = _nsa_prompt(zp, _compress_prompt(kc, vc, cw), Bp, Tp)
    h_b, c_p, n_p_state, m_p = _mlstm_prompt(zp, mlstm_norm[l], Bp, Tp)
    bufs = _merge(xp2d, o_a, h_b, zp, mw, n_p + n_s, 0, 256)
    wl = min(WINDOW, Tp)
    prompt_new = (kc, vc, ks, vs, kw[:, -wl:], vw[:, -wl:], c_p, n_p_state, m_p)

    xs2d = x_sample.reshape(n_s, D_MODEL)
    zs = _project(xs2d, jnp.repeat(pos_s, Bs), norm_mix[l], w_perm, b_perm, 128)
    kc, vc, ks, vs, kw, vw = _split_kv(zs, Bs, Ts)
    assert (past_len + Ts - CMP_LEN) // CMP_STRIDE + 1 == past_len // CMP_STRIDE - 1
    zk_pool, zv_pool = _compress_pool(cache_cmp_k[l], cache_cmp_v[l], page_table, cw)
    o_a_s = _nsa_sample(zs, page_table, zk_pool, zv_pool, cache_sel_k[l], cache_sel_v[l],
                        cache_win_k[l], cache_win_v[l], cw[2])
    h_s, c_s, n_s_state, m_s = _mlstm_sample(zs, mlstm_norm[l], state_mlstm_c[l], state_mlstm_n[l],
                                             state_mlstm_m[l])
    x2, xn, route = _merge(xs2d, o_a_s, h_s, zs, mw, n_p + n_s, n_p, 128, bufs=bufs)
    wk = jnp.concatenate([cache_win_k[l], kw], axis=1)
    wv = jnp.concatenate([cache_win_v[l], vw], axis=1)
    wl = min(WINDOW, past_len + Ts)
    sample_new = (kc, vc, ks, vs, wk[:, -wl:], wv[:, -wl:], c_s, n_s_state, m_s)

    y_p, y_s = _moe(x2, xn, route, w_gate[l], w_up[l], w_down[l], norm_final, n_p)
    return ((y_p.reshape(Bp, Tp, D_MODEL), y_s.reshape(Bs, Ts, D_MODEL))
            + tuple(a[None] for a in prompt_new) + tuple(a[None] for a in sample_new))
```
